```python
import jax
import jax.numpy as jnp
from jax import lax
import numpy as np

D_MODEL = 1024
BATCH = 2
SEQ = 8192
DEPTH = 4
DEC_BATCH = 32
DEC_SEQ = 1
PAST_LEN = 8192
PAGE_SIZE = 128

HEAD_DIM = 64
N_HEADS_TOTAL = D_MODEL // HEAD_DIM
N_MEM_HEADS = 4
N_MEM = 256
N_A_HEADS = (N_HEADS_TOTAL - N_MEM_HEADS) // 2
N_A_KV = 2
N_B_HEADS = N_HEADS_TOTAL - N_MEM_HEADS - N_A_HEADS
N_B_KV = 2
N_C_HEADS = N_HEADS_TOTAL - N_MEM_HEADS
N_C_KV = 2
N_IDX_HEADS = 8
IDX_DIM = 64
MOBA_BLOCK = 256
MOBA_TOPK = 3
CMP_STRIDE = 16
CMP_LEN = 2 * CMP_STRIDE
CMP_HIDDEN = 128
SEL_BLOCK = 4 * CMP_STRIDE
SEL_TOPK = 15
SLC_TAPS = (1.0, 2.0, 2.0, 2.0)
WINDOW = 512
DSA_TOPK = 256
Q_BLOCK = 64
KEY_ALIGN = 256
N_EVEN = (DEPTH + 1) // 2
N_ODD = DEPTH // 2
DEEPNORM_ALPHA = (2.0 * DEPTH) ** 0.25
DEEPNORM_BETA = (8.0 * DEPTH) ** -0.25
ATTN_SCALE = HEAD_DIM ** -0.5
IDX_SCALE = IDX_DIM ** -0.5
LN_EPS = 1e-5
NEG_INF = -1e30

EVEN_SIZES = (N_A_HEADS * HEAD_DIM, N_A_KV * HEAD_DIM, N_A_KV * HEAD_DIM, N_A_HEADS * HEAD_DIM,
              N_B_HEADS * HEAD_DIM, N_B_KV * HEAD_DIM, N_B_KV * HEAD_DIM, N_B_KV * HEAD_DIM,
              N_B_KV * HEAD_DIM, N_B_KV * HEAD_DIM, N_B_KV * HEAD_DIM, N_B_HEADS * 3, N_B_HEADS * HEAD_DIM,
              N_MEM_HEADS * HEAD_DIM, N_MEM_HEADS * HEAD_DIM)
ODD_SIZES = (N_C_HEADS * HEAD_DIM, N_C_KV * HEAD_DIM, N_C_KV * HEAD_DIM, N_IDX_HEADS * IDX_DIM, IDX_DIM,
             N_IDX_HEADS, N_C_HEADS * HEAD_DIM, N_MEM_HEADS * HEAD_DIM, N_MEM_HEADS * HEAD_DIM)

kernel_name = 'moba_nsa_dsa_hybrid_decode_step'


def _layernorm(x, g, b):
    xf = x.astype(jnp.float32)
    mu = jnp.mean(xf, axis=-1, keepdims=True)
    var = jnp.mean(jnp.square(xf - mu), axis=-1, keepdims=True)
    y = (xf - mu) * lax.rsqrt(var + LN_EPS) * g.astype(jnp.float32) + b.astype(jnp.float32)
    return y.astype(x.dtype)


def _split(h, sizes):
    return jnp.split(h, np.cumsum(sizes)[:-1].tolist(), axis=-1)


def _heads(t, n):
    return t.reshape(t.shape[:2] + (n, HEAD_DIM))


def _alibi_slopes(n):
    return jnp.asarray(2.0 ** (-8.0 * np.arange(1, n + 1) / n), dtype=jnp.float32)


def _masked_softmax(s, mask):
    s = jnp.where(mask, s, NEG_INF)
    m = jnp.max(s, axis=-1, keepdims=True)
    p = jnp.where(mask, jnp.exp(s - m), 0.0)
    return p / jnp.maximum(jnp.sum(p, axis=-1, keepdims=True), 1e-30)


def _gather_pages(pool, page_table):
    rows = pool[page_table]
    return rows.reshape((rows.shape[0], rows.shape[1] * rows.shape[2]) + rows.shape[3:])


def _pad_keys(t, n_pad):
    return jnp.pad(t, [(0, 0), (0, n_pad - t.shape[1])] + [(0, 0)] * (t.ndim - 2))


def _qblock_map(fn, q_off, arrays):
    B, T = arrays[0].shape[:2]
    qb = Q_BLOCK if T % Q_BLOCK == 0 else T
    nb = T // qb
    blocked = tuple(a.reshape((B, nb, qb) + a.shape[2:]).swapaxes(0, 1) for a in arrays)
    starts = jnp.arange(nb, dtype=jnp.int32) * qb + q_off
    out = lax.map(lambda xs: fn(xs[0], *xs[1:]), (starts,) + blocked)
    return out.swapaxes(0, 1).reshape((B, T) + out.shape[3:])


def _moba_attention(q, k, v, q_off, slopes):
    B, Lp = k.shape[:2]
    nblk = Lp // MOBA_BLOCK
    topk = min(MOBA_TOPK, nblk)
    rep = N_A_HEADS // N_A_KV
    kb = k.reshape(B, nblk, MOBA_BLOCK, N_A_KV, HEAD_DIM).transpose(0, 3, 1, 2, 4)
    vb = v.reshape(B, nblk, MOBA_BLOCK, N_A_KV, HEAD_DIM).transpose(0, 3, 1, 2, 4)
    kmean = jnp.mean(kb.astype(jnp.float32), axis=3)
    sl = slopes.reshape(1, 1, N_A_KV, rep, 1, 1)
    bi = jnp.arange(B)[:, None, None, None, None]
    gi = jnp.arange(N_A_KV)[None, None, :, None, None]
    offs = jnp.arange(MOBA_BLOCK)

    def block(start, qblk):
        qb = qblk.shape[1]
        pos = start + jnp.arange(qb)
        qg = qblk.reshape(B, qb, N_A_KV, rep, HEAD_DIM)
        gate = jnp.einsum('bqgrd,bgnd->bqgrn', qg.astype(jnp.float32), kmean)
        own = pos // MOBA_BLOCK
        past = (jnp.arange(nblk)[None, :] < own[:, None])[None, :, None, None, :]
        _, sel = lax.top_k(jnp.where(past, gate, NEG_INF), topk)
        own_b = jnp.broadcast_to(own[None, :, None, None, None], sel.shape[:-1] + (1,))
        blocks = jnp.concatenate([sel, own_b], axis=-1)
        blk_ok = jnp.concatenate([sel < own_b, jnp.ones(own_b.shape, bool)], axis=-1)
        kg = kb[bi, gi, blocks]
        vg = vb[bi, gi, blocks]
        kpos = blocks[..., None] * MOBA_BLOCK + offs
        qpos = pos[None, :, None, None, None, None]
        s = jnp.einsum('bqgrd,bqgrjsd->bqgrjs', qg, kg).astype(jnp.float32) * ATTN_SCALE
        s = s - sl * (qpos - kpos).astype(jnp.float32)
        mask = blk_ok[..., None] & (kpos <= qpos)
        nk = blocks.shape[-1] * MOBA_BLOCK
        p = _masked_softmax(s.reshape(s.shape[:4] + (nk,)), mask.reshape(mask.shape[:4] + (nk,)))
        o = jnp.einsum('bqgrk,bqgrkd->bqgrd', p.astype(vg.dtype), vg.reshape(vg.shape[:4] + (nk, HEAD_DIM)))
        return o.reshape(B, qb, N_A_HEADS, HEAD_DIM)

    return _qblock_map(block, q_off, (q,))


def _nsa_compress(raw, pe, w1, w2):
    B, Lp, G, _ = raw.shape
    ch = raw.reshape(B, Lp // CMP_STRIDE, CMP_STRIDE, G, HEAD_DIM)
    blk = jnp.concatenate([ch[:, :-1], ch[:, 1:]], axis=2) + pe[None, None, :, None, :]
    flat = blk.transpose(0, 1, 3, 2, 4).reshape(B, Lp // CMP_STRIDE - 1, G, CMP_LEN * HEAD_DIM)
    return jax.nn.silu(flat @ w1) @ w2


def _nsa_attention(q, gates, kc_raw, vc_raw, ks, vs, kw, vw, q_off, w_off, slopes, pe, w1, w2):
    B, Lp = ks.shape[:2]
    rep = N_B_HEADS // N_B_KV
    kc = _nsa_compress(kc_raw, pe[0], w1[0], w2[0])
    vc = _nsa_compress(vc_raw, pe[1], w1[1], w2[1])
    n_cmp = kc.shape[1]
    cend = jnp.arange(n_cmp) * CMP_STRIDE + (CMP_LEN - 1)
    n_sel = Lp // SEL_BLOCK
    topk = min(SEL_TOPK, n_sel)
    ksb = ks.reshape(B, n_sel, SEL_BLOCK, N_B_KV, HEAD_DIM).transpose(0, 3, 1, 2, 4)
    vsb = vs.reshape(B, n_sel, SEL_BLOCK, N_B_KV, HEAD_DIM).transpose(0, 3, 1, 2, 4)
    kw_pad = jnp.pad(kw, ((0, 0), (WINDOW, 0), (0, 0), (0, 0)))
    vw_pad = jnp.pad(vw, ((0, 0), (WINDOW, 0), (0, 0), (0, 0)))
    sl = slopes.reshape(1, 1, N_B_KV, rep, 1)
    taps = jnp.asarray(SLC_TAPS, dtype=jnp.float32)
    bi = jnp.arange(B)[:, None, None, None]
    gi = jnp.arange(N_B_KV)[None, None, :, None]
    offs = jnp.arange(SEL_BLOCK)

    def block(start, qblk, gblk):
        qb = qblk.shape[1]
        pos = start + jnp.arange(qb)
        qg = qblk.reshape(B, qb, N_B_KV, rep, HEAD_DIM)
        dist_c = (pos[:, None] - cend[None, :]).astype(jnp.float32)[None, :, None, None, :]
        s = jnp.einsum('bqgrd,bngd->bqgrn', qg, kc).astype(jnp.float32) * ATTN_SCALE - sl * dist_c
        p_c = _masked_softmax(s, (cend[None, :] <= pos[:, None])[None, :, None, None, :])
        o_c = jnp.einsum('bqgrn,bngd->bqgrd', p_c.astype(vc.dtype), vc)
        imp = jnp.pad(jnp.sum(p_c, axis=3), ((0, 0), (0, 0), (0, 0), (1, 1)))
        p_slc = jnp.einsum('bqgjt,t->bqgj', imp[..., :4 * n_sel].reshape(B, qb, N_B_KV, n_sel, 4), taps) + imp[..., 4::4]
        own = pos // SEL_BLOCK
        past = (jnp.arange(n_sel)[None, :] < own[:, None])[None, :, None, :]
        _, sel = lax.top_k(jnp.where(past, p_slc, NEG_INF), topk)
        own_b = jnp.broadcast_to(own[None, :, None, None], sel.shape[:-1] + (1,))
        blocks = jnp.concatenate([sel, own_b], axis=-1)
        ok = jnp.concatenate([sel < own_b, jnp.ones(own_b.shape, bool)], axis=-1)
        kg = ksb[bi, gi, blocks]
        vg = vsb[bi, gi, blocks]
        kpos = blocks[..., None] * SEL_BLOCK + offs
        qpos = pos[None, :, None, None, None]
        nk = blocks.shape[-1] * SEL_BLOCK
        s = jnp.einsum('bqgrd,bqgjsd->bqgrjs', qg, kg).astype(jnp.float32) * ATTN_SCALE
        s = s - sl[..., None] * (qpos - kpos).astype(jnp.float32)[:, :, :, None]
        mask = (ok[..., None] & (kpos <= qpos))[:, :, :, None]
        p_s = _masked_softmax(s.reshape(B, qb, N_B_KV, rep, nk), mask.reshape(B, qb, N_B_KV, 1, nk))
        o_s = jnp.einsum('bqgrk,bqgkd->bqgrd', p_s.astype(vg.dtype), vg.reshape(B, qb, N_B_KV, nk, HEAD_DIM))
        nw = WINDOW + qb - 1
        lo = start - (WINDOW - 1)
        kwin = lax.dynamic_slice_in_dim(kw_pad, lo - w_off + WINDOW, nw, axis=1)
        vwin = lax.dynamic_slice_in_dim(vw_pad, lo - w_off + WINDOW, nw, axis=1)
        kpos_w = lo + jnp.arange(nw)
        dist_w = pos[:, None] - kpos_w[None, :]
        mask_w = (dist_w >= 0) & (dist_w < WINDOW) & (kpos_w[None, :] >= w_off)
        s = jnp.einsum('bqgrd,bsgd->bqgrs', qg, kwin).astype(jnp.float32) * ATTN_SCALE
        s = s - sl * dist_w.astype(jnp.float32)[None, :, None, None, :]
        p_w = _masked_softmax(s, mask_w[None, :, None, None, :])
        o_w = jnp.einsum('bqgrs,bsgd->bqgrd', p_w.astype(vwin.dtype), vwin)
        g = jax.nn.sigmoid(gblk.astype(jnp.float32)).reshape(B, qb, N_B_KV, rep, 3)
        o = g[..., 0:1] * o_c + g[..., 1:2] * o_s + g[..., 2:3] * o_w
        return o.reshape(B, qb, N_B_HEADS, HEAD_DIM).astype(q.dtype)

    return _qblock_map(block, q_off, (q, gates))


def _dsa_attention(q, qi, wi, k, v, ki, q_off, n_keys, slopes):
    B, Lp = k.shape[:2]
    rep = N_C_HEADS // N_C_KV
    topk = min(DSA_TOPK, n_keys // 4)
    sl = slopes.reshape(1, 1, N_C_KV, rep, 1)
    bi = jnp.arange(B)[:, None, None]
    kidx = jnp.arange(Lp)

    def block(start, qblk, qiblk, wiblk):
        qb = qblk.shape[1]
        pos = start + jnp.arange(qb)
        rel = jax.nn.relu(jnp.einsum('bqhd,bsd->bqhs', qiblk, ki).astype(jnp.float32) * IDX_SCALE)
        score = jnp.einsum('bqhs,bqh->bqs', rel, wiblk.astype(jnp.float32))
        score = jnp.where(kidx[None, None, :] <= pos[None, :, None], score, NEG_INF)
        _, sel = lax.top_k(score, topk)
        kg = k[bi, sel]
        vg = v[bi, sel]
        dist = pos[None, :, None] - sel
        qg = qblk.reshape(B, qb, N_C_KV, rep, HEAD_DIM)
        s = jnp.einsum('bqgrd,bqkgd->bqgrk', qg, kg).astype(jnp.float32) * ATTN_SCALE
        s = s - sl * dist.astype(jnp.float32)[:, :, None, None, :]
        p = _masked_softmax(s, (dist >= 0)[:, :, None, None, :])
        o = jnp.einsum('bqgrk,bqkgd->bqgrd', p.astype(vg.dtype), vg)
        return o.reshape(B, qb, N_C_HEADS, HEAD_DIM)

    return _qblock_map(block, q_off, (q, qi, wi))


def _mem_attention(q, mk, mv):
    s = jnp.einsum('bthd,bmhd->bthm', q, mk).astype(jnp.float32) * ATTN_SCALE
    p = jax.nn.softmax(s, axis=-1)
    return jnp.einsum('bthm,bmhd->bthd', p.astype(mv.dtype), mv)


def _even_layer(x, mem_k, mem_v, past, w_in, w_out, ln_g, ln_b, cmp_pe, cmp_w1, cmp_w2):
    B, T, _ = x.shape
    (a_q, a_k, a_v, a_z, b_q, b_kc, b_vc, b_ks, b_vs, b_kw, b_vw, b_g, b_z, m_q, m_z) = _split(x @ w_in, EVEN_SIZES)
    new = (_heads(a_k, N_A_KV), _heads(a_v, N_A_KV), _heads(b_kc, N_B_KV), _heads(b_vc, N_B_KV),
           _heads(b_ks, N_B_KV), _heads(b_vs, N_B_KV))
    kw_new, vw_new = _heads(b_kw, N_B_KV), _heads(b_vw, N_B_KV)
    if past is None:
        q_off, w_off = 0, 0
        full = new
        kw, vw = kw_new, vw_new
        n_win = min(WINDOW, T)
    else:
        q_off = past[0].shape[1]
        full = tuple(jnp.concatenate([p, n], axis=1) for p, n in zip(past[:6], new))
        kw = jnp.concatenate([past[6], kw_new], axis=1)
        vw = jnp.concatenate([past[7], vw_new], axis=1)
        n_win = past[6].shape[1]
        w_off = q_off - n_win
    n_pad = -(-(q_off + T) // KEY_ALIGN) * KEY_ALIGN
    full = tuple(_pad_keys(t, n_pad) for t in full)
    slopes = _alibi_slopes(N_A_HEADS + N_B_HEADS)
    o_a = _moba_attention(_heads(a_q, N_A_HEADS), full[0], full[1], q_off, slopes[0::2])
    o_b = _nsa_attention(_heads(b_q, N_B_HEADS), b_g, full[2], full[3], full[4], full[5], kw, vw,
                         q_off, w_off, slopes[1::2], cmp_pe, cmp_w1, cmp_w2)
    o_m = _mem_attention(_heads(m_q, N_MEM_HEADS), mem_k, mem_v)
    mixed = jnp.concatenate([o_a.reshape(B, T, -1) * jax.nn.silu(a_z),
                             o_b.reshape(B, T, -1) * jax.nn.silu(b_z),
                             o_m.reshape(B, T, -1) * jax.nn.silu(m_z)], axis=-1)
    y = _layernorm(DEEPNORM_ALPHA * x + mixed @ w_out, ln_g, ln_b)
    return y, new + (kw[:, -n_win:], vw[:, -n_win:])


def _odd_layer(x, mem_k, mem_v, past, w_in, w_out, ln_g, ln_b):
    B, T, _ = x.shape
    (c_q, c_k, c_v, c_qi, c_ki, c_wi, c_z, m_q, m_z) = _split(x @ w_in, ODD_SIZES)
    new = (_heads(c_k, N_C_KV), _heads(c_v, N_C_KV), c_ki)
    if past is None:
        q_off = 0
        full = new
    else:
        q_off = past[0].shape[1]
        full = tuple(jnp.concatenate([p, n], axis=1) for p, n in zip(past, new))
    n_keys = q_off + T
    n_pad = -(-n_keys // KEY_ALIGN) * KEY_ALIGN
    full = tuple(_pad_keys(t, n_pad) for t in full)
    o_c = _dsa_attention(_heads(c_q, N_C_HEADS), c_qi.reshape(B, T, N_IDX_HEADS, IDX_DIM),
                         c_wi * N_IDX_HEADS ** -0.5, full[0], full[1], full[2], q_off, n_keys,
                         _alibi_slopes(N_C_HEADS))
    o_m = _mem_attention(_heads(m_q, N_MEM_HEADS), mem_k, mem_v)
    mixed = jnp.concatenate([o_c.reshape(B, T, -1) * jax.nn.silu(c_z),
                             o_m.reshape(B, T, -1) * jax.nn.silu(m_z)], axis=-1)
    y = _layernorm(DEEPNORM_ALPHA * x + mixed @ w_out, ln_g, ln_b)
    return y, new


def setup_inputs(seed: int = 0) -> dict:
    key = jax.random.key(seed)
    keys = iter(jax.random.split(key, 40))
    f32 = jnp.float32

    def nrm(shape, scale=1.0):
        return jax.random.normal(next(keys), shape, f32) * scale

    n_pages = PAST_LEN // PAGE_SIZE
    n_used = DEC_BATCH * n_pages
    n_pool = n_used + (n_used + 3) // 4
    win_state = min(WINDOW, PAST_LEN)
    page_table = jax.random.permutation(next(keys), n_pool)[:n_used].reshape(DEC_BATCH, n_pages).astype(jnp.int32)
    kv_even = (N_EVEN, n_pool, PAGE_SIZE, N_A_KV, HEAD_DIM)
    kv_nsa = (N_EVEN, n_pool, PAGE_SIZE, N_B_KV, HEAD_DIM)
    kv_odd = (N_ODD, n_pool, PAGE_SIZE, N_C_KV, HEAD_DIM)
    win = (N_EVEN, DEC_BATCH, win_state, N_B_KV, HEAD_DIM)
    memkv = (DEPTH, DEC_BATCH, N_MEM, N_MEM_HEADS, HEAD_DIM)
    even_in = sum(EVEN_SIZES)
    odd_in = sum(ODD_SIZES)
    mix = N_HEADS_TOTAL * HEAD_DIM
    return {
        'x_prompt': nrm((BATCH, SEQ, D_MODEL)),
        'x_sample': nrm((DEC_BATCH, DEC_SEQ, D_MODEL)),
        'cache_a_k': nrm(kv_even),
        'cache_a_v': nrm(kv_even),
        'cache_b_cmp_k': nrm(kv_nsa),
        'cache_b_cmp_v': nrm(kv_nsa),
        'cache_b_sel_k': nrm(kv_nsa),
        'cache_b_sel_v': nrm(kv_nsa),
        'state_b_win_k': nrm(win),
        'state_b_win_v': nrm(win),
        'cache_c_k': nrm(kv_odd),
        'cache_c_v': nrm(kv_odd),
        'cache_c_idx_k': nrm((N_ODD, n_pool, PAGE_SIZE, IDX_DIM)),
        'cache_mem_k': nrm(memkv),
        'cache_mem_v': nrm(memkv),
        'page_table': page_table,
        'mem_prompt': nrm((BATCH, N_MEM, D_MODEL)),
        'w_in_even': nrm((N_EVEN, D_MODEL, even_in), D_MODEL ** -0.5),
        'w_out_even': nrm((N_EVEN, mix, D_MODEL), mix ** -0.5 * DEEPNORM_BETA),
        'w_in_odd': nrm((N_ODD, D_MODEL, odd_in), D_MODEL ** -0.5),
        'w_out_odd': nrm((N_ODD, mix, D_MODEL), mix ** -0.5 * DEEPNORM_BETA),
        'w_mem_kv': nrm((DEPTH, D_MODEL, 2 * N_MEM_HEADS * HEAD_DIM), D_MODEL ** -0.5),
        'ln_g': 1.0 + nrm((DEPTH, D_MODEL), 0.01),
        'ln_b': nrm((DEPTH, D_MODEL), 0.01),
        'cmp_pe': nrm((N_EVEN, 2, CMP_LEN, HEAD_DIM), 0.1),
        'cmp_w1': nrm((N_EVEN, 2, CMP_LEN * HEAD_DIM, CMP_HIDDEN), (CMP_LEN * HEAD_DIM) ** -0.5),
        'cmp_w2': nrm((N_EVEN, 2, CMP_HIDDEN, HEAD_DIM), CMP_HIDDEN ** -0.5),
    }


def reference(x_prompt, x_sample, cache_a_k, cache_a_v, cache_b_cmp_k, cache_b_cmp_v, cache_b_sel_k, cache_b_sel_v,
              state_b_win_k, state_b_win_v, cache_c_k, cache_c_v, cache_c_idx_k, cache_mem_k, cache_mem_v,
              page_table, mem_prompt, w_in_even, w_out_even, w_in_odd, w_out_odd, w_mem_kv, ln_g, ln_b,
              cmp_pe, cmp_w1, cmp_w2):
    even_paged = (cache_a_k, cache_a_v, cache_b_cmp_k, cache_b_cmp_v, cache_b_sel_k, cache_b_sel_v)
    odd_paged = (cache_c_k, cache_c_v, cache_c_idx_k)
    ev_p = [[] for _ in range(8)]
    ev_s = [[] for _ in range(8)]
    od_p = [[] for _ in range(3)]
    od_s = [[] for _ in range(3)]
    mk_list, mv_list = [], []
    xp, xs = x_prompt, x_sample
    bp = x_prompt.shape[0]
    for layer in range(DEPTH):
        j = layer // 2
        mem_kv = (mem_prompt @ w_mem_kv[layer]).reshape(bp, N_MEM, 2, N_MEM_HEADS, HEAD_DIM)
        mk_p, mv_p = mem_kv[:, :, 0], mem_kv[:, :, 1]
        mk_list.append(mk_p)
        mv_list.append(mv_p)
        if layer % 2 == 0:
            params = (w_in_even[j], w_out_even[j], ln_g[layer], ln_b[layer], cmp_pe[j], cmp_w1[j], cmp_w2[j])
            past = tuple(_gather_pages(c[j], page_table) for c in even_paged) + (state_b_win_k[j], state_b_win_v[j])
            xp, st_p = _even_layer(xp, mk_p, mv_p, None, *params)
            xs, st_s = _even_layer(xs, cache_mem_k[layer], cache_mem_v[layer], past, *params)
            for lst, a in zip(ev_p, st_p):
                lst.append(a)
            for lst, a in zip(ev_s, st_s):
                lst.append(a)
        else:
            params = (w_in_odd[j], w_out_odd[j], ln_g[layer], ln_b[layer])
            past = tuple(_gather_pages(c[j], page_table) for c in odd_paged)
            xp, st_p = _odd_layer(xp, mk_p, mv_p, None, *params)
            xs, st_s = _odd_layer(xs, cache_mem_k[layer], cache_mem_v[layer], past, *params)
            for lst, a in zip(od_p, st_p):
                lst.append(a)
            for lst, a in zip(od_s, st_s):
                lst.append(a)
    a_k_p, a_v_p, cmp_k_p, cmp_v_p, sel_k_p, sel_v_p, win_k_p, win_v_p = [jnp.stack(l) for l in ev_p]
    a_k_s, a_v_s, cmp_k_s, cmp_v_s, sel_k_s, sel_v_s, win_k_s, win_v_s = [jnp.stack(l) for l in ev_s]
    c_k_p, c_v_p, c_idx_p = [jnp.stack(l) for l in od_p]
    c_k_s, c_v_s, c_idx_s = [jnp.stack(l) for l in od_s]
    mem_k_p = jnp.stack(mk_list)
    mem_v_p = jnp.stack(mv_list)
    return (xp, xs, a_k_p, a_k_s, a_v_p, a_v_s, cmp_k_p, cmp_k_s, cmp_v_p, cmp_v_s,
            sel_k_p, sel_k_s, sel_v_p, sel_v_s, win_k_p, win_k_s, win_v_p, win_v_s,
            c_k_p, c_k_s, c_v_p, c_v_s, c_idx_p, c_idx_s, mem_k_p, mem_v_p)
```

```python
import functools

import numpy as np
import jax
import jax.numpy as jnp
from jax import lax
from jax.experimental import pallas as pl
from jax.experimental.pallas import tpu as pltpu

F32 = jnp.float32
BF16 = jnp.bfloat16
I32 = jnp.int32

HEAD_DIM = 64
N_MEM_HEADS = 4
N_A_HEADS = 6
N_B_HEADS = 6
N_C_HEADS = 12
N_KV = 2
N_IDX_HEADS = 8
IDX_DIM = 64
MOBA_BLOCK = 256
MOBA_TOPK = 3
CMP_STRIDE = 16
CMP_LEN = 32
CMP_HIDDEN = 128
SEL_BLOCK = 64
SEL_TOPK = 15
WINDOW = 512
DSA_TOPK = 256
ATTN_SCALE = HEAD_DIM ** -0.5
IDX_SCALE = IDX_DIM ** -0.5
LN_EPS = 1e-5
NEG = -1e30
SENTINEL = -3e38
INT_MIN = -(2 ** 31)

KV_TILE = 256
SEL_TILE = 512
PAGES_PER_STEP = 8
VMEM_LIMIT = 56 * 1024 * 1024

EVEN_ROWS = dict(a_q=0, b_q=384, m_q=768, z=1024, a_k=2048, a_v=2176, b_kc=2304, b_vc=2432,
                 b_ks=2560, b_vs=2688, b_kw=2816, b_vw=2944, b_g=3072)
EVEN_TOTAL = 3200
ODD_ROWS = dict(c_q=0, m_q=768, z=1024, c_qi=2048, c_k=2560, c_v=2688, c_ki=2816, c_wi=2880)
ODD_TOTAL = 3072


def _cparams(sem):
    return pltpu.CompilerParams(dimension_semantics=sem, vmem_limit_bytes=VMEM_LIMIT)


def _alibi(n):
    return 2.0 ** (-8.0 * np.arange(1, n + 1) / n)


def _mm_kernel(a_ref, b_ref, o_ref):
    o_ref[...] = jnp.dot(a_ref[...].astype(BF16), b_ref[...].astype(BF16), preferred_element_type=F32)


def _mm(a, b, bm, bn):
    m, k = a.shape
    n = b.shape[1]
    return pl.pallas_call(
        _mm_kernel, grid=(n // bn, m // bm),
        in_specs=[pl.BlockSpec((bm, k), lambda j, i: (i, 0)), pl.BlockSpec((k, bn), lambda j, i: (0, j))],
        out_specs=pl.BlockSpec((bm, bn), lambda j, i: (i, j)),
        out_shape=jax.ShapeDtypeStruct((m, n), F32),
        compiler_params=_cparams(("parallel", "parallel")), name="mm")(a, b)


def _outln_kernel(*refs, n_parts, alpha):
    o_refs = refs[:n_parts]
    z_ref, x_ref, w_ref, g_ref, b_ref, y_ref = refs[n_parts:]
    o = jnp.concatenate([r[...] for r in o_refs], axis=0)
    z = z_ref[...]
    mixed = (o * (z / (1.0 + jnp.exp(-z)))).astype(BF16)
    y = alpha * x_ref[...] + jnp.dot(w_ref[...], mixed, preferred_element_type=F32)
    mu = jnp.mean(y, axis=0, keepdims=True)
    d = y - mu
    var = jnp.mean(d * d, axis=0, keepdims=True)
    y_ref[...] = d * lax.rsqrt(var + LN_EPS) * g_ref[...] + b_ref[...]


def _outln(o_parts, hT, z_row, xT, woT, g, b, alpha, bn):
    d, n = xT.shape
    zb = z_row // d
    in_specs = [pl.BlockSpec((p.shape[0], bn), lambda j: (0, j)) for p in o_parts]
    in_specs += [pl.BlockSpec((d, bn), lambda j: (zb, j)),
                 pl.BlockSpec((d, bn), lambda j: (0, j)),
                 pl.BlockSpec((d, d), lambda j: (0, 0)),
                 pl.BlockSpec((d, 1), lambda j: (0, 0)),
                 pl.BlockSpec((d, 1), lambda j: (0, 0))]
    return pl.pallas_call(
        functools.partial(_outln_kernel, n_parts=len(o_parts), alpha=alpha), grid=(n // bn,),
        in_specs=in_specs, out_specs=pl.BlockSpec((d, bn), lambda j: (0, j)),
        out_shape=jax.ShapeDtypeStruct((d, n), F32),
        compiler_params=_cparams(("parallel",)), name="outln")(*o_parts, hT, xT, woT, g, b)


def _gather_kernel(pt_ref, *refs, npg, rows, last):
    pool_refs = refs[:npg]
    tail_ref, o_ref = refs[npg], refs[npg + 1]
    j = pl.program_id(1)

    @pl.when(j < last)
    def _():
        for k in range(npg):
            o_ref[0, k * rows:(k + 1) * rows, :] = pool_refs[k][0]

    @pl.when(j == last)
    def _():
        o_ref[0] = tail_ref[0]


def _gather(pool, page_table, tail):
    nb, n_pages = page_table.shape
    rows = pool.shape[1]
    npg = PAGES_PER_STEP
    last = n_pages // npg

    def pool_spec(k):
        return pl.BlockSpec((1, rows, 128),
                            lambda b, j, pt: (pt[b, jnp.minimum(j * npg + k, n_pages - 1)], 0, 0))

    grid_spec = pltpu.PrefetchScalarGridSpec(
        num_scalar_prefetch=1, grid=(nb, last + 1),
        in_specs=[pool_spec(k) for k in range(npg)] + [pl.BlockSpec((1, npg * rows, 128), lambda b, j, pt: (b, 0, 0))],
        out_specs=pl.BlockSpec((1, npg * rows, 128), lambda b, j, pt: (b, j, 0)))
    return pl.pallas_call(
        functools.partial(_gather_kernel, npg=npg, rows=rows, last=last), grid_spec=grid_spec,
        out_shape=jax.ShapeDtypeStruct((nb, (last + 1) * npg * rows, 128), F32),
        compiler_params=_cparams(("parallel", "arbitrary")), name="gather")(page_table, *([pool] * npg), tail)


def _kprep_kernel(x_ref, k_ref, mean_ref, *, nblk):
    x = x_ref[0]
    for g in range(N_KV):
        k_ref[0, g] = x[:, g * HEAD_DIM:(g + 1) * HEAD_DIM].astype(BF16).reshape(nblk, KV_TILE, HEAD_DIM)
    mean_ref[0, 0] = jnp.mean(x.reshape(nblk, KV_TILE, 128), axis=1)


def _kprep(x):
    nb, L, _ = x.shape
    nblk = 4
    steps = L // (nblk * KV_TILE)
    k, mean = pl.pallas_call(
        functools.partial(_kprep_kernel, nblk=nblk), grid=(nb, steps),
        in_specs=[pl.BlockSpec((1, nblk * KV_TILE, 128), lambda b, j: (b, j, 0))],
        out_specs=[pl.BlockSpec((1, N_KV, nblk, KV_TILE, HEAD_DIM), lambda b, j: (b, 0, j, 0, 0)),
                   pl.BlockSpec((1, 1, nblk, 128), lambda b, j: (b, j, 0, 0))],
        out_shape=[jax.ShapeDtypeStruct((nb, N_KV, L // KV_TILE, KV_TILE, HEAD_DIM), BF16),
                   jax.ShapeDtypeStruct((nb, steps, nblk, 128), F32)],
        compiler_params=_cparams(("parallel", "parallel")), name="kprep")(x)
    return k, mean.reshape(nb, L // KV_TILE, 128)


def _vT_tiles(v, tile):
    nb, L, c = v.shape
    return v.reshape(nb, L // tile, tile, c).transpose(0, 1, 3, 2).astype(BF16)


def _flash_update(s, mask, vT, m, l, acc):
    s = jnp.where(mask, s, NEG)
    m_new = jnp.maximum(m, jnp.max(s, axis=0, keepdims=True))
    alpha = jnp.exp(m - m_new)
    p = jnp.where(mask, jnp.exp(s - m_new), 0.0)
    l = alpha * l + jnp.sum(p, axis=0, keepdims=True)
    acc = alpha * acc + jnp.dot(vT, p.astype(BF16), preferred_element_type=F32)
    return m_new, l, acc


def _flash_init(tq):
    return (jnp.full((1, tq), NEG, F32), jnp.zeros((1, tq), F32), jnp.zeros((HEAD_DIM, tq), F32))


def _flash_out(l, acc):
    return acc / jnp.maximum(l, 1e-30)


def _top_rows(vals, rowi, k):
    sel = jnp.zeros(vals.shape, F32)
    for _ in range(k):
        m = jnp.max(vals, axis=0, keepdims=True)
        first = jnp.min(jnp.where(vals == m, rowi, 1 << 20), axis=0, keepdims=True)
        pick = (rowi == first) & (m > SENTINEL)
        sel = jnp.where(pick, 1.0, sel)
        vals = jnp.where(pick, SENTINEL, vals)
    return sel


def _moba_kernel(q_ref, kmean_ref, k_ref, vT_ref, slope_ref, o_ref, sel_ref, *, tq, q_off):
    i = pl.program_id(2)
    t0 = q_off + i * tq
    own = t0 // MOBA_BLOCK
    qpos = t0 + lax.broadcasted_iota(I32, (1, tq), 1)
    q32 = q_ref[...]
    q = (q32 * ATTN_SCALE).astype(BF16)
    slope = slope_ref[0]
    nrow = kmean_ref.shape[2]
    gate = jnp.dot(kmean_ref[0, 0], q32, precision=lax.Precision.HIGHEST, preferred_element_type=F32)
    rowi = lax.broadcasted_iota(I32, (nrow, tq), 0)
    sel_ref[...] = _top_rows(jnp.where(rowi < own, gate, SENTINEL), rowi, MOBA_TOPK)
    krow = lax.broadcasted_iota(I32, (KV_TILE, tq), 0)

    def scores(n):
        s = jnp.dot(k_ref[0, 0, n], q, preferred_element_type=F32)
        kpos = n * KV_TILE + krow
        return s - slope * (qpos - kpos).astype(F32), kpos

    def body(n, carry):
        s, _ = scores(n)
        mask = jnp.broadcast_to(sel_ref[pl.ds(n, 1), :] > 0.5, s.shape)
        return _flash_update(s, mask, vT_ref[0, n], *carry)

    carry = lax.fori_loop(0, own, body, _flash_init(tq))
    s, kpos = scores(own)
    _, l, acc = _flash_update(s, kpos <= qpos, vT_ref[0, own], *carry)
    o_ref[...] = _flash_out(l, acc)


def _moba(qT, kmean, k, vT, nb, nq, tq, q_off):
    nT = k.shape[2]
    slopes = jnp.asarray(np.broadcast_to(_alibi(12)[0::2].astype(np.float32)[:, None, None], (N_A_HEADS, 1, tq)))
    rep = N_A_HEADS // N_KV
    return pl.pallas_call(
        functools.partial(_moba_kernel, tq=tq, q_off=q_off), grid=(nb, N_KV, nq, rep),
        in_specs=[pl.BlockSpec((HEAD_DIM, tq), lambda b, g, i, r: (g * rep + r, b * nq + i)),
                  pl.BlockSpec((1, 1, nT, HEAD_DIM), lambda b, g, i, r: (b, g, 0, 0)),
                  pl.BlockSpec((1, 1, nT, KV_TILE, HEAD_DIM), lambda b, g, i, r: (b, g, 0, 0, 0)),
                  pl.BlockSpec((1, nT, HEAD_DIM, KV_TILE), lambda b, g, i, r: (b, 0, g, 0)),
                  pl.BlockSpec((1, 1, tq), lambda b, g, i, r: (g * rep + r, 0, 0))],
        out_specs=pl.BlockSpec((HEAD_DIM, tq), lambda b, g, i, r: (g * rep + r, b * nq + i)),
        out_shape=jax.ShapeDtypeStruct((N_A_HEADS * HEAD_DIM, nb * nq * tq), F32),
        scratch_shapes=[pltpu.VMEM((nT, tq), F32)],
        compiler_params=_cparams(("parallel", "parallel", "parallel", "arbitrary")), name="moba")(
            qT, kmean, k, vT, slopes)


def _cmp_kernel(r_ref, pe_ref, w1_ref, w2_ref, o_ref):
    r = r_ref[0]
    n = r.shape[0]
    hid = 2 * CMP_HIDDEN
    u = jnp.dot((r + pe_ref[0:1, :]).astype(BF16), w1_ref[:, 0:hid], preferred_element_type=F32)
    lo = jnp.dot((r + pe_ref[1:2, :]).astype(BF16), w1_ref[:, hid:2 * hid], preferred_element_type=F32)
    pre = u + pltpu.roll(lo, n - 1, 0)
    h = pre / (1.0 + jnp.exp(-pre))
    o_ref[0] = jnp.dot(h.astype(BF16), w2_ref[...], preferred_element_type=F32)


def _compress(raw, pe2, w1big, w2big, n_rows):
    nb = raw.shape[0]
    return pl.pallas_call(
        _cmp_kernel, grid=(nb,),
        in_specs=[pl.BlockSpec((1, n_rows, 2048), lambda b: (b, 0, 0)),
                  pl.BlockSpec((2, 2048), lambda b: (0, 0)),
                  pl.BlockSpec((2048, 4 * CMP_HIDDEN), lambda b: (0, 0)),
                  pl.BlockSpec((2 * CMP_HIDDEN, 128), lambda b: (0, 0))],
        out_specs=pl.BlockSpec((1, n_rows, 128), lambda b: (b, 0, 0)),
        out_shape=jax.ShapeDtypeStruct((nb, n_rows, 128), F32),
        compiler_params=_cparams(("parallel",)), name="compress")(raw, pe2, w1big, w2big)


def _cmp_weights(pe, w1, w2):
    w1r = w1.reshape(2, CMP_STRIDE, HEAD_DIM, CMP_HIDDEN)
    eye = jnp.eye(N_KV, dtype=w1.dtype)
    big = jnp.einsum('hpdj,ge->hpdgej', w1r, eye)
    big = big.transpose(1, 3, 2, 0, 4, 5).reshape(CMP_STRIDE * N_KV * HEAD_DIM, 2 * N_KV * CMP_HIDDEN)
    w2big = jnp.einsum('jd,ge->gjed', w2, eye).reshape(N_KV * CMP_HIDDEN, N_KV * HEAD_DIM)
    pe2 = jnp.broadcast_to(pe.reshape(2, CMP_STRIDE, 1, HEAD_DIM), (2, CMP_STRIDE, N_KV, HEAD_DIM)).reshape(2, 2048)
    return pe2, big.astype(BF16), w2big.astype(BF16)


def _nsa_kernel(q_ref, g_ref, kc_ref, vcT_ref, taps_ref, ks_ref, vsT_ref, kw_ref, vwT_ref, slope_ref, o_ref,
                sel_ref, *, tq, q_off, kw_base):
    i = pl.program_id(2)
    t0 = q_off + i * tq
    qpos = t0 + lax.broadcasted_iota(I32, (1, tq), 1)
    rep = N_B_HEADS // N_KV
    n_cmp, n_sel = kc_ref.shape[2], sel_ref.shape[0]
    qs = [(q_ref[r * HEAD_DIM:(r + 1) * HEAD_DIM, :] * ATTN_SCALE).astype(BF16) for r in range(rep)]
    slopes = [slope_ref[0, r:r + 1, :] for r in range(rep)]

    cend = CMP_STRIDE * lax.broadcasted_iota(I32, (n_cmp, tq), 0) + (CMP_LEN - 1)
    cmask = cend <= qpos
    cdist = (qpos - cend).astype(F32)
    kc = kc_ref[0, 0]
    vcT = vcT_ref[0, 0]
    imp = jnp.zeros((n_cmp, tq), F32)
    o_c = []
    for r in range(rep):
        s = jnp.dot(kc, qs[r], preferred_element_type=F32) - slopes[r] * cdist
        s = jnp.where(cmask, s, NEG)
        m = jnp.max(s, axis=0, keepdims=True)
        p = jnp.where(cmask, jnp.exp(s - m), 0.0)
        p = p * (1.0 / jnp.maximum(jnp.sum(p, axis=0, keepdims=True), 1e-30))
        imp = imp + p
        o_c.append(jnp.dot(vcT, p.astype(BF16), preferred_element_type=F32))

    p_slc = jnp.dot(taps_ref[...], imp, precision=lax.Precision.HIGHEST, preferred_element_type=F32)
    p_slc = jnp.concatenate([p_slc, jnp.full((n_sel - p_slc.shape[0], tq), SENTINEL, F32)], axis=0)
    rowi = lax.broadcasted_iota(I32, (n_sel, tq), 0)
    own = qpos >> 6
    sel = _top_rows(jnp.where(rowi < own, p_slc, SENTINEL), rowi, SEL_TOPK)
    sel_ref[...] = jnp.where(rowi == own, 1.0, sel)

    n_sel_tiles = (t0 + tq - 1) // SEL_TILE + 1
    srow = lax.broadcasted_iota(I32, (SEL_TILE, tq), 0)
    per_tile = SEL_TILE // SEL_BLOCK
    o_s = []
    for r in range(rep):
        def body(n, carry, r=r):
            s = jnp.dot(ks_ref[0, 0, n], qs[r], preferred_element_type=F32)
            kpos = n * SEL_TILE + srow
            s = s - slopes[r] * (qpos - kpos).astype(F32)
            rows = sel_ref[pl.ds(pl.multiple_of(n * per_tile, per_tile), per_tile), :]
            blk = jnp.concatenate(
                [jnp.broadcast_to(rows[c:c + 1, :], (SEL_BLOCK, tq)) for c in range(per_tile)], axis=0)
            mask = (blk > 0.5) & (kpos <= qpos)
            return _flash_update(s, mask, vsT_ref[0, n], *carry)
        _, l, acc = lax.fori_loop(0, jnp.minimum(n_sel_tiles, ks_ref.shape[2]), body, _flash_init(tq))
        o_s.append(_flash_out(l, acc))

    lo = jnp.maximum(t0 - (WINDOW - 1) - kw_base, 0) // KV_TILE
    hi = jnp.minimum((t0 + tq - 1 - kw_base) // KV_TILE + 1, kw_ref.shape[2])
    wrow = lax.broadcasted_iota(I32, (KV_TILE, tq), 0)
    o_w = []
    for r in range(rep):
        def body(n, carry, r=r):
            s = jnp.dot(kw_ref[0, 0, n], qs[r], preferred_element_type=F32)
            dist = qpos - (kw_base + n * KV_TILE + wrow)
            s = s - slopes[r] * dist.astype(F32)
            return _flash_update(s, (dist >= 0) & (dist < WINDOW), vwT_ref[0, n], *carry)
        _, l, acc = lax.fori_loop(lo, hi, body, _flash_init(tq))
        o_w.append(_flash_out(l, acc))

    for r in range(rep):
        gt = g_ref[3 * r:3 * r + 3, :]
        gt = 1.0 / (1.0 + jnp.exp(-gt))
        o_ref[r * HEAD_DIM:(r + 1) * HEAD_DIM, :] = (
            gt[0:1, :] * o_c[r] + gt[1:2, :] * o_s[r] + gt[2:3, :] * o_w[r])


def _taps_matrix(n_sel, n_cmp):
    m = np.zeros((n_sel, n_cmp), np.float32)
    for j in range(n_sel):
        for off, w in ((-1, 1.0), (0, 2.0), (1, 2.0), (2, 2.0), (3, 1.0)):
            c = 4 * j + off
            if 0 <= c < n_cmp:
                m[j, c] = w
    return jnp.asarray(m)


def _nsa(qT, q_row, gT, g_row, kc, vcT, ks, vsT, kw, vwT, nb, nq, tq, q_off, kw_base):
    rep = N_B_HEADS // N_KV
    rq = rep * HEAD_DIM
    sl = np.zeros((N_KV, 8, tq), np.float32)
    sl[:, :rep, :] = _alibi(12)[1::2].astype(np.float32).reshape(N_KV, rep, 1)
    nTs, nTw = ks.shape[2], kw.shape[2]
    n_cmp = kc.shape[2]
    n_sel = n_cmp // 4
    return pl.pallas_call(
        functools.partial(_nsa_kernel, tq=tq, q_off=q_off, kw_base=kw_base), grid=(nb, N_KV, nq),
        in_specs=[pl.BlockSpec((rq, tq), lambda b, g, i: (q_row // rq + g, b * nq + i)),
                  pl.BlockSpec((16, tq), lambda b, g, i: (g_row // 16 + g, b * nq + i)),
                  pl.BlockSpec((1, 1, n_cmp, HEAD_DIM), lambda b, g, i: (b, g, 0, 0)),
                  pl.BlockSpec((1, 1, HEAD_DIM, n_cmp), lambda b, g, i: (b, g, 0, 0)),
                  pl.BlockSpec((n_sel, n_cmp), lambda b, g, i: (0, 0)),
                  pl.BlockSpec((1, 1, nTs, SEL_TILE, HEAD_DIM), lambda b, g, i: (b, g, 0, 0, 0)),
                  pl.BlockSpec((1, nTs, HEAD_DIM, SEL_TILE), lambda b, g, i: (b, 0, g, 0)),
                  pl.BlockSpec((1, 1, nTw, KV_TILE, HEAD_DIM), lambda b, g, i: (b, g, 0, 0, 0)),
                  pl.BlockSpec((1, nTw, HEAD_DIM, KV_TILE), lambda b, g, i: (b, 0, g, 0)),
                  pl.BlockSpec((1, 8, tq), lambda b, g, i: (g, 0, 0))],
        out_specs=pl.BlockSpec((rq, tq), lambda b, g, i: (g, b * nq + i)),
        out_shape=jax.ShapeDtypeStruct((N_B_HEADS * HEAD_DIM, nb * nq * tq), F32),
        scratch_shapes=[pltpu.VMEM((n_sel + SEL_TILE // SEL_BLOCK, tq), F32)],
        compiler_params=_cparams(("parallel", "parallel", "arbitrary")), name="nsa")(
            qT, gT, kc, vcT, _taps_matrix(n_sel, n_cmp), ks, vsT, kw, vwT, jnp.asarray(sl))


def _dsa_kernel(qi_ref, wi_ref, ki_ref, q_ref, k_ref, vT_ref, slope_ref, o_ref, key_ref, thr_ref, jb_ref,
                *, tq, q_off, topk):
    i, h = pl.program_id(1), pl.program_id(2)
    t0 = q_off + i * tq
    n_kv = (t0 + tq - 1) // KV_TILE + 1
    qpos = t0 + lax.broadcasted_iota(I32, (1, tq), 1)
    krow = lax.broadcasted_iota(I32, (KV_TILE, tq), 0)

    def count(pred):
        def body(n, c):
            return c + jnp.sum(jnp.where(pred(key_ref[n], n), 1.0, 0.0), axis=0, keepdims=True)
        return lax.fori_loop(0, n_kv, body, jnp.zeros((1, tq), F32))

    @pl.when(h == 0)
    def _():
        w = wi_ref[...] * (IDX_SCALE * N_IDX_HEADS ** -0.5)
        qi = [qi_ref[a * IDX_DIM:(a + 1) * IDX_DIM, :].astype(BF16) for a in range(N_IDX_HEADS)]

        def score_body(n, _):
            ki = ki_ref[0, n]
            acc = jnp.zeros((KV_TILE, tq), F32)
            for a in range(N_IDX_HEADS):
                rel = jnp.dot(ki, qi[a], preferred_element_type=F32)
                acc = acc + jnp.maximum(rel, 0.0) * w[a:a + 1, :]
            sc = jnp.where(n * KV_TILE + krow <= qpos, acc, NEG)
            bits = lax.bitcast_convert_type(sc, I32)
            key_ref[n] = bits ^ ((bits >> 31) & 0x7FFFFFFF)
            return 0
        lax.fori_loop(0, n_kv, score_body, 0)

        kf = float(topk)
        c0 = count(lambda key, n: key >= 0)
        thr = jnp.where(c0 >= kf, 0, INT_MIN).astype(I32)

        def bit_body(j, thr):
            cand = thr | (1 << (30 - j))
            c = count(lambda key, n: key >= cand)
            return jnp.where(c >= kf, cand, thr)
        thr = lax.fori_loop(0, 31, bit_body, thr)
        thr_ref[...] = thr
        need = kf - count(lambda key, n: key > thr)

        def idx_body(j, jb):
            cand = jb | (1 << (14 - j))
            c = count(lambda key, n: (key == thr) & (n * KV_TILE + krow < cand))
            return jnp.where(c <= need, cand, jb)
        jb_ref[...] = lax.fori_loop(0, 15, idx_body, jnp.zeros((1, tq), I32))

    thr = thr_ref[...]
    jb = jb_ref[...]
    g = h // (N_C_HEADS // N_KV)
    q = (q_ref[...] * ATTN_SCALE).astype(BF16)
    slope = slope_ref[0]

    def body(n, carry):
        key = key_ref[n]
        kpos = n * KV_TILE + krow
        mask = ((key > thr) | ((key == thr) & (kpos < jb))) & (kpos <= qpos)
        s = jnp.dot(k_ref[0, g, n], q, preferred_element_type=F32)
        s = s - slope * (qpos - kpos).astype(F32)
        vT = vT_ref[0, n, pl.ds(pl.multiple_of(g * HEAD_DIM, HEAD_DIM), HEAD_DIM), :]
        return _flash_update(s, mask, vT, *carry)
    _, l, acc = lax.fori_loop(0, n_kv, body, _flash_init(tq))
    o_ref[...] = _flash_out(l, acc)


def _dsa(hT, rows, ki, k, vT, nb, nq, tq, q_off, topk):
    nT = k.shape[2]
    slopes = jnp.asarray(np.broadcast_to(_alibi(N_C_HEADS).astype(np.float32)[:, None, None], (N_C_HEADS, 1, tq)))
    qi_b, wi_b = rows['c_qi'] // (N_IDX_HEADS * IDX_DIM), rows['c_wi'] // N_IDX_HEADS
    return pl.pallas_call(
        functools.partial(_dsa_kernel, tq=tq, q_off=q_off, topk=topk), grid=(nb, nq, N_C_HEADS),
        in_specs=[pl.BlockSpec((N_IDX_HEADS * IDX_DIM, tq), lambda b, i, h: (qi_b, b * nq + i)),
                  pl.BlockSpec((N_IDX_HEADS, tq), lambda b, i, h: (wi_b, b * nq + i)),
                  pl.BlockSpec((1, nT, KV_TILE, IDX_DIM), lambda b, i, h: (b, 0, 0, 0)),
                  pl.BlockSpec((HEAD_DIM, tq), lambda b, i, h: (h, b * nq + i)),
                  pl.BlockSpec((1, N_KV, nT, KV_TILE, HEAD_DIM), lambda b, i, h: (b, 0, 0, 0, 0)),
                  pl.BlockSpec((1, nT, N_KV * HEAD_DIM, KV_TILE), lambda b, i, h: (b, 0, 0, 0)),
                  pl.BlockSpec((1, 1, tq), lambda b, i, h: (h, 0, 0))],
        out_specs=pl.BlockSpec((HEAD_DIM, tq), lambda b, i, h: (h, b * nq + i)),
        out_shape=jax.ShapeDtypeStruct((N_C_HEADS * HEAD_DIM, nb * nq * tq), F32),
        scratch_shapes=[pltpu.VMEM((nT, KV_TILE, tq), I32), pltpu.VMEM((1, tq), I32), pltpu.VMEM((1, tq), I32)],
        compiler_params=_cparams(("parallel", "parallel", "arbitrary")), name="dsa")(
            hT, hT, ki, hT, k, vT, slopes)


def _mem_kernel(q_ref, mk_ref, mvT_ref, o_ref):
    q = (q_ref[...] * ATTN_SCALE).astype(BF16)
    s = jnp.dot(mk_ref[0, 0].astype(BF16), q, preferred_element_type=F32)
    p = jnp.exp(s - jnp.max(s, axis=0, keepdims=True))
    l = jnp.sum(p, axis=0, keepdims=True)
    o_ref[...] = jnp.dot(mvT_ref[0, 0].astype(BF16), p.astype(BF16), preferred_element_type=F32) / l


def _mem(hT, q_row, mk, mvT, nb, nq, tq):
    n_mem = mk.shape[2]
    qb = q_row // HEAD_DIM
    return pl.pallas_call(
        _mem_kernel, grid=(nb, nq, N_MEM_HEADS),
        in_specs=[pl.BlockSpec((HEAD_DIM, tq), lambda b, i, h: (qb + h, b * nq + i)),
                  pl.BlockSpec((1, 1, n_mem, HEAD_DIM), lambda b, i, h: (b, h, 0, 0)),
                  pl.BlockSpec((1, 1, HEAD_DIM, n_mem), lambda b, i, h: (b, h, 0, 0))],
        out_specs=pl.BlockSpec((HEAD_DIM, tq), lambda b, i, h: (h, b * nq + i)),
        out_shape=jax.ShapeDtypeStruct((N_MEM_HEADS * HEAD_DIM, nb * nq * tq), F32),
        compiler_params=_cparams(("parallel", "parallel", "parallel")), name="mem")(hT, mk, mvT)


def _even_wT(w_in):
    sizes = (384, 128, 128, 384, 384, 128, 128, 128, 128, 128, 128, 18, 384, 256, 256)
    offs = np.cumsum((0,) + sizes)
    (a_q, a_k, a_v, a_z, b_q, b_kc, b_vc, b_ks, b_vs, b_kw, b_vw, b_g, b_z, m_q, m_z) = [
        w_in[:, offs[n]:offs[n + 1]] for n in range(len(sizes))]
    d = w_in.shape[0]
    bg = jnp.zeros((d, N_KV, 16), w_in.dtype).at[:, :, :9].set(b_g.reshape(d, N_KV, 9)).reshape(d, 32)
    cols = [a_q, b_q, m_q, a_z, b_z, m_z, a_k, a_v, b_kc, b_vc, b_ks, b_vs, b_kw, b_vw, bg]
    w = jnp.concatenate(cols, axis=1)
    w = jnp.pad(w, ((0, 0), (0, EVEN_TOTAL - w.shape[1])))
    return w.T.astype(BF16)


def _odd_wT(w_in):
    sizes = (768, 128, 128, 512, 64, 8, 768, 256, 256)
    offs = np.cumsum((0,) + sizes)
    (c_q, c_k, c_v, c_qi, c_ki, c_wi, c_z, m_q, m_z) = [w_in[:, offs[n]:offs[n + 1]] for n in range(len(sizes))]
    w = jnp.concatenate([c_q, m_q, c_z, m_z, c_qi, c_k, c_v, c_ki, c_wi], axis=1)
    w = jnp.pad(w, ((0, 0), (0, ODD_TOTAL - w.shape[1])))
    return w.T.astype(BF16)


class _Group:
    def __init__(self, nb, t, tq, q_off):
        self.nb, self.t, self.tq, self.q_off = nb, t, tq, q_off
        self.nq = -(-t // tq)
        self.cols = nb * self.nq * tq
        self.sample = q_off > 0

    def q_cols(self, hT, row, n):
        if not self.sample:
            return None
        x = hT[row:row + n, :self.nb]
        return jnp.zeros((n, self.nb, self.tq), F32).at[:, :, 0].set(x).reshape(n, self.cols)

    def tok(self, hT, row, n):
        if self.sample:
            return hT[row:row + n, :self.nb].T.reshape(self.nb, 1, n)
        return hT[row:row + n].reshape(n, self.nb, self.t).transpose(1, 2, 0)

    def from_cols(self, oT):
        if not self.sample:
            return oT
        n = oT.shape[0]
        x = oT.reshape(n, self.nb, self.tq)[:, :, 0]
        return jnp.pad(x, ((0, 0), (0, 128 - self.nb)))


def _full_keys(grp, new_tok, pool, page_table):
    if not grp.sample:
        return new_tok
    nb, _, c = new_tok.shape
    page_tokens = pool.shape[1]
    rows_per_page = page_tokens * c // 128
    tail = jnp.zeros((nb, PAGES_PER_STEP * page_tokens, c), F32).at[:, 0:1, :].set(new_tok)
    tail = tail.reshape(nb, PAGES_PER_STEP * rows_per_page, 128)
    out = _gather(pool.reshape(pool.shape[0], rows_per_page, 128), page_table, tail)
    return out.reshape(nb, -1, c)


def _split_rows(hT, grp, row, n):
    if grp.sample:
        return grp.q_cols(hT, row, n), 0
    return hT, row


def _even_layer(grp, xT, mk, mvT, past, page_table, wT, woT, g, b, cmpw, alpha):
    R = EVEN_ROWS
    nb, nq, tq = grp.nb, grp.nq, grp.tq
    hT = _mm(wT, xT, 640, min(1024, xT.shape[1]))
    names = ('a_k', 'a_v', 'b_kc', 'b_vc', 'b_ks', 'b_vs')
    new = [grp.tok(hT, R[nm], 128) for nm in names]
    kw_new, vw_new = grp.tok(hT, R['b_kw'], 128), grp.tok(hT, R['b_vw'], 128)
    if grp.sample:
        full = [_full_keys(grp, n, p, page_table) for n, p in zip(new, past[:6])]
        kw = jnp.concatenate([past[6].reshape(nb, -1, 128), kw_new], axis=1)
        vw = jnp.concatenate([past[7].reshape(nb, -1, 128), vw_new], axis=1)
        n_win = past[6].shape[1]
        kw_base = grp.q_off - n_win
        win_out = (kw[:, -n_win:], vw[:, -n_win:])
        pad = -kw.shape[1] % KV_TILE
        kw = jnp.pad(kw, ((0, 0), (0, pad), (0, 0)))
        vw = jnp.pad(vw, ((0, 0), (0, pad), (0, 0)))
    else:
        full = new
        kw, vw, kw_base = kw_new, vw_new, 0
        n_win = min(WINDOW, grp.t)
        win_out = (kw[:, -n_win:], vw[:, -n_win:])

    ak, amean = _kprep(full[0])
    amean = amean.reshape(nb, -1, N_KV, HEAD_DIM).transpose(0, 2, 1, 3)
    a_q = grp.q_cols(hT, R['a_q'], 384) if grp.sample else hT
    o_a = _moba(a_q, amean, ak, _vT_tiles(full[1], KV_TILE), nb, nq, tq, grp.q_off)

    pe2k, w1k, w2k = cmpw[0]
    pe2v, w1v, w2v = cmpw[1]
    n_cmp = max(grp.q_off, grp.t) // CMP_STRIDE
    kc = _compress(full[2].reshape(nb, -1, 2048), pe2k, w1k, w2k, n_cmp)
    vc = _compress(full[3].reshape(nb, -1, 2048), pe2v, w1v, w2v, n_cmp)
    kc = kc.reshape(nb, n_cmp, N_KV, HEAD_DIM).transpose(0, 2, 1, 3).astype(BF16)
    vcT = vc.reshape(nb, n_cmp, N_KV, HEAD_DIM).transpose(0, 2, 3, 1).astype(BF16)
    ks = full[4].reshape(nb, -1, SEL_TILE, N_KV, HEAD_DIM).transpose(0, 3, 1, 2, 4).astype(BF16)
    kwt = kw.reshape(nb, -1, KV_TILE, N_KV, HEAD_DIM).transpose(0, 3, 1, 2, 4).astype(BF16)
    if grp.sample:
        b_q, bq_row = grp.q_cols(hT, R['b_q'], 384), 0
        b_g, bg_row = grp.q_cols(hT, R['b_g'], 32), 0
    else:
        b_q, bq_row, b_g, bg_row = hT, R['b_q'], hT, R['b_g']
    o_b = _nsa(b_q, bq_row, b_g, bg_row, kc, vcT, ks, _vT_tiles(full[5], SEL_TILE), kwt, _vT_tiles(vw, KV_TILE),
               nb, nq, tq, grp.q_off, kw_base)

    m_q, mq_row = _split_rows(hT, grp, R['m_q'], 256)
    o_m = _mem(m_q, mq_row, mk, mvT, nb, nq, tq)

    parts = [grp.from_cols(o) for o in (o_a, o_b, o_m)]
    yT = _outln(parts, hT, R['z'], xT, woT, g, b, alpha, min(512, xT.shape[1]))
    state = [n.reshape(nb, -1, N_KV, HEAD_DIM) for n in new]
    state += [w.reshape(nb, -1, N_KV, HEAD_DIM) for w in win_out]
    return yT, state


def _odd_layer(grp, xT, mk, mvT, past, page_table, wT, woT, g, b, alpha):
    R = ODD_ROWS
    nb, nq, tq = grp.nb, grp.nq, grp.tq
    hT = _mm(wT, xT, 768, min(1024, xT.shape[1]))
    new = [grp.tok(hT, R['c_k'], 128), grp.tok(hT, R['c_v'], 128), grp.tok(hT, R['c_ki'], 64)]
    if grp.sample:
        full = [_full_keys(grp, n, p, page_table) for n, p in zip(new, past)]
    else:
        full = new
    n_keys = grp.q_off + grp.t
    topk = min(DSA_TOPK, n_keys // 4)
    ck, _ = _kprep(full[0])
    ki = full[2].reshape(nb, -1, KV_TILE, IDX_DIM).astype(BF16)
    if grp.sample:
        qi = grp.q_cols(hT, R['c_qi'], 512)
        wi = grp.q_cols(hT, R['c_wi'], 8)
        cq = grp.q_cols(hT, R['c_q'], 768)
        src = jnp.concatenate([cq, jnp.zeros((256, cq.shape[1]), F32), qi, wi], axis=0)
        rows = dict(c_qi=1024, c_wi=1536)
    else:
        src, rows = hT, R
    o_c = _dsa(src, rows, ki, ck, _vT_tiles(full[1], KV_TILE), nb, nq, tq, grp.q_off, topk)
    m_q, mq_row = _split_rows(hT, grp, R['m_q'], 256)
    o_m = _mem(m_q, mq_row, mk, mvT, nb, nq, tq)
    parts = [grp.from_cols(o) for o in (o_c, o_m)]
    yT = _outln(parts, hT, R['z'], xT, woT, g, b, alpha, min(512, xT.shape[1]))
    state = [new[0].reshape(nb, -1, N_KV, HEAD_DIM), new[1].reshape(nb, -1, N_KV, HEAD_DIM), new[2]]
    return yT, state


def kernel(x_prompt, x_sample, cache_a_k, cache_a_v, cache_b_cmp_k, cache_b_cmp_v, cache_b_sel_k, cache_b_sel_v,
           state_b_win_k, state_b_win_v, cache_c_k, cache_c_v, cache_c_idx_k, cache_mem_k, cache_mem_v,
           page_table, mem_prompt, w_in_even, w_out_even, w_in_odd, w_out_odd, w_mem_kv, ln_g, ln_b,
           cmp_pe, cmp_w1, cmp_w2):
    bp, seq, d = x_prompt.shape
    bs = x_sample.shape[0]
    depth = w_mem_kv.shape[0]
    n_mem = mem_prompt.shape[1]
    past_len = page_table.shape[1] * cache_a_k.shape[2]
    alpha = (2.0 * depth) ** 0.25
    assert x_sample.shape[1] == 1 and bs <= 128 and seq % KV_TILE == 0 and past_len % 1024 == 0

    gp = _Group(bp, seq, KV_TILE, 0)
    gs = _Group(bs, 1, 128, past_len)
    xp = x_prompt.reshape(bp * seq, d).T
    xs = jnp.pad(x_sample.reshape(bs, d).T, ((0, 0), (0, 128 - bs)))
    even_paged = (cache_a_k, cache_a_v, cache_b_cmp_k, cache_b_cmp_v, cache_b_sel_k, cache_b_sel_v)
    odd_paged = (cache_c_k, cache_c_v, cache_c_idx_k)
    ev_p, ev_s, od_p, od_s, mk_list, mv_list = [], [], [], [], [], []
    for layer in range(depth):
        j = layer // 2
        mem_kv = _mm(mem_prompt.reshape(bp * n_mem, d), w_mem_kv[layer], bp * n_mem, 2 * N_MEM_HEADS * HEAD_DIM)
        mem_kv = mem_kv.reshape(bp, n_mem, 2, N_MEM_HEADS, HEAD_DIM)
        mk_p, mv_p = mem_kv[:, :, 0], mem_kv[:, :, 1]
        mk_list.append(mk_p)
        mv_list.append(mv_p)
        mem_p = (mk_p.transpose(0, 2, 1, 3), mv_p.transpose(0, 2, 3, 1))
        mem_s = (cache_mem_k[layer].transpose(0, 2, 1, 3), cache_mem_v[layer].transpose(0, 2, 3, 1))
        g = ln_g[layer].reshape(d, 1)
        b = ln_b[layer].reshape(d, 1)
        if layer % 2 == 0:
            wT = _even_wT(w_in_even[j])
            woT = w_out_even[j].T.astype(BF16)
            cmpw = [_cmp_weights(cmp_pe[j, c], cmp_w1[j, c], cmp_w2[j, c]) for c in range(2)]
            past = tuple(c[j] for c in even_paged) + (state_b_win_k[j], state_b_win_v[j])
            xp, st_p = _even_layer(gp, xp, *mem_p, None, None, wT, woT, g, b, cmpw, alpha)
            xs, st_s = _even_layer(gs, xs, *mem_s, past, page_table, wT, woT, g, b, cmpw, alpha)
            ev_p.append(st_p)
            ev_s.append(st_s)
        else:
            wT = _odd_wT(w_in_odd[j])
            woT = w_out_odd[j].T.astype(BF16)
            past = tuple(c[j] for c in odd_paged)
            xp, st_p = _odd_layer(gp, xp, *mem_p, None, None, wT, woT, g, b, alpha)
            xs, st_s = _odd_layer(gs, xs, *mem_s, past, page_table, wT, woT, g, b, alpha)
            od_p.append(st_p)
            od_s.append(st_s)
    y_p = xp.T.reshape(bp, seq, d)
    y_s = xs[:, :bs].T.reshape(bs, 1, d)
    ev_p = [jnp.stack(l) for l in zip(*ev_p)]
    ev_s = [jnp.stack(l) for l in zip(*ev_s)]
    od_p = [jnp.stack(l) for l in zip(*od_p)]
    od_s = [jnp.stack(l) for l in zip(*od_s)]
    out = [y_p, y_s]
    for p, s in zip(ev_p, ev_s):
        out += [p, s]
    for p, s in zip(od_p, od_s):
        out += [p, s]
    out += [jnp.stack(mk_list), jnp.stack(mv_list)]
    return tuple(out)
```

```python
import functools

import numpy as np
import jax
import jax.numpy as jnp
from jax import lax
from jax.experimental import pallas as pl
from jax.experimental.pallas import tpu as pltpu

F32 = jnp.float32
BF16 = jnp.bfloat16
I32 = jnp.int32

HEAD_DIM = 64
N_MEM_HEADS = 4
N_A_HEADS = 6
N_B_HEADS = 6
N_C_HEADS = 12
N_KV = 2
N_IDX_HEADS = 8
IDX_DIM = 64
MOBA_BLOCK = 256
MOBA_TOPK = 3
CMP_STRIDE = 16
CMP_LEN = 32
CMP_HIDDEN = 128
SEL_BLOCK = 64
SEL_TOPK = 15
WINDOW = 512
DSA_TOPK = 256
ATTN_SCALE = HEAD_DIM ** -0.5
IDX_SCALE = IDX_DIM ** -0.5
LN_EPS = 1e-5
LOG2E = 1.4426950408889634
NEG = -1e30
SENTINEL = -3e38
INT_MIN = -(2 ** 31)

KV_TILE = 256
SEL_TILE = 512
KD = 256
FEAT_POS = 128
FEAT_SEL = 144
COUNT_GROUP = 4
HEADS_PER_STEP = 3
SAMPLE_COLS = 128
PAGES_PER_STEP = 8
VMEM_LIMIT = 56 * 1024 * 1024

EVEN_ROWS = dict(a_q=0, b_q=384, m_q=768, z=1024, a_k=2048, a_v=2176, b_kc=2304, b_vc=2432,
                 b_ks=2560, b_vs=2688, b_kw=2816, b_vw=2944, b_g=3072)
EVEN_TOTAL = 3200
ODD_ROWS = dict(c_q=0, m_q=768, z=1024, c_qi=2048, c_k=2560, c_v=2688, c_ki=2816, c_wi=2880)
ODD_TOTAL = 3072


def _cparams(sem):
    return pltpu.CompilerParams(dimension_semantics=sem, vmem_limit_bytes=VMEM_LIMIT)


def _alibi(n):
    return 2.0 ** (-8.0 * np.arange(1, n + 1) / n)


def _mm_kernel(a_ref, b_ref, o_ref):
    o_ref[...] = jnp.dot(a_ref[...].astype(BF16), b_ref[...].astype(BF16), preferred_element_type=F32)


def _mm(a, b, bm, bn):
    m, k = a.shape
    n = b.shape[1]
    return pl.pallas_call(
        _mm_kernel, grid=(n // bn, m // bm),
        in_specs=[pl.BlockSpec((bm, k), lambda j, i: (i, 0)), pl.BlockSpec((k, bn), lambda j, i: (0, j))],
        out_specs=pl.BlockSpec((bm, bn), lambda j, i: (i, j)),
        out_shape=jax.ShapeDtypeStruct((m, n), F32),
        compiler_params=_cparams(("parallel", "parallel")), name="mm")(a, b)


def _outln_kernel(*refs, n_parts, alpha):
    o_refs = refs[:n_parts]
    z_ref, x_ref, w_ref, g_ref, b_ref, y_ref = refs[n_parts:]
    o = jnp.concatenate([r[...] for r in o_refs], axis=0)
    z = z_ref[...]
    mixed = (o * (z / (1.0 + jnp.exp(-z)))).astype(BF16)
    y = alpha * x_ref[...] + jnp.dot(w_ref[...], mixed, preferred_element_type=F32)
    mu = jnp.mean(y, axis=0, keepdims=True)
    d = y - mu
    var = jnp.mean(d * d, axis=0, keepdims=True)
    y_ref[...] = d * lax.rsqrt(var + LN_EPS) * g_ref[...] + b_ref[...]


def _outln(o_parts, hT, z_row, xT, woT, g, b, alpha, bn):
    d, n = xT.shape
    zb = z_row // d
    in_specs = [pl.BlockSpec((p.shape[0], bn), lambda j: (0, j)) for p in o_parts]
    in_specs += [pl.BlockSpec((d, bn), lambda j: (zb, j)),
                 pl.BlockSpec((d, bn), lambda j: (0, j)),
                 pl.BlockSpec((d, d), lambda j: (0, 0)),
                 pl.BlockSpec((d, 1), lambda j: (0, 0)),
                 pl.BlockSpec((d, 1), lambda j: (0, 0))]
    return pl.pallas_call(
        functools.partial(_outln_kernel, n_parts=len(o_parts), alpha=alpha), grid=(n // bn,),
        in_specs=in_specs, out_specs=pl.BlockSpec((d, bn), lambda j: (0, j)),
        out_shape=jax.ShapeDtypeStruct((d, n), F32),
        compiler_params=_cparams(("parallel",)), name="outln")(*o_parts, hT, xT, woT, g, b)


def _gather_kernel(pt_ref, *refs, npg, rows, last):
    pool_refs = refs[:npg]
    tail_ref, o_ref = refs[npg], refs[npg + 1]
    j = pl.program_id(1)

    @pl.when(j < last)
    def _():
        for k in range(npg):
            o_ref[0, k * rows:(k + 1) * rows, :] = pool_refs[k][0]

    @pl.when(j == last)
    def _():
        o_ref[0] = tail_ref[0]


def _gather(pool, page_table, tail):
    nb, n_pages = page_table.shape
    rows = pool.shape[1]
    npg = PAGES_PER_STEP
    last = n_pages // npg

    def pool_spec(k):
        return pl.BlockSpec((1, rows, 128),
                            lambda b, j, pt: (pt[b, jnp.minimum(j * npg + k, n_pages - 1)], 0, 0))

    grid_spec = pltpu.PrefetchScalarGridSpec(
        num_scalar_prefetch=1, grid=(nb, last + 1),
        in_specs=[pool_spec(k) for k in range(npg)] + [pl.BlockSpec((1, npg * rows, 128), lambda b, j, pt: (b, 0, 0))],
        out_specs=pl.BlockSpec((1, npg * rows, 128), lambda b, j, pt: (b, j, 0)))
    return pl.pallas_call(
        functools.partial(_gather_kernel, npg=npg, rows=rows, last=last), grid_spec=grid_spec,
        out_shape=jax.ShapeDtypeStruct((nb, (last + 1) * npg * rows, 128), F32),
        compiler_params=_cparams(("parallel", "arbitrary")), name="gather")(page_table, *([pool] * npg), tail)


def _kprep_kernel(x_ref, k_ref, mean_ref, *, tile):
    x = x_ref[0]
    nt = x.shape[0] // tile
    r = lax.broadcasted_iota(I32, (tile, 128), 0)
    c = lax.broadcasted_iota(I32, (tile, 128), 1)
    sel_c = c - (FEAT_SEL - FEAT_POS)
    feat = jnp.where(c < 3, r >> 4, jnp.where(c < 6, r & 15, jnp.where(sel_c == (r >> 6), 1, 0)))
    feat = feat.astype(F32).astype(BF16)
    for t in range(nt):
        k_ref[0, t] = jnp.concatenate([x[t * tile:(t + 1) * tile].astype(BF16), feat], axis=1)
    nblk = x.shape[0] // MOBA_BLOCK
    mean_ref[0, 0] = jnp.mean(x.reshape(nblk, MOBA_BLOCK, 128), axis=1)


def _kprep(x, tile):
    nb, L, _ = x.shape
    rows = 1024
    steps = L // rows
    k, mean = pl.pallas_call(
        functools.partial(_kprep_kernel, tile=tile), grid=(nb, steps),
        in_specs=[pl.BlockSpec((1, rows, 128), lambda b, j: (b, j, 0))],
        out_specs=[pl.BlockSpec((1, rows // tile, tile, KD), lambda b, j: (b, j, 0, 0)),
                   pl.BlockSpec((1, 1, rows // MOBA_BLOCK, 128), lambda b, j: (b, j, 0, 0))],
        out_shape=[jax.ShapeDtypeStruct((nb, L // tile, tile, KD), BF16),
                   jax.ShapeDtypeStruct((nb, steps, rows // MOBA_BLOCK, 128), F32)],
        compiler_params=_cparams(("parallel", "parallel")), name="kprep")(x)
    return k, mean.reshape(nb, L // MOBA_BLOCK, 128)


def _vT_tiles(v, tile):
    nb, L, c = v.shape
    return v.reshape(nb, L // tile, tile, c).transpose(0, 1, 3, 2).astype(BF16)


class _Cols:
    def __init__(self, nb, t, q_off):
        self.nb, self.t, self.q_off = nb, t, q_off
        self.sample = q_off > 0
        self.tq = 1 if self.sample else KV_TILE
        self.nq = 1 if self.sample else t // KV_TILE
        self.span = 1 if self.sample else KV_TILE

    def n_hc(self, heads):
        return 1 if self.sample else heads // HEADS_PER_STEP

    def step_cols(self):
        return SAMPLE_COLS if self.sample else HEADS_PER_STEP * self.tq

    def head_group(self, heads):
        return np.arange(heads) // (heads // N_KV)

    def per_head(self, vals, heads):
        vals = jnp.asarray(vals, F32)
        k = vals.shape[0]
        if self.sample:
            return jnp.tile(jnp.pad(vals, ((0, 0), (0, SAMPLE_COLS - heads))), (1, self.nb))
        n_hc = heads // HEADS_PER_STEP
        shape = (k, self.nb, self.nq, n_hc, HEADS_PER_STEP, self.tq)
        return jnp.broadcast_to(vals.reshape(k, 1, 1, n_hc, HEADS_PER_STEP, 1), shape).reshape(k, -1)

    def per_token(self, rows, heads):
        nb, nq, tq = self.nb, self.nq, self.tq
        k = rows.shape[0] // heads
        if self.sample:
            x = rows[:, :nb].reshape(heads, k, nb).transpose(1, 2, 0)
            return jnp.pad(x, ((0, 0), (0, 0), (0, SAMPLE_COLS - heads))).reshape(k, nb * SAMPLE_COLS)
        n_hc = heads // HEADS_PER_STEP
        x = rows.reshape(n_hc, HEADS_PER_STEP, k, nb, nq, tq).transpose(2, 3, 4, 0, 1, 5)
        return x.reshape(k, -1)

    def qpos_row(self, heads):
        if self.sample:
            return jnp.full((1, self.nb * SAMPLE_COLS), self.q_off, F32)
        pos = (jnp.arange(self.nq, dtype=F32)[:, None] * self.tq + jnp.arange(self.tq, dtype=F32)[None, :])
        shape = (1, self.nb, self.nq, heads // HEADS_PER_STEP, HEADS_PER_STEP, self.tq)
        return jnp.broadcast_to(pos.reshape(1, 1, self.nq, 1, 1, self.tq), shape).reshape(1, -1)

    def columns(self, rows, heads, slopes, extra=()):
        x = self.per_token(rows, heads)
        grp = self.head_group(heads)
        gsel = self.per_head(np.stack([grp == 0, grp == 1]).astype(np.float32), heads)
        qf = jnp.concatenate([x * gsel[0:1], x * gsel[1:2]], axis=0)
        s2 = np.asarray(slopes, np.float64) * LOG2E
        hi = s2.astype(jnp.bfloat16).astype(np.float64)
        mid = (s2 - hi).astype(jnp.bfloat16).astype(np.float64)
        lo = (s2 - hi - mid).astype(jnp.bfloat16).astype(np.float64)
        pieces = np.stack([16 * hi, 16 * mid, 16 * lo, hi, mid, lo] + [np.zeros_like(hi)] * 10)
        qa = jnp.concatenate([(qf * (ATTN_SCALE * LOG2E)).astype(BF16), self.per_head(pieces, heads).astype(BF16),
                              jnp.zeros((KD - FEAT_POS - 16, x.shape[1]), BF16)], axis=0)
        meta = [self.per_head((hi + mid + lo)[None, :], heads), self.qpos_row(heads)] + list(extra)
        meta.append(jnp.zeros((8 - sum(m.shape[0] for m in meta), x.shape[1]), F32))
        return qa, qf, jnp.concatenate(meta, axis=0)

    def collapse(self, o, heads):
        nb, nq, tq = self.nb, self.nq, self.tq
        if self.sample:
            o = o.reshape(N_KV, HEAD_DIM, nb, SAMPLE_COLS)
            grp = self.head_group(heads)
            x = jnp.stack([o[grp[h], :, :, h] for h in range(heads)])
            return jnp.pad(x.reshape(heads * HEAD_DIM, nb), ((0, 0), (0, 128 - nb)))
        n_hc = heads // HEADS_PER_STEP
        o = o.reshape(HEAD_DIM, nb, nq, n_hc, HEADS_PER_STEP, tq).transpose(3, 4, 0, 1, 2, 5)
        return o.reshape(heads * HEAD_DIM, nb * nq * tq)

    def tokens(self, rows):
        n = rows.shape[0]
        if self.sample:
            return rows[:, :self.nb].T.reshape(self.nb, 1, n)
        return rows.reshape(n, self.nb, self.t).transpose(1, 2, 0)

    def tile_cols(self, rows):
        if not self.sample:
            return rows
        n = rows.shape[0]
        return jnp.broadcast_to(rows[:, :self.nb, None], (n, self.nb, SAMPLE_COLS)).reshape(n, self.nb * SAMPLE_COLS)

    def from_tile_cols(self, o):
        if not self.sample:
            return o
        n = o.shape[0]
        return jnp.pad(o.reshape(n, self.nb, SAMPLE_COLS)[:, :, 0], ((0, 0), (0, 128 - self.nb)))


def _softmax_step(s, c, ok, vT, m, l, acc_ref):
    smax = jnp.max(s, axis=0, keepdims=True)
    live = smax > 0.5 * NEG
    if ok is not None:
        live = live & _row_flag(ok, s.shape[1])
    m_new = jnp.where(live, jnp.maximum(m, smax + c), m)
    shift = jnp.where(live, m_new - c, -NEG)
    p = jnp.exp2(s - shift)
    alpha = jnp.exp2(m - m_new)
    l = alpha * l + jnp.sum(p, axis=0, keepdims=True)
    acc_ref[...] = alpha * acc_ref[...] + jnp.dot(vT, p.astype(BF16), preferred_element_type=F32)
    return m_new, l


def _row_flag(ok, n):
    if ok.ndim == 0:
        return (jnp.zeros((1, n), I32) + ok.astype(I32)) > 0
    return ok


def _softmax_init(n):
    return jnp.full((1, n), NEG, F32), jnp.zeros((1, n), F32)


def _tile_pairs(n_tiles, step, carry):
    def body(u, carry):
        carry = step(2 * u, None, carry)
        t2 = 2 * u + 1
        return step(jnp.minimum(t2, n_tiles - 1), t2 < n_tiles, carry)
    return lax.fori_loop(0, (n_tiles + 1) // 2, body, carry)


def _value_rows(vT_ref, n, g, mixed):
    if mixed:
        return vT_ref[0, n]
    return vT_ref[0, n, pl.ds(pl.multiple_of(g * HEAD_DIM, HEAD_DIM), HEAD_DIM), :]


def _top_rows(vals, rowi, k):
    sel = jnp.zeros(vals.shape, F32)
    for _ in range(k):
        m = jnp.max(vals, axis=0, keepdims=True)
        first = jnp.min(jnp.where(vals == m, rowi, 1 << 20), axis=0, keepdims=True)
        pick = (rowi == first) & (m > SENTINEL)
        sel = jnp.where(pick, 1.0, sel)
        vals = jnp.where(pick, SENTINEL, vals)
    return sel


def _moba_kernel(qa_ref, qf_ref, meta_ref, kmean_ref, k_ref, vT_ref, o_ref, sel_ref, acc_ref, *, span, q_off, mixed):
    i, g = pl.program_id(1), pl.program_id(2)
    n = qa_ref.shape[1]
    own = (q_off + i * span) // MOBA_BLOCK
    slope = meta_ref[0:1, :]
    qpos = meta_ref[1:2, :].astype(I32)
    nrow = kmean_ref.shape[1]
    gate = jnp.dot(kmean_ref[0], qf_ref[...], precision=lax.Precision.HIGHEST, preferred_element_type=F32)
    rowi = lax.broadcasted_iota(I32, (nrow, n), 0)
    sel_ref[...] = _top_rows(jnp.where(rowi < own, gate, SENTINEL), rowi, MOBA_TOPK)
    qa = qa_ref[...]
    acc_ref[...] = jnp.zeros(acc_ref.shape, F32)

    def step(t, ok, carry):
        s = jnp.dot(k_ref[0, t], qa, preferred_element_type=F32)
        c = slope * (t * KV_TILE - qpos).astype(F32)
        sel = sel_ref[pl.ds(t, 1), :] > 0.5
        ok = sel if ok is None else sel & _row_flag(ok, n)
        return _softmax_step(s, c, ok, _value_rows(vT_ref, t, g, mixed), *carry, acc_ref)

    m, l = _tile_pairs(own, step, _softmax_init(n))
    s = jnp.dot(k_ref[0, own], qa, preferred_element_type=F32)
    krow = lax.broadcasted_iota(I32, (KV_TILE, n), 0)
    s = jnp.where(krow <= qpos - own * KV_TILE, s, NEG)
    c = slope * (own * KV_TILE - qpos).astype(F32)
    _, l = _softmax_step(s, c, None, _value_rows(vT_ref, own, g, mixed), m, l, acc_ref)
    o_ref[...] = acc_ref[...] / jnp.maximum(l, 1e-30)


def _moba(cols, qa, qf, meta, kmean, k, vT):
    nb, nq = cols.nb, cols.nq
    n_hc, n = cols.n_hc(N_A_HEADS), cols.step_cols()
    nT, nblk = k.shape[1], kmean.shape[1]
    vrows = N_KV * HEAD_DIM if cols.sample else HEAD_DIM
    col = lambda b, i, g: (0, (b * nq + i) * n_hc + g)
    return pl.pallas_call(
        functools.partial(_moba_kernel, span=cols.span, q_off=cols.q_off, mixed=cols.sample), grid=(nb, nq, n_hc),
        in_specs=[pl.BlockSpec((KD, n), col),
                  pl.BlockSpec((N_KV * HEAD_DIM, n), col),
                  pl.BlockSpec((8, n), col),
                  pl.BlockSpec((1, nblk, N_KV * HEAD_DIM), lambda b, i, g: (b, 0, 0)),
                  pl.BlockSpec((1, nT, KV_TILE, KD), lambda b, i, g: (b, 0, 0, 0)),
                  pl.BlockSpec((1, nT, N_KV * HEAD_DIM, KV_TILE), lambda b, i, g: (b, 0, 0, 0))],
        out_specs=pl.BlockSpec((vrows, n), col),
        out_shape=jax.ShapeDtypeStruct((vrows, qa.shape[1]), F32),
        scratch_shapes=[pltpu.VMEM((nblk, n), F32), pltpu.VMEM((vrows, n), F32)],
        compiler_params=_cparams(("parallel", "parallel", "arbitrary")), name="moba")(qa, qf, meta, kmean, k, vT)


def _cmp_kernel(r_ref, pe_ref, w1_ref, w2_ref, o_ref):
    r = r_ref[0]
    n = r.shape[0]
    hid = 2 * CMP_HIDDEN
    u = jnp.dot((r + pe_ref[0:1, :]).astype(BF16), w1_ref[:, 0:hid], preferred_element_type=F32)
    lo = jnp.dot((r + pe_ref[1:2, :]).astype(BF16), w1_ref[:, hid:2 * hid], preferred_element_type=F32)
    pre = u + pltpu.roll(lo, n - 1, 0)
    h = pre / (1.0 + jnp.exp(-pre))
    o_ref[0] = jnp.dot(h.astype(BF16), w2_ref[...], preferred_element_type=F32)


def _compress(raw, pe2, w1big, w2big, n_rows):
    nb = raw.shape[0]
    return pl.pallas_call(
        _cmp_kernel, grid=(nb,),
        in_specs=[pl.BlockSpec((1, n_rows, 2048), lambda b: (b, 0, 0)),
                  pl.BlockSpec((2, 2048), lambda b: (0, 0)),
                  pl.BlockSpec((2048, 4 * CMP_HIDDEN), lambda b: (0, 0)),
                  pl.BlockSpec((2 * CMP_HIDDEN, 128), lambda b: (0, 0))],
        out_specs=pl.BlockSpec((1, n_rows, 128), lambda b: (b, 0, 0)),
        out_shape=jax.ShapeDtypeStruct((nb, n_rows, 128), F32),
        compiler_params=_cparams(("parallel",)), name="compress")(raw, pe2, w1big, w2big)


def _cmp_weights(pe, w1, w2):
    w1r = w1.reshape(2, CMP_STRIDE, HEAD_DIM, CMP_HIDDEN)
    eye = jnp.eye(N_KV, dtype=w1.dtype)
    big = jnp.einsum('hpdj,ge->hpdgej', w1r, eye)
    big = big.transpose(1, 3, 2, 0, 4, 5).reshape(CMP_STRIDE * N_KV * HEAD_DIM, 2 * N_KV * CMP_HIDDEN)
    w2big = jnp.einsum('jd,ge->gjed', w2, eye).reshape(N_KV * CMP_HIDDEN, N_KV * HEAD_DIM)
    pe2 = jnp.broadcast_to(pe.reshape(2, CMP_STRIDE, 1, HEAD_DIM), (2, CMP_STRIDE, N_KV, HEAD_DIM)).reshape(2, 2048)
    return pe2, big.astype(BF16), w2big.astype(BF16)


def _nsa_kernel(qa_ref, meta_ref, kc_ref, vcT_ref, taps_ref, grp_ref, ks_ref, vsT_ref, kw_ref, vwT_ref, o_ref,
                sel_ref, qs_ref, acc_ref, *, span, q_off, kw_base, mixed):
    i, g = pl.program_id(1), pl.program_id(2)
    n = qa_ref.shape[1]
    nq = n if mixed else n // HEADS_PER_STEP
    rep = n // nq
    t0 = q_off + i * span
    slope = meta_ref[0:1, :]
    qpos = meta_ref[1:2, :].astype(I32)
    qa = qa_ref[...]
    n_cmp = kc_ref.shape[1]
    n_tab = sel_ref.shape[0]

    def vrows(ref):
        if mixed:
            return ref[0]
        return ref[0, pl.ds(pl.multiple_of(g * HEAD_DIM, HEAD_DIM), HEAD_DIM), :]

    cend = CMP_STRIDE * lax.broadcasted_iota(I32, (n_cmp, n), 0) + (CMP_LEN - 1)
    cmask = cend <= qpos
    s = jnp.dot(kc_ref[0], qa[0:N_KV * HEAD_DIM, :], preferred_element_type=F32) + slope * (cend - qpos).astype(F32)
    s = jnp.where(cmask, s, NEG)
    p = jnp.where(cmask, jnp.exp2(s - jnp.max(s, axis=0, keepdims=True)), 0.0)
    p = p * (1.0 / jnp.maximum(jnp.sum(p, axis=0, keepdims=True), 1e-30))
    o_c = jnp.dot(vrows(vcT_ref), p.astype(BF16), preferred_element_type=F32)

    if mixed:
        imp = jnp.dot(p, grp_ref[...], precision=lax.Precision.HIGHEST, preferred_element_type=F32)
    else:
        imp = p[:, 0:nq]
        for r in range(1, rep):
            imp = imp + p[:, r * nq:(r + 1) * nq]
    p_slc = jnp.dot(taps_ref[...], imp, precision=lax.Precision.HIGHEST, preferred_element_type=F32)
    p_slc = jnp.concatenate([p_slc, jnp.full((n_tab - p_slc.shape[0], nq), SENTINEL, F32)], axis=0)
    rowi = lax.broadcasted_iota(I32, (n_tab, nq), 0)
    own = qpos[:, 0:nq] >> 6
    sel = _top_rows(jnp.where(rowi < own, p_slc, SENTINEL), rowi, SEL_TOPK)
    sel_ref[...] = jnp.where(rowi == own, 1.0, sel)

    per_tile = SEL_TILE // SEL_BLOCK
    last = (t0 + span - 1) // SEL_TILE
    qs_ref[0] = qa
    qs_ref[1] = qa
    acc_ref[...] = jnp.zeros(acc_ref.shape, F32)

    def sel_scores(t, slot):
        rows = sel_ref[pl.ds(pl.multiple_of(t * per_tile, per_tile), per_tile), :]
        bias = jnp.where(rows > 0.5, 0.0, NEG)
        if rep > 1:
            bias = jnp.concatenate([bias] * rep, axis=1)
        bias = jnp.concatenate([bias, jnp.zeros((per_tile, n), F32)], axis=0)
        qs_ref[slot, FEAT_SEL:FEAT_SEL + 2 * per_tile, :] = bias.astype(BF16)
        s = jnp.dot(ks_ref[0, t], qs_ref[slot], preferred_element_type=F32)
        return s, slope * (t * SEL_TILE - qpos).astype(F32)

    def sel_step(t, ok, carry):
        s, c = sel_scores(t, 0 if ok is None else 1)
        return _softmax_step(s, c, ok, _value_rows(vsT_ref, t, g, mixed), *carry, acc_ref)

    m, l = _tile_pairs(last, sel_step, _softmax_init(n))
    s, c = sel_scores(last, 0)
    srow = lax.broadcasted_iota(I32, (SEL_TILE, n), 0)
    s = jnp.where(srow <= qpos - last * SEL_TILE, s, NEG)
    _, l = _softmax_step(s, c, None, _value_rows(vsT_ref, last, g, mixed), m, l, acc_ref)
    o_s = acc_ref[...] / jnp.maximum(l, 1e-30)

    lo = jnp.maximum(t0 - (WINDOW - 1) - kw_base, 0) // KV_TILE
    hi = jnp.minimum((t0 + span - 1 - kw_base) // KV_TILE + 1, kw_ref.shape[1])
    wrow = lax.broadcasted_iota(I32, (KV_TILE, n), 0)
    acc_ref[...] = jnp.zeros(acc_ref.shape, F32)

    def win_body(t, carry):
        s = jnp.dot(kw_ref[0, t], qa, preferred_element_type=F32)
        base = kw_base + t * KV_TILE
        dist = qpos - base - wrow
        s = jnp.where((dist >= 0) & (dist < WINDOW), s, NEG)
        return _softmax_step(s, slope * (base - qpos).astype(F32), None, _value_rows(vwT_ref, t, g, mixed),
                             *carry, acc_ref)

    _, l = lax.fori_loop(lo, hi, win_body, _softmax_init(n))
    o_w = acc_ref[...] / jnp.maximum(l, 1e-30)

    gt = 1.0 / (1.0 + jnp.exp(-meta_ref[2:5, :]))
    o_ref[...] = gt[0:1, :] * o_c + gt[1:2, :] * o_s + gt[2:3, :] * o_w


def _taps_matrix(n_sel, n_cmp):
    m = np.zeros((n_sel, n_cmp), np.float32)
    for j in range(n_sel):
        for off, w in ((-1, 1.0), (0, 2.0), (1, 2.0), (2, 2.0), (3, 1.0)):
            c = 4 * j + off
            if 0 <= c < n_cmp:
                m[j, c] = w
    return jnp.asarray(m)


def _group_matrix(cols):
    grp = cols.head_group(N_B_HEADS)
    m = np.zeros((SAMPLE_COLS, SAMPLE_COLS), np.float32)
    m[:N_B_HEADS, :N_B_HEADS] = grp[:, None] == grp[None, :]
    return jnp.asarray(m)


def _nsa(cols, qa, meta, kc, vcT, ks, vsT, kw, vwT, kw_base):
    nb, nq = cols.nb, cols.nq
    n_hc, n = cols.n_hc(N_B_HEADS), cols.step_cols()
    nTs, nTw = ks.shape[1], kw.shape[1]
    n_cmp = kc.shape[1]
    n_sel = n_cmp // 4
    n_tab = n_sel + SEL_TILE // SEL_BLOCK
    nsel_cols = n if cols.sample else n // HEADS_PER_STEP
    vrows = N_KV * HEAD_DIM if cols.sample else HEAD_DIM
    col = lambda b, i, g: (0, (b * nq + i) * n_hc + g)
    seq3 = lambda b, i, g: (b, 0, 0)
    seq4 = lambda b, i, g: (b, 0, 0, 0)
    return pl.pallas_call(
        functools.partial(_nsa_kernel, span=cols.span, q_off=cols.q_off, kw_base=kw_base, mixed=cols.sample),
        grid=(nb, nq, n_hc),
        in_specs=[pl.BlockSpec((KD, n), col),
                  pl.BlockSpec((8, n), col),
                  pl.BlockSpec((1, n_cmp, N_KV * HEAD_DIM), seq3),
                  pl.BlockSpec((1, N_KV * HEAD_DIM, n_cmp), seq3),
                  pl.BlockSpec((n_sel, n_cmp), lambda b, i, g: (0, 0)),
                  pl.BlockSpec((SAMPLE_COLS, SAMPLE_COLS), lambda b, i, g: (0, 0)),
                  pl.BlockSpec((1, nTs, SEL_TILE, KD), seq4),
                  pl.BlockSpec((1, nTs, N_KV * HEAD_DIM, SEL_TILE), seq4),
                  pl.BlockSpec((1, nTw, KV_TILE, KD), seq4),
                  pl.BlockSpec((1, nTw, N_KV * HEAD_DIM, KV_TILE), seq4)],
        out_specs=pl.BlockSpec((vrows, n), col),
        out_shape=jax.ShapeDtypeStruct((vrows, qa.shape[1]), F32),
        scratch_shapes=[pltpu.VMEM((n_tab, nsel_cols), F32), pltpu.VMEM((2, KD, n), BF16), pltpu.VMEM((vrows, n), F32)],
        compiler_params=_cparams(("parallel", "parallel", "arbitrary")), name="nsa")(
            qa, meta, kc, vcT, _taps_matrix(n_sel, n_cmp), _group_matrix(cols), ks, vsT, kw, vwT)


def _dsa_kernel(qi_ref, wi_ref, ki_ref, qa_ref, meta_ref, k_ref, vT_ref, o_ref, key_ref, acc_ref,
                *, span, q_off, topk, mixed, hc_per_group):
    i, hc = pl.program_id(1), pl.program_id(2)
    n = qa_ref.shape[1]
    nq = qi_ref.shape[1]
    rep = n // nq
    t0 = q_off + i * span
    n_kv = (t0 + span - 1) // KV_TILE + 1
    krow = lax.broadcasted_iota(I32, (KV_TILE, nq), 0)

    @pl.when(hc == 0)
    def _():
        qpos = t0 + (lax.broadcasted_iota(I32, (1, nq), 1) if span > 1 else jnp.zeros((1, nq), I32))
        w = wi_ref[...] * (IDX_SCALE * N_IDX_HEADS ** -0.5)
        qi = [qi_ref[a * IDX_DIM:(a + 1) * IDX_DIM, :].astype(BF16) for a in range(N_IDX_HEADS)]

        def score_body(t, _):
            ki = ki_ref[0, t]
            acc = jnp.zeros((KV_TILE, nq), F32)
            for a in range(N_IDX_HEADS):
                rel = jnp.dot(ki, qi[a], preferred_element_type=F32)
                acc = acc + jnp.maximum(rel, 0.0) * w[a:a + 1, :]
            sc = jnp.where(t * KV_TILE + krow <= qpos, acc, NEG)
            bits = lax.bitcast_convert_type(sc, I32)
            key_ref[t] = bits ^ ((bits >> 31) & 0x7FFFFFFF)
            return 0
        lax.fori_loop(0, n_kv, score_body, 0)
        n_grp = (n_kv + COUNT_GROUP - 1) // COUNT_GROUP

        def pad_body(t, _):
            key_ref[t] = jnp.full((KV_TILE, nq), INT_MIN, I32)
            return 0
        lax.fori_loop(n_kv, n_grp * COUNT_GROUP, pad_body, 0)

        def count(pred):
            def body(u, c):
                for j in range(COUNT_GROUP):
                    t = u * COUNT_GROUP + j
                    c = c + jnp.sum(jnp.where(pred(key_ref[t], t), 1.0, 0.0), axis=0, keepdims=True)
                return c
            return lax.fori_loop(0, n_grp, body, jnp.zeros((1, nq), F32))

        kf = float(topk)
        thr = jnp.where(count(lambda key, t: key >= 0) >= kf, 0, INT_MIN).astype(I32)

        def bit_body(j, thr):
            cand = thr | (1 << (30 - j))
            return jnp.where(count(lambda key, t: key >= cand) >= kf, cand, thr)
        thr = lax.fori_loop(0, 31, bit_body, thr)
        need = kf - count(lambda key, t: key > thr)

        def tie_search():
            def idx_body(j, jb):
                cand = jb | (1 << (14 - j))
                c = count(lambda key, t: (key == thr) & (t * KV_TILE + krow < cand))
                return jnp.where(c <= need, cand, jb)
            return lax.fori_loop(0, 15, idx_body, jnp.zeros((1, nq), I32))

        n_ge = count(lambda key, t: key >= thr)
        jb = lax.cond(jnp.max(n_ge) > kf, tie_search, lambda: jnp.full((1, nq), (1 << 15) - 1, I32))

        def bias_body(t, _):
            key = key_ref[t]
            kpos = t * KV_TILE + krow
            keep = ((key > thr) | ((key == thr) & (kpos < jb))) & (kpos <= qpos)
            key_ref[t] = lax.bitcast_convert_type(jnp.where(keep, 0.0, NEG), I32)
            return 0
        lax.fori_loop(0, n_kv, bias_body, 0)

    g = hc // hc_per_group
    slope = meta_ref[0:1, :]
    qpos_c = meta_ref[1:2, :].astype(I32)
    qa = qa_ref[...]
    acc_ref[...] = jnp.zeros(acc_ref.shape, F32)

    def step(t, ok, carry):
        bias = lax.bitcast_convert_type(key_ref[t], F32)
        if rep > 1:
            bias = jnp.concatenate([bias] * rep, axis=1)
        s = jnp.dot(k_ref[0, t], qa, preferred_element_type=F32) + bias
        c = slope * (t * KV_TILE - qpos_c).astype(F32)
        return _softmax_step(s, c, ok, _value_rows(vT_ref, t, g, mixed), *carry, acc_ref)
    _, l = _tile_pairs(n_kv, step, _softmax_init(n))
    o_ref[...] = acc_ref[...] / jnp.maximum(l, 1e-30)


def _dsa(cols, qi, qi_row, wi, wi_row, ki, qa, meta, k, vT, topk):
    nb, nq = cols.nb, cols.nq
    n_hc, n = cols.n_hc(N_C_HEADS), cols.step_cols()
    ncol = SAMPLE_COLS if cols.sample else cols.tq
    nT = k.shape[1]
    assert nT % COUNT_GROUP == 0
    vrows = N_KV * HEAD_DIM if cols.sample else HEAD_DIM
    qi_b, wi_b = qi_row // (N_IDX_HEADS * IDX_DIM), wi_row // N_IDX_HEADS
    col = lambda b, i, h: (0, (b * nq + i) * n_hc + h)
    seq4 = lambda b, i, h: (b, 0, 0, 0)
    return pl.pallas_call(
        functools.partial(_dsa_kernel, span=cols.span, q_off=cols.q_off, topk=topk, mixed=cols.sample,
                          hc_per_group=max(n_hc // N_KV, 1)),
        grid=(nb, nq, n_hc),
        in_specs=[pl.BlockSpec((N_IDX_HEADS * IDX_DIM, ncol), lambda b, i, h: (qi_b, b * nq + i)),
                  pl.BlockSpec((N_IDX_HEADS, ncol), lambda b, i, h: (wi_b, b * nq + i)),
                  pl.BlockSpec((1, nT, KV_TILE, IDX_DIM), seq4),
                  pl.BlockSpec((KD, n), col),
                  pl.BlockSpec((8, n), col),
                  pl.BlockSpec((1, nT, KV_TILE, KD), seq4),
                  pl.BlockSpec((1, nT, N_KV * HEAD_DIM, KV_TILE), seq4)],
        out_specs=pl.BlockSpec((vrows, n), col),
        out_shape=jax.ShapeDtypeStruct((vrows, qa.shape[1]), F32),
        scratch_shapes=[pltpu.VMEM((nT, KV_TILE, ncol), I32), pltpu.VMEM((vrows, n), F32)],
        compiler_params=_cparams(("parallel", "parallel", "arbitrary")), name="dsa")(
            qi, wi, ki, qa, meta, k, vT)


def _mem_kernel(q_ref, mk_ref, mvT_ref, o_ref):
    q = (q_ref[...] * ATTN_SCALE).astype(BF16)
    s = jnp.dot(mk_ref[0, 0].astype(BF16), q, preferred_element_type=F32)
    p = jnp.exp(s - jnp.max(s, axis=0, keepdims=True))
    l = jnp.sum(p, axis=0, keepdims=True)
    o_ref[...] = jnp.dot(mvT_ref[0, 0].astype(BF16), p.astype(BF16), preferred_element_type=F32) / l


def _mem(hT, q_row, mk, mvT, nb, nq, tq):
    n_mem = mk.shape[2]
    qb = q_row // HEAD_DIM
    return pl.pallas_call(
        _mem_kernel, grid=(nb, nq, N_MEM_HEADS),
        in_specs=[pl.BlockSpec((HEAD_DIM, tq), lambda b, i, h: (qb + h, b * nq + i)),
                  pl.BlockSpec((1, 1, n_mem, HEAD_DIM), lambda b, i, h: (b, h, 0, 0)),
                  pl.BlockSpec((1, 1, HEAD_DIM, n_mem), lambda b, i, h: (b, h, 0, 0))],
        out_specs=pl.BlockSpec((HEAD_DIM, tq), lambda b, i, h: (h, b * nq + i)),
        out_shape=jax.ShapeDtypeStruct((N_MEM_HEADS * HEAD_DIM, nb * nq * tq), F32),
        compiler_params=_cparams(("parallel", "parallel", "parallel")), name="mem")(hT, mk, mvT)


def _mem_branch(cols, hT, q_row, mk, mvT):
    n = N_MEM_HEADS * HEAD_DIM
    if cols.sample:
        o = _mem(cols.tile_cols(hT[q_row:q_row + n]), 0, mk, mvT, cols.nb, 1, SAMPLE_COLS)
        return cols.from_tile_cols(o)
    return _mem(hT, q_row, mk, mvT, cols.nb, cols.nq, cols.tq)


def _even_wT(w_in):
    sizes = (384, 128, 128, 384, 384, 128, 128, 128, 128, 128, 128, 18, 384, 256, 256)
    offs = np.cumsum((0,) + sizes)
    (a_q, a_k, a_v, a_z, b_q, b_kc, b_vc, b_ks, b_vs, b_kw, b_vw, b_g, b_z, m_q, m_z) = [
        w_in[:, offs[n]:offs[n + 1]] for n in range(len(sizes))]
    d = w_in.shape[0]
    bg = jnp.pad(b_g, ((0, 0), (0, 32 - b_g.shape[1])))
    cols = [a_q, b_q, m_q, a_z, b_z, m_z, a_k, a_v, b_kc, b_vc, b_ks, b_vs, b_kw, b_vw, bg]
    w = jnp.concatenate(cols, axis=1)
    w = jnp.pad(w, ((0, 0), (0, EVEN_TOTAL - w.shape[1])))
    return w.T.astype(BF16)


def _odd_wT(w_in):
    sizes = (768, 128, 128, 512, 64, 8, 768, 256, 256)
    offs = np.cumsum((0,) + sizes)
    (c_q, c_k, c_v, c_qi, c_ki, c_wi, c_z, m_q, m_z) = [w_in[:, offs[n]:offs[n + 1]] for n in range(len(sizes))]
    w = jnp.concatenate([c_q, m_q, c_z, m_z, c_qi, c_k, c_v, c_ki, c_wi], axis=1)
    w = jnp.pad(w, ((0, 0), (0, ODD_TOTAL - w.shape[1])))
    return w.T.astype(BF16)


def _full_keys(cols, new_tok, pool, page_table):
    if not cols.sample:
        return new_tok
    nb, _, c = new_tok.shape
    page_tokens = pool.shape[1]
    rows_per_page = page_tokens * c // 128
    tail = jnp.zeros((nb, PAGES_PER_STEP * page_tokens, c), F32).at[:, 0:1, :].set(new_tok)
    tail = tail.reshape(nb, PAGES_PER_STEP * rows_per_page, 128)
    out = _gather(pool.reshape(pool.shape[0], rows_per_page, 128), page_table, tail)
    return out.reshape(nb, -1, c)


def _gate_cols(cols, gT):
    return [cols.per_token(gT, N_B_HEADS)]


def _even_layer(cols, xT, mk, mvT, past, page_table, wT, woT, g, b, cmpw, alpha):
    R = EVEN_ROWS
    nb = cols.nb
    hT = _mm(wT, xT, 640, min(1024, xT.shape[1]))
    names = ('a_k', 'a_v', 'b_kc', 'b_vc', 'b_ks', 'b_vs')
    new = [cols.tokens(hT[R[nm]:R[nm] + 128]) for nm in names]
    kw_new, vw_new = cols.tokens(hT[R['b_kw']:R['b_kw'] + 128]), cols.tokens(hT[R['b_vw']:R['b_vw'] + 128])
    if cols.sample:
        full = [_full_keys(cols, n, p, page_table) for n, p in zip(new, past[:6])]
        kw = jnp.concatenate([past[6].reshape(nb, -1, 128), kw_new], axis=1)
        vw = jnp.concatenate([past[7].reshape(nb, -1, 128), vw_new], axis=1)
        n_win = past[6].shape[1]
        kw_base = cols.q_off - n_win
        win_out = (kw[:, -n_win:], vw[:, -n_win:])
        pad = -kw.shape[1] % 1024
        kw = jnp.pad(kw, ((0, 0), (0, pad), (0, 0)))
        vw = jnp.pad(vw, ((0, 0), (0, pad), (0, 0)))
    else:
        full = new
        kw, vw, kw_base = kw_new, vw_new, 0
        n_win = min(WINDOW, cols.t)
        win_out = (kw[:, -n_win:], vw[:, -n_win:])

    sl = _alibi(N_A_HEADS + N_B_HEADS)
    ak, amean = _kprep(full[0], KV_TILE)
    qa, qf, meta = cols.columns(hT[R['a_q']:R['a_q'] + 384], N_A_HEADS, sl[0::2])
    o_a = _moba(cols, qa, qf, meta, amean, ak, _vT_tiles(full[1], KV_TILE))

    pe2k, w1k, w2k = cmpw[0]
    pe2v, w1v, w2v = cmpw[1]
    n_cmp = max(cols.q_off, cols.t) // CMP_STRIDE
    kc = _compress(full[2].reshape(nb, -1, 2048), pe2k, w1k, w2k, n_cmp).astype(BF16)
    vcT = _compress(full[3].reshape(nb, -1, 2048), pe2v, w1v, w2v, n_cmp).transpose(0, 2, 1).astype(BF16)
    ks, _ = _kprep(full[4], SEL_TILE)
    kwt, _ = _kprep(kw, KV_TILE)
    qa, _, meta = cols.columns(hT[R['b_q']:R['b_q'] + 384], N_B_HEADS, sl[1::2],
                               extra=_gate_cols(cols, hT[R['b_g']:R['b_g'] + 18]))
    o_b = _nsa(cols, qa, meta, kc, vcT, ks, _vT_tiles(full[5], SEL_TILE), kwt, _vT_tiles(vw, KV_TILE), kw_base)

    o_m = _mem_branch(cols, hT, R['m_q'], mk, mvT)
    parts = [cols.collapse(o_a, N_A_HEADS), cols.collapse(o_b, N_B_HEADS), o_m]
    yT = _outln(parts, hT, R['z'], xT, woT, g, b, alpha, min(512, xT.shape[1]))
    state = [n.reshape(nb, -1, N_KV, HEAD_DIM) for n in new]
    state += [w.reshape(nb, -1, N_KV, HEAD_DIM) for w in win_out]
    return yT, state


def _odd_layer(cols, xT, mk, mvT, past, page_table, wT, woT, g, b, alpha):
    R = ODD_ROWS
    nb = cols.nb
    hT = _mm(wT, xT, 768, min(1024, xT.shape[1]))
    new = [cols.tokens(hT[R['c_k']:R['c_k'] + 128]), cols.tokens(hT[R['c_v']:R['c_v'] + 128]),
           cols.tokens(hT[R['c_ki']:R['c_ki'] + IDX_DIM])]
    if cols.sample:
        full = [_full_keys(cols, n, p, page_table) for n, p in zip(new, past)]
    else:
        full = new
    topk = min(DSA_TOPK, (cols.q_off + cols.t) // 4)
    ck, _ = _kprep(full[0], KV_TILE)
    ki = full[2].reshape(nb, -1, KV_TILE, IDX_DIM).astype(BF16)
    qa, _, meta = cols.columns(hT[R['c_q']:R['c_q'] + 768], N_C_HEADS, _alibi(N_C_HEADS))
    if cols.sample:
        qi = cols.tile_cols(hT[R['c_qi']:R['c_qi'] + 512])
        wi = cols.tile_cols(hT[R['c_wi']:R['c_wi'] + 8])
        o_c = _dsa(cols, qi, 0, wi, 0, ki, qa, meta, ck, _vT_tiles(full[1], KV_TILE), topk)
    else:
        o_c = _dsa(cols, hT, R['c_qi'], hT, R['c_wi'], ki, qa, meta, ck, _vT_tiles(full[1], KV_TILE), topk)
    o_m = _mem_branch(cols, hT, R['m_q'], mk, mvT)
    parts = [cols.collapse(o_c, N_C_HEADS), o_m]
    yT = _outln(parts, hT, R['z'], xT, woT, g, b, alpha, min(512, xT.shape[1]))
    state = [new[0].reshape(nb, -1, N_KV, HEAD_DIM), new[1].reshape(nb, -1, N_KV, HEAD_DIM), new[2]]
    return yT, state


def kernel(x_prompt, x_sample, cache_a_k, cache_a_v, cache_b_cmp_k, cache_b_cmp_v, cache_b_sel_k, cache_b_sel_v,
           state_b_win_k, state_b_win_v, cache_c_k, cache_c_v, cache_c_idx_k, cache_mem_k, cache_mem_v,
           page_table, mem_prompt, w_in_even, w_out_even, w_in_odd, w_out_odd, w_mem_kv, ln_g, ln_b,
           cmp_pe, cmp_w1, cmp_w2):
    bp, seq, d = x_prompt.shape
    bs = x_sample.shape[0]
    depth = w_mem_kv.shape[0]
    n_mem = mem_prompt.shape[1]
    past_len = page_table.shape[1] * cache_a_k.shape[2]
    alpha = (2.0 * depth) ** 0.25
    assert x_sample.shape[1] == 1 and bs <= 128 and seq % 1024 == 0 and past_len % 1024 == 0

    gp = _Cols(bp, seq, 0)
    gs = _Cols(bs, 1, past_len)
    xp = x_prompt.reshape(bp * seq, d).T
    xs = jnp.pad(x_sample.reshape(bs, d).T, ((0, 0), (0, 128 - bs)))
    even_paged = (cache_a_k, cache_a_v, cache_b_cmp_k, cache_b_cmp_v, cache_b_sel_k, cache_b_sel_v)
    odd_paged = (cache_c_k, cache_c_v, cache_c_idx_k)
    ev_p, ev_s, od_p, od_s, mk_list, mv_list = [], [], [], [], [], []
    for layer in range(depth):
        j = layer // 2
        mem_kv = _mm(mem_prompt.reshape(bp * n_mem, d), w_mem_kv[layer], bp * n_mem, 2 * N_MEM_HEADS * HEAD_DIM)
        mem_kv = mem_kv.reshape(bp, n_mem, 2, N_MEM_HEADS, HEAD_DIM)
        mk_p, mv_p = mem_kv[:, :, 0], mem_kv[:, :, 1]
        mk_list.append(mk_p)
        mv_list.append(mv_p)
        mem_p = (mk_p.transpose(0, 2, 1, 3), mv_p.transpose(0, 2, 3, 1))
        mem_s = (cache_mem_k[layer].transpose(0, 2, 1, 3), cache_mem_v[layer].transpose(0, 2, 3, 1))
        g = ln_g[layer].reshape(d, 1)
        b = ln_b[layer].reshape(d, 1)
        if layer % 2 == 0:
            wT = _even_wT(w_in_even[j])
            woT = w_out_even[j].T.astype(BF16)
            cmpw = [_cmp_weights(cmp_pe[j, c], cmp_w1[j, c], cmp_w2[j, c]) for c in range(2)]
            past = tuple(c[j] for c in even_paged) + (state_b_win_k[j], state_b_win_v[j])
            xp, st_p = _even_layer(gp, xp, *mem_p, None, None, wT, woT, g, b, cmpw, alpha)
            xs, st_s = _even_layer(gs, xs, *mem_s, past, page_table, wT, woT, g, b, cmpw, alpha)
            ev_p.append(st_p)
            ev_s.append(st_s)
        else:
            wT = _odd_wT(w_in_odd[j])
            woT = w_out_odd[j].T.astype(BF16)
            past = tuple(c[j] for c in odd_paged)
            xp, st_p = _odd_layer(gp, xp, *mem_p, None, None, wT, woT, g, b, alpha)
            xs, st_s = _odd_layer(gs, xs, *mem_s, past, page_table, wT, woT, g, b, alpha)
            od_p.append(st_p)
            od_s.append(st_s)
    y_p = xp.T.reshape(bp, seq, d)
    y_s = xs[:, :bs].T.reshape(bs, 1, d)
    ev_p = [jnp.stack(l) for l in zip(*ev_p)]
    ev_s = [jnp.stack(l) for l in zip(*ev_s)]
    od_p = [jnp.stack(l) for l in zip(*od_p)]
    od_s = [jnp.stack(l) for l in zip(*od_s)]
    out = [y_p, y_s]
    for p, s in zip(ev_p, ev_s):
        out += [p, s]
    for p, s in zip(od_p, od_s):
        out += [p, s]
    out += [jnp.stack(mk_list), jnp.stack(mv_list)]
    return tuple(out)
```

```python
import functools

import numpy as np
import jax
import jax.numpy as jnp
from jax import lax
from jax.experimental import pallas as pl
from jax.experimental.pallas import tpu as pltpu

F32 = jnp.float32
BF16 = jnp.bfloat16
I32 = jnp.int32

HEAD_DIM = 64
N_MEM_HEADS = 4
N_A_HEADS = 6
N_B_HEADS = 6
N_C_HEADS = 12
N_KV = 2
N_IDX_HEADS = 8
IDX_DIM = 64
MOBA_BLOCK = 256
MOBA_TOPK = 3
CMP_STRIDE = 16
CMP_LEN = 32
CMP_HIDDEN = 128
SEL_BLOCK = 64
SEL_TOPK = 15
WINDOW = 512
DSA_TOPK = 256
ATTN_SCALE = HEAD_DIM ** -0.5
IDX_SCALE = IDX_DIM ** -0.5
LN_EPS = 1e-5
LOG2E = 1.4426950408889634
NEG = -1e30
SENTINEL = -3e38
INT_MIN = -(2 ** 31)

KV_TILE = 256
SEL_TILE = 512
KD = 256
FEAT_POS = 128
FEAT_SEL = 144
COUNT_GROUP = 4
HEADS_PER_STEP = 3
SAMPLE_COLS = 128
PAGES_PER_STEP = 8
GATHER_PAGES = 16
VMEM_LIMIT = 56 * 1024 * 1024

EVEN_ROWS = dict(a_q=0, b_q=384, m_q=768, z=1024, a_k=2048, a_v=2176, b_kc=2304, b_vc=2432,
                 b_ks=2560, b_vs=2688, b_kw=2816, b_vw=2944, b_g=3072)
EVEN_TOTAL = 3200
ODD_ROWS = dict(c_q=0, m_q=768, z=1024, c_qi=2048, c_k=2560, c_v=2688, c_ki=2816, c_wi=2880)
ODD_TOTAL = 3072


def _cparams(sem):
    return pltpu.CompilerParams(dimension_semantics=sem, vmem_limit_bytes=VMEM_LIMIT)


def _alibi(n):
    return 2.0 ** (-8.0 * np.arange(1, n + 1) / n)


def _mm_kernel(a_ref, b_ref, o_ref):
    o_ref[...] = jnp.dot(a_ref[...].astype(BF16), b_ref[...].astype(BF16), preferred_element_type=F32)


def _mm(a, b, bm, bn):
    m, k = a.shape
    n = b.shape[1]
    return pl.pallas_call(
        _mm_kernel, grid=(n // bn, m // bm),
        in_specs=[pl.BlockSpec((bm, k), lambda j, i: (i, 0)), pl.BlockSpec((k, bn), lambda j, i: (0, j))],
        out_specs=pl.BlockSpec((bm, bn), lambda j, i: (i, j)),
        out_shape=jax.ShapeDtypeStruct((m, n), F32),
        compiler_params=_cparams(("parallel", "parallel")), name="mm")(a, b)


def _outln_kernel(*refs, n_parts, alpha):
    o_refs = refs[:n_parts]
    z_ref, x_ref, w_ref, g_ref, b_ref, y_ref = refs[n_parts:]
    o = jnp.concatenate([r[...] for r in o_refs], axis=0)
    z = z_ref[...]
    mixed = (o * (z / (1.0 + jnp.exp(-z)))).astype(BF16)
    y = alpha * x_ref[...] + jnp.dot(w_ref[...], mixed, preferred_element_type=F32)
    mu = jnp.mean(y, axis=0, keepdims=True)
    d = y - mu
    var = jnp.mean(d * d, axis=0, keepdims=True)
    y_ref[...] = d * lax.rsqrt(var + LN_EPS) * g_ref[...] + b_ref[...]


def _outln(o_parts, hT, z_row, xT, woT, g, b, alpha, bn):
    d, n = xT.shape
    zb = z_row // d
    in_specs = [pl.BlockSpec((p.shape[0], bn), lambda j: (0, j)) for p in o_parts]
    in_specs += [pl.BlockSpec((d, bn), lambda j: (zb, j)),
                 pl.BlockSpec((d, bn), lambda j: (0, j)),
                 pl.BlockSpec((d, d), lambda j: (0, 0)),
                 pl.BlockSpec((d, 1), lambda j: (0, 0)),
                 pl.BlockSpec((d, 1), lambda j: (0, 0))]
    return pl.pallas_call(
        functools.partial(_outln_kernel, n_parts=len(o_parts), alpha=alpha), grid=(n // bn,),
        in_specs=in_specs, out_specs=pl.BlockSpec((d, bn), lambda j: (0, j)),
        out_shape=jax.ShapeDtypeStruct((d, n), F32),
        compiler_params=_cparams(("parallel",)), name="outln")(*o_parts, hT, xT, woT, g, b)


def _key_features(tile):
    r = lax.broadcasted_iota(I32, (tile, 128), 0)
    c = lax.broadcasted_iota(I32, (tile, 128), 1)
    sel_c = c - (FEAT_SEL - FEAT_POS)
    feat = jnp.where(c < 3, r >> 4, jnp.where(c < 6, r & 15, jnp.where(sel_c == (r >> 6), 1, 0)))
    return feat.astype(F32).astype(BF16)


def _emit_tiles(x, outs, mode, tile, t0=0):
    nt = x.shape[0] // tile
    if mode == 'raw':
        outs[0][0, t0 * tile:(t0 + nt) * tile, :] = x
        return
    feat = _key_features(tile) if mode in ('k', 'ks') else None
    for t in range(nt):
        xt = x[t * tile:(t + 1) * tile]
        if mode in ('k', 'ks'):
            outs[0][0, t0 + t] = jnp.concatenate([xt.astype(BF16), feat], axis=1)
        elif mode == 'v':
            outs[0][0, t0 + t] = jnp.transpose(xt).astype(BF16)
        else:
            outs[0][0, t0 + t] = xt.astype(BF16)
    if mode == 'k':
        nblk = x.shape[0] // MOBA_BLOCK
        outs[1][0, 0, t0:t0 + nblk, :] = jnp.mean(x.reshape(nblk, MOBA_BLOCK, x.shape[1]), axis=1)


def _zero_tiles(outs, mode, tile, t0, t1):
    for t in range(t0, t1):
        outs[0][0, t] = jnp.zeros(outs[0].shape[2:], outs[0].dtype)
    if mode == 'k':
        outs[1][0, 0, t0:t1, :] = jnp.zeros((t1 - t0, outs[1].shape[3]), F32)


def _tile_outputs(mode, nb, n_steps, rows, tile, c):
    nt = rows // tile
    if mode == 'raw':
        return ([pl.BlockSpec((1, rows, c), lambda b, j, *_: (b, j, 0))],
                [jax.ShapeDtypeStruct((nb, n_steps * rows, c), F32)])
    if mode in ('k', 'ks'):
        shape, blk = (nb, n_steps * nt, tile, KD), (1, nt, tile, KD)
    elif mode == 'v':
        shape, blk = (nb, n_steps * nt, c, tile), (1, nt, c, tile)
    else:
        shape, blk = (nb, n_steps * nt, tile, c), (1, nt, tile, c)
    specs = [pl.BlockSpec(blk, lambda b, j, *_: (b, j, 0, 0))]
    shapes = [jax.ShapeDtypeStruct(shape, BF16)]
    if mode == 'k':
        nblk = rows // MOBA_BLOCK
        specs.append(pl.BlockSpec((1, 1, nblk, c), lambda b, j, *_: (b, j, 0, 0)))
        shapes.append(jax.ShapeDtypeStruct((nb, n_steps, nblk, c), F32))
    return specs, shapes


def _gather_kernel(pt_ref, *refs, npg, mode, tile, last, has_tail):
    pool_refs = refs[:npg]
    outs = refs[npg + (1 if has_tail else 0):]
    j = pl.program_id(1)

    def pages():
        _emit_tiles(jnp.concatenate([r[0] for r in pool_refs], axis=0), outs, mode, tile)

    if not has_tail:
        pages()
        return
    pl.when(j < last)(pages)

    @pl.when(j == last)
    def _():
        _emit_tiles(refs[npg][0], outs, mode, tile)
        if mode != 'raw':
            _zero_tiles(outs, mode, tile, 1, outs[0].shape[1])


def _gather(pool, page_table, new_tok, mode, tile):
    nb, n_pages = page_table.shape
    n_pool, page_tokens, c = pool.shape
    npg = GATHER_PAGES if n_pages % GATHER_PAGES == 0 else PAGES_PER_STEP
    last = n_pages // npg
    has_tail = new_tok is not None
    rows = npg * page_tokens
    assert rows % tile == 0 and (has_tail or mode == 'raw')

    def pool_spec(k):
        return pl.BlockSpec((1, page_tokens, c),
                            lambda b, j, pt: (pt[b, jnp.minimum(j * npg + k, n_pages - 1)], 0, 0))

    in_specs = [pool_spec(k) for k in range(npg)]
    args = [pool] * npg
    if has_tail:
        tail = jnp.zeros((nb, tile, c), F32).at[:, 0:1, :].set(new_tok)
        in_specs.append(pl.BlockSpec((1, tile, c), lambda b, j, pt: (b, 0, 0)))
        args.append(tail)
    n_steps = last + (1 if has_tail else 0)
    out_specs, out_shape = _tile_outputs(mode, nb, n_steps, rows, tile, c)
    grid_spec = pltpu.PrefetchScalarGridSpec(num_scalar_prefetch=1, grid=(nb, n_steps), in_specs=in_specs,
                                             out_specs=out_specs)
    out = pl.pallas_call(
        functools.partial(_gather_kernel, npg=npg, mode=mode, tile=tile, last=last, has_tail=has_tail),
        grid_spec=grid_spec, out_shape=out_shape,
        compiler_params=_cparams(("parallel", "arbitrary")), name="gather_" + mode)(page_table, *args)
    if mode == 'k':
        return out[0], out[1].reshape(nb, -1, c)
    return out[0]


def _kprep_kernel(x_ref, *outs, mode, tile):
    _emit_tiles(x_ref[0], outs, mode, tile)


def _kprep(x, tile, mode='k'):
    nb, L, c = x.shape
    rows = 1024
    out_specs, out_shape = _tile_outputs(mode, nb, L // rows, rows, tile, c)
    out = pl.pallas_call(
        functools.partial(_kprep_kernel, mode=mode, tile=tile), grid=(nb, L // rows),
        in_specs=[pl.BlockSpec((1, rows, c), lambda b, j: (b, j, 0))],
        out_specs=out_specs, out_shape=out_shape,
        compiler_params=_cparams(("parallel", "parallel")), name="kprep")(x)
    if mode == 'k':
        return out[0], out[1].reshape(nb, -1, c)
    return out[0]


def _vT_tiles(v, tile):
    nb, L, c = v.shape
    return v.reshape(nb, L // tile, tile, c).transpose(0, 1, 3, 2).astype(BF16)


class _Cols:
    def __init__(self, nb, t, q_off):
        self.nb, self.t, self.q_off = nb, t, q_off
        self.sample = q_off > 0
        self.tq = 1 if self.sample else KV_TILE
        self.nq = 1 if self.sample else t // KV_TILE
        self.span = 1 if self.sample else KV_TILE

    def n_hc(self, heads):
        return 1 if self.sample else heads // HEADS_PER_STEP

    def step_cols(self):
        return SAMPLE_COLS if self.sample else HEADS_PER_STEP * self.tq

    def head_group(self, heads):
        return np.arange(heads) // (heads // N_KV)

    def per_head(self, vals, heads):
        vals = jnp.asarray(vals, F32)
        k = vals.shape[0]
        if self.sample:
            return jnp.tile(jnp.pad(vals, ((0, 0), (0, SAMPLE_COLS - heads))), (1, self.nb))
        n_hc = heads // HEADS_PER_STEP
        shape = (k, self.nb, self.nq, n_hc, HEADS_PER_STEP, self.tq)
        return jnp.broadcast_to(vals.reshape(k, 1, 1, n_hc, HEADS_PER_STEP, 1), shape).reshape(k, -1)

    def per_token(self, rows, heads):
        nb, nq, tq = self.nb, self.nq, self.tq
        k = rows.shape[0] // heads
        if self.sample:
            x = rows[:, :nb].reshape(heads, k, nb).transpose(1, 2, 0)
            return jnp.pad(x, ((0, 0), (0, 0), (0, SAMPLE_COLS - heads))).reshape(k, nb * SAMPLE_COLS)
        n_hc = heads // HEADS_PER_STEP
        x = rows.reshape(n_hc, HEADS_PER_STEP, k, nb, nq, tq).transpose(2, 3, 4, 0, 1, 5)
        return x.reshape(k, -1)

    def qpos_row(self, heads):
        if self.sample:
            return jnp.full((1, self.nb * SAMPLE_COLS), self.q_off, F32)
        pos = (jnp.arange(self.nq, dtype=F32)[:, None] * self.tq + jnp.arange(self.tq, dtype=F32)[None, :])
        shape = (1, self.nb, self.nq, heads // HEADS_PER_STEP, HEADS_PER_STEP, self.tq)
        return jnp.broadcast_to(pos.reshape(1, 1, self.nq, 1, 1, self.tq), shape).reshape(1, -1)

    def columns(self, rows, heads, slopes, extra=()):
        x = self.per_token(rows, heads)
        grp = self.head_group(heads)
        gsel = self.per_head(np.stack([grp == 0, grp == 1]).astype(np.float32), heads)
        qf = jnp.concatenate([x * gsel[0:1], x * gsel[1:2]], axis=0)
        s2 = np.asarray(slopes, np.float64) * LOG2E
        hi = s2.astype(jnp.bfloat16).astype(np.float64)
        mid = (s2 - hi).astype(jnp.bfloat16).astype(np.float64)
        lo = (s2 - hi - mid).astype(jnp.bfloat16).astype(np.float64)
        pieces = np.stack([16 * hi, 16 * mid, 16 * lo, hi, mid, lo] + [np.zeros_like(hi)] * 10)
        qa = jnp.concatenate([(qf * (ATTN_SCALE * LOG2E)).astype(BF16), self.per_head(pieces, heads).astype(BF16),
                              jnp.zeros((KD - FEAT_POS - 16, x.shape[1]), BF16)], axis=0)
        meta = [self.per_head((hi + mid + lo)[None, :], heads), self.qpos_row(heads)] + list(extra)
        meta.append(jnp.zeros((8 - sum(m.shape[0] for m in meta), x.shape[1]), F32))
        return qa, qf, jnp.concatenate(meta, axis=0)

    def collapse(self, o, heads):
        nb, nq, tq = self.nb, self.nq, self.tq
        if self.sample:
            o = o.reshape(N_KV, HEAD_DIM, nb, SAMPLE_COLS)
            grp = self.head_group(heads)
            x = jnp.stack([o[grp[h], :, :, h] for h in range(heads)])
            return jnp.pad(x.reshape(heads * HEAD_DIM, nb), ((0, 0), (0, 128 - nb)))
        n_hc = heads // HEADS_PER_STEP
        o = o.reshape(HEAD_DIM, nb, nq, n_hc, HEADS_PER_STEP, tq).transpose(3, 4, 0, 1, 2, 5)
        return o.reshape(heads * HEAD_DIM, nb * nq * tq)

    def tokens(self, rows):
        n = rows.shape[0]
        if self.sample:
            return rows[:, :self.nb].T.reshape(self.nb, 1, n)
        return rows.reshape(n, self.nb, self.t).transpose(1, 2, 0)

    def tile_cols(self, rows):
        if not self.sample:
            return rows
        n = rows.shape[0]
        return jnp.broadcast_to(rows[:, :self.nb, None], (n, self.nb, SAMPLE_COLS)).reshape(n, self.nb * SAMPLE_COLS)

    def from_tile_cols(self, o):
        if not self.sample:
            return o
        n = o.shape[0]
        return jnp.pad(o.reshape(n, self.nb, SAMPLE_COLS)[:, :, 0], ((0, 0), (0, 128 - self.nb)))


def _softmax_step(s, c, ok, vT, m, l, acc_ref):
    smax = jnp.max(s, axis=0, keepdims=True)
    live = smax > 0.5 * NEG
    if ok is not None:
        live = live & _row_flag(ok, s.shape[1])
    m_new = jnp.where(live, jnp.maximum(m, smax + c), m)
    shift = jnp.where(live, m_new - c, -NEG)
    p = jnp.exp2(s - shift)
    alpha = jnp.exp2(m - m_new)
    l = alpha * l + jnp.sum(p, axis=0, keepdims=True)
    acc_ref[...] = alpha * acc_ref[...] + jnp.dot(vT, p.astype(BF16), preferred_element_type=F32)
    return m_new, l


def _row_flag(ok, n):
    if ok.ndim == 0:
        return (jnp.zeros((1, n), I32) + ok.astype(I32)) > 0
    return ok


def _softmax_init(n):
    return jnp.full((1, n), NEG, F32), jnp.zeros((1, n), F32)


def _tile_pairs(n_tiles, step, carry):
    def body(u, carry):
        carry = step(2 * u, None, carry)
        t2 = 2 * u + 1
        return step(jnp.minimum(t2, n_tiles - 1), t2 < n_tiles, carry)
    return lax.fori_loop(0, (n_tiles + 1) // 2, body, carry)


def _value_rows(vT_ref, n, g, mixed):
    if mixed:
        return vT_ref[0, n]
    return vT_ref[0, n, pl.ds(pl.multiple_of(g * HEAD_DIM, HEAD_DIM), HEAD_DIM), :]


def _top_rows(vals, rowi, k):
    sel = jnp.zeros(vals.shape, F32)
    for _ in range(k):
        m = jnp.max(vals, axis=0, keepdims=True)
        first = jnp.min(jnp.where(vals == m, rowi, 1 << 20), axis=0, keepdims=True)
        pick = (rowi == first) & (m > SENTINEL)
        sel = jnp.where(pick, 1.0, sel)
        vals = jnp.where(pick, SENTINEL, vals)
    return sel


def _moba_kernel(qa_ref, qf_ref, meta_ref, kmean_ref, k_ref, vT_ref, o_ref, sel_ref, acc_ref, *, span, q_off, mixed):
    i, g = pl.program_id(1), pl.program_id(2)
    n = qa_ref.shape[1]
    own = (q_off + i * span) // MOBA_BLOCK
    slope = meta_ref[0:1, :]
    qpos = meta_ref[1:2, :].astype(I32)
    nrow = kmean_ref.shape[1]
    gate = jnp.dot(kmean_ref[0], qf_ref[...], precision=lax.Precision.HIGHEST, preferred_element_type=F32)
    rowi = lax.broadcasted_iota(I32, (nrow, n), 0)
    sel_ref[...] = _top_rows(jnp.where(rowi < own, gate, SENTINEL), rowi, MOBA_TOPK)
    qa = qa_ref[...]
    acc_ref[...] = jnp.zeros(acc_ref.shape, F32)

    def step(t, ok, carry):
        s = jnp.dot(k_ref[0, t], qa, preferred_element_type=F32)
        c = slope * (t * KV_TILE - qpos).astype(F32)
        sel = sel_ref[pl.ds(t, 1), :] > 0.5
        ok = sel if ok is None else sel & _row_flag(ok, n)
        return _softmax_step(s, c, ok, _value_rows(vT_ref, t, g, mixed), *carry, acc_ref)

    m, l = _tile_pairs(own, step, _softmax_init(n))
    s = jnp.dot(k_ref[0, own], qa, preferred_element_type=F32)
    krow = lax.broadcasted_iota(I32, (KV_TILE, n), 0)
    s = jnp.where(krow <= qpos - own * KV_TILE, s, NEG)
    c = slope * (own * KV_TILE - qpos).astype(F32)
    _, l = _softmax_step(s, c, None, _value_rows(vT_ref, own, g, mixed), m, l, acc_ref)
    o_ref[...] = acc_ref[...] / jnp.maximum(l, 1e-30)


def _moba(cols, qa, qf, meta, kmean, k, vT):
    nb, nq = cols.nb, cols.nq
    n_hc, n = cols.n_hc(N_A_HEADS), cols.step_cols()
    nT, nblk = k.shape[1], kmean.shape[1]
    vrows = N_KV * HEAD_DIM if cols.sample else HEAD_DIM
    col = lambda b, i, g: (0, (b * nq + i) * n_hc + g)
    return pl.pallas_call(
        functools.partial(_moba_kernel, span=cols.span, q_off=cols.q_off, mixed=cols.sample), grid=(nb, nq, n_hc),
        in_specs=[pl.BlockSpec((KD, n), col),
                  pl.BlockSpec((N_KV * HEAD_DIM, n), col),
                  pl.BlockSpec((8, n), col),
                  pl.BlockSpec((1, nblk, N_KV * HEAD_DIM), lambda b, i, g: (b, 0, 0)),
                  pl.BlockSpec((1, nT, KV_TILE, KD), lambda b, i, g: (b, 0, 0, 0)),
                  pl.BlockSpec((1, nT, N_KV * HEAD_DIM, KV_TILE), lambda b, i, g: (b, 0, 0, 0))],
        out_specs=pl.BlockSpec((vrows, n), col),
        out_shape=jax.ShapeDtypeStruct((vrows, qa.shape[1]), F32),
        scratch_shapes=[pltpu.VMEM((nblk, n), F32), pltpu.VMEM((vrows, n), F32)],
        compiler_params=_cparams(("parallel", "parallel", "arbitrary")), name="moba")(qa, qf, meta, kmean, k, vT)


def _cmp_kernel(r_ref, pe_ref, w1_ref, w2_ref, o_ref):
    r = r_ref[0]
    n = r.shape[0]
    hid = 2 * CMP_HIDDEN
    u = jnp.dot((r + pe_ref[0:1, :]).astype(BF16), w1_ref[:, 0:hid], preferred_element_type=F32)
    lo = jnp.dot((r + pe_ref[1:2, :]).astype(BF16), w1_ref[:, hid:2 * hid], preferred_element_type=F32)
    pre = u + pltpu.roll(lo, n - 1, 0)
    h = pre / (1.0 + jnp.exp(-pre))
    o_ref[0] = jnp.dot(h.astype(BF16), w2_ref[...], preferred_element_type=F32)


def _compress(raw, pe2, w1big, w2big, n_rows):
    nb = raw.shape[0]
    return pl.pallas_call(
        _cmp_kernel, grid=(nb,),
        in_specs=[pl.BlockSpec((1, n_rows, 2048), lambda b: (b, 0, 0)),
                  pl.BlockSpec((2, 2048), lambda b: (0, 0)),
                  pl.BlockSpec((2048, 4 * CMP_HIDDEN), lambda b: (0, 0)),
                  pl.BlockSpec((2 * CMP_HIDDEN, 128), lambda b: (0, 0))],
        out_specs=pl.BlockSpec((1, n_rows, 128), lambda b: (b, 0, 0)),
        out_shape=jax.ShapeDtypeStruct((nb, n_rows, 128), F32),
        compiler_params=_cparams(("parallel",)), name="compress")(raw, pe2, w1big, w2big)


def _cmp_weights(pe, w1, w2):
    w1r = w1.reshape(2, CMP_STRIDE, HEAD_DIM, CMP_HIDDEN)
    eye = jnp.eye(N_KV, dtype=w1.dtype)
    big = jnp.einsum('hpdj,ge->hpdgej', w1r, eye)
    big = big.transpose(1, 3, 2, 0, 4, 5).reshape(CMP_STRIDE * N_KV * HEAD_DIM, 2 * N_KV * CMP_HIDDEN)
    w2big = jnp.einsum('jd,ge->gjed', w2, eye).reshape(N_KV * CMP_HIDDEN, N_KV * HEAD_DIM)
    pe2 = jnp.broadcast_to(pe.reshape(2, CMP_STRIDE, 1, HEAD_DIM), (2, CMP_STRIDE, N_KV, HEAD_DIM)).reshape(2, 2048)
    return pe2, big.astype(BF16), w2big.astype(BF16)


def _nsa_kernel(qa_ref, meta_ref, kc_ref, vcT_ref, taps_ref, grp_ref, ks_ref, vsT_ref, kw_ref, vwT_ref, o_ref,
                sel_ref, qs_ref, acc_ref, *, span, q_off, kw_base, mixed):
    i, g = pl.program_id(1), pl.program_id(2)
    n = qa_ref.shape[1]
    nq = n if mixed else n // HEADS_PER_STEP
    rep = n // nq
    t0 = q_off + i * span
    slope = meta_ref[0:1, :]
    qpos = meta_ref[1:2, :].astype(I32)
    qa = qa_ref[...]
    n_cmp = kc_ref.shape[1]
    n_tab = sel_ref.shape[0]

    def vrows(ref):
        if mixed:
            return ref[0]
        return ref[0, pl.ds(pl.multiple_of(g * HEAD_DIM, HEAD_DIM), HEAD_DIM), :]

    cend = CMP_STRIDE * lax.broadcasted_iota(I32, (n_cmp, n), 0) + (CMP_LEN - 1)
    cmask = cend <= qpos
    s = jnp.dot(kc_ref[0], qa[0:N_KV * HEAD_DIM, :], preferred_element_type=F32) + slope * (cend - qpos).astype(F32)
    s = jnp.where(cmask, s, NEG)
    p = jnp.where(cmask, jnp.exp2(s - jnp.max(s, axis=0, keepdims=True)), 0.0)
    p = p * (1.0 / jnp.maximum(jnp.sum(p, axis=0, keepdims=True), 1e-30))
    o_c = jnp.dot(vrows(vcT_ref), p.astype(BF16), preferred_element_type=F32)

    if mixed:
        imp = jnp.dot(p, grp_ref[...], precision=lax.Precision.HIGHEST, preferred_element_type=F32)
    else:
        imp = p[:, 0:nq]
        for r in range(1, rep):
            imp = imp + p[:, r * nq:(r + 1) * nq]
    p_slc = jnp.dot(taps_ref[...], imp, precision=lax.Precision.HIGHEST, preferred_element_type=F32)
    p_slc = jnp.concatenate([p_slc, jnp.full((n_tab - p_slc.shape[0], nq), SENTINEL, F32)], axis=0)
    rowi = lax.broadcasted_iota(I32, (n_tab, nq), 0)
    own = qpos[:, 0:nq] >> 6
    sel = _top_rows(jnp.where(rowi < own, p_slc, SENTINEL), rowi, SEL_TOPK)
    sel_ref[...] = jnp.where(rowi == own, 1.0, sel)

    per_tile = SEL_TILE // SEL_BLOCK
    last = (t0 + span - 1) // SEL_TILE
    qs_ref[0] = qa
    qs_ref[1] = qa
    acc_ref[...] = jnp.zeros(acc_ref.shape, F32)

    def sel_scores(t, slot):
        rows = sel_ref[pl.ds(pl.multiple_of(t * per_tile, per_tile), per_tile), :]
        bias = jnp.where(rows > 0.5, 0.0, NEG)
        if rep > 1:
            bias = jnp.concatenate([bias] * rep, axis=1)
        bias = jnp.concatenate([bias, jnp.zeros((per_tile, n), F32)], axis=0)
        qs_ref[slot, FEAT_SEL:FEAT_SEL + 2 * per_tile, :] = bias.astype(BF16)
        s = jnp.dot(ks_ref[0, t], qs_ref[slot], preferred_element_type=F32)
        return s, slope * (t * SEL_TILE - qpos).astype(F32)

    def sel_step(t, ok, carry):
        s, c = sel_scores(t, 0 if ok is None else 1)
        return _softmax_step(s, c, ok, _value_rows(vsT_ref, t, g, mixed), *carry, acc_ref)

    m, l = _tile_pairs(last, sel_step, _softmax_init(n))
    s, c = sel_scores(last, 0)
    srow = lax.broadcasted_iota(I32, (SEL_TILE, n), 0)
    s = jnp.where(srow <= qpos - last * SEL_TILE, s, NEG)
    _, l = _softmax_step(s, c, None, _value_rows(vsT_ref, last, g, mixed), m, l, acc_ref)
    o_s = acc_ref[...] / jnp.maximum(l, 1e-30)

    lo = jnp.maximum(t0 - (WINDOW - 1) - kw_base, 0) // KV_TILE
    hi = jnp.minimum((t0 + span - 1 - kw_base) // KV_TILE + 1, kw_ref.shape[1])
    wrow = lax.broadcasted_iota(I32, (KV_TILE, n), 0)
    acc_ref[...] = jnp.zeros(acc_ref.shape, F32)

    def win_body(t, carry):
        s = jnp.dot(kw_ref[0, t], qa, preferred_element_type=F32)
        base = kw_base + t * KV_TILE
        dist = qpos - base - wrow
        s = jnp.where((dist >= 0) & (dist < WINDOW), s, NEG)
        return _softmax_step(s, slope * (base - qpos).astype(F32), None, _value_rows(vwT_ref, t, g, mixed),
                             *carry, acc_ref)

    _, l = lax.fori_loop(lo, hi, win_body, _softmax_init(n))
    o_w = acc_ref[...] / jnp.maximum(l, 1e-30)

    gt = 1.0 / (1.0 + jnp.exp(-meta_ref[2:5, :]))
    o_ref[...] = gt[0:1, :] * o_c + gt[1:2, :] * o_s + gt[2:3, :] * o_w


def _taps_matrix(n_sel, n_cmp):
    m = np.zeros((n_sel, n_cmp), np.float32)
    for j in range(n_sel):
        for off, w in ((-1, 1.0), (0, 2.0), (1, 2.0), (2, 2.0), (3, 1.0)):
            c = 4 * j + off
            if 0 <= c < n_cmp:
                m[j, c] = w
    return jnp.asarray(m)


def _group_matrix(cols):
    grp = cols.head_group(N_B_HEADS)
    m = np.zeros((SAMPLE_COLS, SAMPLE_COLS), np.float32)
    m[:N_B_HEADS, :N_B_HEADS] = grp[:, None] == grp[None, :]
    return jnp.asarray(m)


def _nsa(cols, qa, meta, kc, vcT, ks, vsT, kw, vwT, kw_base):
    nb, nq = cols.nb, cols.nq
    n_hc, n = cols.n_hc(N_B_HEADS), cols.step_cols()
    nTs, nTw = ks.shape[1], kw.shape[1]
    n_cmp = kc.shape[1]
    n_sel = n_cmp // 4
    n_tab = n_sel + SEL_TILE // SEL_BLOCK
    nsel_cols = n if cols.sample else n // HEADS_PER_STEP
    vrows = N_KV * HEAD_DIM if cols.sample else HEAD_DIM
    col = lambda b, i, g: (0, (b * nq + i) * n_hc + g)
    seq3 = lambda b, i, g: (b, 0, 0)
    seq4 = lambda b, i, g: (b, 0, 0, 0)
    return pl.pallas_call(
        functools.partial(_nsa_kernel, span=cols.span, q_off=cols.q_off, kw_base=kw_base, mixed=cols.sample),
        grid=(nb, nq, n_hc),
        in_specs=[pl.BlockSpec((KD, n), col),
                  pl.BlockSpec((8, n), col),
                  pl.BlockSpec((1, n_cmp, N_KV * HEAD_DIM), seq3),
                  pl.BlockSpec((1, N_KV * HEAD_DIM, n_cmp), seq3),
                  pl.BlockSpec((n_sel, n_cmp), lambda b, i, g: (0, 0)),
                  pl.BlockSpec((SAMPLE_COLS, SAMPLE_COLS), lambda b, i, g: (0, 0)),
                  pl.BlockSpec((1, nTs, SEL_TILE, KD), seq4),
                  pl.BlockSpec((1, nTs, N_KV * HEAD_DIM, SEL_TILE), seq4),
                  pl.BlockSpec((1, nTw, KV_TILE, KD), seq4),
                  pl.BlockSpec((1, nTw, N_KV * HEAD_DIM, KV_TILE), seq4)],
        out_specs=pl.BlockSpec((vrows, n), col),
        out_shape=jax.ShapeDtypeStruct((vrows, qa.shape[1]), F32),
        scratch_shapes=[pltpu.VMEM((n_tab, nsel_cols), F32), pltpu.VMEM((2, KD, n), BF16), pltpu.VMEM((vrows, n), F32)],
        compiler_params=_cparams(("parallel", "parallel", "arbitrary")), name="nsa")(
            qa, meta, kc, vcT, _taps_matrix(n_sel, n_cmp), _group_matrix(cols), ks, vsT, kw, vwT)


def _dsa_kernel(qi_ref, wi_ref, ki_ref, qa_ref, meta_ref, k_ref, vT_ref, o_ref, key_ref, row_ref, acc_ref,
                *, span, q_off, topk, mixed, hc_per_group):
    i, hc = pl.program_id(1), pl.program_id(2)
    n = qa_ref.shape[1]
    nq = qi_ref.shape[1]
    rep = n // nq
    t0 = q_off + i * span
    n_kv = (t0 + span - 1) // KV_TILE + 1
    krow = lax.broadcasted_iota(I32, (KV_TILE, nq), 0)

    def search(count, shape):
        kf = float(topk)
        thr = jnp.where(count(lambda key, kpos: key >= 0) >= kf, 0, INT_MIN).astype(I32)

        def bit_body(j, thr):
            cand = thr | (1 << (30 - j))
            return jnp.where(count(lambda key, kpos: key >= cand) >= kf, cand, thr)
        thr = lax.fori_loop(0, 31, bit_body, thr)
        need = kf - count(lambda key, kpos: key > thr)

        def tie_search():
            def idx_body(j, jb):
                cand = jb | (1 << (14 - j))
                c = count(lambda key, kpos: (key == thr) & (kpos < cand))
                return jnp.where(c <= need, cand, jb)
            return lax.fori_loop(0, 15, idx_body, jnp.zeros(shape, I32))

        n_ge = count(lambda key, kpos: key >= thr)
        jb = lax.cond(jnp.max(n_ge) > kf, tie_search, lambda: jnp.full(shape, (1 << 15) - 1, I32))
        return thr, jb

    @pl.when(hc == 0)
    def _():
        qpos = t0 + (lax.broadcasted_iota(I32, (1, nq), 1) if span > 1 else jnp.zeros((1, nq), I32))
        w = wi_ref[...] * (IDX_SCALE * N_IDX_HEADS ** -0.5)
        qi = [qi_ref[a * IDX_DIM:(a + 1) * IDX_DIM, :].astype(BF16) for a in range(N_IDX_HEADS)]
        if mixed:
            row_ref[...] = jnp.full(row_ref.shape, INT_MIN, I32)

        def score_body(t, _):
            ki = ki_ref[0, t]
            acc = jnp.zeros((KV_TILE, nq), F32)
            for a in range(N_IDX_HEADS):
                rel = jnp.dot(ki, qi[a], preferred_element_type=F32)
                acc = acc + jnp.maximum(rel, 0.0) * w[a:a + 1, :]
            sc = jnp.where(t * KV_TILE + krow <= qpos, acc, NEG)
            bits = lax.bitcast_convert_type(sc, I32)
            key = bits ^ ((bits >> 31) & 0x7FFFFFFF)
            key_ref[t] = key
            if mixed:
                row_ref[pl.ds(t, 1), :] = jnp.transpose(key)[0:1, :]
            return 0
        lax.fori_loop(0, n_kv, score_body, 0)

        if mixed:
            keys = row_ref[...]
            kpos_r = (lax.broadcasted_iota(I32, keys.shape, 0) * KV_TILE + lax.broadcasted_iota(I32, keys.shape, 1))

            def count(pred):
                c = jnp.sum(jnp.where(pred(keys, kpos_r), 1.0, 0.0), axis=0, keepdims=True)
                return jnp.sum(c, axis=1, keepdims=True)
            thr, jb = search(count, (1, 1))
        else:
            n_grp = (n_kv + COUNT_GROUP - 1) // COUNT_GROUP

            def pad_body(t, _):
                key_ref[t] = jnp.full((KV_TILE, nq), INT_MIN, I32)
                return 0
            lax.fori_loop(n_kv, n_grp * COUNT_GROUP, pad_body, 0)

            def count(pred):
                def body(u, c):
                    for j in range(COUNT_GROUP):
                        t = u * COUNT_GROUP + j
                        hit = pred(key_ref[t], t * KV_TILE + krow)
                        c = c + jnp.sum(jnp.where(hit, 1.0, 0.0), axis=0, keepdims=True)
                    return c
                return lax.fori_loop(0, n_grp, body, jnp.zeros((1, nq), F32))
            thr, jb = search(count, (1, nq))

        def bias_body(t, _):
            key = key_ref[t]
            kpos = t * KV_TILE + krow
            keep = ((key > thr) | ((key == thr) & (kpos < jb))) & (kpos <= qpos)
            key_ref[t] = lax.bitcast_convert_type(jnp.where(keep, 0.0, NEG), I32)
            return 0
        lax.fori_loop(0, n_kv, bias_body, 0)

    g = hc // hc_per_group
    slope = meta_ref[0:1, :]
    qpos_c = meta_ref[1:2, :].astype(I32)
    qa = qa_ref[...]
    acc_ref[...] = jnp.zeros(acc_ref.shape, F32)

    def step(t, ok, carry):
        bias = lax.bitcast_convert_type(key_ref[t], F32)
        if rep > 1:
            bias = jnp.concatenate([bias] * rep, axis=1)
        s = jnp.dot(k_ref[0, t], qa, preferred_element_type=F32) + bias
        c = slope * (t * KV_TILE - qpos_c).astype(F32)
        return _softmax_step(s, c, ok, _value_rows(vT_ref, t, g, mixed), *carry, acc_ref)
    _, l = _tile_pairs(n_kv, step, _softmax_init(n))
    o_ref[...] = acc_ref[...] / jnp.maximum(l, 1e-30)


def _dsa(cols, qi, qi_row, wi, wi_row, ki, qa, meta, k, vT, topk):
    nb, nq = cols.nb, cols.nq
    n_hc, n = cols.n_hc(N_C_HEADS), cols.step_cols()
    ncol = SAMPLE_COLS if cols.sample else cols.tq
    nT = k.shape[1]
    assert nT % COUNT_GROUP == 0
    vrows = N_KV * HEAD_DIM if cols.sample else HEAD_DIM
    qi_b, wi_b = qi_row // (N_IDX_HEADS * IDX_DIM), wi_row // N_IDX_HEADS
    col = lambda b, i, h: (0, (b * nq + i) * n_hc + h)
    seq4 = lambda b, i, h: (b, 0, 0, 0)
    return pl.pallas_call(
        functools.partial(_dsa_kernel, span=cols.span, q_off=cols.q_off, topk=topk, mixed=cols.sample,
                          hc_per_group=max(n_hc // N_KV, 1)),
        grid=(nb, nq, n_hc),
        in_specs=[pl.BlockSpec((N_IDX_HEADS * IDX_DIM, ncol), lambda b, i, h: (qi_b, b * nq + i)),
                  pl.BlockSpec((N_IDX_HEADS, ncol), lambda b, i, h: (wi_b, b * nq + i)),
                  pl.BlockSpec((1, nT, KV_TILE, IDX_DIM), seq4),
                  pl.BlockSpec((KD, n), col),
                  pl.BlockSpec((8, n), col),
                  pl.BlockSpec((1, nT, KV_TILE, KD), seq4),
                  pl.BlockSpec((1, nT, N_KV * HEAD_DIM, KV_TILE), seq4)],
        out_specs=pl.BlockSpec((vrows, n), col),
        out_shape=jax.ShapeDtypeStruct((vrows, qa.shape[1]), F32),
        scratch_shapes=[pltpu.VMEM((nT, KV_TILE, ncol), I32), pltpu.VMEM((-(-nT // 8) * 8, KV_TILE), I32),
                        pltpu.VMEM((vrows, n), F32)],
        compiler_params=_cparams(("parallel", "parallel", "arbitrary")), name="dsa")(
            qi, wi, ki, qa, meta, k, vT)


def _mem_kernel(q_ref, mk_ref, mvT_ref, o_ref):
    for h in range(N_MEM_HEADS):
        rows = slice(h * HEAD_DIM, (h + 1) * HEAD_DIM)
        q = (q_ref[rows, :] * ATTN_SCALE).astype(BF16)
        s = jnp.dot(mk_ref[0, h].astype(BF16), q, preferred_element_type=F32)
        p = jnp.exp(s - jnp.max(s, axis=0, keepdims=True))
        l = jnp.sum(p, axis=0, keepdims=True)
        o_ref[rows, :] = jnp.dot(mvT_ref[0, h].astype(BF16), p.astype(BF16), preferred_element_type=F32) / l


def _mem(hT, q_row, mk, mvT, nb, nq, tq):
    n_mem = mk.shape[2]
    n = N_MEM_HEADS * HEAD_DIM
    qb = q_row // n
    return pl.pallas_call(
        _mem_kernel, grid=(nb, nq),
        in_specs=[pl.BlockSpec((n, tq), lambda b, i: (qb, b * nq + i)),
                  pl.BlockSpec((1, N_MEM_HEADS, n_mem, HEAD_DIM), lambda b, i: (b, 0, 0, 0)),
                  pl.BlockSpec((1, N_MEM_HEADS, HEAD_DIM, n_mem), lambda b, i: (b, 0, 0, 0))],
        out_specs=pl.BlockSpec((n, tq), lambda b, i: (0, b * nq + i)),
        out_shape=jax.ShapeDtypeStruct((n, nb * nq * tq), F32),
        compiler_params=_cparams(("parallel", "parallel")), name="mem")(hT, mk, mvT)


def _mem_branch(cols, hT, q_row, mk, mvT):
    n = N_MEM_HEADS * HEAD_DIM
    if cols.sample:
        o = _mem(cols.tile_cols(hT[q_row:q_row + n]), 0, mk, mvT, cols.nb, 1, SAMPLE_COLS)
        return cols.from_tile_cols(o)
    return _mem(hT, q_row, mk, mvT, cols.nb, cols.nq, cols.tq)


def _even_wT(w_in):
    sizes = (384, 128, 128, 384, 384, 128, 128, 128, 128, 128, 128, 18, 384, 256, 256)
    offs = np.cumsum((0,) + sizes)
    (a_q, a_k, a_v, a_z, b_q, b_kc, b_vc, b_ks, b_vs, b_kw, b_vw, b_g, b_z, m_q, m_z) = [
        w_in[:, offs[n]:offs[n + 1]] for n in range(len(sizes))]
    d = w_in.shape[0]
    bg = jnp.pad(b_g, ((0, 0), (0, 32 - b_g.shape[1])))
    cols = [a_q, b_q, m_q, a_z, b_z, m_z, a_k, a_v, b_kc, b_vc, b_ks, b_vs, b_kw, b_vw, bg]
    w = jnp.concatenate(cols, axis=1)
    w = jnp.pad(w, ((0, 0), (0, EVEN_TOTAL - w.shape[1])))
    return w.T.astype(BF16)


def _odd_wT(w_in):
    sizes = (768, 128, 128, 512, 64, 8, 768, 256, 256)
    offs = np.cumsum((0,) + sizes)
    (c_q, c_k, c_v, c_qi, c_ki, c_wi, c_z, m_q, m_z) = [w_in[:, offs[n]:offs[n + 1]] for n in range(len(sizes))]
    w = jnp.concatenate([c_q, m_q, c_z, m_z, c_qi, c_k, c_v, c_ki, c_wi], axis=1)
    w = jnp.pad(w, ((0, 0), (0, ODD_TOTAL - w.shape[1])))
    return w.T.astype(BF16)


def _keys(cols, new_tok, pool, page_table, tile, mode):
    if cols.sample:
        return _gather(pool.reshape(pool.shape[0], pool.shape[1], -1), page_table, new_tok, mode, tile)
    return _kprep(new_tok, tile, mode)


def _values(cols, new_tok, pool, page_table, tile):
    if cols.sample:
        return _gather(pool.reshape(pool.shape[0], pool.shape[1], -1), page_table, new_tok, 'v', tile)
    return _vT_tiles(new_tok, tile)


def _raw_rows(cols, new_tok, pool, page_table):
    if cols.sample:
        return _gather(pool.reshape(pool.shape[0], pool.shape[1], -1), page_table, None, 'raw', KV_TILE)
    return new_tok


def _gate_cols(cols, gT):
    return [cols.per_token(gT, N_B_HEADS)]


def _even_layer(cols, xT, mk, mvT, past, page_table, wT, woT, g, b, cmpw, alpha):
    R = EVEN_ROWS
    nb = cols.nb
    hT = _mm(wT, xT, 640, min(1024, xT.shape[1]))
    names = ('a_k', 'a_v', 'b_kc', 'b_vc', 'b_ks', 'b_vs')
    new = [cols.tokens(hT[R[nm]:R[nm] + 128]) for nm in names]
    kw_new, vw_new = cols.tokens(hT[R['b_kw']:R['b_kw'] + 128]), cols.tokens(hT[R['b_vw']:R['b_vw'] + 128])
    pools = past[:6] if cols.sample else [None] * 6
    if cols.sample:
        kw = jnp.concatenate([past[6].reshape(nb, -1, 128), kw_new], axis=1)
        vw = jnp.concatenate([past[7].reshape(nb, -1, 128), vw_new], axis=1)
        n_win = past[6].shape[1]
        kw_base = cols.q_off - n_win
        win_out = (kw[:, -n_win:], vw[:, -n_win:])
        pad = -kw.shape[1] % 1024
        kw = jnp.pad(kw, ((0, 0), (0, pad), (0, 0)))
        vw = jnp.pad(vw, ((0, 0), (0, pad), (0, 0)))
    else:
        kw, vw, kw_base = kw_new, vw_new, 0
        n_win = min(WINDOW, cols.t)
        win_out = (kw[:, -n_win:], vw[:, -n_win:])

    sl = _alibi(N_A_HEADS + N_B_HEADS)
    ak, amean = _keys(cols, new[0], pools[0], page_table, KV_TILE, 'k')
    avT = _values(cols, new[1], pools[1], page_table, KV_TILE)
    qa, qf, meta = cols.columns(hT[R['a_q']:R['a_q'] + 384], N_A_HEADS, sl[0::2])
    o_a = _moba(cols, qa, qf, meta, amean, ak, avT)

    pe2k, w1k, w2k = cmpw[0]
    pe2v, w1v, w2v = cmpw[1]
    n_cmp = max(cols.q_off, cols.t) // CMP_STRIDE
    kc_raw = _raw_rows(cols, new[2], pools[2], page_table).reshape(nb, -1, 2048)
    vc_raw = _raw_rows(cols, new[3], pools[3], page_table).reshape(nb, -1, 2048)
    kc = _compress(kc_raw, pe2k, w1k, w2k, n_cmp).astype(BF16)
    vcT = _compress(vc_raw, pe2v, w1v, w2v, n_cmp).transpose(0, 2, 1).astype(BF16)
    ks = _keys(cols, new[4], pools[4], page_table, SEL_TILE, 'ks')
    vsT = _values(cols, new[5], pools[5], page_table, SEL_TILE)
    kwt = _kprep(kw, KV_TILE, 'ks')
    qa, _, meta = cols.columns(hT[R['b_q']:R['b_q'] + 384], N_B_HEADS, sl[1::2],
                               extra=_gate_cols(cols, hT[R['b_g']:R['b_g'] + 18]))
    o_b = _nsa(cols, qa, meta, kc, vcT, ks, vsT, kwt, _vT_tiles(vw, KV_TILE), kw_base)

    o_m = _mem_branch(cols, hT, R['m_q'], mk, mvT)
    parts = [cols.collapse(o_a, N_A_HEADS), cols.collapse(o_b, N_B_HEADS), o_m]
    yT = _outln(parts, hT, R['z'], xT, woT, g, b, alpha, min(512, xT.shape[1]))
    state = [n.reshape(nb, -1, N_KV, HEAD_DIM) for n in new]
    state += [w.reshape(nb, -1, N_KV, HEAD_DIM) for w in win_out]
    return yT, state


def _odd_layer(cols, xT, mk, mvT, past, page_table, wT, woT, g, b, alpha):
    R = ODD_ROWS
    nb = cols.nb
    hT = _mm(wT, xT, 768, min(1024, xT.shape[1]))
    new = [cols.tokens(hT[R['c_k']:R['c_k'] + 128]), cols.tokens(hT[R['c_v']:R['c_v'] + 128]),
           cols.tokens(hT[R['c_ki']:R['c_ki'] + IDX_DIM])]
    pools = past if cols.sample else [None] * 3
    topk = min(DSA_TOPK, (cols.q_off + cols.t) // 4)
    ck = _keys(cols, new[0], pools[0], page_table, KV_TILE, 'ks')
    cvT = _values(cols, new[1], pools[1], page_table, KV_TILE)
    if cols.sample:
        ki = _gather(pools[2], page_table, new[2], 'i', KV_TILE)
    else:
        ki = new[2].reshape(nb, -1, KV_TILE, IDX_DIM).astype(BF16)
    qa, _, meta = cols.columns(hT[R['c_q']:R['c_q'] + 768], N_C_HEADS, _alibi(N_C_HEADS))
    if cols.sample:
        qi = cols.tile_cols(hT[R['c_qi']:R['c_qi'] + 512])
        wi = cols.tile_cols(hT[R['c_wi']:R['c_wi'] + 8])
        o_c = _dsa(cols, qi, 0, wi, 0, ki, qa, meta, ck, cvT, topk)
    else:
        o_c = _dsa(cols, hT, R['c_qi'], hT, R['c_wi'], ki, qa, meta, ck, cvT, topk)
    o_m = _mem_branch(cols, hT, R['m_q'], mk, mvT)
    parts = [cols.collapse(o_c, N_C_HEADS), o_m]
    yT = _outln(parts, hT, R['z'], xT, woT, g, b, alpha, min(512, xT.shape[1]))
    state = [new[0].reshape(nb, -1, N_KV, HEAD_DIM), new[1].reshape(nb, -1, N_KV, HEAD_DIM), new[2]]
    return yT, state


def kernel(x_prompt, x_sample, cache_a_k, cache_a_v, cache_b_cmp_k, cache_b_cmp_v, cache_b_sel_k, cache_b_sel_v,
           state_b_win_k, state_b_win_v, cache_c_k, cache_c_v, cache_c_idx_k, cache_mem_k, cache_mem_v,
           page_table, mem_prompt, w_in_even, w_out_even, w_in_odd, w_out_odd, w_mem_kv, ln_g, ln_b,
           cmp_pe, cmp_w1, cmp_w2):
    bp, seq, d = x_prompt.shape
    bs = x_sample.shape[0]
    depth = w_mem_kv.shape[0]
    n_mem = mem_prompt.shape[1]
    past_len = page_table.shape[1] * cache_a_k.shape[2]
    alpha = (2.0 * depth) ** 0.25
    assert x_sample.shape[1] == 1 and bs <= 128 and seq % 1024 == 0 and past_len % 1024 == 0

    gp = _Cols(bp, seq, 0)
    gs = _Cols(bs, 1, past_len)
    xp = x_prompt.reshape(bp * seq, d).T
    xs = jnp.pad(x_sample.reshape(bs, d).T, ((0, 0), (0, 128 - bs)))
    even_paged = (cache_a_k, cache_a_v, cache_b_cmp_k, cache_b_cmp_v, cache_b_sel_k, cache_b_sel_v)
    odd_paged = (cache_c_k, cache_c_v, cache_c_idx_k)
    ev_p, ev_s, od_p, od_s, mk_list, mv_list = [], [], [], [], [], []
    for layer in range(depth):
        j = layer // 2
        mem_kv = _mm(mem_prompt.reshape(bp * n_mem, d), w_mem_kv[layer], bp * n_mem, 2 * N_MEM_HEADS * HEAD_DIM)
        mem_kv = mem_kv.reshape(bp, n_mem, 2, N_MEM_HEADS, HEAD_DIM)
        mk_p, mv_p = mem_kv[:, :, 0], mem_kv[:, :, 1]
        mk_list.append(mk_p)
        mv_list.append(mv_p)
        mem_p = (mk_p.transpose(0, 2, 1, 3), mv_p.transpose(0, 2, 3, 1))
        mem_s = (cache_mem_k[layer].transpose(0, 2, 1, 3), cache_mem_v[layer].transpose(0, 2, 3, 1))
        g = ln_g[layer].reshape(d, 1)
        b = ln_b[layer].reshape(d, 1)
        if layer % 2 == 0:
            wT = _even_wT(w_in_even[j])
            woT = w_out_even[j].T.astype(BF16)
            cmpw = [_cmp_weights(cmp_pe[j, c], cmp_w1[j, c], cmp_w2[j, c]) for c in range(2)]
            past = tuple(c[j] for c in even_paged) + (state_b_win_k[j], state_b_win_v[j])
            xp, st_p = _even_layer(gp, xp, *mem_p, None, None, wT, woT, g, b, cmpw, alpha)
            xs, st_s = _even_layer(gs, xs, *mem_s, past, page_table, wT, woT, g, b, cmpw, alpha)
            ev_p.append(st_p)
            ev_s.append(st_s)
        else:
            wT = _odd_wT(w_in_odd[j])
            woT = w_out_odd[j].T.astype(BF16)
            past = tuple(c[j] for c in odd_paged)
            xp, st_p = _odd_layer(gp, xp, *mem_p, None, None, wT, woT, g, b, alpha)
            xs, st_s = _odd_layer(gs, xs, *mem_s, past, page_table, wT, woT, g, b, alpha)
            od_p.append(st_p)
            od_s.append(st_s)
    y_p = xp.T.reshape(bp, seq, d)
    y_s = xs[:, :bs].T.reshape(bs, 1, d)
    ev_p = [jnp.stack(l) for l in zip(*ev_p)]
    ev_s = [jnp.stack(l) for l in zip(*ev_s)]
    od_p = [jnp.stack(l) for l in zip(*od_p)]
    od_s = [jnp.stack(l) for l in zip(*od_s)]
    out = [y_p, y_s]
    for p, s in zip(ev_p, ev_s):
        out += [p, s]
    for p, s in zip(od_p, od_s):
        out += [p, s]
    out += [jnp.stack(mk_list), jnp.stack(mv_list)]
    return tuple(out)
```

```python
import functools

import numpy as np
import jax
import jax.numpy as jnp
from jax import lax
from jax.experimental import pallas as pl
from jax.experimental.pallas import tpu as pltpu

F32 = jnp.float32
BF16 = jnp.bfloat16
I32 = jnp.int32

HEAD_DIM = 64
N_MEM_HEADS = 4
N_A_HEADS = 6
N_B_HEADS = 6
N_C_HEADS = 12
N_KV = 2
N_IDX_HEADS = 8
IDX_DIM = 64
MOBA_BLOCK = 256
MOBA_TOPK = 3
CMP_STRIDE = 16
CMP_LEN = 32
CMP_HIDDEN = 128
SEL_BLOCK = 64
SEL_TOPK = 15
WINDOW = 512
DSA_TOPK = 256
ATTN_SCALE = HEAD_DIM ** -0.5
IDX_SCALE = IDX_DIM ** -0.5
LN_EPS = 1e-5
LOG2E = 1.4426950408889634
NEG = -1e30
SENTINEL = -3e38
INT_MIN = -(2 ** 31)

KV_TILE = 256
SEL_TILE = 512
KD = 256
FEAT_POS = 128
FEAT_SEL = 144
COUNT_GROUP = 4
HEADS_PER_STEP = 3
SAMPLE_COLS = 128
PAGES_PER_STEP = 8
GATHER_PAGES = 16
VMEM_LIMIT = 56 * 1024 * 1024

EVEN_ROWS = dict(a_q=0, b_q=384, m_q=768, z=1024, a_k=2048, a_v=2176, b_kc=2304, b_vc=2432,
                 b_ks=2560, b_vs=2688, b_kw=2816, b_vw=2944, b_g=3072)
EVEN_TOTAL = 3200
ODD_ROWS = dict(c_q=0, m_q=768, z=1024, c_qi=2048, c_k=2560, c_v=2688, c_ki=2816, c_wi=2880)
ODD_TOTAL = 3072


def _cparams(sem):
    return pltpu.CompilerParams(dimension_semantics=sem, vmem_limit_bytes=VMEM_LIMIT)


def _alibi(n):
    return 2.0 ** (-8.0 * np.arange(1, n + 1) / n)


def _mm_kernel(a_ref, b_ref, o_ref):
    o_ref[...] = jnp.dot(a_ref[...].astype(BF16), b_ref[...].astype(BF16), preferred_element_type=F32)


def _mm(a, b, bm, bn):
    m, k = a.shape
    n = b.shape[1]
    return pl.pallas_call(
        _mm_kernel, grid=(n // bn, m // bm),
        in_specs=[pl.BlockSpec((bm, k), lambda j, i: (i, 0)), pl.BlockSpec((k, bn), lambda j, i: (0, j))],
        out_specs=pl.BlockSpec((bm, bn), lambda j, i: (i, j)),
        out_shape=jax.ShapeDtypeStruct((m, n), F32),
        compiler_params=_cparams(("parallel", "parallel")), name="mm")(a, b)


def _outln_kernel(*refs, n_parts, alpha):
    o_refs = refs[:n_parts]
    z_ref, x_ref, w_ref, g_ref, b_ref, y_ref = refs[n_parts:]
    o = jnp.concatenate([r[...] for r in o_refs], axis=0)
    z = z_ref[...]
    mixed = (o * (z / (1.0 + jnp.exp(-z)))).astype(BF16)
    y = alpha * x_ref[...] + jnp.dot(w_ref[...], mixed, preferred_element_type=F32)
    mu = jnp.mean(y, axis=0, keepdims=True)
    d = y - mu
    var = jnp.mean(d * d, axis=0, keepdims=True)
    y_ref[...] = d * lax.rsqrt(var + LN_EPS) * g_ref[...] + b_ref[...]


def _outln(o_parts, hT, z_row, xT, woT, g, b, alpha, bn):
    d, n = xT.shape
    zb = z_row // d
    in_specs = [pl.BlockSpec((p.shape[0], bn), lambda j: (0, j)) for p in o_parts]
    in_specs += [pl.BlockSpec((d, bn), lambda j: (zb, j)),
                 pl.BlockSpec((d, bn), lambda j: (0, j)),
                 pl.BlockSpec((d, d), lambda j: (0, 0)),
                 pl.BlockSpec((d, 1), lambda j: (0, 0)),
                 pl.BlockSpec((d, 1), lambda j: (0, 0))]
    return pl.pallas_call(
        functools.partial(_outln_kernel, n_parts=len(o_parts), alpha=alpha), grid=(n // bn,),
        in_specs=in_specs, out_specs=pl.BlockSpec((d, bn), lambda j: (0, j)),
        out_shape=jax.ShapeDtypeStruct((d, n), F32),
        compiler_params=_cparams(("parallel",)), name="outln")(*o_parts, hT, xT, woT, g, b)


def _key_features(tile):
    r = lax.broadcasted_iota(I32, (tile, 128), 0)
    c = lax.broadcasted_iota(I32, (tile, 128), 1)
    sel_c = c - (FEAT_SEL - FEAT_POS)
    feat = jnp.where(c < 3, r >> 4, jnp.where(c < 6, r & 15, jnp.where(sel_c == (r >> 6), 1, 0)))
    return feat.astype(F32).astype(BF16)


def _emit_tiles(x, outs, mode, tile, t0=0):
    nt = x.shape[0] // tile
    if mode == 'raw':
        outs[0][0, t0 * tile:(t0 + nt) * tile, :] = x
        return
    feat = _key_features(tile) if mode in ('k', 'ks') else None
    for t in range(nt):
        xt = x[t * tile:(t + 1) * tile]
        if mode in ('k', 'ks'):
            outs[0][0, t0 + t] = jnp.concatenate([xt.astype(BF16), feat], axis=1)
        elif mode == 'v':
            outs[0][0, t0 + t] = jnp.transpose(xt).astype(BF16)
        else:
            outs[0][0, t0 + t] = xt.astype(BF16)
    if mode == 'k':
        nblk = x.shape[0] // MOBA_BLOCK
        outs[1][0, 0, t0:t0 + nblk, :] = jnp.mean(x.reshape(nblk, MOBA_BLOCK, x.shape[1]), axis=1)


def _zero_tiles(outs, mode, tile, t0, t1):
    for t in range(t0, t1):
        outs[0][0, t] = jnp.zeros(outs[0].shape[2:], outs[0].dtype)
    if mode == 'k':
        outs[1][0, 0, t0:t1, :] = jnp.zeros((t1 - t0, outs[1].shape[3]), F32)


def _tile_outputs(mode, nb, n_steps, rows, tile, c):
    nt = rows // tile
    if mode == 'raw':
        return ([pl.BlockSpec((1, rows, c), lambda b, j, *_: (b, j, 0))],
                [jax.ShapeDtypeStruct((nb, n_steps * rows, c), F32)])
    if mode in ('k', 'ks'):
        shape, blk = (nb, n_steps * nt, tile, KD), (1, nt, tile, KD)
    elif mode == 'v':
        shape, blk = (nb, n_steps * nt, c, tile), (1, nt, c, tile)
    else:
        shape, blk = (nb, n_steps * nt, tile, c), (1, nt, tile, c)
    specs = [pl.BlockSpec(blk, lambda b, j, *_: (b, j, 0, 0))]
    shapes = [jax.ShapeDtypeStruct(shape, BF16)]
    if mode == 'k':
        nblk = rows // MOBA_BLOCK
        specs.append(pl.BlockSpec((1, 1, nblk, c), lambda b, j, *_: (b, j, 0, 0)))
        shapes.append(jax.ShapeDtypeStruct((nb, n_steps, nblk, c), F32))
    return specs, shapes


def _gather_kernel(pt_ref, *refs, npg, mode, tile, last, has_tail):
    pool_refs = refs[:npg]
    outs = refs[npg + (1 if has_tail else 0):]
    j = pl.program_id(1)

    def token_major(p):
        c = p.shape[0]
        if c == p.shape[1]:
            return jnp.transpose(p)
        return jnp.transpose(jnp.concatenate([p, jnp.zeros((p.shape[1] - c, p.shape[1]), p.dtype)], axis=0))[:, :c]

    def pages():
        if mode == 'v':
            xT = jnp.concatenate([r[0] for r in pool_refs], axis=1)
            for t in range(xT.shape[1] // tile):
                outs[0][0, t] = xT[:, t * tile:(t + 1) * tile].astype(BF16)
        else:
            _emit_tiles(jnp.concatenate([token_major(r[0]) for r in pool_refs], axis=0), outs, mode, tile)

    if not has_tail:
        pages()
        return
    pl.when(j < last)(pages)

    @pl.when(j == last)
    def _():
        _emit_tiles(refs[npg][0], outs, mode, tile)
        if mode != 'raw':
            _zero_tiles(outs, mode, tile, 1, outs[0].shape[1])


def _gather(pool, page_table, new_tok, mode, tile):
    nb, n_pages = page_table.shape
    n_pool, c, page_tokens = pool.shape
    npg = GATHER_PAGES if n_pages % GATHER_PAGES == 0 else PAGES_PER_STEP
    last = n_pages // npg
    has_tail = new_tok is not None
    rows = npg * page_tokens
    assert rows % tile == 0 and (has_tail or mode == 'raw')

    def pool_spec(k):
        return pl.BlockSpec((1, c, page_tokens),
                            lambda b, j, pt: (pt[b, jnp.minimum(j * npg + k, n_pages - 1)], 0, 0))

    in_specs = [pool_spec(k) for k in range(npg)]
    args = [pool] * npg
    if has_tail:
        tail = jnp.zeros((nb, tile, c), F32).at[:, 0:1, :].set(new_tok)
        in_specs.append(pl.BlockSpec((1, tile, c), lambda b, j, pt: (b, 0, 0)))
        args.append(tail)
    n_steps = last + (1 if has_tail else 0)
    out_specs, out_shape = _tile_outputs(mode, nb, n_steps, rows, tile, c)
    grid_spec = pltpu.PrefetchScalarGridSpec(num_scalar_prefetch=1, grid=(nb, n_steps), in_specs=in_specs,
                                             out_specs=out_specs)
    out = pl.pallas_call(
        functools.partial(_gather_kernel, npg=npg, mode=mode, tile=tile, last=last, has_tail=has_tail),
        grid_spec=grid_spec, out_shape=out_shape,
        compiler_params=_cparams(("parallel", "arbitrary")), name="gather_" + mode)(page_table, *args)
    if mode == 'k':
        return out[0], out[1].reshape(nb, -1, c)
    return out[0]


def _kprep_kernel(x_ref, *outs, mode, tile):
    _emit_tiles(x_ref[0], outs, mode, tile)


def _kprep(x, tile, mode='k'):
    nb, L, c = x.shape
    rows = 1024
    out_specs, out_shape = _tile_outputs(mode, nb, L // rows, rows, tile, c)
    out = pl.pallas_call(
        functools.partial(_kprep_kernel, mode=mode, tile=tile), grid=(nb, L // rows),
        in_specs=[pl.BlockSpec((1, rows, c), lambda b, j: (b, j, 0))],
        out_specs=out_specs, out_shape=out_shape,
        compiler_params=_cparams(("parallel", "parallel")), name="kprep")(x)
    if mode == 'k':
        return out[0], out[1].reshape(nb, -1, c)
    return out[0]


def _vT_tiles(v, tile):
    nb, L, c = v.shape
    return v.reshape(nb, L // tile, tile, c).transpose(0, 1, 3, 2).astype(BF16)


class _Cols:
    def __init__(self, nb, t, q_off):
        self.nb, self.t, self.q_off = nb, t, q_off
        self.sample = q_off > 0
        self.tq = 1 if self.sample else KV_TILE
        self.nq = 1 if self.sample else t // KV_TILE
        self.span = 1 if self.sample else KV_TILE

    def n_hc(self, heads):
        return 1 if self.sample else heads // HEADS_PER_STEP

    def step_cols(self):
        return SAMPLE_COLS if self.sample else HEADS_PER_STEP * self.tq

    def head_group(self, heads):
        return np.arange(heads) // (heads // N_KV)

    def per_head(self, vals, heads):
        vals = jnp.asarray(vals, F32)
        k = vals.shape[0]
        if self.sample:
            return jnp.tile(jnp.pad(vals, ((0, 0), (0, SAMPLE_COLS - heads))), (1, self.nb))
        n_hc = heads // HEADS_PER_STEP
        shape = (k, self.nb, self.nq, n_hc, HEADS_PER_STEP, self.tq)
        return jnp.broadcast_to(vals.reshape(k, 1, 1, n_hc, HEADS_PER_STEP, 1), shape).reshape(k, -1)

    def per_token(self, rows, heads):
        nb, nq, tq = self.nb, self.nq, self.tq
        k = rows.shape[0] // heads
        if self.sample:
            x = rows[:, :nb].reshape(heads, k, nb).transpose(1, 2, 0)
            return jnp.pad(x, ((0, 0), (0, 0), (0, SAMPLE_COLS - heads))).reshape(k, nb * SAMPLE_COLS)
        n_hc = heads // HEADS_PER_STEP
        x = rows.reshape(n_hc, HEADS_PER_STEP, k, nb, nq, tq).transpose(2, 3, 4, 0, 1, 5)
        return x.reshape(k, -1)

    def qpos_row(self, heads):
        if self.sample:
            return jnp.full((1, self.nb * SAMPLE_COLS), self.q_off, F32)
        pos = (jnp.arange(self.nq, dtype=F32)[:, None] * self.tq + jnp.arange(self.tq, dtype=F32)[None, :])
        shape = (1, self.nb, self.nq, heads // HEADS_PER_STEP, HEADS_PER_STEP, self.tq)
        return jnp.broadcast_to(pos.reshape(1, 1, self.nq, 1, 1, self.tq), shape).reshape(1, -1)

    def columns(self, rows, heads, slopes, extra=()):
        x = self.per_token(rows, heads)
        grp = self.head_group(heads)
        gsel = self.per_head(np.stack([grp == 0, grp == 1]).astype(np.float32), heads)
        qf = jnp.concatenate([x * gsel[0:1], x * gsel[1:2]], axis=0)
        s2 = np.asarray(slopes, np.float64) * LOG2E
        hi = s2.astype(jnp.bfloat16).astype(np.float64)
        mid = (s2 - hi).astype(jnp.bfloat16).astype(np.float64)
        lo = (s2 - hi - mid).astype(jnp.bfloat16).astype(np.float64)
        pieces = np.stack([16 * hi, 16 * mid, 16 * lo, hi, mid, lo] + [np.zeros_like(hi)] * 10)
        qa = jnp.concatenate([(qf * (ATTN_SCALE * LOG2E)).astype(BF16), self.per_head(pieces, heads).astype(BF16),
                              jnp.zeros((KD - FEAT_POS - 16, x.shape[1]), BF16)], axis=0)
        meta = [self.per_head((hi + mid + lo)[None, :], heads), self.qpos_row(heads)] + list(extra)
        meta.append(jnp.zeros((8 - sum(m.shape[0] for m in meta), x.shape[1]), F32))
        return qa, qf, jnp.concatenate(meta, axis=0)

    def collapse(self, o, heads):
        nb, nq, tq = self.nb, self.nq, self.tq
        if self.sample:
            o = o.reshape(N_KV, HEAD_DIM, nb, SAMPLE_COLS)
            grp = self.head_group(heads)
            x = jnp.stack([o[grp[h], :, :, h] for h in range(heads)])
            return jnp.pad(x.reshape(heads * HEAD_DIM, nb), ((0, 0), (0, 128 - nb)))
        n_hc = heads // HEADS_PER_STEP
        o = o.reshape(HEAD_DIM, nb, nq, n_hc, HEADS_PER_STEP, tq).transpose(3, 4, 0, 1, 2, 5)
        return o.reshape(heads * HEAD_DIM, nb * nq * tq)

    def tokens(self, rows):
        n = rows.shape[0]
        if self.sample:
            return rows[:, :self.nb].T.reshape(self.nb, 1, n)
        return rows.reshape(n, self.nb, self.t).transpose(1, 2, 0)

    def tile_cols(self, rows):
        if not self.sample:
            return rows
        n = rows.shape[0]
        return jnp.broadcast_to(rows[:, :self.nb, None], (n, self.nb, SAMPLE_COLS)).reshape(n, self.nb * SAMPLE_COLS)

    def from_tile_cols(self, o):
        if not self.sample:
            return o
        n = o.shape[0]
        return jnp.pad(o.reshape(n, self.nb, SAMPLE_COLS)[:, :, 0], ((0, 0), (0, 128 - self.nb)))


def _softmax_step(s, c, ok, vT, m, l, acc_ref):
    smax = jnp.max(s, axis=0, keepdims=True)
    live = smax > 0.5 * NEG
    if ok is not None:
        live = live & _row_flag(ok, s.shape[1])
    m_new = jnp.where(live, jnp.maximum(m, smax + c), m)
    shift = jnp.where(live, m_new - c, -NEG)
    p = jnp.exp2(s - shift)
    alpha = jnp.exp2(m - m_new)
    l = alpha * l + jnp.sum(p, axis=0, keepdims=True)
    acc_ref[...] = alpha * acc_ref[...] + jnp.dot(vT, p.astype(BF16), preferred_element_type=F32)
    return m_new, l


def _row_flag(ok, n):
    if ok.ndim == 0:
        return (jnp.zeros((1, n), I32) + ok.astype(I32)) > 0
    return ok


def _group(mixed):
    return 4 if mixed else 2


def _softmax_init(n):
    return jnp.full((1, n), NEG, F32), jnp.zeros((1, n), F32)


def _tile_groups(n_tiles, step, carry, group):
    def body(u, carry):
        carry = step(group * u, 0, None, carry)
        for j in range(1, group):
            t = group * u + j
            carry = step(jnp.minimum(t, n_tiles - 1), j, t < n_tiles, carry)
        return carry
    return lax.fori_loop(0, (n_tiles + group - 1) // group, body, carry)


def _value_rows(vT_ref, n, g, mixed):
    if mixed:
        return vT_ref[0, n]
    return vT_ref[0, n, pl.ds(pl.multiple_of(g * HEAD_DIM, HEAD_DIM), HEAD_DIM), :]


def _top_rows(vals, rowi, k):
    sel = jnp.zeros(vals.shape, F32)
    for _ in range(k):
        m = jnp.max(vals, axis=0, keepdims=True)
        first = jnp.min(jnp.where(vals == m, rowi, 1 << 20), axis=0, keepdims=True)
        pick = (rowi == first) & (m > SENTINEL)
        sel = jnp.where(pick, 1.0, sel)
        vals = jnp.where(pick, SENTINEL, vals)
    return sel


def _moba_kernel(qa_ref, qf_ref, meta_ref, kmean_ref, k_ref, vT_ref, o_ref, sel_ref, acc_ref, *, span, q_off, mixed):
    i, g = pl.program_id(1), pl.program_id(2)
    n = qa_ref.shape[1]
    own = (q_off + i * span) // MOBA_BLOCK
    slope = meta_ref[0:1, :]
    qpos = meta_ref[1:2, :].astype(I32)
    nrow = kmean_ref.shape[1]
    gate = jnp.dot(kmean_ref[0], qf_ref[...], precision=lax.Precision.HIGHEST, preferred_element_type=F32)
    rowi = lax.broadcasted_iota(I32, (nrow, n), 0)
    sel_ref[...] = _top_rows(jnp.where(rowi < own, gate, SENTINEL), rowi, MOBA_TOPK)
    qa = qa_ref[...]
    acc_ref[...] = jnp.zeros(acc_ref.shape, F32)

    def step(t, slot, ok, carry):
        s = jnp.dot(k_ref[0, t], qa, preferred_element_type=F32)
        c = slope * (t * KV_TILE - qpos).astype(F32)
        sel = sel_ref[pl.ds(t, 1), :] > 0.5
        ok = sel if ok is None else sel & _row_flag(ok, n)
        return _softmax_step(s, c, ok, _value_rows(vT_ref, t, g, mixed), *carry, acc_ref)

    m, l = _tile_groups(own, step, _softmax_init(n), _group(mixed))
    s = jnp.dot(k_ref[0, own], qa, preferred_element_type=F32)
    krow = lax.broadcasted_iota(I32, (KV_TILE, n), 0)
    s = jnp.where(krow <= qpos - own * KV_TILE, s, NEG)
    c = slope * (own * KV_TILE - qpos).astype(F32)
    _, l = _softmax_step(s, c, None, _value_rows(vT_ref, own, g, mixed), m, l, acc_ref)
    o_ref[...] = acc_ref[...] / jnp.maximum(l, 1e-30)


def _moba(cols, qa, qf, meta, kmean, k, vT):
    nb, nq = cols.nb, cols.nq
    n_hc, n = cols.n_hc(N_A_HEADS), cols.step_cols()
    nT, nblk = k.shape[1], kmean.shape[1]
    vrows = N_KV * HEAD_DIM if cols.sample else HEAD_DIM
    col = lambda b, i, g: (0, (b * nq + i) * n_hc + g)
    return pl.pallas_call(
        functools.partial(_moba_kernel, span=cols.span, q_off=cols.q_off, mixed=cols.sample), grid=(nb, nq, n_hc),
        in_specs=[pl.BlockSpec((KD, n), col),
                  pl.BlockSpec((N_KV * HEAD_DIM, n), col),
                  pl.BlockSpec((8, n), col),
                  pl.BlockSpec((1, nblk, N_KV * HEAD_DIM), lambda b, i, g: (b, 0, 0)),
                  pl.BlockSpec((1, nT, KV_TILE, KD), lambda b, i, g: (b, 0, 0, 0)),
                  pl.BlockSpec((1, nT, N_KV * HEAD_DIM, KV_TILE), lambda b, i, g: (b, 0, 0, 0))],
        out_specs=pl.BlockSpec((vrows, n), col),
        out_shape=jax.ShapeDtypeStruct((vrows, qa.shape[1]), F32),
        scratch_shapes=[pltpu.VMEM((nblk, n), F32), pltpu.VMEM((vrows, n), F32)],
        compiler_params=_cparams(("parallel", "parallel", "arbitrary")), name="moba")(qa, qf, meta, kmean, k, vT)


def _cmp_kernel(r_ref, pe_ref, w1_ref, w2_ref, o_ref):
    r = r_ref[0]
    n = r.shape[0]
    hid = 2 * CMP_HIDDEN
    u = jnp.dot((r + pe_ref[0:1, :]).astype(BF16), w1_ref[:, 0:hid], preferred_element_type=F32)
    lo = jnp.dot((r + pe_ref[1:2, :]).astype(BF16), w1_ref[:, hid:2 * hid], preferred_element_type=F32)
    pre = u + pltpu.roll(lo, n - 1, 0)
    h = pre / (1.0 + jnp.exp(-pre))
    o_ref[0] = jnp.dot(h.astype(BF16), w2_ref[...], preferred_element_type=F32)


def _compress(raw, pe2, w1big, w2big, n_rows):
    nb = raw.shape[0]
    return pl.pallas_call(
        _cmp_kernel, grid=(nb,),
        in_specs=[pl.BlockSpec((1, n_rows, 2048), lambda b: (b, 0, 0)),
                  pl.BlockSpec((2, 2048), lambda b: (0, 0)),
                  pl.BlockSpec((2048, 4 * CMP_HIDDEN), lambda b: (0, 0)),
                  pl.BlockSpec((2 * CMP_HIDDEN, 128), lambda b: (0, 0))],
        out_specs=pl.BlockSpec((1, n_rows, 128), lambda b: (b, 0, 0)),
        out_shape=jax.ShapeDtypeStruct((nb, n_rows, 128), F32),
        compiler_params=_cparams(("parallel",)), name="compress")(raw, pe2, w1big, w2big)


def _cmp_weights(pe, w1, w2):
    w1r = w1.reshape(2, CMP_STRIDE, HEAD_DIM, CMP_HIDDEN)
    eye = jnp.eye(N_KV, dtype=w1.dtype)
    big = jnp.einsum('hpdj,ge->hpdgej', w1r, eye)
    big = big.transpose(1, 3, 2, 0, 4, 5).reshape(CMP_STRIDE * N_KV * HEAD_DIM, 2 * N_KV * CMP_HIDDEN)
    w2big = jnp.einsum('jd,ge->gjed', w2, eye).reshape(N_KV * CMP_HIDDEN, N_KV * HEAD_DIM)
    pe2 = jnp.broadcast_to(pe.reshape(2, CMP_STRIDE, 1, HEAD_DIM), (2, CMP_STRIDE, N_KV, HEAD_DIM)).reshape(2, 2048)
    return pe2, big.astype(BF16), w2big.astype(BF16)


def _nsa_kernel(qa_ref, meta_ref, kc_ref, vcT_ref, taps_ref, grp_ref, ks_ref, vsT_ref, kw_ref, vwT_ref, o_ref,
                sel_ref, qs_ref, acc_ref, *, span, q_off, kw_base, mixed):
    i, g = pl.program_id(1), pl.program_id(2)
    n = qa_ref.shape[1]
    nq = n if mixed else n // HEADS_PER_STEP
    rep = n // nq
    t0 = q_off + i * span
    slope = meta_ref[0:1, :]
    qpos = meta_ref[1:2, :].astype(I32)
    qa = qa_ref[...]
    n_cmp = kc_ref.shape[1]
    n_tab = sel_ref.shape[0]

    def vrows(ref):
        if mixed:
            return ref[0]
        return ref[0, pl.ds(pl.multiple_of(g * HEAD_DIM, HEAD_DIM), HEAD_DIM), :]

    cend = CMP_STRIDE * lax.broadcasted_iota(I32, (n_cmp, n), 0) + (CMP_LEN - 1)
    cmask = cend <= qpos
    s = jnp.dot(kc_ref[0], qa[0:N_KV * HEAD_DIM, :], preferred_element_type=F32) + slope * (cend - qpos).astype(F32)
    s = jnp.where(cmask, s, NEG)
    p = jnp.where(cmask, jnp.exp2(s - jnp.max(s, axis=0, keepdims=True)), 0.0)
    p = p * (1.0 / jnp.maximum(jnp.sum(p, axis=0, keepdims=True), 1e-30))
    o_c = jnp.dot(vrows(vcT_ref), p.astype(BF16), preferred_element_type=F32)

    if mixed:
        imp = jnp.dot(p, grp_ref[...], precision=lax.Precision.HIGHEST, preferred_element_type=F32)
    else:
        imp = p[:, 0:nq]
        for r in range(1, rep):
            imp = imp + p[:, r * nq:(r + 1) * nq]
    p_slc = jnp.dot(taps_ref[...], imp, precision=lax.Precision.HIGHEST, preferred_element_type=F32)
    p_slc = jnp.concatenate([p_slc, jnp.full((n_tab - p_slc.shape[0], nq), SENTINEL, F32)], axis=0)
    rowi = lax.broadcasted_iota(I32, (n_tab, nq), 0)
    own = qpos[:, 0:nq] >> 6
    sel = _top_rows(jnp.where(rowi < own, p_slc, SENTINEL), rowi, SEL_TOPK)
    sel_ref[...] = jnp.where(rowi == own, 1.0, sel)

    per_tile = SEL_TILE // SEL_BLOCK
    last = (t0 + span - 1) // SEL_TILE
    for slot in range(qs_ref.shape[0]):
        qs_ref[slot] = qa
    acc_ref[...] = jnp.zeros(acc_ref.shape, F32)

    def sel_scores(t, slot):
        rows = sel_ref[pl.ds(pl.multiple_of(t * per_tile, per_tile), per_tile), :]
        bias = jnp.where(rows > 0.5, 0.0, NEG)
        if rep > 1:
            bias = jnp.concatenate([bias] * rep, axis=1)
        bias = jnp.concatenate([bias, jnp.zeros((per_tile, n), F32)], axis=0)
        qs_ref[slot, FEAT_SEL:FEAT_SEL + 2 * per_tile, :] = bias.astype(BF16)
        s = jnp.dot(ks_ref[0, t], qs_ref[slot], preferred_element_type=F32)
        return s, slope * (t * SEL_TILE - qpos).astype(F32)

    def sel_step(t, slot, ok, carry):
        s, c = sel_scores(t, slot)
        return _softmax_step(s, c, ok, _value_rows(vsT_ref, t, g, mixed), *carry, acc_ref)

    m, l = _tile_groups(last, sel_step, _softmax_init(n), qs_ref.shape[0])
    s, c = sel_scores(last, 0)
    srow = lax.broadcasted_iota(I32, (SEL_TILE, n), 0)
    s = jnp.where(srow <= qpos - last * SEL_TILE, s, NEG)
    _, l = _softmax_step(s, c, None, _value_rows(vsT_ref, last, g, mixed), m, l, acc_ref)
    o_s = acc_ref[...] / jnp.maximum(l, 1e-30)

    lo = jnp.maximum(t0 - (WINDOW - 1) - kw_base, 0) // KV_TILE
    hi = jnp.minimum((t0 + span - 1 - kw_base) // KV_TILE + 1, kw_ref.shape[1])
    wrow = lax.broadcasted_iota(I32, (KV_TILE, n), 0)
    acc_ref[...] = jnp.zeros(acc_ref.shape, F32)

    def win_body(t, carry):
        s = jnp.dot(kw_ref[0, t], qa, preferred_element_type=F32)
        base = kw_base + t * KV_TILE
        dist = qpos - base - wrow
        s = jnp.where((dist >= 0) & (dist < WINDOW), s, NEG)
        return _softmax_step(s, slope * (base - qpos).astype(F32), None, _value_rows(vwT_ref, t, g, mixed),
                             *carry, acc_ref)

    _, l = lax.fori_loop(lo, hi, win_body, _softmax_init(n))
    o_w = acc_ref[...] / jnp.maximum(l, 1e-30)

    gt = 1.0 / (1.0 + jnp.exp(-meta_ref[2:5, :]))
    o_ref[...] = gt[0:1, :] * o_c + gt[1:2, :] * o_s + gt[2:3, :] * o_w


def _taps_matrix(n_sel, n_cmp):
    m = np.zeros((n_sel, n_cmp), np.float32)
    for j in range(n_sel):
        for off, w in ((-1, 1.0), (0, 2.0), (1, 2.0), (2, 2.0), (3, 1.0)):
            c = 4 * j + off
            if 0 <= c < n_cmp:
                m[j, c] = w
    return jnp.asarray(m)


def _group_matrix(cols):
    grp = cols.head_group(N_B_HEADS)
    m = np.zeros((SAMPLE_COLS, SAMPLE_COLS), np.float32)
    m[:N_B_HEADS, :N_B_HEADS] = grp[:, None] == grp[None, :]
    return jnp.asarray(m)


def _nsa(cols, qa, meta, kc, vcT, ks, vsT, kw, vwT, kw_base):
    nb, nq = cols.nb, cols.nq
    n_hc, n = cols.n_hc(N_B_HEADS), cols.step_cols()
    nTs, nTw = ks.shape[1], kw.shape[1]
    n_cmp = kc.shape[1]
    n_sel = n_cmp // 4
    n_tab = n_sel + SEL_TILE // SEL_BLOCK
    nsel_cols = n if cols.sample else n // HEADS_PER_STEP
    vrows = N_KV * HEAD_DIM if cols.sample else HEAD_DIM
    col = lambda b, i, g: (0, (b * nq + i) * n_hc + g)
    seq3 = lambda b, i, g: (b, 0, 0)
    seq4 = lambda b, i, g: (b, 0, 0, 0)
    return pl.pallas_call(
        functools.partial(_nsa_kernel, span=cols.span, q_off=cols.q_off, kw_base=kw_base, mixed=cols.sample),
        grid=(nb, nq, n_hc),
        in_specs=[pl.BlockSpec((KD, n), col),
                  pl.BlockSpec((8, n), col),
                  pl.BlockSpec((1, n_cmp, N_KV * HEAD_DIM), seq3),
                  pl.BlockSpec((1, N_KV * HEAD_DIM, n_cmp), seq3),
                  pl.BlockSpec((n_sel, n_cmp), lambda b, i, g: (0, 0)),
                  pl.BlockSpec((SAMPLE_COLS, SAMPLE_COLS), lambda b, i, g: (0, 0)),
                  pl.BlockSpec((1, nTs, SEL_TILE, KD), seq4),
                  pl.BlockSpec((1, nTs, N_KV * HEAD_DIM, SEL_TILE), seq4),
                  pl.BlockSpec((1, nTw, KV_TILE, KD), seq4),
                  pl.BlockSpec((1, nTw, N_KV * HEAD_DIM, KV_TILE), seq4)],
        out_specs=pl.BlockSpec((vrows, n), col),
        out_shape=jax.ShapeDtypeStruct((vrows, qa.shape[1]), F32),
        scratch_shapes=[pltpu.VMEM((n_tab, nsel_cols), F32), pltpu.VMEM((_group(cols.sample), KD, n), BF16), pltpu.VMEM((vrows, n), F32)],
        compiler_params=_cparams(("parallel", "parallel", "arbitrary")), name="nsa")(
            qa, meta, kc, vcT, _taps_matrix(n_sel, n_cmp), _group_matrix(cols), ks, vsT, kw, vwT)


def _dsa_kernel(qi_ref, wi_ref, ki_ref, qa_ref, meta_ref, k_ref, vT_ref, o_ref, key_ref, row_ref, acc_ref,
                *, span, q_off, topk, mixed, hc_per_group):
    i, hc = pl.program_id(1), pl.program_id(2)
    n = qa_ref.shape[1]
    nq = qi_ref.shape[1]
    rep = n // nq
    t0 = q_off + i * span
    n_kv = (t0 + span - 1) // KV_TILE + 1
    krow = lax.broadcasted_iota(I32, (KV_TILE, nq), 0)

    def search(count, shape):
        kf = float(topk)
        thr = jnp.where(count(lambda key, kpos: key >= 0) >= kf, 0, INT_MIN).astype(I32)

        def bit_body(j, thr):
            cand = thr | (1 << (30 - j))
            return jnp.where(count(lambda key, kpos: key >= cand) >= kf, cand, thr)
        thr = lax.fori_loop(0, 31, bit_body, thr)
        need = kf - count(lambda key, kpos: key > thr)

        def tie_search():
            def idx_body(j, jb):
                cand = jb | (1 << (14 - j))
                c = count(lambda key, kpos: (key == thr) & (kpos < cand))
                return jnp.where(c <= need, cand, jb)
            return lax.fori_loop(0, 15, idx_body, jnp.zeros(shape, I32))

        n_ge = count(lambda key, kpos: key >= thr)
        jb = lax.cond(jnp.max(n_ge) > kf, tie_search, lambda: jnp.full(shape, (1 << 15) - 1, I32))
        return thr, jb

    @pl.when(hc == 0)
    def _():
        qpos = t0 + (lax.broadcasted_iota(I32, (1, nq), 1) if span > 1 else jnp.zeros((1, nq), I32))
        w = wi_ref[...] * (IDX_SCALE * N_IDX_HEADS ** -0.5)
        qi = [qi_ref[a * IDX_DIM:(a + 1) * IDX_DIM, :].astype(BF16) for a in range(N_IDX_HEADS)]
        if mixed:
            row_ref[...] = jnp.full(row_ref.shape, INT_MIN, I32)

        def score_body(t, _):
            ki = ki_ref[0, t]
            acc = jnp.zeros((KV_TILE, nq), F32)
            for a in range(N_IDX_HEADS):
                rel = jnp.dot(ki, qi[a], preferred_element_type=F32)
                acc = acc + jnp.maximum(rel, 0.0) * w[a:a + 1, :]
            sc = jnp.where(t * KV_TILE + krow <= qpos, acc, NEG)
            bits = lax.bitcast_convert_type(sc, I32)
            key = bits ^ ((bits >> 31) & 0x7FFFFFFF)
            key_ref[t] = key
            if mixed:
                row_ref[pl.ds(t, 1), :] = jnp.transpose(key)[0:1, :]
            return 0
        lax.fori_loop(0, n_kv, score_body, 0)

        if mixed:
            keys = row_ref[...]
            kpos_r = (lax.broadcasted_iota(I32, keys.shape, 0) * KV_TILE + lax.broadcasted_iota(I32, keys.shape, 1))

            def count(pred):
                c = jnp.sum(jnp.where(pred(keys, kpos_r), 1.0, 0.0), axis=0, keepdims=True)
                return jnp.sum(c, axis=1, keepdims=True)
            thr, jb = search(count, (1, 1))
        else:
            n_grp = (n_kv + COUNT_GROUP - 1) // COUNT_GROUP

            def pad_body(t, _):
                key_ref[t] = jnp.full((KV_TILE, nq), INT_MIN, I32)
                return 0
            lax.fori_loop(n_kv, n_grp * COUNT_GROUP, pad_body, 0)

            def count(pred):
                def body(u, c):
                    for j in range(COUNT_GROUP):
                        t = u * COUNT_GROUP + j
                        hit = pred(key_ref[t], t * KV_TILE + krow)
                        c = c + jnp.sum(jnp.where(hit, 1.0, 0.0), axis=0, keepdims=True)
                    return c
                return lax.fori_loop(0, n_grp, body, jnp.zeros((1, nq), F32))
            thr, jb = search(count, (1, nq))

        def bias_body(t, _):
            key = key_ref[t]
            kpos = t * KV_TILE + krow
            keep = ((key > thr) | ((key == thr) & (kpos < jb))) & (kpos <= qpos)
            key_ref[t] = lax.bitcast_convert_type(jnp.where(keep, 0.0, NEG), I32)
            return 0
        lax.fori_loop(0, n_kv, bias_body, 0)

    g = hc // hc_per_group
    slope = meta_ref[0:1, :]
    qpos_c = meta_ref[1:2, :].astype(I32)
    qa = qa_ref[...]
    acc_ref[...] = jnp.zeros(acc_ref.shape, F32)

    def step(t, slot, ok, carry):
        bias = lax.bitcast_convert_type(key_ref[t], F32)
        if rep > 1:
            bias = jnp.concatenate([bias] * rep, axis=1)
        s = jnp.dot(k_ref[0, t], qa, preferred_element_type=F32) + bias
        c = slope * (t * KV_TILE - qpos_c).astype(F32)
        return _softmax_step(s, c, ok, _value_rows(vT_ref, t, g, mixed), *carry, acc_ref)
    _, l = _tile_groups(n_kv, step, _softmax_init(n), _group(mixed))
    o_ref[...] = acc_ref[...] / jnp.maximum(l, 1e-30)


def _dsa(cols, qi, qi_row, wi, wi_row, ki, qa, meta, k, vT, topk):
    nb, nq = cols.nb, cols.nq
    n_hc, n = cols.n_hc(N_C_HEADS), cols.step_cols()
    ncol = SAMPLE_COLS if cols.sample else cols.tq
    nT = k.shape[1]
    assert nT % COUNT_GROUP == 0
    vrows = N_KV * HEAD_DIM if cols.sample else HEAD_DIM
    qi_b, wi_b = qi_row // (N_IDX_HEADS * IDX_DIM), wi_row // N_IDX_HEADS
    col = lambda b, i, h: (0, (b * nq + i) * n_hc + h)
    seq4 = lambda b, i, h: (b, 0, 0, 0)
    return pl.pallas_call(
        functools.partial(_dsa_kernel, span=cols.span, q_off=cols.q_off, topk=topk, mixed=cols.sample,
                          hc_per_group=max(n_hc // N_KV, 1)),
        grid=(nb, nq, n_hc),
        in_specs=[pl.BlockSpec((N_IDX_HEADS * IDX_DIM, ncol), lambda b, i, h: (qi_b, b * nq + i)),
                  pl.BlockSpec((N_IDX_HEADS, ncol), lambda b, i, h: (wi_b, b * nq + i)),
                  pl.BlockSpec((1, nT, KV_TILE, IDX_DIM), seq4),
                  pl.BlockSpec((KD, n), col),
                  pl.BlockSpec((8, n), col),
                  pl.BlockSpec((1, nT, KV_TILE, KD), seq4),
                  pl.BlockSpec((1, nT, N_KV * HEAD_DIM, KV_TILE), seq4)],
        out_specs=pl.BlockSpec((vrows, n), col),
        out_shape=jax.ShapeDtypeStruct((vrows, qa.shape[1]), F32),
        scratch_shapes=[pltpu.VMEM((nT, KV_TILE, ncol), I32), pltpu.VMEM((-(-nT // 8) * 8, KV_TILE), I32),
                        pltpu.VMEM((vrows, n), F32)],
        compiler_params=_cparams(("parallel", "parallel", "arbitrary")), name="dsa")(
            qi, wi, ki, qa, meta, k, vT)


def _mem_kernel(q_ref, mk_ref, mvT_ref, o_ref):
    for h in range(N_MEM_HEADS):
        rows = slice(h * HEAD_DIM, (h + 1) * HEAD_DIM)
        q = (q_ref[rows, :] * ATTN_SCALE).astype(BF16)
        s = jnp.dot(mk_ref[0, h].astype(BF16), q, preferred_element_type=F32)
        p = jnp.exp(s - jnp.max(s, axis=0, keepdims=True))
        l = jnp.sum(p, axis=0, keepdims=True)
        o_ref[rows, :] = jnp.dot(mvT_ref[0, h].astype(BF16), p.astype(BF16), preferred_element_type=F32) / l


def _mem(hT, q_row, mk, mvT, nb, nq, tq):
    n_mem = mk.shape[2]
    n = N_MEM_HEADS * HEAD_DIM
    qb = q_row // n
    return pl.pallas_call(
        _mem_kernel, grid=(nb, nq),
        in_specs=[pl.BlockSpec((n, tq), lambda b, i: (qb, b * nq + i)),
                  pl.BlockSpec((1, N_MEM_HEADS, n_mem, HEAD_DIM), lambda b, i: (b, 0, 0, 0)),
                  pl.BlockSpec((1, N_MEM_HEADS, HEAD_DIM, n_mem), lambda b, i: (b, 0, 0, 0))],
        out_specs=pl.BlockSpec((n, tq), lambda b, i: (0, b * nq + i)),
        out_shape=jax.ShapeDtypeStruct((n, nb * nq * tq), F32),
        compiler_params=_cparams(("parallel", "parallel")), name="mem")(hT, mk, mvT)


def _mem_branch(cols, hT, q_row, mk, mvT):
    n = N_MEM_HEADS * HEAD_DIM
    if cols.sample:
        o = _mem(cols.tile_cols(hT[q_row:q_row + n]), 0, mk, mvT, cols.nb, 1, SAMPLE_COLS)
        return cols.from_tile_cols(o)
    return _mem(hT, q_row, mk, mvT, cols.nb, cols.nq, cols.tq)


def _even_wT(w_in):
    sizes = (384, 128, 128, 384, 384, 128, 128, 128, 128, 128, 128, 18, 384, 256, 256)
    offs = np.cumsum((0,) + sizes)
    (a_q, a_k, a_v, a_z, b_q, b_kc, b_vc, b_ks, b_vs, b_kw, b_vw, b_g, b_z, m_q, m_z) = [
        w_in[:, offs[n]:offs[n + 1]] for n in range(len(sizes))]
    d = w_in.shape[0]
    bg = jnp.pad(b_g, ((0, 0), (0, 32 - b_g.shape[1])))
    cols = [a_q, b_q, m_q, a_z, b_z, m_z, a_k, a_v, b_kc, b_vc, b_ks, b_vs, b_kw, b_vw, bg]
    w = jnp.concatenate(cols, axis=1)
    w = jnp.pad(w, ((0, 0), (0, EVEN_TOTAL - w.shape[1])))
    return w.T.astype(BF16)


def _odd_wT(w_in):
    sizes = (768, 128, 128, 512, 64, 8, 768, 256, 256)
    offs = np.cumsum((0,) + sizes)
    (c_q, c_k, c_v, c_qi, c_ki, c_wi, c_z, m_q, m_z) = [w_in[:, offs[n]:offs[n + 1]] for n in range(len(sizes))]
    w = jnp.concatenate([c_q, m_q, c_z, m_z, c_qi, c_k, c_v, c_ki, c_wi], axis=1)
    w = jnp.pad(w, ((0, 0), (0, ODD_TOTAL - w.shape[1])))
    return w.T.astype(BF16)


def _keys(cols, new_tok, pool, page_table, tile, mode):
    if cols.sample:
        return _gather(pool, page_table, new_tok, mode, tile)
    return _kprep(new_tok, tile, mode)


def _values(cols, new_tok, pool, page_table, tile):
    if cols.sample:
        return _gather(pool, page_table, new_tok, 'v', tile)
    return _vT_tiles(new_tok, tile)


def _raw_rows(cols, new_tok, pool, page_table):
    if cols.sample:
        return _gather(pool, page_table, None, 'raw', KV_TILE)
    return new_tok


def _gate_cols(cols, gT):
    return [cols.per_token(gT, N_B_HEADS)]


def _even_layer(cols, xT, mk, mvT, past, page_table, wT, woT, g, b, cmpw, alpha):
    R = EVEN_ROWS
    nb = cols.nb
    hT = _mm(wT, xT, 640, min(1024, xT.shape[1]))
    names = ('a_k', 'a_v', 'b_kc', 'b_vc', 'b_ks', 'b_vs')
    new = [cols.tokens(hT[R[nm]:R[nm] + 128]) for nm in names]
    kw_new, vw_new = cols.tokens(hT[R['b_kw']:R['b_kw'] + 128]), cols.tokens(hT[R['b_vw']:R['b_vw'] + 128])
    pools = past[:6] if cols.sample else [None] * 6
    if cols.sample:
        kw = jnp.concatenate([past[6].reshape(nb, -1, 128), kw_new], axis=1)
        vw = jnp.concatenate([past[7].reshape(nb, -1, 128), vw_new], axis=1)
        n_win = past[6].shape[1]
        kw_base = cols.q_off - n_win
        win_out = (kw[:, -n_win:], vw[:, -n_win:])
        pad = -kw.shape[1] % 1024
        kw = jnp.pad(kw, ((0, 0), (0, pad), (0, 0)))
        vw = jnp.pad(vw, ((0, 0), (0, pad), (0, 0)))
    else:
        kw, vw, kw_base = kw_new, vw_new, 0
        n_win = min(WINDOW, cols.t)
        win_out = (kw[:, -n_win:], vw[:, -n_win:])

    sl = _alibi(N_A_HEADS + N_B_HEADS)
    ak, amean = _keys(cols, new[0], pools[0], page_table, KV_TILE, 'k')
    avT = _values(cols, new[1], pools[1], page_table, KV_TILE)
    qa, qf, meta = cols.columns(hT[R['a_q']:R['a_q'] + 384], N_A_HEADS, sl[0::2])
    o_a = _moba(cols, qa, qf, meta, amean, ak, avT)

    pe2k, w1k, w2k = cmpw[0]
    pe2v, w1v, w2v = cmpw[1]
    n_cmp = max(cols.q_off, cols.t) // CMP_STRIDE
    kc_raw = _raw_rows(cols, new[2], pools[2], page_table).reshape(nb, -1, 2048)
    vc_raw = _raw_rows(cols, new[3], pools[3], page_table).reshape(nb, -1, 2048)
    kc = _compress(kc_raw, pe2k, w1k, w2k, n_cmp).astype(BF16)
    vcT = _compress(vc_raw, pe2v, w1v, w2v, n_cmp).transpose(0, 2, 1).astype(BF16)
    ks = _keys(cols, new[4], pools[4], page_table, SEL_TILE, 'ks')
    vsT = _values(cols, new[5], pools[5], page_table, SEL_TILE)
    kwt = _kprep(kw, KV_TILE, 'ks')
    qa, _, meta = cols.columns(hT[R['b_q']:R['b_q'] + 384], N_B_HEADS, sl[1::2],
                               extra=_gate_cols(cols, hT[R['b_g']:R['b_g'] + 18]))
    o_b = _nsa(cols, qa, meta, kc, vcT, ks, vsT, kwt, _vT_tiles(vw, KV_TILE), kw_base)

    o_m = _mem_branch(cols, hT, R['m_q'], mk, mvT)
    parts = [cols.collapse(o_a, N_A_HEADS), cols.collapse(o_b, N_B_HEADS), o_m]
    yT = _outln(parts, hT, R['z'], xT, woT, g, b, alpha, min(512, xT.shape[1]))
    state = [n.reshape(nb, -1, N_KV, HEAD_DIM) for n in new]
    state += [w.reshape(nb, -1, N_KV, HEAD_DIM) for w in win_out]
    return yT, state


def _odd_layer(cols, xT, mk, mvT, past, page_table, wT, woT, g, b, alpha):
    R = ODD_ROWS
    nb = cols.nb
    hT = _mm(wT, xT, 768, min(1024, xT.shape[1]))
    new = [cols.tokens(hT[R['c_k']:R['c_k'] + 128]), cols.tokens(hT[R['c_v']:R['c_v'] + 128]),
           cols.tokens(hT[R['c_ki']:R['c_ki'] + IDX_DIM])]
    pools = past if cols.sample else [None] * 3
    topk = min(DSA_TOPK, (cols.q_off + cols.t) // 4)
    ck = _keys(cols, new[0], pools[0], page_table, KV_TILE, 'ks')
    cvT = _values(cols, new[1], pools[1], page_table, KV_TILE)
    if cols.sample:
        ki = _gather(pools[2], page_table, new[2], 'i', KV_TILE)
    else:
        ki = new[2].reshape(nb, -1, KV_TILE, IDX_DIM).astype(BF16)
    qa, _, meta = cols.columns(hT[R['c_q']:R['c_q'] + 768], N_C_HEADS, _alibi(N_C_HEADS))
    if cols.sample:
        qi = cols.tile_cols(hT[R['c_qi']:R['c_qi'] + 512])
        wi = cols.tile_cols(hT[R['c_wi']:R['c_wi'] + 8])
        o_c = _dsa(cols, qi, 0, wi, 0, ki, qa, meta, ck, cvT, topk)
    else:
        o_c = _dsa(cols, hT, R['c_qi'], hT, R['c_wi'], ki, qa, meta, ck, cvT, topk)
    o_m = _mem_branch(cols, hT, R['m_q'], mk, mvT)
    parts = [cols.collapse(o_c, N_C_HEADS), o_m]
    yT = _outln(parts, hT, R['z'], xT, woT, g, b, alpha, min(512, xT.shape[1]))
    state = [new[0].reshape(nb, -1, N_KV, HEAD_DIM), new[1].reshape(nb, -1, N_KV, HEAD_DIM), new[2]]
    return yT, state


def kernel(x_prompt, x_sample, cache_a_k, cache_a_v, cache_b_cmp_k, cache_b_cmp_v, cache_b_sel_k, cache_b_sel_v,
           state_b_win_k, state_b_win_v, cache_c_k, cache_c_v, cache_c_idx_k, cache_mem_k, cache_mem_v,
           page_table, mem_prompt, w_in_even, w_out_even, w_in_odd, w_out_odd, w_mem_kv, ln_g, ln_b,
           cmp_pe, cmp_w1, cmp_w2):
    bp, seq, d = x_prompt.shape
    bs = x_sample.shape[0]
    depth = w_mem_kv.shape[0]
    n_mem = mem_prompt.shape[1]
    past_len = page_table.shape[1] * cache_a_k.shape[2]
    alpha = (2.0 * depth) ** 0.25
    assert x_sample.shape[1] == 1 and bs <= 128 and seq % 1024 == 0 and past_len % 1024 == 0

    gp = _Cols(bp, seq, 0)
    gs = _Cols(bs, 1, past_len)
    xp = x_prompt.reshape(bp * seq, d).T
    xs = jnp.pad(x_sample.reshape(bs, d).T, ((0, 0), (0, 128 - bs)))
    n_pool = cache_a_k.shape[1]

    def pages(c):
        return jnp.moveaxis(c.reshape(c.shape[:3] + (-1,)), 2, 3).reshape(c.shape[0] * n_pool, -1, c.shape[2])

    even_paged = tuple(pages(c) for c in (cache_a_k, cache_a_v, cache_b_cmp_k, cache_b_cmp_v, cache_b_sel_k,
                                          cache_b_sel_v))
    odd_paged = tuple(pages(c) for c in (cache_c_k, cache_c_v, cache_c_idx_k))
    ev_p, ev_s, od_p, od_s, mk_list, mv_list = [], [], [], [], [], []
    for layer in range(depth):
        j = layer // 2
        mem_kv = _mm(mem_prompt.reshape(bp * n_mem, d), w_mem_kv[layer], bp * n_mem, 2 * N_MEM_HEADS * HEAD_DIM)
        mem_kv = mem_kv.reshape(bp, n_mem, 2, N_MEM_HEADS, HEAD_DIM)
        mk_p, mv_p = mem_kv[:, :, 0], mem_kv[:, :, 1]
        mk_list.append(mk_p)
        mv_list.append(mv_p)
        mem_p = (mk_p.transpose(0, 2, 1, 3), mv_p.transpose(0, 2, 3, 1))
        mem_s = (cache_mem_k[layer].transpose(0, 2, 1, 3), cache_mem_v[layer].transpose(0, 2, 3, 1))
        g = ln_g[layer].reshape(d, 1)
        b = ln_b[layer].reshape(d, 1)
        if layer % 2 == 0:
            wT = _even_wT(w_in_even[j])
            woT = w_out_even[j].T.astype(BF16)
            cmpw = [_cmp_weights(cmp_pe[j, c], cmp_w1[j, c], cmp_w2[j, c]) for c in range(2)]
            past = even_paged + (state_b_win_k[j], state_b_win_v[j])
            xp, st_p = _even_layer(gp, xp, *mem_p, None, None, wT, woT, g, b, cmpw, alpha)
            xs, st_s = _even_layer(gs, xs, *mem_s, past, page_table + j * n_pool, wT, woT, g, b, cmpw, alpha)
            ev_p.append(st_p)
            ev_s.append(st_s)
        else:
            wT = _odd_wT(w_in_odd[j])
            woT = w_out_odd[j].T.astype(BF16)
            past = odd_paged
            xp, st_p = _odd_layer(gp, xp, *mem_p, None, None, wT, woT, g, b, alpha)
            xs, st_s = _odd_layer(gs, xs, *mem_s, past, page_table + j * n_pool, wT, woT, g, b, alpha)
            od_p.append(st_p)
            od_s.append(st_s)
    y_p = xp.T.reshape(bp, seq, d)
    y_s = xs[:, :bs].T.reshape(bs, 1, d)
    ev_p = [jnp.stack(l) for l in zip(*ev_p)]
    ev_s = [jnp.stack(l) for l in zip(*ev_s)]
    od_p = [jnp.stack(l) for l in zip(*od_p)]
    od_s = [jnp.stack(l) for l in zip(*od_s)]
    out = [y_p, y_s]
    for p, s in zip(ev_p, ev_s):
        out += [p, s]
    for p, s in zip(od_p, od_s):
        out += [p, s]
    out += [jnp.stack(mk_list), jnp.stack(mv_list)]
    return tuple(out)
```

```python
import functools

import numpy as np
import jax
import jax.numpy as jnp
from jax import lax
from jax.experimental import pallas as pl
from jax.experimental.pallas import tpu as pltpu

F32 = jnp.float32
BF16 = jnp.bfloat16
I32 = jnp.int32

HEAD_DIM = 64
N_MEM_HEADS = 4
N_A_HEADS = 6
N_B_HEADS = 6
N_C_HEADS = 12
N_KV = 2
N_IDX_HEADS = 8
IDX_DIM = 64
MOBA_BLOCK = 256
MOBA_TOPK = 3
CMP_STRIDE = 16
CMP_LEN = 32
CMP_HIDDEN = 128
SEL_BLOCK = 64
SEL_TOPK = 15
WINDOW = 512
DSA_TOPK = 256
ATTN_SCALE = HEAD_DIM ** -0.5
IDX_SCALE = IDX_DIM ** -0.5
LN_EPS = 1e-5
LOG2E = 1.4426950408889634
NEG = -1e30
SENTINEL = -3e38
INT_MIN = -(2 ** 31)

KV_TILE = 256
SEL_TILE = 512
KD = 256
FEAT_POS = 128
FEAT_SEL = 144
COUNT_GROUP = 4
SAMPLE_COLS = 128
PAGES_PER_STEP = 8
GATHER_PAGES = 16
VMEM_LIMIT = 56 * 1024 * 1024

EVEN_ROWS = dict(a_q=0, b_q=384, m_q=768, z=1024, a_k=2048, a_v=2176, b_kc=2304, b_vc=2432,
                 b_ks=2560, b_vs=2688, b_kw=2816, b_vw=2944, b_g=3072)
EVEN_TOTAL = 3200
ODD_ROWS = dict(c_q=0, m_q=768, z=1024, c_qi=2048, c_k=2560, c_v=2688, c_ki=2816, c_wi=2880)
ODD_TOTAL = 3072


def _cparams(sem):
    return pltpu.CompilerParams(dimension_semantics=sem, vmem_limit_bytes=VMEM_LIMIT)


def _alibi(n):
    return 2.0 ** (-8.0 * np.arange(1, n + 1) / n)


def _mm_kernel(a_ref, b_ref, o_ref, *, precise):
    if precise:
        o_ref[...] = jnp.dot(a_ref[...], b_ref[...], precision=lax.Precision.HIGHEST, preferred_element_type=F32)
    else:
        o_ref[...] = jnp.dot(a_ref[...].astype(BF16), b_ref[...].astype(BF16), preferred_element_type=F32)


def _mm(a, b, bm, bn):
    m, k = a.shape
    n = b.shape[1]
    return pl.pallas_call(
        functools.partial(_mm_kernel, precise=a.dtype == F32 and b.dtype == F32 and n <= 128), grid=(n // bn, m // bm),
        in_specs=[pl.BlockSpec((bm, k), lambda j, i: (i, 0)), pl.BlockSpec((k, bn), lambda j, i: (0, j))],
        out_specs=pl.BlockSpec((bm, bn), lambda j, i: (i, j)),
        out_shape=jax.ShapeDtypeStruct((m, n), F32),
        compiler_params=_cparams(("parallel", "parallel")), name="mm")(a, b)


def _outln_kernel(*refs, n_parts, alpha):
    o_refs = refs[:n_parts]
    z_ref, x_ref, w_ref, g_ref, b_ref, y_ref = refs[n_parts:]
    o = jnp.concatenate([r[...] for r in o_refs], axis=0)
    z = z_ref[...]
    mixed = o * (z / (1.0 + jnp.exp(-z)))
    if w_ref.dtype == F32:
        y = jnp.dot(w_ref[...], mixed, precision=lax.Precision.HIGHEST, preferred_element_type=F32)
    else:
        y = jnp.dot(w_ref[...], mixed.astype(BF16), preferred_element_type=F32)
    y = alpha * x_ref[...] + y
    mu = jnp.mean(y, axis=0, keepdims=True)
    d = y - mu
    var = jnp.mean(d * d, axis=0, keepdims=True)
    y_ref[...] = d * lax.rsqrt(var + LN_EPS) * g_ref[...] + b_ref[...]


def _outln(o_parts, hT, z_row, xT, woT, g, b, alpha, bn):
    d, n = xT.shape
    zb = z_row // d
    in_specs = [pl.BlockSpec((p.shape[0], bn), lambda j: (0, j)) for p in o_parts]
    in_specs += [pl.BlockSpec((d, bn), lambda j: (zb, j)),
                 pl.BlockSpec((d, bn), lambda j: (0, j)),
                 pl.BlockSpec((d, d), lambda j: (0, 0)),
                 pl.BlockSpec((d, 1), lambda j: (0, 0)),
                 pl.BlockSpec((d, 1), lambda j: (0, 0))]
    return pl.pallas_call(
        functools.partial(_outln_kernel, n_parts=len(o_parts), alpha=alpha), grid=(n // bn,),
        in_specs=in_specs, out_specs=pl.BlockSpec((d, bn), lambda j: (0, j)),
        out_shape=jax.ShapeDtypeStruct((d, n), F32),
        compiler_params=_cparams(("parallel",)), name="outln")(*o_parts, hT, xT, woT, g, b)


def _key_features(tile):
    r = lax.broadcasted_iota(I32, (tile, 128), 0)
    c = lax.broadcasted_iota(I32, (tile, 128), 1)
    sel_c = c - (FEAT_SEL - FEAT_POS)
    feat = jnp.where(c < 3, r >> 4, jnp.where(c < 6, r & 15, jnp.where(sel_c == (r >> 6), 1, 0)))
    return feat.astype(F32).astype(BF16)


def _emit_tiles(x, outs, mode, tile, t0=0):
    nt = x.shape[0] // tile
    if mode == 'raw':
        outs[0][0, t0 * tile:(t0 + nt) * tile, :] = x
        return
    feat = _key_features(tile) if mode in ('k', 'ks') else None
    for t in range(nt):
        xt = x[t * tile:(t + 1) * tile]
        if mode in ('k', 'ks'):
            outs[0][0, t0 + t] = jnp.concatenate([xt.astype(BF16), feat], axis=1)
        elif mode == 'v':
            outs[0][0, t0 + t] = jnp.transpose(xt).astype(BF16)
        else:
            outs[0][0, t0 + t] = xt.astype(BF16)
    if mode == 'k':
        nblk = x.shape[0] // MOBA_BLOCK
        outs[1][0, 0, t0:t0 + nblk, :] = jnp.mean(x.reshape(nblk, MOBA_BLOCK, x.shape[1]), axis=1)


def _zero_tiles(outs, mode, tile, t0, t1):
    for t in range(t0, t1):
        outs[0][0, t] = jnp.zeros(outs[0].shape[2:], outs[0].dtype)
    if mode == 'k':
        outs[1][0, 0, t0:t1, :] = jnp.zeros((t1 - t0, outs[1].shape[3]), F32)


def _tile_outputs(mode, nb, n_steps, rows, tile, c):
    nt = rows // tile
    if mode == 'raw':
        return ([pl.BlockSpec((1, rows, c), lambda b, j, *_: (b, j, 0))],
                [jax.ShapeDtypeStruct((nb, n_steps * rows, c), F32)])
    if mode in ('k', 'ks'):
        shape, blk = (nb, n_steps * nt, tile, KD), (1, nt, tile, KD)
    elif mode == 'v':
        shape, blk = (nb, n_steps * nt, c, tile), (1, nt, c, tile)
    else:
        shape, blk = (nb, n_steps * nt, tile, c), (1, nt, tile, c)
    specs = [pl.BlockSpec(blk, lambda b, j, *_: (b, j, 0, 0))]
    shapes = [jax.ShapeDtypeStruct(shape, BF16)]
    if mode == 'k':
        nblk = rows // MOBA_BLOCK
        specs.append(pl.BlockSpec((1, 1, nblk, c), lambda b, j, *_: (b, j, 0, 0)))
        shapes.append(jax.ShapeDtypeStruct((nb, n_steps, nblk, c), F32))
    return specs, shapes


def _gather_kernel(pt_ref, *refs, npg, mode, tile, last, has_tail):
    pool_refs = refs[:npg]
    outs = refs[npg + (1 if has_tail else 0):]
    j = pl.program_id(1)

    def token_major(p):
        c = p.shape[0]
        if c == p.shape[1]:
            return jnp.transpose(p)
        return jnp.transpose(jnp.concatenate([p, jnp.zeros((p.shape[1] - c, p.shape[1]), p.dtype)], axis=0))[:, :c]

    def pages():
        if mode == 'v':
            xT = jnp.concatenate([r[0] for r in pool_refs], axis=1)
            for t in range(xT.shape[1] // tile):
                outs[0][0, t] = xT[:, t * tile:(t + 1) * tile].astype(BF16)
        else:
            _emit_tiles(jnp.concatenate([token_major(r[0]) for r in pool_refs], axis=0), outs, mode, tile)

    if not has_tail:
        pages()
        return
    pl.when(j < last)(pages)

    @pl.when(j == last)
    def _():
        _emit_tiles(refs[npg][0], outs, mode, tile)
        if mode != 'raw':
            _zero_tiles(outs, mode, tile, 1, outs[0].shape[1])


def _gather(pool, page_table, new_tok, mode, tile):
    nb, n_pages = page_table.shape
    n_pool, c, page_tokens = pool.shape
    npg = GATHER_PAGES if n_pages % GATHER_PAGES == 0 else PAGES_PER_STEP
    last = n_pages // npg
    has_tail = new_tok is not None
    rows = npg * page_tokens
    assert rows % tile == 0 and (has_tail or mode == 'raw')

    def pool_spec(k):
        return pl.BlockSpec((1, c, page_tokens),
                            lambda b, j, pt: (pt[b, jnp.minimum(j * npg + k, n_pages - 1)], 0, 0))

    in_specs = [pool_spec(k) for k in range(npg)]
    args = [pool] * npg
    if has_tail:
        tail = jnp.zeros((nb, tile, c), F32).at[:, 0:1, :].set(new_tok)
        in_specs.append(pl.BlockSpec((1, tile, c), lambda b, j, pt: (b, 0, 0)))
        args.append(tail)
    n_steps = last + (1 if has_tail else 0)
    out_specs, out_shape = _tile_outputs(mode, nb, n_steps, rows, tile, c)
    grid_spec = pltpu.PrefetchScalarGridSpec(num_scalar_prefetch=1, grid=(nb, n_steps), in_specs=in_specs,
                                             out_specs=out_specs)
    out = pl.pallas_call(
        functools.partial(_gather_kernel, npg=npg, mode=mode, tile=tile, last=last, has_tail=has_tail),
        grid_spec=grid_spec, out_shape=out_shape,
        compiler_params=_cparams(("parallel", "arbitrary")), name="gather_" + mode)(page_table, *args)
    if mode == 'k':
        return out[0], out[1].reshape(nb, -1, c)
    return out[0]


def _kprep_kernel(x_ref, *outs, mode, tile):
    _emit_tiles(x_ref[0], outs, mode, tile)


def _kprep(x, tile, mode='k'):
    nb, L, c = x.shape
    rows = 1024
    out_specs, out_shape = _tile_outputs(mode, nb, L // rows, rows, tile, c)
    out = pl.pallas_call(
        functools.partial(_kprep_kernel, mode=mode, tile=tile), grid=(nb, L // rows),
        in_specs=[pl.BlockSpec((1, rows, c), lambda b, j: (b, j, 0))],
        out_specs=out_specs, out_shape=out_shape,
        compiler_params=_cparams(("parallel", "parallel")), name="kprep")(x)
    if mode == 'k':
        return out[0], out[1].reshape(nb, -1, c)
    return out[0]


def _vT_tiles(v, tile):
    nb, L, c = v.shape
    return v.reshape(nb, L // tile, tile, c).transpose(0, 1, 3, 2).astype(BF16)


class _Cols:
    def __init__(self, nb, t, q_off):
        self.nb, self.t, self.q_off = nb, t, q_off
        self.sample = q_off > 0
        self.tq = 1 if self.sample else KV_TILE
        self.nq = 1 if self.sample else t // KV_TILE
        self.span = 1 if self.sample else KV_TILE

    def n_hc(self, heads):
        return 1 if self.sample else N_KV

    def step_cols(self, heads):
        return SAMPLE_COLS if self.sample else heads // N_KV * self.tq

    def head_group(self, heads):
        return np.arange(heads) // (heads // N_KV)

    def per_head(self, vals, heads):
        vals = jnp.asarray(vals, F32)
        k = vals.shape[0]
        if self.sample:
            return jnp.tile(jnp.pad(vals, ((0, 0), (0, SAMPLE_COLS - heads))), (1, self.nb))
        hps = heads // N_KV
        shape = (k, self.nb, self.nq, N_KV, hps, self.tq)
        return jnp.broadcast_to(vals.reshape(k, 1, 1, N_KV, hps, 1), shape).reshape(k, -1)

    def per_token(self, rows, heads):
        nb, nq, tq = self.nb, self.nq, self.tq
        k = rows.shape[0] // heads
        if self.sample:
            x = rows[:, :nb].reshape(heads, k, nb).transpose(1, 2, 0)
            return jnp.pad(x, ((0, 0), (0, 0), (0, SAMPLE_COLS - heads))).reshape(k, nb * SAMPLE_COLS)
        x = rows.reshape(N_KV, heads // N_KV, k, nb, nq, tq).transpose(2, 3, 4, 0, 1, 5)
        return x.reshape(k, -1)

    def qpos_row(self, heads):
        if self.sample:
            return jnp.full((1, self.nb * SAMPLE_COLS), self.q_off, F32)
        pos = (jnp.arange(self.nq, dtype=F32)[:, None] * self.tq + jnp.arange(self.tq, dtype=F32)[None, :])
        shape = (1, self.nb, self.nq, N_KV, heads // N_KV, self.tq)
        return jnp.broadcast_to(pos.reshape(1, 1, self.nq, 1, 1, self.tq), shape).reshape(1, -1)

    def columns(self, rows, heads, slopes, extra=()):
        x = self.per_token(rows, heads)
        grp = self.head_group(heads)
        gsel = self.per_head(np.stack([grp == 0, grp == 1]).astype(np.float32), heads)
        qf = jnp.concatenate([x * gsel[0:1], x * gsel[1:2]], axis=0)
        s2 = np.asarray(slopes, np.float64) * LOG2E
        hi = s2.astype(jnp.bfloat16).astype(np.float64)
        mid = (s2 - hi).astype(jnp.bfloat16).astype(np.float64)
        lo = (s2 - hi - mid).astype(jnp.bfloat16).astype(np.float64)
        pieces = np.stack([16 * hi, 16 * mid, 16 * lo, hi, mid, lo] + [np.zeros_like(hi)] * 10)
        qa = jnp.concatenate([(qf * (ATTN_SCALE * LOG2E)).astype(BF16), self.per_head(pieces, heads).astype(BF16),
                              jnp.zeros((KD - FEAT_POS - 16, x.shape[1]), BF16)], axis=0)
        meta = [self.per_head((hi + mid + lo)[None, :], heads), self.qpos_row(heads)] + list(extra)
        meta.append(jnp.zeros((8 - sum(m.shape[0] for m in meta), x.shape[1]), F32))
        return qa, qf, jnp.concatenate(meta, axis=0)

    def collapse(self, o, heads):
        nb, nq, tq = self.nb, self.nq, self.tq
        if self.sample:
            o = o.reshape(N_KV, HEAD_DIM, nb, SAMPLE_COLS)
            grp = self.head_group(heads)
            x = jnp.stack([o[grp[h], :, :, h] for h in range(heads)])
            return jnp.pad(x.reshape(heads * HEAD_DIM, nb), ((0, 0), (0, 128 - nb)))
        o = o.reshape(HEAD_DIM, nb, nq, N_KV, heads // N_KV, tq).transpose(3, 4, 0, 1, 2, 5)
        return o.reshape(heads * HEAD_DIM, nb * nq * tq)

    def tokens(self, rows):
        n = rows.shape[0]
        if self.sample:
            return rows[:, :self.nb].T.reshape(self.nb, 1, n)
        return rows.reshape(n, self.nb, self.t).transpose(1, 2, 0)

    def tile_cols(self, rows):
        if not self.sample:
            return rows
        n = rows.shape[0]
        return jnp.broadcast_to(rows[:, :self.nb, None], (n, self.nb, SAMPLE_COLS)).reshape(n, self.nb * SAMPLE_COLS)

    def from_tile_cols(self, o):
        if not self.sample:
            return o
        n = o.shape[0]
        return jnp.pad(o.reshape(n, self.nb, SAMPLE_COLS)[:, :, 0], ((0, 0), (0, 128 - self.nb)))


def _softmax_step(s, c, ok, vT, m, l, acc_ref):
    smax = jnp.max(s, axis=0, keepdims=True)
    live = smax > 0.5 * NEG
    if ok is not None:
        live = live & _row_flag(ok, s.shape[1])
    m_new = jnp.where(live, jnp.maximum(m, smax + c), m)
    shift = jnp.where(live, m_new - c, -NEG)
    p = jnp.exp2(s - shift)
    alpha = jnp.exp2(m - m_new)
    l = alpha * l + jnp.sum(p, axis=0, keepdims=True)
    acc_ref[...] = alpha * acc_ref[...] + jnp.dot(vT, p.astype(BF16), preferred_element_type=F32)
    return m_new, l


def _row_flag(ok, n):
    if ok.ndim == 0:
        return (jnp.zeros((1, n), I32) + ok.astype(I32)) > 0
    return ok


def _group(mixed):
    return 4 if mixed else 2


def _softmax_init(n):
    return jnp.full((1, n), NEG, F32), jnp.zeros((1, n), F32)


def _tile_groups(n_tiles, step, carry, group):
    def body(u, carry):
        carry = step(group * u, 0, None, carry)
        for j in range(1, group):
            t = group * u + j
            carry = step(jnp.minimum(t, n_tiles - 1), j, t < n_tiles, carry)
        return carry
    return lax.fori_loop(0, (n_tiles + group - 1) // group, body, carry)


def _value_rows(vT_ref, n, g, mixed):
    if mixed:
        return vT_ref[0, n]
    return vT_ref[0, n, pl.ds(pl.multiple_of(g * HEAD_DIM, HEAD_DIM), HEAD_DIM), :]


def _top_rows(vals, rowi, k):
    sel = jnp.zeros(vals.shape, F32)
    for _ in range(k):
        m = jnp.max(vals, axis=0, keepdims=True)
        first = jnp.min(jnp.where(vals == m, rowi, 1 << 20), axis=0, keepdims=True)
        pick = (rowi == first) & (m > SENTINEL)
        sel = jnp.where(pick, 1.0, sel)
        vals = jnp.where(pick, SENTINEL, vals)
    return sel


def _moba_kernel(qa_ref, qf_ref, meta_ref, kmean_ref, k_ref, vT_ref, o_ref, sel_ref, acc_ref, *, span, q_off, mixed):
    i, g = pl.program_id(1), pl.program_id(2)
    n = qa_ref.shape[1]
    own = (q_off + i * span) // MOBA_BLOCK
    slope = meta_ref[0:1, :]
    qpos = meta_ref[1:2, :].astype(I32)
    nrow = kmean_ref.shape[1]
    gate = jnp.dot(kmean_ref[0], qf_ref[...], precision=lax.Precision.HIGHEST, preferred_element_type=F32)
    rowi = lax.broadcasted_iota(I32, (nrow, n), 0)
    sel_ref[...] = _top_rows(jnp.where(rowi < own, gate, SENTINEL), rowi, MOBA_TOPK)
    qa = qa_ref[...]
    acc_ref[...] = jnp.zeros(acc_ref.shape, F32)

    def step(t, slot, ok, carry):
        s = jnp.dot(k_ref[0, t], qa, preferred_element_type=F32)
        c = slope * (t * KV_TILE - qpos).astype(F32)
        sel = sel_ref[pl.ds(t, 1), :] > 0.5
        ok = sel if ok is None else sel & _row_flag(ok, n)
        return _softmax_step(s, c, ok, _value_rows(vT_ref, t, g, mixed), *carry, acc_ref)

    m, l = _tile_groups(own, step, _softmax_init(n), _group(mixed))
    s = jnp.dot(k_ref[0, own], qa, preferred_element_type=F32)
    krow = lax.broadcasted_iota(I32, (KV_TILE, n), 0)
    s = jnp.where(krow <= qpos - own * KV_TILE, s, NEG)
    c = slope * (own * KV_TILE - qpos).astype(F32)
    _, l = _softmax_step(s, c, None, _value_rows(vT_ref, own, g, mixed), m, l, acc_ref)
    o_ref[...] = acc_ref[...] / jnp.maximum(l, 1e-30)


def _moba(cols, qa, qf, meta, kmean, k, vT):
    nb, nq = cols.nb, cols.nq
    n_hc, n = cols.n_hc(N_A_HEADS), cols.step_cols(N_A_HEADS)
    nT, nblk = k.shape[1], kmean.shape[1]
    vrows = N_KV * HEAD_DIM if cols.sample else HEAD_DIM
    col = lambda b, i, g: (0, (b * nq + i) * n_hc + g)
    return pl.pallas_call(
        functools.partial(_moba_kernel, span=cols.span, q_off=cols.q_off, mixed=cols.sample), grid=(nb, nq, n_hc),
        in_specs=[pl.BlockSpec((KD, n), col),
                  pl.BlockSpec((N_KV * HEAD_DIM, n), col),
                  pl.BlockSpec((8, n), col),
                  pl.BlockSpec((1, nblk, N_KV * HEAD_DIM), lambda b, i, g: (b, 0, 0)),
                  pl.BlockSpec((1, nT, KV_TILE, KD), lambda b, i, g: (b, 0, 0, 0)),
                  pl.BlockSpec((1, nT, N_KV * HEAD_DIM, KV_TILE), lambda b, i, g: (b, 0, 0, 0))],
        out_specs=pl.BlockSpec((vrows, n), col),
        out_shape=jax.ShapeDtypeStruct((vrows, qa.shape[1]), F32),
        scratch_shapes=[pltpu.VMEM((nblk, n), F32), pltpu.VMEM((vrows, n), F32)],
        compiler_params=_cparams(("parallel", "parallel", "arbitrary")), name="moba")(qa, qf, meta, kmean, k, vT)


def _cmp_kernel(r_ref, pe_ref, w1_ref, w2_ref, o_ref):
    r = r_ref[0]
    n = r.shape[0]
    hid = 2 * CMP_HIDDEN
    u = jnp.dot((r + pe_ref[0:1, :]).astype(BF16), w1_ref[:, 0:hid], preferred_element_type=F32)
    lo = jnp.dot((r + pe_ref[1:2, :]).astype(BF16), w1_ref[:, hid:2 * hid], preferred_element_type=F32)
    pre = u + pltpu.roll(lo, n - 1, 0)
    h = pre / (1.0 + jnp.exp(-pre))
    o_ref[0] = jnp.dot(h.astype(BF16), w2_ref[...], preferred_element_type=F32)


def _compress(raw, pe2, w1big, w2big, n_rows):
    nb = raw.shape[0]
    return pl.pallas_call(
        _cmp_kernel, grid=(nb,),
        in_specs=[pl.BlockSpec((1, n_rows, 2048), lambda b: (b, 0, 0)),
                  pl.BlockSpec((2, 2048), lambda b: (0, 0)),
                  pl.BlockSpec((2048, 4 * CMP_HIDDEN), lambda b: (0, 0)),
                  pl.BlockSpec((2 * CMP_HIDDEN, 128), lambda b: (0, 0))],
        out_specs=pl.BlockSpec((1, n_rows, 128), lambda b: (b, 0, 0)),
        out_shape=jax.ShapeDtypeStruct((nb, n_rows, 128), F32),
        compiler_params=_cparams(("parallel",)), name="compress")(raw, pe2, w1big, w2big)


def _cmp_weights(pe, w1, w2):
    w1r = w1.reshape(2, CMP_STRIDE, HEAD_DIM, CMP_HIDDEN)
    eye = jnp.eye(N_KV, dtype=w1.dtype)
    big = jnp.einsum('hpdj,ge->hpdgej', w1r, eye)
    big = big.transpose(1, 3, 2, 0, 4, 5).reshape(CMP_STRIDE * N_KV * HEAD_DIM, 2 * N_KV * CMP_HIDDEN)
    w2big = jnp.einsum('jd,ge->gjed', w2, eye).reshape(N_KV * CMP_HIDDEN, N_KV * HEAD_DIM)
    pe2 = jnp.broadcast_to(pe.reshape(2, CMP_STRIDE, 1, HEAD_DIM), (2, CMP_STRIDE, N_KV, HEAD_DIM)).reshape(2, 2048)
    return pe2, big.astype(BF16), w2big.astype(BF16)


def _nsa_kernel(qa_ref, meta_ref, kc_ref, vcT_ref, taps_ref, grp_ref, ks_ref, vsT_ref, kw_ref, vwT_ref, o_ref,
                sel_ref, qs_ref, acc_ref, *, span, q_off, kw_base, mixed):
    i, g = pl.program_id(1), pl.program_id(2)
    n = qa_ref.shape[1]
    nq = n if mixed else n // (N_B_HEADS // N_KV)
    rep = n // nq
    t0 = q_off + i * span
    slope = meta_ref[0:1, :]
    qpos = meta_ref[1:2, :].astype(I32)
    qa = qa_ref[...]
    n_cmp = kc_ref.shape[1]
    n_tab = sel_ref.shape[0]

    def vrows(ref):
        if mixed:
            return ref[0]
        return ref[0, pl.ds(pl.multiple_of(g * HEAD_DIM, HEAD_DIM), HEAD_DIM), :]

    cend = CMP_STRIDE * lax.broadcasted_iota(I32, (n_cmp, n), 0) + (CMP_LEN - 1)
    cmask = cend <= qpos
    s = jnp.dot(kc_ref[0], qa[0:N_KV * HEAD_DIM, :], preferred_element_type=F32) + slope * (cend - qpos).astype(F32)
    s = jnp.where(cmask, s, NEG)
    p = jnp.where(cmask, jnp.exp2(s - jnp.max(s, axis=0, keepdims=True)), 0.0)
    p = p * (1.0 / jnp.maximum(jnp.sum(p, axis=0, keepdims=True), 1e-30))
    o_c = jnp.dot(vrows(vcT_ref), p.astype(BF16), preferred_element_type=F32)

    if mixed:
        imp = jnp.dot(p, grp_ref[...], precision=lax.Precision.HIGHEST, preferred_element_type=F32)
    else:
        imp = p[:, 0:nq]
        for r in range(1, rep):
            imp = imp + p[:, r * nq:(r + 1) * nq]
    p_slc = jnp.dot(taps_ref[...], imp, precision=lax.Precision.HIGHEST, preferred_element_type=F32)
    p_slc = jnp.concatenate([p_slc, jnp.full((n_tab - p_slc.shape[0], nq), SENTINEL, F32)], axis=0)
    rowi = lax.broadcasted_iota(I32, (n_tab, nq), 0)
    own = qpos[:, 0:nq] >> 6
    sel = _top_rows(jnp.where(rowi < own, p_slc, SENTINEL), rowi, SEL_TOPK)
    sel_ref[...] = jnp.where(rowi == own, 1.0, sel)

    per_tile = SEL_TILE // SEL_BLOCK
    last = (t0 + span - 1) // SEL_TILE
    for slot in range(qs_ref.shape[0]):
        qs_ref[slot] = qa
    acc_ref[...] = jnp.zeros(acc_ref.shape, F32)

    def sel_scores(t, slot):
        rows = sel_ref[pl.ds(pl.multiple_of(t * per_tile, per_tile), per_tile), :]
        bias = jnp.where(rows > 0.5, 0.0, NEG)
        if rep > 1:
            bias = jnp.concatenate([bias] * rep, axis=1)
        bias = jnp.concatenate([bias, jnp.zeros((per_tile, n), F32)], axis=0)
        qs_ref[slot, FEAT_SEL:FEAT_SEL + 2 * per_tile, :] = bias.astype(BF16)
        s = jnp.dot(ks_ref[0, t], qs_ref[slot], preferred_element_type=F32)
        return s, slope * (t * SEL_TILE - qpos).astype(F32)

    def sel_step(t, slot, ok, carry):
        s, c = sel_scores(t, slot)
        return _softmax_step(s, c, ok, _value_rows(vsT_ref, t, g, mixed), *carry, acc_ref)

    m, l = _tile_groups(last, sel_step, _softmax_init(n), qs_ref.shape[0])
    s, c = sel_scores(last, 0)
    srow = lax.broadcasted_iota(I32, (SEL_TILE, n), 0)
    s = jnp.where(srow <= qpos - last * SEL_TILE, s, NEG)
    _, l = _softmax_step(s, c, None, _value_rows(vsT_ref, last, g, mixed), m, l, acc_ref)
    o_s = acc_ref[...] / jnp.maximum(l, 1e-30)

    lo = jnp.maximum(t0 - (WINDOW - 1) - kw_base, 0) // KV_TILE
    hi = jnp.minimum((t0 + span - 1 - kw_base) // KV_TILE + 1, kw_ref.shape[1])
    wrow = lax.broadcasted_iota(I32, (KV_TILE, n), 0)
    acc_ref[...] = jnp.zeros(acc_ref.shape, F32)

    def win_body(t, carry):
        s = jnp.dot(kw_ref[0, t], qa, preferred_element_type=F32)
        base = kw_base + t * KV_TILE
        dist = qpos - base - wrow
        s = jnp.where((dist >= 0) & (dist < WINDOW), s, NEG)
        return _softmax_step(s, slope * (base - qpos).astype(F32), None, _value_rows(vwT_ref, t, g, mixed),
                             *carry, acc_ref)

    _, l = lax.fori_loop(lo, hi, win_body, _softmax_init(n))
    o_w = acc_ref[...] / jnp.maximum(l, 1e-30)

    gt = 1.0 / (1.0 + jnp.exp(-meta_ref[2:5, :]))
    o_ref[...] = gt[0:1, :] * o_c + gt[1:2, :] * o_s + gt[2:3, :] * o_w


def _taps_matrix(n_sel, n_cmp):
    m = np.zeros((n_sel, n_cmp), np.float32)
    for j in range(n_sel):
        for off, w in ((-1, 1.0), (0, 2.0), (1, 2.0), (2, 2.0), (3, 1.0)):
            c = 4 * j + off
            if 0 <= c < n_cmp:
                m[j, c] = w
    return jnp.asarray(m)


def _group_matrix(cols):
    grp = cols.head_group(N_B_HEADS)
    m = np.zeros((SAMPLE_COLS, SAMPLE_COLS), np.float32)
    m[:N_B_HEADS, :N_B_HEADS] = grp[:, None] == grp[None, :]
    return jnp.asarray(m)


def _nsa(cols, qa, meta, kc, vcT, ks, vsT, kw, vwT, kw_base):
    nb, nq = cols.nb, cols.nq
    n_hc, n = cols.n_hc(N_B_HEADS), cols.step_cols(N_B_HEADS)
    nTs, nTw = ks.shape[1], kw.shape[1]
    n_cmp = kc.shape[1]
    n_sel = n_cmp // 4
    n_tab = n_sel + SEL_TILE // SEL_BLOCK
    nsel_cols = n if cols.sample else n // (N_B_HEADS // N_KV)
    vrows = N_KV * HEAD_DIM if cols.sample else HEAD_DIM
    col = lambda b, i, g: (0, (b * nq + i) * n_hc + g)
    seq3 = lambda b, i, g: (b, 0, 0)
    seq4 = lambda b, i, g: (b, 0, 0, 0)
    return pl.pallas_call(
        functools.partial(_nsa_kernel, span=cols.span, q_off=cols.q_off, kw_base=kw_base, mixed=cols.sample),
        grid=(nb, nq, n_hc),
        in_specs=[pl.BlockSpec((KD, n), col),
                  pl.BlockSpec((8, n), col),
                  pl.BlockSpec((1, n_cmp, N_KV * HEAD_DIM), seq3),
                  pl.BlockSpec((1, N_KV * HEAD_DIM, n_cmp), seq3),
                  pl.BlockSpec((n_sel, n_cmp), lambda b, i, g: (0, 0)),
                  pl.BlockSpec((SAMPLE_COLS, SAMPLE_COLS), lambda b, i, g: (0, 0)),
                  pl.BlockSpec((1, nTs, SEL_TILE, KD), seq4),
                  pl.BlockSpec((1, nTs, N_KV * HEAD_DIM, SEL_TILE), seq4),
                  pl.BlockSpec((1, nTw, KV_TILE, KD), seq4),
                  pl.BlockSpec((1, nTw, N_KV * HEAD_DIM, KV_TILE), seq4)],
        out_specs=pl.BlockSpec((vrows, n), col),
        out_shape=jax.ShapeDtypeStruct((vrows, qa.shape[1]), F32),
        scratch_shapes=[pltpu.VMEM((n_tab, nsel_cols), F32), pltpu.VMEM((_group(cols.sample), KD, n), BF16), pltpu.VMEM((vrows, n), F32)],
        compiler_params=_cparams(("parallel", "parallel", "arbitrary")), name="nsa")(
            qa, meta, kc, vcT, _taps_matrix(n_sel, n_cmp), _group_matrix(cols), ks, vsT, kw, vwT)


def _dsa_kernel(qi_ref, wi_ref, ki_ref, qa_ref, meta_ref, k_ref, vT_ref, o_ref, key_ref, row_ref, acc_ref,
                *, span, q_off, topk, mixed, hc_per_group):
    i, hc = pl.program_id(1), pl.program_id(2)
    n = qa_ref.shape[1]
    nq = qi_ref.shape[1]
    rep = n // nq
    t0 = q_off + i * span
    n_kv = (t0 + span - 1) // KV_TILE + 1
    krow = lax.broadcasted_iota(I32, (KV_TILE, nq), 0)

    def search(count, shape):
        kf = float(topk)
        thr = jnp.where(count(lambda key, kpos: key >= 0) >= kf, 0, INT_MIN).astype(I32)

        def bit_body(j, thr):
            cand = thr | (1 << (30 - j))
            return jnp.where(count(lambda key, kpos: key >= cand) >= kf, cand, thr)
        thr = lax.fori_loop(0, 31, bit_body, thr)
        need = kf - count(lambda key, kpos: key > thr)

        def tie_search():
            def idx_body(j, jb):
                cand = jb | (1 << (14 - j))
                c = count(lambda key, kpos: (key == thr) & (kpos < cand))
                return jnp.where(c <= need, cand, jb)
            return lax.fori_loop(0, 15, idx_body, jnp.zeros(shape, I32))

        n_ge = count(lambda key, kpos: key >= thr)
        jb = lax.cond(jnp.max(n_ge) > kf, tie_search, lambda: jnp.full(shape, (1 << 15) - 1, I32))
        return thr, jb

    @pl.when(hc == 0)
    def _():
        qpos = t0 + (lax.broadcasted_iota(I32, (1, nq), 1) if span > 1 else jnp.zeros((1, nq), I32))
        w = wi_ref[...] * (IDX_SCALE * N_IDX_HEADS ** -0.5)
        qi = [qi_ref[a * IDX_DIM:(a + 1) * IDX_DIM, :].astype(BF16) for a in range(N_IDX_HEADS)]
        if mixed:
            row_ref[...] = jnp.full(row_ref.shape, INT_MIN, I32)

        def score_body(t, _):
            ki = ki_ref[0, t]
            acc = jnp.zeros((KV_TILE, nq), F32)
            for a in range(N_IDX_HEADS):
                rel = jnp.dot(ki, qi[a], preferred_element_type=F32)
                acc = acc + jnp.maximum(rel, 0.0) * w[a:a + 1, :]
            sc = jnp.where(t * KV_TILE + krow <= qpos, acc, NEG)
            bits = lax.bitcast_convert_type(sc, I32)
            key = bits ^ ((bits >> 31) & 0x7FFFFFFF)
            key_ref[t] = key
            if mixed:
                row_ref[pl.ds(t, 1), :] = jnp.transpose(key)[0:1, :]
            return 0
        lax.fori_loop(0, n_kv, score_body, 0)

        if mixed:
            keys = row_ref[...]
            kpos_r = (lax.broadcasted_iota(I32, keys.shape, 0) * KV_TILE + lax.broadcasted_iota(I32, keys.shape, 1))

            def count(pred):
                c = jnp.sum(jnp.where(pred(keys, kpos_r), 1.0, 0.0), axis=0, keepdims=True)
                return jnp.sum(c, axis=1, keepdims=True)
            thr, jb = search(count, (1, 1))
        else:
            n_grp = (n_kv + COUNT_GROUP - 1) // COUNT_GROUP

            def pad_body(t, _):
                key_ref[t] = jnp.full((KV_TILE, nq), INT_MIN, I32)
                return 0
            lax.fori_loop(n_kv, n_grp * COUNT_GROUP, pad_body, 0)

            def count(pred):
                def body(u, c):
                    for j in range(COUNT_GROUP):
                        t = u * COUNT_GROUP + j
                        hit = pred(key_ref[t], t * KV_TILE + krow)
                        c = c + jnp.sum(jnp.where(hit, 1.0, 0.0), axis=0, keepdims=True)
                    return c
                return lax.fori_loop(0, n_grp, body, jnp.zeros((1, nq), F32))
            thr, jb = search(count, (1, nq))

        def bias_body(t, _):
            key = key_ref[t]
            kpos = t * KV_TILE + krow
            keep = ((key > thr) | ((key == thr) & (kpos < jb))) & (kpos <= qpos)
            key_ref[t] = lax.bitcast_convert_type(jnp.where(keep, 0.0, NEG), I32)
            return 0
        lax.fori_loop(0, n_kv, bias_body, 0)

    g = hc // hc_per_group
    slope = meta_ref[0:1, :]
    qpos_c = meta_ref[1:2, :].astype(I32)
    qa = qa_ref[...]
    acc_ref[...] = jnp.zeros(acc_ref.shape, F32)

    def step(t, slot, ok, carry):
        bias = lax.bitcast_convert_type(key_ref[t], F32)
        if rep > 1:
            bias = jnp.concatenate([bias] * rep, axis=1)
        s = jnp.dot(k_ref[0, t], qa, preferred_element_type=F32) + bias
        c = slope * (t * KV_TILE - qpos_c).astype(F32)
        return _softmax_step(s, c, ok, _value_rows(vT_ref, t, g, mixed), *carry, acc_ref)
    _, l = _tile_groups(n_kv, step, _softmax_init(n), _group(mixed))
    o_ref[...] = acc_ref[...] / jnp.maximum(l, 1e-30)


def _dsa(cols, qi, qi_row, wi, wi_row, ki, qa, meta, k, vT, topk):
    nb, nq = cols.nb, cols.nq
    n_hc, n = cols.n_hc(N_C_HEADS), cols.step_cols(N_C_HEADS)
    ncol = SAMPLE_COLS if cols.sample else cols.tq
    nT = k.shape[1]
    assert nT % COUNT_GROUP == 0
    vrows = N_KV * HEAD_DIM if cols.sample else HEAD_DIM
    qi_b, wi_b = qi_row // (N_IDX_HEADS * IDX_DIM), wi_row // N_IDX_HEADS
    col = lambda b, i, h: (0, (b * nq + i) * n_hc + h)
    seq4 = lambda b, i, h: (b, 0, 0, 0)
    return pl.pallas_call(
        functools.partial(_dsa_kernel, span=cols.span, q_off=cols.q_off, topk=topk, mixed=cols.sample,
                          hc_per_group=max(n_hc // N_KV, 1)),
        grid=(nb, nq, n_hc),
        in_specs=[pl.BlockSpec((N_IDX_HEADS * IDX_DIM, ncol), lambda b, i, h: (qi_b, b * nq + i)),
                  pl.BlockSpec((N_IDX_HEADS, ncol), lambda b, i, h: (wi_b, b * nq + i)),
                  pl.BlockSpec((1, nT, KV_TILE, IDX_DIM), seq4),
                  pl.BlockSpec((KD, n), col),
                  pl.BlockSpec((8, n), col),
                  pl.BlockSpec((1, nT, KV_TILE, KD), seq4),
                  pl.BlockSpec((1, nT, N_KV * HEAD_DIM, KV_TILE), seq4)],
        out_specs=pl.BlockSpec((vrows, n), col),
        out_shape=jax.ShapeDtypeStruct((vrows, qa.shape[1]), F32),
        scratch_shapes=[pltpu.VMEM((nT, KV_TILE, ncol), I32), pltpu.VMEM((-(-nT // 8) * 8, KV_TILE), I32),
                        pltpu.VMEM((vrows, n), F32)],
        compiler_params=_cparams(("parallel", "parallel", "arbitrary")), name="dsa")(
            qi, wi, ki, qa, meta, k, vT)


def _mem_kernel(q_ref, mk_ref, mvT_ref, o_ref):
    for h in range(N_MEM_HEADS):
        rows = slice(h * HEAD_DIM, (h + 1) * HEAD_DIM)
        q = (q_ref[rows, :] * ATTN_SCALE).astype(BF16)
        s = jnp.dot(mk_ref[0, h].astype(BF16), q, preferred_element_type=F32)
        p = jnp.exp(s - jnp.max(s, axis=0, keepdims=True))
        l = jnp.sum(p, axis=0, keepdims=True)
        o_ref[rows, :] = jnp.dot(mvT_ref[0, h].astype(BF16), p.astype(BF16), preferred_element_type=F32) / l


def _mem(hT, q_row, mk, mvT, nb, nq, tq):
    n_mem = mk.shape[2]
    n = N_MEM_HEADS * HEAD_DIM
    qb = q_row // n
    return pl.pallas_call(
        _mem_kernel, grid=(nb, nq),
        in_specs=[pl.BlockSpec((n, tq), lambda b, i: (qb, b * nq + i)),
                  pl.BlockSpec((1, N_MEM_HEADS, n_mem, HEAD_DIM), lambda b, i: (b, 0, 0, 0)),
                  pl.BlockSpec((1, N_MEM_HEADS, HEAD_DIM, n_mem), lambda b, i: (b, 0, 0, 0))],
        out_specs=pl.BlockSpec((n, tq), lambda b, i: (0, b * nq + i)),
        out_shape=jax.ShapeDtypeStruct((n, nb * nq * tq), F32),
        compiler_params=_cparams(("parallel", "parallel")), name="mem")(hT, mk, mvT)


def _mem_branch(cols, hT, q_row, mk, mvT):
    n = N_MEM_HEADS * HEAD_DIM
    if cols.sample:
        o = _mem(cols.tile_cols(hT[q_row:q_row + n]), 0, mk, mvT, cols.nb, 1, SAMPLE_COLS)
        return cols.from_tile_cols(o)
    return _mem(hT, q_row, mk, mvT, cols.nb, cols.nq, cols.tq)


def _even_wT(w_in):
    sizes = (384, 128, 128, 384, 384, 128, 128, 128, 128, 128, 128, 18, 384, 256, 256)
    offs = np.cumsum((0,) + sizes)
    (a_q, a_k, a_v, a_z, b_q, b_kc, b_vc, b_ks, b_vs, b_kw, b_vw, b_g, b_z, m_q, m_z) = [
        w_in[:, offs[n]:offs[n + 1]] for n in range(len(sizes))]
    d = w_in.shape[0]
    bg = jnp.pad(b_g, ((0, 0), (0, 32 - b_g.shape[1])))
    cols = [a_q, b_q, m_q, a_z, b_z, m_z, a_k, a_v, b_kc, b_vc, b_ks, b_vs, b_kw, b_vw, bg]
    w = jnp.concatenate(cols, axis=1)
    w = jnp.pad(w, ((0, 0), (0, EVEN_TOTAL - w.shape[1])))
    return w.T


def _odd_wT(w_in):
    sizes = (768, 128, 128, 512, 64, 8, 768, 256, 256)
    offs = np.cumsum((0,) + sizes)
    (c_q, c_k, c_v, c_qi, c_ki, c_wi, c_z, m_q, m_z) = [w_in[:, offs[n]:offs[n + 1]] for n in range(len(sizes))]
    w = jnp.concatenate([c_q, m_q, c_z, m_z, c_qi, c_k, c_v, c_ki, c_wi], axis=1)
    w = jnp.pad(w, ((0, 0), (0, ODD_TOTAL - w.shape[1])))
    return w.T


def _keys(cols, new_tok, pool, page_table, tile, mode):
    if cols.sample:
        return _gather(pool, page_table, new_tok, mode, tile)
    return _kprep(new_tok, tile, mode)


def _values(cols, new_tok, pool, page_table, tile):
    if cols.sample:
        return _gather(pool, page_table, new_tok, 'v', tile)
    return _vT_tiles(new_tok, tile)


def _raw_rows(cols, new_tok, pool, page_table):
    if cols.sample:
        return _gather(pool, page_table, None, 'raw', KV_TILE)
    return new_tok


def _gate_cols(cols, gT):
    return [cols.per_token(gT, N_B_HEADS)]


def _even_layer(cols, xT, mk, mvT, past, page_table, wT, woT, g, b, cmpw, alpha):
    R = EVEN_ROWS
    nb = cols.nb
    hT = _mm(wT, xT, 640, min(1024, xT.shape[1]))
    names = ('a_k', 'a_v', 'b_kc', 'b_vc', 'b_ks', 'b_vs')
    new = [cols.tokens(hT[R[nm]:R[nm] + 128]) for nm in names]
    kw_new, vw_new = cols.tokens(hT[R['b_kw']:R['b_kw'] + 128]), cols.tokens(hT[R['b_vw']:R['b_vw'] + 128])
    pools = past[:6] if cols.sample else [None] * 6
    if cols.sample:
        kw = jnp.concatenate([past[6].reshape(nb, -1, 128), kw_new], axis=1)
        vw = jnp.concatenate([past[7].reshape(nb, -1, 128), vw_new], axis=1)
        n_win = past[6].shape[1]
        kw_base = cols.q_off - n_win
        win_out = (kw[:, -n_win:], vw[:, -n_win:])
        pad = -kw.shape[1] % 1024
        kw = jnp.pad(kw, ((0, 0), (0, pad), (0, 0)))
        vw = jnp.pad(vw, ((0, 0), (0, pad), (0, 0)))
    else:
        kw, vw, kw_base = kw_new, vw_new, 0
        n_win = min(WINDOW, cols.t)
        win_out = (kw[:, -n_win:], vw[:, -n_win:])

    sl = _alibi(N_A_HEADS + N_B_HEADS)
    ak, amean = _keys(cols, new[0], pools[0], page_table, KV_TILE, 'k')
    avT = _values(cols, new[1], pools[1], page_table, KV_TILE)
    qa, qf, meta = cols.columns(hT[R['a_q']:R['a_q'] + 384], N_A_HEADS, sl[0::2])
    o_a = _moba(cols, qa, qf, meta, amean, ak, avT)

    pe2k, w1k, w2k = cmpw[0]
    pe2v, w1v, w2v = cmpw[1]
    n_cmp = max(cols.q_off, cols.t) // CMP_STRIDE
    kc_raw = _raw_rows(cols, new[2], pools[2], page_table).reshape(nb, -1, 2048)
    vc_raw = _raw_rows(cols, new[3], pools[3], page_table).reshape(nb, -1, 2048)
    kc = _compress(kc_raw, pe2k, w1k, w2k, n_cmp).astype(BF16)
    vcT = _compress(vc_raw, pe2v, w1v, w2v, n_cmp).transpose(0, 2, 1).astype(BF16)
    ks = _keys(cols, new[4], pools[4], page_table, SEL_TILE, 'ks')
    vsT = _values(cols, new[5], pools[5], page_table, SEL_TILE)
    kwt = _kprep(kw, KV_TILE, 'ks')
    qa, _, meta = cols.columns(hT[R['b_q']:R['b_q'] + 384], N_B_HEADS, sl[1::2],
                               extra=_gate_cols(cols, hT[R['b_g']:R['b_g'] + 18]))
    o_b = _nsa(cols, qa, meta, kc, vcT, ks, vsT, kwt, _vT_tiles(vw, KV_TILE), kw_base)

    o_m = _mem_branch(cols, hT, R['m_q'], mk, mvT)
    parts = [cols.collapse(o_a, N_A_HEADS), cols.collapse(o_b, N_B_HEADS), o_m]
    yT = _outln(parts, hT, R['z'], xT, woT, g, b, alpha, min(512, xT.shape[1]))
    state = [n.reshape(nb, -1, N_KV, HEAD_DIM) for n in new]
    state += [w.reshape(nb, -1, N_KV, HEAD_DIM) for w in win_out]
    return yT, state


def _odd_layer(cols, xT, mk, mvT, past, page_table, wT, woT, g, b, alpha):
    R = ODD_ROWS
    nb = cols.nb
    hT = _mm(wT, xT, 768, min(1024, xT.shape[1]))
    new = [cols.tokens(hT[R['c_k']:R['c_k'] + 128]), cols.tokens(hT[R['c_v']:R['c_v'] + 128]),
           cols.tokens(hT[R['c_ki']:R['c_ki'] + IDX_DIM])]
    pools = past if cols.sample else [None] * 3
    topk = min(DSA_TOPK, (cols.q_off + cols.t) // 4)
    ck = _keys(cols, new[0], pools[0], page_table, KV_TILE, 'ks')
    cvT = _values(cols, new[1], pools[1], page_table, KV_TILE)
    if cols.sample:
        ki = _gather(pools[2], page_table, new[2], 'i', KV_TILE)
    else:
        ki = new[2].reshape(nb, -1, KV_TILE, IDX_DIM).astype(BF16)
    qa, _, meta = cols.columns(hT[R['c_q']:R['c_q'] + 768], N_C_HEADS, _alibi(N_C_HEADS))
    if cols.sample:
        qi = cols.tile_cols(hT[R['c_qi']:R['c_qi'] + 512])
        wi = cols.tile_cols(hT[R['c_wi']:R['c_wi'] + 8])
        o_c = _dsa(cols, qi, 0, wi, 0, ki, qa, meta, ck, cvT, topk)
    else:
        o_c = _dsa(cols, hT, R['c_qi'], hT, R['c_wi'], ki, qa, meta, ck, cvT, topk)
    o_m = _mem_branch(cols, hT, R['m_q'], mk, mvT)
    parts = [cols.collapse(o_c, N_C_HEADS), o_m]
    yT = _outln(parts, hT, R['z'], xT, woT, g, b, alpha, min(512, xT.shape[1]))
    state = [new[0].reshape(nb, -1, N_KV, HEAD_DIM), new[1].reshape(nb, -1, N_KV, HEAD_DIM), new[2]]
    return yT, state


def kernel(x_prompt, x_sample, cache_a_k, cache_a_v, cache_b_cmp_k, cache_b_cmp_v, cache_b_sel_k, cache_b_sel_v,
           state_b_win_k, state_b_win_v, cache_c_k, cache_c_v, cache_c_idx_k, cache_mem_k, cache_mem_v,
           page_table, mem_prompt, w_in_even, w_out_even, w_in_odd, w_out_odd, w_mem_kv, ln_g, ln_b,
           cmp_pe, cmp_w1, cmp_w2):
    bp, seq, d = x_prompt.shape
    bs = x_sample.shape[0]
    depth = w_mem_kv.shape[0]
    n_mem = mem_prompt.shape[1]
    past_len = page_table.shape[1] * cache_a_k.shape[2]
    alpha = (2.0 * depth) ** 0.25
    assert x_sample.shape[1] == 1 and bs <= 128 and seq % 1024 == 0 and past_len % 1024 == 0

    gp = _Cols(bp, seq, 0)
    gs = _Cols(bs, 1, past_len)
    xp = x_prompt.reshape(bp * seq, d).T
    xs = jnp.pad(x_sample.reshape(bs, d).T, ((0, 0), (0, 128 - bs)))
    n_pool = cache_a_k.shape[1]

    def pages(c):
        return jnp.moveaxis(c.reshape(c.shape[:3] + (-1,)), 2, 3).reshape(c.shape[0] * n_pool, -1, c.shape[2])

    even_paged = tuple(pages(c) for c in (cache_a_k, cache_a_v, cache_b_cmp_k, cache_b_cmp_v, cache_b_sel_k,
                                          cache_b_sel_v))
    odd_paged = tuple(pages(c) for c in (cache_c_k, cache_c_v, cache_c_idx_k))
    ev_p, ev_s, od_p, od_s, mk_list, mv_list = [], [], [], [], [], []
    for layer in range(depth):
        j = layer // 2
        mem_kv = _mm(mem_prompt.reshape(bp * n_mem, d), w_mem_kv[layer], bp * n_mem, 2 * N_MEM_HEADS * HEAD_DIM)
        mem_kv = mem_kv.reshape(bp, n_mem, 2, N_MEM_HEADS, HEAD_DIM)
        mk_p, mv_p = mem_kv[:, :, 0], mem_kv[:, :, 1]
        mk_list.append(mk_p)
        mv_list.append(mv_p)
        mem_p = (mk_p.transpose(0, 2, 1, 3), mv_p.transpose(0, 2, 3, 1))
        mem_s = (cache_mem_k[layer].transpose(0, 2, 1, 3), cache_mem_v[layer].transpose(0, 2, 3, 1))
        g = ln_g[layer].reshape(d, 1)
        b = ln_b[layer].reshape(d, 1)
        if layer % 2 == 0:
            wT = _even_wT(w_in_even[j])
            woT = w_out_even[j].T
            cmpw = [_cmp_weights(cmp_pe[j, c], cmp_w1[j, c], cmp_w2[j, c]) for c in range(2)]
            past = even_paged + (state_b_win_k[j], state_b_win_v[j])
            xp, st_p = _even_layer(gp, xp, *mem_p, None, None, wT.astype(BF16), woT.astype(BF16), g, b, cmpw, alpha)
            xs, st_s = _even_layer(gs, xs, *mem_s, past, page_table + j * n_pool, wT, woT, g, b, cmpw, alpha)
            ev_p.append(st_p)
            ev_s.append(st_s)
        else:
            wT = _odd_wT(w_in_odd[j])
            woT = w_out_odd[j].T
            past = odd_paged
            xp, st_p = _odd_layer(gp, xp, *mem_p, None, None, wT.astype(BF16), woT.astype(BF16), g, b, alpha)
            xs, st_s = _odd_layer(gs, xs, *mem_s, past, page_table + j * n_pool, wT, woT, g, b, alpha)
            od_p.append(st_p)
            od_s.append(st_s)
    y_p = xp.T.reshape(bp, seq, d)
    y_s = xs[:, :bs].T.reshape(bs, 1, d)
    ev_p = [jnp.stack(l) for l in zip(*ev_p)]
    ev_s = [jnp.stack(l) for l in zip(*ev_s)]
    od_p = [jnp.stack(l) for l in zip(*od_p)]
    od_s = [jnp.stack(l) for l in zip(*od_s)]
    out = [y_p, y_s]
    for p, s in zip(ev_p, ev_s):
        out += [p, s]
    for p, s in zip(od_p, od_s):
        out += [p, s]
    out += [jnp.stack(mk_list), jnp.stack(mv_list)]
    return tuple(out)
```

```python
import functools

import numpy as np
import jax
import jax.numpy as jnp
from jax import lax
from jax.experimental import pallas as pl
from jax.experimental.pallas import tpu as pltpu

F32 = jnp.float32
BF16 = jnp.bfloat16
I32 = jnp.int32

HEAD_DIM = 64
N_MEM_HEADS = 4
N_A_HEADS = 6
N_B_HEADS = 6
N_C_HEADS = 12
N_KV = 2
N_IDX_HEADS = 8
IDX_DIM = 64
MOBA_BLOCK = 256
MOBA_TOPK = 3
CMP_STRIDE = 16
CMP_LEN = 32
CMP_HIDDEN = 128
SEL_BLOCK = 64
SEL_TOPK = 15
WINDOW = 512
DSA_TOPK = 256
ATTN_SCALE = HEAD_DIM ** -0.5
IDX_SCALE = IDX_DIM ** -0.5
LN_EPS = 1e-5
LOG2E = 1.4426950408889634
NEG = -1e30
SENTINEL = -3e38
INT_MIN = -(2 ** 31)

KV_TILE = 256
SEL_TILE = 512
KD = 256
FEAT_POS = 128
FEAT_SEL = 144
COUNT_GROUP = 4
SAMPLE_COLS = 128
PAGES_PER_STEP = 8
GATHER_PAGES = 16
VMEM_LIMIT = 56 * 1024 * 1024

EVEN_ROWS = dict(a_q=0, b_q=384, m_q=768, z=1024, a_k=2048, a_v=2176, b_kc=2304, b_vc=2432,
                 b_ks=2560, b_vs=2688, b_kw=2816, b_vw=2944, b_g=3072)
EVEN_TOTAL = 3200
ODD_ROWS = dict(c_q=0, m_q=768, z=1024, c_qi=2048, c_k=2560, c_v=2688, c_ki=2816, c_wi=2880)
ODD_TOTAL = 3072


def _cparams(sem):
    return pltpu.CompilerParams(dimension_semantics=sem, vmem_limit_bytes=VMEM_LIMIT)


def _alibi(n):
    return 2.0 ** (-8.0 * np.arange(1, n + 1) / n)


def _mm_kernel(a_ref, b_ref, o_ref, *, precise):
    if precise:
        o_ref[...] = jnp.dot(a_ref[...], b_ref[...], precision=lax.Precision.HIGHEST, preferred_element_type=F32)
    else:
        o_ref[...] = jnp.dot(a_ref[...].astype(BF16), b_ref[...].astype(BF16), preferred_element_type=F32)


def _mm(a, b, bm, bn):
    m, k = a.shape
    n = b.shape[1]
    return pl.pallas_call(
        functools.partial(_mm_kernel, precise=a.dtype == F32 and b.dtype == F32 and n <= 128), grid=(n // bn, m // bm),
        in_specs=[pl.BlockSpec((bm, k), lambda j, i: (i, 0)), pl.BlockSpec((k, bn), lambda j, i: (0, j))],
        out_specs=pl.BlockSpec((bm, bn), lambda j, i: (i, j)),
        out_shape=jax.ShapeDtypeStruct((m, n), F32),
        compiler_params=_cparams(("parallel", "parallel")), name="mm")(a, b)


def _outln_kernel(*refs, n_parts, alpha):
    o_refs = refs[:n_parts]
    z_ref, x_ref, w_ref, g_ref, b_ref, y_ref = refs[n_parts:]
    o = jnp.concatenate([r[...] for r in o_refs], axis=0)
    z = z_ref[...]
    mixed = o * (z / (1.0 + jnp.exp(-z)))
    if w_ref.dtype == F32:
        y = jnp.dot(w_ref[...], mixed, precision=lax.Precision.HIGHEST, preferred_element_type=F32)
    else:
        y = jnp.dot(w_ref[...], mixed.astype(BF16), preferred_element_type=F32)
    y = alpha * x_ref[...] + y
    mu = jnp.mean(y, axis=0, keepdims=True)
    d = y - mu
    var = jnp.mean(d * d, axis=0, keepdims=True)
    y_ref[...] = d * lax.rsqrt(var + LN_EPS) * g_ref[...] + b_ref[...]


def _outln(o_parts, hT, z_row, xT, woT, g, b, alpha, bn):
    d, n = xT.shape
    zb = z_row // d
    in_specs = [pl.BlockSpec((p.shape[0], bn), lambda j: (0, j)) for p in o_parts]
    in_specs += [pl.BlockSpec((d, bn), lambda j: (zb, j)),
                 pl.BlockSpec((d, bn), lambda j: (0, j)),
                 pl.BlockSpec((d, d), lambda j: (0, 0)),
                 pl.BlockSpec((d, 1), lambda j: (0, 0)),
                 pl.BlockSpec((d, 1), lambda j: (0, 0))]
    return pl.pallas_call(
        functools.partial(_outln_kernel, n_parts=len(o_parts), alpha=alpha), grid=(n // bn,),
        in_specs=in_specs, out_specs=pl.BlockSpec((d, bn), lambda j: (0, j)),
        out_shape=jax.ShapeDtypeStruct((d, n), F32),
        compiler_params=_cparams(("parallel",)), name="outln")(*o_parts, hT, xT, woT, g, b)


def _key_features(tile):
    r = lax.broadcasted_iota(I32, (tile, 128), 0)
    c = lax.broadcasted_iota(I32, (tile, 128), 1)
    sel_c = c - (FEAT_SEL - FEAT_POS)
    feat = jnp.where(c < 3, r >> 4, jnp.where(c < 6, r & 15, jnp.where(sel_c == (r >> 6), 1, 0)))
    return feat.astype(F32).astype(BF16)


def _emit_tiles(x, outs, mode, tile, t0=0):
    nt = x.shape[0] // tile
    if mode == 'raw':
        outs[0][0, t0 * tile:(t0 + nt) * tile, :] = x
        return
    feat = _key_features(tile) if mode in ('k', 'ks') else None
    for t in range(nt):
        xt = x[t * tile:(t + 1) * tile]
        if mode in ('k', 'ks'):
            outs[0][0, t0 + t] = jnp.concatenate([xt.astype(BF16), feat], axis=1)
        elif mode == 'v':
            outs[0][0, t0 + t] = jnp.transpose(xt).astype(BF16)
        else:
            outs[0][0, t0 + t] = xt.astype(BF16)
    if mode == 'k':
        nblk = x.shape[0] // MOBA_BLOCK
        outs[1][0, 0, t0:t0 + nblk, :] = jnp.mean(x.reshape(nblk, MOBA_BLOCK, x.shape[1]), axis=1)


def _zero_tiles(outs, mode, tile, t0, t1):
    for t in range(t0, t1):
        outs[0][0, t] = jnp.zeros(outs[0].shape[2:], outs[0].dtype)
    if mode == 'k':
        outs[1][0, 0, t0:t1, :] = jnp.zeros((t1 - t0, outs[1].shape[3]), F32)


def _tile_outputs(mode, nb, n_steps, rows, tile, c):
    nt = rows // tile
    if mode == 'raw':
        return ([pl.BlockSpec((1, rows, c), lambda b, j, *_: (b, j, 0))],
                [jax.ShapeDtypeStruct((nb, n_steps * rows, c), F32)])
    if mode in ('k', 'ks'):
        shape, blk = (nb, n_steps * nt, tile, KD), (1, nt, tile, KD)
    elif mode == 'v':
        shape, blk = (nb, n_steps * nt, c, tile), (1, nt, c, tile)
    else:
        shape, blk = (nb, n_steps * nt, tile, c), (1, nt, tile, c)
    specs = [pl.BlockSpec(blk, lambda b, j, *_: (b, j, 0, 0))]
    shapes = [jax.ShapeDtypeStruct(shape, BF16)]
    if mode == 'k':
        nblk = rows // MOBA_BLOCK
        specs.append(pl.BlockSpec((1, 1, nblk, c), lambda b, j, *_: (b, j, 0, 0)))
        shapes.append(jax.ShapeDtypeStruct((nb, n_steps, nblk, c), F32))
    return specs, shapes


def _gather_kernel(pt_ref, *refs, npg, mode, tile, last, has_tail):
    pool_refs = refs[:npg]
    outs = refs[npg + (1 if has_tail else 0):]
    j = pl.program_id(1)

    def token_major(p):
        c = p.shape[0]
        if c == p.shape[1]:
            return jnp.transpose(p)
        return jnp.transpose(jnp.concatenate([p, jnp.zeros((p.shape[1] - c, p.shape[1]), p.dtype)], axis=0))[:, :c]

    def pages():
        if mode == 'v':
            xT = jnp.concatenate([r[0] for r in pool_refs], axis=1)
            for t in range(xT.shape[1] // tile):
                outs[0][0, t] = xT[:, t * tile:(t + 1) * tile].astype(BF16)
        else:
            _emit_tiles(jnp.concatenate([token_major(r[0]) for r in pool_refs], axis=0), outs, mode, tile)

    if not has_tail:
        pages()
        return
    pl.when(j < last)(pages)

    @pl.when(j == last)
    def _():
        _emit_tiles(refs[npg][0], outs, mode, tile)
        if mode != 'raw':
            _zero_tiles(outs, mode, tile, 1, outs[0].shape[1])


def _gather(pool, page_table, new_tok, mode, tile):
    nb, n_pages = page_table.shape
    n_pool, c, page_tokens = pool.shape
    npg = GATHER_PAGES if n_pages % GATHER_PAGES == 0 else PAGES_PER_STEP
    last = n_pages // npg
    has_tail = new_tok is not None
    rows = npg * page_tokens
    assert rows % tile == 0 and (has_tail or mode == 'raw')

    def pool_spec(k):
        return pl.BlockSpec((1, c, page_tokens),
                            lambda b, j, pt: (pt[b, jnp.minimum(j * npg + k, n_pages - 1)], 0, 0))

    in_specs = [pool_spec(k) for k in range(npg)]
    args = [pool] * npg
    if has_tail:
        tail = jnp.zeros((nb, tile, c), F32).at[:, 0:1, :].set(new_tok)
        in_specs.append(pl.BlockSpec((1, tile, c), lambda b, j, pt: (b, 0, 0)))
        args.append(tail)
    n_steps = last + (1 if has_tail else 0)
    out_specs, out_shape = _tile_outputs(mode, nb, n_steps, rows, tile, c)
    grid_spec = pltpu.PrefetchScalarGridSpec(num_scalar_prefetch=1, grid=(nb, n_steps), in_specs=in_specs,
                                             out_specs=out_specs)
    out = pl.pallas_call(
        functools.partial(_gather_kernel, npg=npg, mode=mode, tile=tile, last=last, has_tail=has_tail),
        grid_spec=grid_spec, out_shape=out_shape,
        compiler_params=_cparams(("parallel", "arbitrary")), name="gather_" + mode)(page_table, *args)
    if mode == 'k':
        return out[0], out[1].reshape(nb, -1, c)
    return out[0]


def _kprep_kernel(x_ref, *outs, mode, tile):
    _emit_tiles(x_ref[0], outs, mode, tile)


def _kprep(x, tile, mode='k'):
    nb, L, c = x.shape
    rows = 1024
    out_specs, out_shape = _tile_outputs(mode, nb, L // rows, rows, tile, c)
    out = pl.pallas_call(
        functools.partial(_kprep_kernel, mode=mode, tile=tile), grid=(nb, L // rows),
        in_specs=[pl.BlockSpec((1, rows, c), lambda b, j: (b, j, 0))],
        out_specs=out_specs, out_shape=out_shape,
        compiler_params=_cparams(("parallel", "parallel")), name="kprep")(x)
    if mode == 'k':
        return out[0], out[1].reshape(nb, -1, c)
    return out[0]


def _vT_tiles(v, tile):
    nb, L, c = v.shape
    return v.reshape(nb, L // tile, tile, c).transpose(0, 1, 3, 2).astype(BF16)


class _Cols:
    def __init__(self, nb, t, q_off):
        self.nb, self.t, self.q_off = nb, t, q_off
        self.sample = q_off > 0
        self.tq = 1 if self.sample else KV_TILE
        self.nq = 1 if self.sample else t // KV_TILE
        self.span = 1 if self.sample else KV_TILE

    def n_hc(self, heads):
        return 1 if self.sample else N_KV

    def step_cols(self, heads):
        return SAMPLE_COLS if self.sample else heads // N_KV * self.tq

    def head_group(self, heads):
        return np.arange(heads) // (heads // N_KV)

    def per_head(self, vals, heads):
        vals = jnp.asarray(vals, F32)
        k = vals.shape[0]
        if self.sample:
            return jnp.tile(jnp.pad(vals, ((0, 0), (0, SAMPLE_COLS - heads))), (1, self.nb))
        hps = heads // N_KV
        shape = (k, self.nb, self.nq, N_KV, hps, self.tq)
        return jnp.broadcast_to(vals.reshape(k, 1, 1, N_KV, hps, 1), shape).reshape(k, -1)

    def per_token(self, rows, heads):
        nb, nq, tq = self.nb, self.nq, self.tq
        k = rows.shape[0] // heads
        if self.sample:
            x = rows[:, :nb].reshape(heads, k, nb).transpose(1, 2, 0)
            return jnp.pad(x, ((0, 0), (0, 0), (0, SAMPLE_COLS - heads))).reshape(k, nb * SAMPLE_COLS)
        x = rows.reshape(N_KV, heads // N_KV, k, nb, nq, tq).transpose(2, 3, 4, 0, 1, 5)
        return x.reshape(k, -1)

    def qpos_row(self, heads):
        if self.sample:
            return jnp.full((1, self.nb * SAMPLE_COLS), self.q_off, F32)
        pos = (jnp.arange(self.nq, dtype=F32)[:, None] * self.tq + jnp.arange(self.tq, dtype=F32)[None, :])
        shape = (1, self.nb, self.nq, N_KV, heads // N_KV, self.tq)
        return jnp.broadcast_to(pos.reshape(1, 1, self.nq, 1, 1, self.tq), shape).reshape(1, -1)

    def columns(self, rows, heads, slopes, extra=()):
        x = self.per_token(rows, heads)
        grp = self.head_group(heads)
        gsel = self.per_head(np.stack([grp == 0, grp == 1]).astype(np.float32), heads)
        qf = jnp.concatenate([x * gsel[0:1], x * gsel[1:2]], axis=0)
        s2 = np.asarray(slopes, np.float64) * LOG2E
        hi = s2.astype(jnp.bfloat16).astype(np.float64)
        mid = (s2 - hi).astype(jnp.bfloat16).astype(np.float64)
        lo = (s2 - hi - mid).astype(jnp.bfloat16).astype(np.float64)
        pieces = np.stack([16 * hi, 16 * mid, 16 * lo, hi, mid, lo] + [np.zeros_like(hi)] * 10)
        qa = jnp.concatenate([(qf * (ATTN_SCALE * LOG2E)).astype(BF16), self.per_head(pieces, heads).astype(BF16),
                              jnp.zeros((KD - FEAT_POS - 16, x.shape[1]), BF16)], axis=0)
        meta = [self.per_head((hi + mid + lo)[None, :], heads), self.qpos_row(heads)] + list(extra)
        meta.append(jnp.zeros((8 - sum(m.shape[0] for m in meta), x.shape[1]), F32))
        return qa, qf, jnp.concatenate(meta, axis=0)

    def collapse(self, o, heads):
        nb, nq, tq = self.nb, self.nq, self.tq
        if self.sample:
            o = o.reshape(N_KV, HEAD_DIM, nb, SAMPLE_COLS)
            grp = self.head_group(heads)
            x = jnp.stack([o[grp[h], :, :, h] for h in range(heads)])
            return jnp.pad(x.reshape(heads * HEAD_DIM, nb), ((0, 0), (0, 128 - nb)))
        o = o.reshape(HEAD_DIM, nb, nq, N_KV, heads // N_KV, tq).transpose(3, 4, 0, 1, 2, 5)
        return o.reshape(heads * HEAD_DIM, nb * nq * tq)

    def tokens(self, rows):
        n = rows.shape[0]
        if self.sample:
            return rows[:, :self.nb].T.reshape(self.nb, 1, n)
        return rows.reshape(n, self.nb, self.t).transpose(1, 2, 0)

    def tile_cols(self, rows):
        if not self.sample:
            return rows
        n = rows.shape[0]
        return jnp.broadcast_to(rows[:, :self.nb, None], (n, self.nb, SAMPLE_COLS)).reshape(n, self.nb * SAMPLE_COLS)

    def from_tile_cols(self, o):
        if not self.sample:
            return o
        n = o.shape[0]
        return jnp.pad(o.reshape(n, self.nb, SAMPLE_COLS)[:, :, 0], ((0, 0), (0, 128 - self.nb)))


def _softmax_step(s, c, ok, vT, m, l, acc_ref):
    smax = jnp.max(s, axis=0, keepdims=True)
    live = smax > 0.5 * NEG
    if ok is not None:
        live = live & _row_flag(ok, s.shape[1])
    m_new = jnp.where(live, jnp.maximum(m, smax + c), m)
    shift = jnp.where(live, m_new - c, -NEG)
    p = jnp.exp2(s - shift)
    alpha = jnp.exp2(m - m_new)
    l = alpha * l + jnp.sum(p, axis=0, keepdims=True)
    acc_ref[...] = alpha * acc_ref[...] + jnp.dot(vT, p.astype(BF16), preferred_element_type=F32)
    return m_new, l


def _row_flag(ok, n):
    if ok.ndim == 0:
        return (jnp.zeros((1, n), I32) + ok.astype(I32)) > 0
    return ok


def _group(mixed):
    return 4 if mixed else 2


def _softmax_init(n):
    return jnp.full((1, n), NEG, F32), jnp.zeros((1, n), F32)


def _tile_groups(n_tiles, step, carry, group):
    def body(u, carry):
        carry = step(group * u, 0, None, carry)
        for j in range(1, group):
            t = group * u + j
            carry = step(jnp.minimum(t, n_tiles - 1), j, t < n_tiles, carry)
        return carry
    return lax.fori_loop(0, (n_tiles + group - 1) // group, body, carry)


def _dense_attend(k_ref, n_k, vT_ref, n_v, qa, slope, qpos, blk, pos0=0, mask_fn=None, bias=None):
    n = qa.shape[1]
    tile = k_ref.shape[2]
    rows = n_k * tile
    s = jnp.dot(k_ref[0, 0:n_k].reshape(rows, KD), qa, preferred_element_type=F32)
    if bias is not None:
        s = s + bias
    nblk = rows // blk
    s = s.reshape(nblk, blk, n)
    b3 = lax.broadcasted_iota(I32, (nblk, 1, 1), 0)
    base = pos0 + (b3 * blk) // tile * tile
    s = s + slope * (base - qpos).astype(F32)
    if mask_fn is not None:
        kpos = pos0 + b3 * blk + lax.broadcasted_iota(I32, (nblk, blk, n), 1)
        s = jnp.where(mask_fn(b3, kpos), s, NEG)
    s = s.reshape(rows, n)
    m = jnp.max(s, axis=0, keepdims=True)
    p = jnp.where(s > 0.5 * NEG, jnp.exp2(s - m), 0.0)
    l = jnp.sum(p, axis=0, keepdims=True)
    vT = jnp.concatenate([vT_ref[0, t] for t in range(n_v)], axis=1)
    return jnp.dot(vT, p.astype(BF16), preferred_element_type=F32) / jnp.maximum(l, 1e-30)


def _value_rows(vT_ref, n, g, mixed):
    if mixed:
        return vT_ref[0, n]
    return vT_ref[0, n, pl.ds(pl.multiple_of(g * HEAD_DIM, HEAD_DIM), HEAD_DIM), :]


def _top_rows(vals, rowi, k):
    sel = jnp.zeros(vals.shape, F32)
    for _ in range(k):
        m = jnp.max(vals, axis=0, keepdims=True)
        first = jnp.min(jnp.where(vals == m, rowi, 1 << 20), axis=0, keepdims=True)
        pick = (rowi == first) & (m > SENTINEL)
        sel = jnp.where(pick, 1.0, sel)
        vals = jnp.where(pick, SENTINEL, vals)
    return sel


def _moba_kernel(qa_ref, qf_ref, meta_ref, kmean_ref, k_ref, vT_ref, o_ref, sel_ref, acc_ref, *, span, q_off, mixed):
    i, g = pl.program_id(1), pl.program_id(2)
    n = qa_ref.shape[1]
    own = (q_off + i * span) // MOBA_BLOCK
    slope = meta_ref[0:1, :]
    qpos = meta_ref[1:2, :].astype(I32)
    nrow = kmean_ref.shape[1]
    gate = jnp.dot(kmean_ref[0], qf_ref[...], precision=lax.Precision.HIGHEST, preferred_element_type=F32)
    rowi = lax.broadcasted_iota(I32, (nrow, n), 0)
    sel_ref[...] = _top_rows(jnp.where(rowi < own, gate, SENTINEL), rowi, MOBA_TOPK)
    qa = qa_ref[...]
    if mixed:
        own_s = q_off // MOBA_BLOCK
        sel3 = sel_ref[0:own_s + 1, :][:, None, :] > 0.5
        o_ref[...] = _dense_attend(k_ref, own_s + 1, vT_ref, own_s + 1, qa, slope, qpos, KV_TILE,
                                   mask_fn=lambda b3, kpos: ((b3 < own_s) & sel3) | ((b3 >= own_s) & (kpos <= qpos)))
        return
    acc_ref[...] = jnp.zeros(acc_ref.shape, F32)

    def step(t, slot, ok, carry):
        s = jnp.dot(k_ref[0, t], qa, preferred_element_type=F32)
        c = slope * (t * KV_TILE - qpos).astype(F32)
        sel = sel_ref[pl.ds(t, 1), :] > 0.5
        ok = sel if ok is None else sel & _row_flag(ok, n)
        return _softmax_step(s, c, ok, _value_rows(vT_ref, t, g, mixed), *carry, acc_ref)

    m, l = _tile_groups(own, step, _softmax_init(n), _group(mixed))
    s = jnp.dot(k_ref[0, own], qa, preferred_element_type=F32)
    krow = lax.broadcasted_iota(I32, (KV_TILE, n), 0)
    s = jnp.where(krow <= qpos - own * KV_TILE, s, NEG)
    c = slope * (own * KV_TILE - qpos).astype(F32)
    _, l = _softmax_step(s, c, None, _value_rows(vT_ref, own, g, mixed), m, l, acc_ref)
    o_ref[...] = acc_ref[...] / jnp.maximum(l, 1e-30)


def _moba(cols, qa, qf, meta, kmean, k, vT):
    nb, nq = cols.nb, cols.nq
    n_hc, n = cols.n_hc(N_A_HEADS), cols.step_cols(N_A_HEADS)
    nT, nblk = k.shape[1], kmean.shape[1]
    vrows = N_KV * HEAD_DIM if cols.sample else HEAD_DIM
    col = lambda b, i, g: (0, (b * nq + i) * n_hc + g)
    return pl.pallas_call(
        functools.partial(_moba_kernel, span=cols.span, q_off=cols.q_off, mixed=cols.sample), grid=(nb, nq, n_hc),
        in_specs=[pl.BlockSpec((KD, n), col),
                  pl.BlockSpec((N_KV * HEAD_DIM, n), col),
                  pl.BlockSpec((8, n), col),
                  pl.BlockSpec((1, nblk, N_KV * HEAD_DIM), lambda b, i, g: (b, 0, 0)),
                  pl.BlockSpec((1, nT, KV_TILE, KD), lambda b, i, g: (b, 0, 0, 0)),
                  pl.BlockSpec((1, nT, N_KV * HEAD_DIM, KV_TILE), lambda b, i, g: (b, 0, 0, 0))],
        out_specs=pl.BlockSpec((vrows, n), col),
        out_shape=jax.ShapeDtypeStruct((vrows, qa.shape[1]), F32),
        scratch_shapes=[pltpu.VMEM((nblk, n), F32), pltpu.VMEM((vrows, n), F32)],
        compiler_params=_cparams(("parallel", "parallel", "arbitrary")), name="moba")(qa, qf, meta, kmean, k, vT)


def _cmp_kernel(r_ref, pe_ref, w1_ref, w2_ref, o_ref):
    n = o_ref.shape[1]
    hid = 2 * CMP_HIDDEN
    c = r_ref.shape[2]
    u = jnp.zeros((n, hid), F32)
    lo = jnp.zeros((n, hid), F32)
    for p in range(CMP_STRIDE):
        rp = r_ref[0, pl.ds(p, n, stride=CMP_STRIDE), :]
        cols = slice(p * c, (p + 1) * c)
        u = u + jnp.dot((rp + pe_ref[0:1, cols]).astype(BF16), w1_ref[cols, 0:hid], preferred_element_type=F32)
        lo = lo + jnp.dot((rp + pe_ref[1:2, cols]).astype(BF16), w1_ref[cols, hid:2 * hid],
                          preferred_element_type=F32)
    pre = u + pltpu.roll(lo, n - 1, 0)
    h = pre / (1.0 + jnp.exp(-pre))
    o_ref[0] = jnp.dot(h.astype(BF16), w2_ref[...], preferred_element_type=F32)


def _compress(raw, pe2, w1big, w2big, n_rows):
    nb, _, c = raw.shape
    return pl.pallas_call(
        _cmp_kernel, grid=(nb,),
        in_specs=[pl.BlockSpec((1, n_rows * CMP_STRIDE, c), lambda b: (b, 0, 0)),
                  pl.BlockSpec((2, 2048), lambda b: (0, 0)),
                  pl.BlockSpec((2048, 4 * CMP_HIDDEN), lambda b: (0, 0)),
                  pl.BlockSpec((2 * CMP_HIDDEN, 128), lambda b: (0, 0))],
        out_specs=pl.BlockSpec((1, n_rows, 128), lambda b: (b, 0, 0)),
        out_shape=jax.ShapeDtypeStruct((nb, n_rows, 128), F32),
        compiler_params=_cparams(("parallel",)), name="compress")(raw, pe2, w1big, w2big)


def _cmp_weights(pe, w1, w2):
    w1r = w1.reshape(2, CMP_STRIDE, HEAD_DIM, CMP_HIDDEN)
    eye = jnp.eye(N_KV, dtype=w1.dtype)
    big = jnp.einsum('hpdj,ge->hpdgej', w1r, eye)
    big = big.transpose(1, 3, 2, 0, 4, 5).reshape(CMP_STRIDE * N_KV * HEAD_DIM, 2 * N_KV * CMP_HIDDEN)
    w2big = jnp.einsum('jd,ge->gjed', w2, eye).reshape(N_KV * CMP_HIDDEN, N_KV * HEAD_DIM)
    pe2 = jnp.broadcast_to(pe.reshape(2, CMP_STRIDE, 1, HEAD_DIM), (2, CMP_STRIDE, N_KV, HEAD_DIM)).reshape(2, 2048)
    return pe2, big.astype(BF16), w2big.astype(BF16)


def _nsa_kernel(qa_ref, meta_ref, kc_ref, vcT_ref, taps_ref, grp_ref, ks_ref, vsT_ref, kw_ref, vwT_ref, o_ref,
                sel_ref, qs_ref, acc_ref, *, span, q_off, kw_base, mixed):
    i, g = pl.program_id(1), pl.program_id(2)
    n = qa_ref.shape[1]
    nq = n if mixed else n // (N_B_HEADS // N_KV)
    rep = n // nq
    t0 = q_off + i * span
    slope = meta_ref[0:1, :]
    qpos = meta_ref[1:2, :].astype(I32)
    qa = qa_ref[...]
    n_cmp = kc_ref.shape[1]
    n_tab = sel_ref.shape[0]

    def vrows(ref):
        if mixed:
            return ref[0]
        return ref[0, pl.ds(pl.multiple_of(g * HEAD_DIM, HEAD_DIM), HEAD_DIM), :]

    cend = CMP_STRIDE * lax.broadcasted_iota(I32, (n_cmp, n), 0) + (CMP_LEN - 1)
    cmask = cend <= qpos
    s = jnp.dot(kc_ref[0], qa[0:N_KV * HEAD_DIM, :], preferred_element_type=F32) + slope * (cend - qpos).astype(F32)
    s = jnp.where(cmask, s, NEG)
    p = jnp.where(cmask, jnp.exp2(s - jnp.max(s, axis=0, keepdims=True)), 0.0)
    p = p * (1.0 / jnp.maximum(jnp.sum(p, axis=0, keepdims=True), 1e-30))
    o_c = jnp.dot(vrows(vcT_ref), p.astype(BF16), preferred_element_type=F32)

    if mixed:
        imp = jnp.dot(p, grp_ref[...], precision=lax.Precision.HIGHEST, preferred_element_type=F32)
    else:
        imp = p[:, 0:nq]
        for r in range(1, rep):
            imp = imp + p[:, r * nq:(r + 1) * nq]
    p_slc = jnp.dot(taps_ref[...], imp, precision=lax.Precision.HIGHEST, preferred_element_type=F32)
    p_slc = jnp.concatenate([p_slc, jnp.full((n_tab - p_slc.shape[0], nq), SENTINEL, F32)], axis=0)
    rowi = lax.broadcasted_iota(I32, (n_tab, nq), 0)
    own = qpos[:, 0:nq] >> 6
    sel = _top_rows(jnp.where(rowi < own, p_slc, SENTINEL), rowi, SEL_TOPK)
    sel_ref[...] = jnp.where(rowi == own, 1.0, sel)

    gt = 1.0 / (1.0 + jnp.exp(-meta_ref[2:5, :]))
    if mixed:
        n_s = q_off // SEL_TILE + 1
        sel3 = sel_ref[0:n_s * (SEL_TILE // SEL_BLOCK), :][:, None, :] > 0.5
        o_s = _dense_attend(ks_ref, n_s, vsT_ref, n_s, qa, slope, qpos, SEL_BLOCK,
                            mask_fn=lambda b3, kpos: sel3 & (kpos <= qpos))
        n_w = min((q_off - kw_base) // KV_TILE + 1, kw_ref.shape[1])
        o_w = _dense_attend(kw_ref, n_w, vwT_ref, n_w, qa, slope, qpos, KV_TILE, pos0=kw_base,
                            mask_fn=lambda b3, kpos: (kpos <= qpos) & (kpos > qpos - WINDOW))
        o_ref[...] = gt[0:1, :] * o_c + gt[1:2, :] * o_s + gt[2:3, :] * o_w
        return

    per_tile = SEL_TILE // SEL_BLOCK
    last = (t0 + span - 1) // SEL_TILE
    for slot in range(qs_ref.shape[0]):
        qs_ref[slot] = qa
    acc_ref[...] = jnp.zeros(acc_ref.shape, F32)

    def sel_scores(t, slot):
        rows = sel_ref[pl.ds(pl.multiple_of(t * per_tile, per_tile), per_tile), :]
        bias = jnp.where(rows > 0.5, 0.0, NEG)
        if rep > 1:
            bias = jnp.concatenate([bias] * rep, axis=1)
        bias = jnp.concatenate([bias, jnp.zeros((per_tile, n), F32)], axis=0)
        qs_ref[slot, FEAT_SEL:FEAT_SEL + 2 * per_tile, :] = bias.astype(BF16)
        s = jnp.dot(ks_ref[0, t], qs_ref[slot], preferred_element_type=F32)
        return s, slope * (t * SEL_TILE - qpos).astype(F32)

    def sel_step(t, slot, ok, carry):
        s, c = sel_scores(t, slot)
        return _softmax_step(s, c, ok, _value_rows(vsT_ref, t, g, mixed), *carry, acc_ref)

    m, l = _tile_groups(last, sel_step, _softmax_init(n), qs_ref.shape[0])
    s, c = sel_scores(last, 0)
    srow = lax.broadcasted_iota(I32, (SEL_TILE, n), 0)
    s = jnp.where(srow <= qpos - last * SEL_TILE, s, NEG)
    _, l = _softmax_step(s, c, None, _value_rows(vsT_ref, last, g, mixed), m, l, acc_ref)
    o_s = acc_ref[...] / jnp.maximum(l, 1e-30)

    lo = jnp.maximum(t0 - (WINDOW - 1) - kw_base, 0) // KV_TILE
    hi = jnp.minimum((t0 + span - 1 - kw_base) // KV_TILE + 1, kw_ref.shape[1])
    wrow = lax.broadcasted_iota(I32, (KV_TILE, n), 0)
    acc_ref[...] = jnp.zeros(acc_ref.shape, F32)

    def win_body(t, carry):
        s = jnp.dot(kw_ref[0, t], qa, preferred_element_type=F32)
        base = kw_base + t * KV_TILE
        dist = qpos - base - wrow
        s = jnp.where((dist >= 0) & (dist < WINDOW), s, NEG)
        return _softmax_step(s, slope * (base - qpos).astype(F32), None, _value_rows(vwT_ref, t, g, mixed),
                             *carry, acc_ref)

    _, l = lax.fori_loop(lo, hi, win_body, _softmax_init(n))
    o_w = acc_ref[...] / jnp.maximum(l, 1e-30)

    o_ref[...] = gt[0:1, :] * o_c + gt[1:2, :] * o_s + gt[2:3, :] * o_w


def _taps_matrix(n_sel, n_cmp):
    m = np.zeros((n_sel, n_cmp), np.float32)
    for j in range(n_sel):
        for off, w in ((-1, 1.0), (0, 2.0), (1, 2.0), (2, 2.0), (3, 1.0)):
            c = 4 * j + off
            if 0 <= c < n_cmp:
                m[j, c] = w
    return jnp.asarray(m)


def _group_matrix(cols):
    grp = cols.head_group(N_B_HEADS)
    m = np.zeros((SAMPLE_COLS, SAMPLE_COLS), np.float32)
    m[:N_B_HEADS, :N_B_HEADS] = grp[:, None] == grp[None, :]
    return jnp.asarray(m)


def _nsa(cols, qa, meta, kc, vcT, ks, vsT, kw, vwT, kw_base):
    nb, nq = cols.nb, cols.nq
    n_hc, n = cols.n_hc(N_B_HEADS), cols.step_cols(N_B_HEADS)
    nTs, nTw = ks.shape[1], kw.shape[1]
    n_cmp = kc.shape[1]
    n_sel = n_cmp // 4
    n_tab = n_sel + SEL_TILE // SEL_BLOCK
    nsel_cols = n if cols.sample else n // (N_B_HEADS // N_KV)
    vrows = N_KV * HEAD_DIM if cols.sample else HEAD_DIM
    col = lambda b, i, g: (0, (b * nq + i) * n_hc + g)
    seq3 = lambda b, i, g: (b, 0, 0)
    seq4 = lambda b, i, g: (b, 0, 0, 0)
    return pl.pallas_call(
        functools.partial(_nsa_kernel, span=cols.span, q_off=cols.q_off, kw_base=kw_base, mixed=cols.sample),
        grid=(nb, nq, n_hc),
        in_specs=[pl.BlockSpec((KD, n), col),
                  pl.BlockSpec((8, n), col),
                  pl.BlockSpec((1, n_cmp, N_KV * HEAD_DIM), seq3),
                  pl.BlockSpec((1, N_KV * HEAD_DIM, n_cmp), seq3),
                  pl.BlockSpec((n_sel, n_cmp), lambda b, i, g: (0, 0)),
                  pl.BlockSpec((SAMPLE_COLS, SAMPLE_COLS), lambda b, i, g: (0, 0)),
                  pl.BlockSpec((1, nTs, SEL_TILE, KD), seq4),
                  pl.BlockSpec((1, nTs, N_KV * HEAD_DIM, SEL_TILE), seq4),
                  pl.BlockSpec((1, nTw, KV_TILE, KD), seq4),
                  pl.BlockSpec((1, nTw, N_KV * HEAD_DIM, KV_TILE), seq4)],
        out_specs=pl.BlockSpec((vrows, n), col),
        out_shape=jax.ShapeDtypeStruct((vrows, qa.shape[1]), F32),
        scratch_shapes=[pltpu.VMEM((n_tab, nsel_cols), F32), pltpu.VMEM((_group(cols.sample), KD, n), BF16), pltpu.VMEM((vrows, n), F32)],
        compiler_params=_cparams(("parallel", "parallel", "arbitrary")), name="nsa")(
            qa, meta, kc, vcT, _taps_matrix(n_sel, n_cmp), _group_matrix(cols), ks, vsT, kw, vwT)


def _dsa_kernel(qi_ref, wi_ref, ki_ref, qa_ref, meta_ref, k_ref, vT_ref, o_ref, key_ref, row_ref, acc_ref,
                *, span, q_off, topk, mixed, hc_per_group):
    i, hc = pl.program_id(1), pl.program_id(2)
    n = qa_ref.shape[1]
    nq = qi_ref.shape[1]
    rep = n // nq
    t0 = q_off + i * span
    n_kv = (t0 + span - 1) // KV_TILE + 1
    krow = lax.broadcasted_iota(I32, (KV_TILE, nq), 0)

    def search(count, shape):
        kf = float(topk)
        thr = jnp.where(count(lambda key, kpos: key >= 0) >= kf, 0, INT_MIN).astype(I32)

        def bit_body(j, thr):
            cand = thr | (1 << (30 - j))
            return jnp.where(count(lambda key, kpos: key >= cand) >= kf, cand, thr)
        thr = lax.fori_loop(0, 31, bit_body, thr)
        need = kf - count(lambda key, kpos: key > thr)

        def tie_search():
            def idx_body(j, jb):
                cand = jb | (1 << (14 - j))
                c = count(lambda key, kpos: (key == thr) & (kpos < cand))
                return jnp.where(c <= need, cand, jb)
            return lax.fori_loop(0, 15, idx_body, jnp.zeros(shape, I32))

        n_ge = count(lambda key, kpos: key >= thr)
        jb = lax.cond(jnp.max(n_ge) > kf, tie_search, lambda: jnp.full(shape, (1 << 15) - 1, I32))
        return thr, jb

    @pl.when(hc == 0)
    def _():
        qpos = t0 + (lax.broadcasted_iota(I32, (1, nq), 1) if span > 1 else jnp.zeros((1, nq), I32))
        w = wi_ref[...] * (IDX_SCALE * N_IDX_HEADS ** -0.5)
        if mixed:
            qi = qi_ref[...].astype(BF16)
        else:
            qi = [qi_ref[a * IDX_DIM:(a + 1) * IDX_DIM, :].astype(BF16) for a in range(N_IDX_HEADS)]
        if mixed:
            row_ref[...] = jnp.full(row_ref.shape, INT_MIN, I32)

        def score_body(t, _):
            ki = ki_ref[0, t]
            if mixed:
                rel = jnp.dot(ki, qi, preferred_element_type=F32)
                acc = jnp.sum(jnp.maximum(rel, 0.0) * w[0:1, :], axis=1, keepdims=True)
                acc = jnp.broadcast_to(acc, (KV_TILE, nq))
            else:
                acc = jnp.zeros((KV_TILE, nq), F32)
                for a in range(N_IDX_HEADS):
                    rel = jnp.dot(ki, qi[a], preferred_element_type=F32)
                    acc = acc + jnp.maximum(rel, 0.0) * w[a:a + 1, :]
            sc = jnp.where(t * KV_TILE + krow <= qpos, acc, NEG)
            bits = lax.bitcast_convert_type(sc, I32)
            key = bits ^ ((bits >> 31) & 0x7FFFFFFF)
            key_ref[t] = key
            if mixed:
                row_ref[pl.ds(t, 1), :] = jnp.transpose(key)[0:1, :]
            return 0
        lax.fori_loop(0, n_kv, score_body, 0)

        if mixed:
            keys = row_ref[...]
            kpos_r = (lax.broadcasted_iota(I32, keys.shape, 0) * KV_TILE + lax.broadcasted_iota(I32, keys.shape, 1))

            def count(pred):
                c = jnp.sum(jnp.where(pred(keys, kpos_r), 1.0, 0.0), axis=0, keepdims=True)
                return jnp.sum(c, axis=1, keepdims=True)
            thr, jb = search(count, (1, 1))
        else:
            n_grp = (n_kv + COUNT_GROUP - 1) // COUNT_GROUP

            def pad_body(t, _):
                key_ref[t] = jnp.full((KV_TILE, nq), INT_MIN, I32)
                return 0
            lax.fori_loop(n_kv, n_grp * COUNT_GROUP, pad_body, 0)

            def count(pred):
                def body(u, c):
                    for j in range(COUNT_GROUP):
                        t = u * COUNT_GROUP + j
                        hit = pred(key_ref[t], t * KV_TILE + krow)
                        c = c + jnp.sum(jnp.where(hit, 1.0, 0.0), axis=0, keepdims=True)
                    return c
                return lax.fori_loop(0, n_grp, body, jnp.zeros((1, nq), F32))
            thr, jb = search(count, (1, nq))

        def bias_body(t, _):
            key = key_ref[t]
            kpos = t * KV_TILE + krow
            keep = ((key > thr) | ((key == thr) & (kpos < jb))) & (kpos <= qpos)
            key_ref[t] = lax.bitcast_convert_type(jnp.where(keep, 0.0, NEG), I32)
            return 0
        lax.fori_loop(0, n_kv, bias_body, 0)

    g = hc // hc_per_group
    slope = meta_ref[0:1, :]
    qpos_c = meta_ref[1:2, :].astype(I32)
    qa = qa_ref[...]
    if mixed:
        n_t = q_off // KV_TILE + 1
        bias = lax.bitcast_convert_type(key_ref[0:n_t], F32).reshape(n_t * KV_TILE, nq)
        o_ref[...] = _dense_attend(k_ref, n_t, vT_ref, n_t, qa, slope, qpos_c, KV_TILE, bias=bias)
        return
    acc_ref[...] = jnp.zeros(acc_ref.shape, F32)

    def step(t, slot, ok, carry):
        bias = lax.bitcast_convert_type(key_ref[t], F32)
        if rep > 1:
            bias = jnp.concatenate([bias] * rep, axis=1)
        s = jnp.dot(k_ref[0, t], qa, preferred_element_type=F32) + bias
        c = slope * (t * KV_TILE - qpos_c).astype(F32)
        return _softmax_step(s, c, ok, _value_rows(vT_ref, t, g, mixed), *carry, acc_ref)
    _, l = _tile_groups(n_kv, step, _softmax_init(n), _group(mixed))
    o_ref[...] = acc_ref[...] / jnp.maximum(l, 1e-30)


def _dsa(cols, qi, qi_row, wi, wi_row, ki, qa, meta, k, vT, topk):
    nb, nq = cols.nb, cols.nq
    n_hc, n = cols.n_hc(N_C_HEADS), cols.step_cols(N_C_HEADS)
    ncol = SAMPLE_COLS if cols.sample else cols.tq
    nT = k.shape[1]
    assert nT % COUNT_GROUP == 0
    vrows = N_KV * HEAD_DIM if cols.sample else HEAD_DIM
    qi_rows = IDX_DIM if cols.sample else N_IDX_HEADS * IDX_DIM
    qi_b, wi_b = qi_row // qi_rows, wi_row // N_IDX_HEADS
    col = lambda b, i, h: (0, (b * nq + i) * n_hc + h)
    seq4 = lambda b, i, h: (b, 0, 0, 0)
    return pl.pallas_call(
        functools.partial(_dsa_kernel, span=cols.span, q_off=cols.q_off, topk=topk, mixed=cols.sample,
                          hc_per_group=max(n_hc // N_KV, 1)),
        grid=(nb, nq, n_hc),
        in_specs=[pl.BlockSpec((qi_rows, ncol), lambda b, i, h: (qi_b, b * nq + i)),
                  pl.BlockSpec((N_IDX_HEADS, ncol), lambda b, i, h: (wi_b, b * nq + i)),
                  pl.BlockSpec((1, nT, KV_TILE, IDX_DIM), seq4),
                  pl.BlockSpec((KD, n), col),
                  pl.BlockSpec((8, n), col),
                  pl.BlockSpec((1, nT, KV_TILE, KD), seq4),
                  pl.BlockSpec((1, nT, N_KV * HEAD_DIM, KV_TILE), seq4)],
        out_specs=pl.BlockSpec((vrows, n), col),
        out_shape=jax.ShapeDtypeStruct((vrows, qa.shape[1]), F32),
        scratch_shapes=[pltpu.VMEM((nT, KV_TILE, ncol), I32), pltpu.VMEM((-(-nT // 8) * 8, KV_TILE), I32),
                        pltpu.VMEM((vrows, n), F32)],
        compiler_params=_cparams(("parallel", "parallel", "arbitrary")), name="dsa")(
            qi, wi, ki, qa, meta, k, vT)


def _mem_kernel(q_ref, mk_ref, mvT_ref, o_ref):
    for h in range(N_MEM_HEADS):
        rows = slice(h * HEAD_DIM, (h + 1) * HEAD_DIM)
        q = (q_ref[rows, :] * ATTN_SCALE).astype(BF16)
        s = jnp.dot(mk_ref[0, h].astype(BF16), q, preferred_element_type=F32)
        p = jnp.exp(s - jnp.max(s, axis=0, keepdims=True))
        l = jnp.sum(p, axis=0, keepdims=True)
        o_ref[rows, :] = jnp.dot(mvT_ref[0, h].astype(BF16), p.astype(BF16), preferred_element_type=F32) / l


def _mem(hT, q_row, mk, mvT, nb, nq, tq):
    n_mem = mk.shape[2]
    n = N_MEM_HEADS * HEAD_DIM
    qb = q_row // n
    return pl.pallas_call(
        _mem_kernel, grid=(nb, nq),
        in_specs=[pl.BlockSpec((n, tq), lambda b, i: (qb, b * nq + i)),
                  pl.BlockSpec((1, N_MEM_HEADS, n_mem, HEAD_DIM), lambda b, i: (b, 0, 0, 0)),
                  pl.BlockSpec((1, N_MEM_HEADS, HEAD_DIM, n_mem), lambda b, i: (b, 0, 0, 0))],
        out_specs=pl.BlockSpec((n, tq), lambda b, i: (0, b * nq + i)),
        out_shape=jax.ShapeDtypeStruct((n, nb * nq * tq), F32),
        compiler_params=_cparams(("parallel", "parallel")), name="mem")(hT, mk, mvT)


def _mem_branch(cols, hT, q_row, mk, mvT):
    n = N_MEM_HEADS * HEAD_DIM
    if cols.sample:
        o = _mem(cols.tile_cols(hT[q_row:q_row + n]), 0, mk, mvT, cols.nb, 1, SAMPLE_COLS)
        return cols.from_tile_cols(o)
    return _mem(hT, q_row, mk, mvT, cols.nb, cols.nq, cols.tq)


def _even_wT(w_in):
    sizes = (384, 128, 128, 384, 384, 128, 128, 128, 128, 128, 128, 18, 384, 256, 256)
    offs = np.cumsum((0,) + sizes)
    (a_q, a_k, a_v, a_z, b_q, b_kc, b_vc, b_ks, b_vs, b_kw, b_vw, b_g, b_z, m_q, m_z) = [
        w_in[:, offs[n]:offs[n + 1]] for n in range(len(sizes))]
    d = w_in.shape[0]
    bg = jnp.pad(b_g, ((0, 0), (0, 32 - b_g.shape[1])))
    cols = [a_q, b_q, m_q, a_z, b_z, m_z, a_k, a_v, b_kc, b_vc, b_ks, b_vs, b_kw, b_vw, bg]
    w = jnp.concatenate(cols, axis=1)
    w = jnp.pad(w, ((0, 0), (0, EVEN_TOTAL - w.shape[1])))
    return w.T


def _odd_wT(w_in):
    sizes = (768, 128, 128, 512, 64, 8, 768, 256, 256)
    offs = np.cumsum((0,) + sizes)
    (c_q, c_k, c_v, c_qi, c_ki, c_wi, c_z, m_q, m_z) = [w_in[:, offs[n]:offs[n + 1]] for n in range(len(sizes))]
    w = jnp.concatenate([c_q, m_q, c_z, m_z, c_qi, c_k, c_v, c_ki, c_wi], axis=1)
    w = jnp.pad(w, ((0, 0), (0, ODD_TOTAL - w.shape[1])))
    return w.T


def _keys(cols, new_tok, pool, page_table, tile, mode):
    if cols.sample:
        return _gather(pool, page_table, new_tok, mode, tile)
    return _kprep(new_tok, tile, mode)


def _values(cols, new_tok, pool, page_table, tile):
    if cols.sample:
        return _gather(pool, page_table, new_tok, 'v', tile)
    return _vT_tiles(new_tok, tile)


def _raw_rows(cols, new_tok, pool, page_table):
    if cols.sample:
        return _gather(pool, page_table, None, 'raw', KV_TILE)
    return new_tok


def _gate_cols(cols, gT):
    return [cols.per_token(gT, N_B_HEADS)]


def _even_layer(cols, xT, mk, mvT, past, page_table, wT, woT, g, b, cmpw, alpha):
    R = EVEN_ROWS
    nb = cols.nb
    hT = _mm(wT, xT, 640, min(1024, xT.shape[1]))
    names = ('a_k', 'a_v', 'b_kc', 'b_vc', 'b_ks', 'b_vs')
    new = [cols.tokens(hT[R[nm]:R[nm] + 128]) for nm in names]
    kw_new, vw_new = cols.tokens(hT[R['b_kw']:R['b_kw'] + 128]), cols.tokens(hT[R['b_vw']:R['b_vw'] + 128])
    pools = past[:6] if cols.sample else [None] * 6
    if cols.sample:
        kw = jnp.concatenate([past[6].reshape(nb, -1, 128), kw_new], axis=1)
        vw = jnp.concatenate([past[7].reshape(nb, -1, 128), vw_new], axis=1)
        n_win = past[6].shape[1]
        kw_base = cols.q_off - n_win
        win_out = (kw[:, -n_win:], vw[:, -n_win:])
        pad = -kw.shape[1] % 1024
        kw = jnp.pad(kw, ((0, 0), (0, pad), (0, 0)))
        vw = jnp.pad(vw, ((0, 0), (0, pad), (0, 0)))
    else:
        kw, vw, kw_base = kw_new, vw_new, 0
        n_win = min(WINDOW, cols.t)
        win_out = (kw[:, -n_win:], vw[:, -n_win:])

    sl = _alibi(N_A_HEADS + N_B_HEADS)
    ak, amean = _keys(cols, new[0], pools[0], page_table, KV_TILE, 'k')
    avT = _values(cols, new[1], pools[1], page_table, KV_TILE)
    qa, qf, meta = cols.columns(hT[R['a_q']:R['a_q'] + 384], N_A_HEADS, sl[0::2])
    o_a = _moba(cols, qa, qf, meta, amean, ak, avT)

    pe2k, w1k, w2k = cmpw[0]
    pe2v, w1v, w2v = cmpw[1]
    n_cmp = max(cols.q_off, cols.t) // CMP_STRIDE
    kc_raw = _raw_rows(cols, new[2], pools[2], page_table)
    vc_raw = _raw_rows(cols, new[3], pools[3], page_table)
    kc = _compress(kc_raw, pe2k, w1k, w2k, n_cmp).astype(BF16)
    vcT = _compress(vc_raw, pe2v, w1v, w2v, n_cmp).transpose(0, 2, 1).astype(BF16)
    ks = _keys(cols, new[4], pools[4], page_table, SEL_TILE, 'ks')
    vsT = _values(cols, new[5], pools[5], page_table, SEL_TILE)
    kwt = _kprep(kw, KV_TILE, 'ks')
    qa, _, meta = cols.columns(hT[R['b_q']:R['b_q'] + 384], N_B_HEADS, sl[1::2],
                               extra=_gate_cols(cols, hT[R['b_g']:R['b_g'] + 18]))
    o_b = _nsa(cols, qa, meta, kc, vcT, ks, vsT, kwt, _vT_tiles(vw, KV_TILE), kw_base)

    o_m = _mem_branch(cols, hT, R['m_q'], mk, mvT)
    parts = [cols.collapse(o_a, N_A_HEADS), cols.collapse(o_b, N_B_HEADS), o_m]
    yT = _outln(parts, hT, R['z'], xT, woT, g, b, alpha, min(512, xT.shape[1]))
    state = [n.reshape(nb, -1, N_KV, HEAD_DIM) for n in new]
    state += [w.reshape(nb, -1, N_KV, HEAD_DIM) for w in win_out]
    return yT, state


def _odd_layer(cols, xT, mk, mvT, past, page_table, wT, woT, g, b, alpha):
    R = ODD_ROWS
    nb = cols.nb
    hT = _mm(wT, xT, 768, min(1024, xT.shape[1]))
    new = [cols.tokens(hT[R['c_k']:R['c_k'] + 128]), cols.tokens(hT[R['c_v']:R['c_v'] + 128]),
           cols.tokens(hT[R['c_ki']:R['c_ki'] + IDX_DIM])]
    pools = past if cols.sample else [None] * 3
    topk = min(DSA_TOPK, (cols.q_off + cols.t) // 4)
    ck = _keys(cols, new[0], pools[0], page_table, KV_TILE, 'ks')
    cvT = _values(cols, new[1], pools[1], page_table, KV_TILE)
    if cols.sample:
        ki = _gather(pools[2], page_table, new[2], 'i', KV_TILE)
    else:
        ki = new[2].reshape(nb, -1, KV_TILE, IDX_DIM).astype(BF16)
    qa, _, meta = cols.columns(hT[R['c_q']:R['c_q'] + 768], N_C_HEADS, _alibi(N_C_HEADS))
    if cols.sample:
        qi = cols.per_token(hT[R['c_qi']:R['c_qi'] + 512], N_IDX_HEADS)
        wi = jnp.pad(cols.per_token(hT[R['c_wi']:R['c_wi'] + 8], N_IDX_HEADS), ((0, 7), (0, 0)))
        o_c = _dsa(cols, qi, 0, wi, 0, ki, qa, meta, ck, cvT, topk)
    else:
        o_c = _dsa(cols, hT, R['c_qi'], hT, R['c_wi'], ki, qa, meta, ck, cvT, topk)
    o_m = _mem_branch(cols, hT, R['m_q'], mk, mvT)
    parts = [cols.collapse(o_c, N_C_HEADS), o_m]
    yT = _outln(parts, hT, R['z'], xT, woT, g, b, alpha, min(512, xT.shape[1]))
    state = [new[0].reshape(nb, -1, N_KV, HEAD_DIM), new[1].reshape(nb, -1, N_KV, HEAD_DIM), new[2]]
    return yT, state


def kernel(x_prompt, x_sample, cache_a_k, cache_a_v, cache_b_cmp_k, cache_b_cmp_v, cache_b_sel_k, cache_b_sel_v,
           state_b_win_k, state_b_win_v, cache_c_k, cache_c_v, cache_c_idx_k, cache_mem_k, cache_mem_v,
           page_table, mem_prompt, w_in_even, w_out_even, w_in_odd, w_out_odd, w_mem_kv, ln_g, ln_b,
           cmp_pe, cmp_w1, cmp_w2):
    bp, seq, d = x_prompt.shape
    bs = x_sample.shape[0]
    depth = w_mem_kv.shape[0]
    n_mem = mem_prompt.shape[1]
    past_len = page_table.shape[1] * cache_a_k.shape[2]
    alpha = (2.0 * depth) ** 0.25
    assert x_sample.shape[1] == 1 and bs <= 128 and seq % 1024 == 0 and past_len % 1024 == 0

    gp = _Cols(bp, seq, 0)
    gs = _Cols(bs, 1, past_len)
    xp = x_prompt.reshape(bp * seq, d).T
    xs = jnp.pad(x_sample.reshape(bs, d).T, ((0, 0), (0, 128 - bs)))
    n_pool = cache_a_k.shape[1]

    def pages(c):
        return jnp.moveaxis(c.reshape(c.shape[:3] + (-1,)), 2, 3).reshape(c.shape[0] * n_pool, -1, c.shape[2])

    even_paged = tuple(pages(c) for c in (cache_a_k, cache_a_v, cache_b_cmp_k, cache_b_cmp_v, cache_b_sel_k,
                                          cache_b_sel_v))
    odd_paged = tuple(pages(c) for c in (cache_c_k, cache_c_v, cache_c_idx_k))
    ev_p, ev_s, od_p, od_s, mk_list, mv_list = [], [], [], [], [], []
    for layer in range(depth):
        j = layer // 2
        mem_kv = _mm(mem_prompt.reshape(bp * n_mem, d), w_mem_kv[layer], bp * n_mem, 2 * N_MEM_HEADS * HEAD_DIM)
        mem_kv = mem_kv.reshape(bp, n_mem, 2, N_MEM_HEADS, HEAD_DIM)
        mk_p, mv_p = mem_kv[:, :, 0], mem_kv[:, :, 1]
        mk_list.append(mk_p)
        mv_list.append(mv_p)
        mem_p = (mk_p.transpose(0, 2, 1, 3), mv_p.transpose(0, 2, 3, 1))
        mem_s = (cache_mem_k[layer].transpose(0, 2, 1, 3), cache_mem_v[layer].transpose(0, 2, 3, 1))
        g = ln_g[layer].reshape(d, 1)
        b = ln_b[layer].reshape(d, 1)
        if layer % 2 == 0:
            wT = _even_wT(w_in_even[j])
            woT = w_out_even[j].T
            cmpw = [_cmp_weights(cmp_pe[j, c], cmp_w1[j, c], cmp_w2[j, c]) for c in range(2)]
            past = even_paged + (state_b_win_k[j], state_b_win_v[j])
            xp, st_p = _even_layer(gp, xp, *mem_p, None, None, wT.astype(BF16), woT.astype(BF16), g, b, cmpw, alpha)
            xs, st_s = _even_layer(gs, xs, *mem_s, past, page_table + j * n_pool, wT, woT, g, b, cmpw, alpha)
            ev_p.append(st_p)
            ev_s.append(st_s)
        else:
            wT = _odd_wT(w_in_odd[j])
            woT = w_out_odd[j].T
            past = odd_paged
            xp, st_p = _odd_layer(gp, xp, *mem_p, None, None, wT.astype(BF16), woT.astype(BF16), g, b, alpha)
            xs, st_s = _odd_layer(gs, xs, *mem_s, past, page_table + j * n_pool, wT, woT, g, b, alpha)
            od_p.append(st_p)
            od_s.append(st_s)
    y_p = xp.T.reshape(bp, seq, d)
    y_s = xs[:, :bs].T.reshape(bs, 1, d)
    ev_p = [jnp.stack(l) for l in zip(*ev_p)]
    ev_s = [jnp.stack(l) for l in zip(*ev_s)]
    od_p = [jnp.stack(l) for l in zip(*od_p)]
    od_s = [jnp.stack(l) for l in zip(*od_s)]
    out = [y_p, y_s]
    for p, s in zip(ev_p, ev_s):
        out += [p, s]
    for p, s in zip(od_p, od_s):
        out += [p, s]
    out += [jnp.stack(mk_list), jnp.stack(mv_list)]
    return tuple(out)
```

```python
import functools

import numpy as np
import jax
import jax.numpy as jnp
from jax import lax
from jax.experimental import pallas as pl
from jax.experimental.pallas import tpu as pltpu

F32 = jnp.float32
BF16 = jnp.bfloat16
I32 = jnp.int32

HEAD_DIM = 64
N_MEM_HEADS = 4
N_A_HEADS = 6
N_B_HEADS = 6
N_C_HEADS = 12
N_KV = 2
N_IDX_HEADS = 8
IDX_DIM = 64
MOBA_BLOCK = 256
MOBA_TOPK = 3
CMP_STRIDE = 16
CMP_LEN = 32
CMP_HIDDEN = 128
SEL_BLOCK = 64
SEL_TOPK = 15
WINDOW = 512
DSA_TOPK = 256
ATTN_SCALE = HEAD_DIM ** -0.5
IDX_SCALE = IDX_DIM ** -0.5
LN_EPS = 1e-5
LOG2E = 1.4426950408889634
NEG = -1e30
SENTINEL = -3e38
INT_MIN = -(2 ** 31)

KV_TILE = 256
SEL_TILE = 512
KD = 256
FEAT_POS = 128
FEAT_SEL = 144
COUNT_GROUP = 4
SAMPLE_COLS = 128
META_ROWS = 16
META_PIECES = 8
PAGES_PER_STEP = 8
GATHER_PAGES = 16
VMEM_LIMIT = 56 * 1024 * 1024

EVEN_ROWS = dict(a_q=0, b_q=384, m_q=768, z=1024, a_k=2048, a_v=2176, b_kc=2304, b_vc=2432,
                 b_ks=2560, b_vs=2688, b_kw=2816, b_vw=2944, b_g=3072)
EVEN_TOTAL = 3200
ODD_ROWS = dict(c_q=0, m_q=768, z=1024, c_qi=2048, c_k=2560, c_v=2688, c_ki=2816, c_wi=2880)
ODD_TOTAL = 3072


def _cparams(sem):
    return pltpu.CompilerParams(dimension_semantics=sem, vmem_limit_bytes=VMEM_LIMIT)


def _alibi(n):
    return 2.0 ** (-8.0 * np.arange(1, n + 1) / n)


def _mm_kernel(a_ref, b_ref, o_ref, *, precise):
    if precise:
        o_ref[...] = jnp.dot(a_ref[...], b_ref[...], precision=lax.Precision.HIGHEST, preferred_element_type=F32)
    else:
        o_ref[...] = jnp.dot(a_ref[...].astype(BF16), b_ref[...].astype(BF16), preferred_element_type=F32)


def _mm(a, b, bm, bn):
    m, k = a.shape
    n = b.shape[1]
    return pl.pallas_call(
        functools.partial(_mm_kernel, precise=a.dtype == F32 and b.dtype == F32 and n <= 128), grid=(n // bn, m // bm),
        in_specs=[pl.BlockSpec((bm, k), lambda j, i: (i, 0)), pl.BlockSpec((k, bn), lambda j, i: (0, j))],
        out_specs=pl.BlockSpec((bm, bn), lambda j, i: (i, j)),
        out_shape=jax.ShapeDtypeStruct((m, n), F32),
        compiler_params=_cparams(("parallel", "parallel")), name="mm")(a, b)


def _outln_kernel(*refs, n_parts, alpha):
    o_refs = refs[:n_parts]
    z_ref, x_ref, w_ref, g_ref, b_ref, y_ref = refs[n_parts:]
    o = jnp.concatenate([r[...] for r in o_refs], axis=0)
    z = z_ref[...]
    mixed = o * (z / (1.0 + jnp.exp(-z)))
    if w_ref.dtype == F32:
        y = jnp.dot(w_ref[...], mixed, precision=lax.Precision.HIGHEST, preferred_element_type=F32)
    else:
        y = jnp.dot(w_ref[...], mixed.astype(BF16), preferred_element_type=F32)
    y = alpha * x_ref[...] + y
    mu = jnp.mean(y, axis=0, keepdims=True)
    d = y - mu
    var = jnp.mean(d * d, axis=0, keepdims=True)
    y_ref[...] = d * lax.rsqrt(var + LN_EPS) * g_ref[...] + b_ref[...]


def _outln(o_parts, hT, z_row, xT, woT, g, b, alpha, bn):
    d, n = xT.shape
    zb = z_row // d
    in_specs = [pl.BlockSpec((p.shape[0], bn), lambda j: (0, j)) for p in o_parts]
    in_specs += [pl.BlockSpec((d, bn), lambda j: (zb, j)),
                 pl.BlockSpec((d, bn), lambda j: (0, j)),
                 pl.BlockSpec((d, d), lambda j: (0, 0)),
                 pl.BlockSpec((d, 1), lambda j: (0, 0)),
                 pl.BlockSpec((d, 1), lambda j: (0, 0))]
    return pl.pallas_call(
        functools.partial(_outln_kernel, n_parts=len(o_parts), alpha=alpha), grid=(n // bn,),
        in_specs=in_specs, out_specs=pl.BlockSpec((d, bn), lambda j: (0, j)),
        out_shape=jax.ShapeDtypeStruct((d, n), F32),
        compiler_params=_cparams(("parallel",)), name="outln")(*o_parts, hT, xT, woT, g, b)


def _key_features(tile):
    r = lax.broadcasted_iota(I32, (tile, 128), 0)
    c = lax.broadcasted_iota(I32, (tile, 128), 1)
    sel_c = c - (FEAT_SEL - FEAT_POS)
    feat = jnp.where(c < 3, r >> 4, jnp.where(c < 6, r & 15, jnp.where(sel_c == (r >> 6), 1, 0)))
    return feat.astype(F32).astype(BF16)


def _emit_tiles(x, outs, mode, tile, t0=0):
    nt = x.shape[0] // tile
    if mode == 'raw':
        outs[0][0, t0 * tile:(t0 + nt) * tile, :] = x
        return
    feat = _key_features(tile) if mode in ('k', 'ks') else None
    for t in range(nt):
        xt = x[t * tile:(t + 1) * tile]
        if mode in ('k', 'ks'):
            outs[0][0, t0 + t] = jnp.concatenate([xt.astype(BF16), feat], axis=1)
        elif mode == 'v':
            outs[0][0, t0 + t] = jnp.transpose(xt).astype(BF16)
        else:
            outs[0][0, t0 + t] = xt.astype(BF16)
    if mode == 'k':
        nblk = x.shape[0] // MOBA_BLOCK
        outs[1][0, 0, t0:t0 + nblk, :] = jnp.mean(x.reshape(nblk, MOBA_BLOCK, x.shape[1]), axis=1)


def _zero_tiles(outs, mode, tile, t0, t1):
    for t in range(t0, t1):
        outs[0][0, t] = jnp.zeros(outs[0].shape[2:], outs[0].dtype)
    if mode == 'k':
        outs[1][0, 0, t0:t1, :] = jnp.zeros((t1 - t0, outs[1].shape[3]), F32)


def _tile_outputs(mode, nb, n_steps, rows, tile, c):
    nt = rows // tile
    if mode == 'raw':
        return ([pl.BlockSpec((1, rows, c), lambda b, j, *_: (b, j, 0))],
                [jax.ShapeDtypeStruct((nb, n_steps * rows, c), F32)])
    if mode in ('k', 'ks'):
        shape, blk = (nb, n_steps * nt, tile, KD), (1, nt, tile, KD)
    elif mode == 'v':
        shape, blk = (nb, n_steps * nt, c, tile), (1, nt, c, tile)
    else:
        shape, blk = (nb, n_steps * nt, tile, c), (1, nt, tile, c)
    specs = [pl.BlockSpec(blk, lambda b, j, *_: (b, j, 0, 0))]
    shapes = [jax.ShapeDtypeStruct(shape, BF16)]
    if mode == 'k':
        nblk = rows // MOBA_BLOCK
        specs.append(pl.BlockSpec((1, 1, nblk, c), lambda b, j, *_: (b, j, 0, 0)))
        shapes.append(jax.ShapeDtypeStruct((nb, n_steps, nblk, c), F32))
    return specs, shapes


def _gather_kernel(pt_ref, *refs, npg, mode, tile, last, has_tail):
    pool_refs = refs[:npg]
    outs = refs[npg + (1 if has_tail else 0):]
    j = pl.program_id(1)

    def token_major(p):
        c = p.shape[0]
        if c == p.shape[1]:
            return jnp.transpose(p)
        return jnp.transpose(jnp.concatenate([p, jnp.zeros((p.shape[1] - c, p.shape[1]), p.dtype)], axis=0))[:, :c]

    def pages():
        if mode == 'v':
            xT = jnp.concatenate([r[0] for r in pool_refs], axis=1)
            for t in range(xT.shape[1] // tile):
                outs[0][0, t] = xT[:, t * tile:(t + 1) * tile].astype(BF16)
        else:
            _emit_tiles(jnp.concatenate([token_major(r[0]) for r in pool_refs], axis=0), outs, mode, tile)

    if not has_tail:
        pages()
        return
    pl.when(j < last)(pages)

    @pl.when(j == last)
    def _():
        _emit_tiles(refs[npg][0], outs, mode, tile)
        if mode != 'raw':
            _zero_tiles(outs, mode, tile, 1, outs[0].shape[1])


def _gather(pool, page_table, new_tok, mode, tile):
    nb, n_pages = page_table.shape
    n_pool, c, page_tokens = pool.shape
    npg = GATHER_PAGES if n_pages % GATHER_PAGES == 0 else PAGES_PER_STEP
    last = n_pages // npg
    has_tail = new_tok is not None
    rows = npg * page_tokens
    assert rows % tile == 0 and (has_tail or mode == 'raw')

    def pool_spec(k):
        return pl.BlockSpec((1, c, page_tokens),
                            lambda b, j, pt: (pt[b, jnp.minimum(j * npg + k, n_pages - 1)], 0, 0))

    in_specs = [pool_spec(k) for k in range(npg)]
    args = [pool] * npg
    if has_tail:
        tail = jnp.zeros((nb, tile, c), F32).at[:, 0:1, :].set(new_tok)
        in_specs.append(pl.BlockSpec((1, tile, c), lambda b, j, pt: (b, 0, 0)))
        args.append(tail)
    n_steps = last + (1 if has_tail else 0)
    out_specs, out_shape = _tile_outputs(mode, nb, n_steps, rows, tile, c)
    grid_spec = pltpu.PrefetchScalarGridSpec(num_scalar_prefetch=1, grid=(nb, n_steps), in_specs=in_specs,
                                             out_specs=out_specs)
    out = pl.pallas_call(
        functools.partial(_gather_kernel, npg=npg, mode=mode, tile=tile, last=last, has_tail=has_tail),
        grid_spec=grid_spec, out_shape=out_shape,
        compiler_params=_cparams(("parallel", "arbitrary")), name="gather_" + mode)(page_table, *args)
    if mode == 'k':
        return out[0], out[1].reshape(nb, -1, c)
    return out[0]


def _kprep_kernel(x_ref, *outs, mode, tile):
    _emit_tiles(x_ref[0], outs, mode, tile)


def _kprep(x, tile, mode='k'):
    nb, L, c = x.shape
    rows = 1024
    out_specs, out_shape = _tile_outputs(mode, nb, L // rows, rows, tile, c)
    out = pl.pallas_call(
        functools.partial(_kprep_kernel, mode=mode, tile=tile), grid=(nb, L // rows),
        in_specs=[pl.BlockSpec((1, rows, c), lambda b, j: (b, j, 0))],
        out_specs=out_specs, out_shape=out_shape,
        compiler_params=_cparams(("parallel", "parallel")), name="kprep")(x)
    if mode == 'k':
        return out[0], out[1].reshape(nb, -1, c)
    return out[0]


def _vT_tiles(v, tile):
    nb, L, c = v.shape
    return v.reshape(nb, L // tile, tile, c).transpose(0, 1, 3, 2).astype(BF16)


class _Cols:
    def __init__(self, nb, t, q_off):
        self.nb, self.t, self.q_off = nb, t, q_off
        self.sample = q_off > 0
        self.tq = 1 if self.sample else KV_TILE
        self.nq = 1 if self.sample else t // KV_TILE
        self.span = 1 if self.sample else KV_TILE

    def n_hc(self, heads):
        return 1 if self.sample else N_KV

    def step_cols(self, heads):
        return SAMPLE_COLS if self.sample else heads // N_KV * self.tq

    def head_group(self, heads):
        return np.arange(heads) // (heads // N_KV)

    def per_head(self, vals, heads):
        vals = jnp.asarray(vals, F32)
        k = vals.shape[0]
        if self.sample:
            return jnp.tile(jnp.pad(vals, ((0, 0), (0, SAMPLE_COLS - heads))), (1, self.nb))
        hps = heads // N_KV
        shape = (k, self.nb, self.nq, N_KV, hps, self.tq)
        return jnp.broadcast_to(vals.reshape(k, 1, 1, N_KV, hps, 1), shape).reshape(k, -1)

    def per_token(self, rows, heads):
        nb, nq, tq = self.nb, self.nq, self.tq
        k = rows.shape[0] // heads
        if self.sample:
            x = rows[:, :nb].reshape(heads, k, nb).transpose(1, 2, 0)
            return jnp.pad(x, ((0, 0), (0, 0), (0, SAMPLE_COLS - heads))).reshape(k, nb * SAMPLE_COLS)
        x = rows.reshape(N_KV, heads // N_KV, k, nb, nq, tq).transpose(2, 3, 4, 0, 1, 5)
        return x.reshape(k, -1)

    def qpos_row(self, heads):
        if self.sample:
            return jnp.full((1, self.nb * SAMPLE_COLS), self.q_off, F32)
        pos = (jnp.arange(self.nq, dtype=F32)[:, None] * self.tq + jnp.arange(self.tq, dtype=F32)[None, :])
        shape = (1, self.nb, self.nq, N_KV, heads // N_KV, self.tq)
        return jnp.broadcast_to(pos.reshape(1, 1, self.nq, 1, 1, self.tq), shape).reshape(1, -1)

    def meta(self, heads, slopes, extra=()):
        s2 = np.asarray(slopes, np.float64) * LOG2E
        hi = s2.astype(jnp.bfloat16).astype(np.float64)
        mid = (s2 - hi).astype(jnp.bfloat16).astype(np.float64)
        lo = (s2 - hi - mid).astype(jnp.bfloat16).astype(np.float64)
        rows = [self.per_head((hi + mid + lo)[None, :], heads), self.qpos_row(heads)] + list(extra)
        n = rows[0].shape[1]
        rows.append(jnp.zeros((META_PIECES - sum(m.shape[0] for m in rows), n), F32))
        rows.append(self.per_head(np.stack([16 * hi, 16 * mid, 16 * lo, hi, mid, lo]), heads))
        rows.append(jnp.zeros((2, n), F32))
        return jnp.concatenate(rows, axis=0)

    def columns(self, rows, heads, meta):
        x = self.per_token(rows, heads)
        grp = self.head_group(heads)
        gsel = self.per_head(np.stack([grp == 0, grp == 1]).astype(np.float32), heads)
        qf = jnp.concatenate([x * gsel[0:1], x * gsel[1:2]], axis=0)
        qa = jnp.concatenate([(qf * (ATTN_SCALE * LOG2E)).astype(BF16), meta[META_PIECES:].astype(BF16),
                              jnp.zeros((8, x.shape[1]), BF16),
                              jnp.zeros((KD - FEAT_POS - 16, x.shape[1]), BF16)], axis=0)
        return qa, qf

    def collapse(self, o, heads):
        nb, nq, tq = self.nb, self.nq, self.tq
        if self.sample:
            o = o.reshape(N_KV, HEAD_DIM, nb, SAMPLE_COLS)
            grp = self.head_group(heads)
            x = jnp.stack([o[grp[h], :, :, h] for h in range(heads)])
            return jnp.pad(x.reshape(heads * HEAD_DIM, nb), ((0, 0), (0, 128 - nb)))
        o = o.reshape(HEAD_DIM, nb, nq, N_KV, heads // N_KV, tq).transpose(3, 4, 0, 1, 2, 5)
        return o.reshape(heads * HEAD_DIM, nb * nq * tq)

    def tokens(self, rows):
        n = rows.shape[0]
        if self.sample:
            return rows[:, :self.nb].T.reshape(self.nb, 1, n)
        return rows.reshape(n, self.nb, self.t).transpose(1, 2, 0)

    def tile_cols(self, rows):
        if not self.sample:
            return rows
        n = rows.shape[0]
        return jnp.broadcast_to(rows[:, :self.nb, None], (n, self.nb, SAMPLE_COLS)).reshape(n, self.nb * SAMPLE_COLS)

    def from_tile_cols(self, o):
        if not self.sample:
            return o
        n = o.shape[0]
        return jnp.pad(o.reshape(n, self.nb, SAMPLE_COLS)[:, :, 0], ((0, 0), (0, 128 - self.nb)))


def _softmax_step(s, c, ok, vT, m, l, acc_ref):
    smax = jnp.max(s, axis=0, keepdims=True)
    live = smax > 0.5 * NEG
    if ok is not None:
        live = live & _row_flag(ok, s.shape[1])
    m_new = jnp.where(live, jnp.maximum(m, smax + c), m)
    shift = jnp.where(live, m_new - c, -NEG)
    p = jnp.exp2(s - shift)
    alpha = jnp.exp2(m - m_new)
    l = alpha * l + jnp.sum(p, axis=0, keepdims=True)
    acc_ref[...] = alpha * acc_ref[...] + jnp.dot(vT, p.astype(BF16), preferred_element_type=F32)
    return m_new, l


def _row_flag(ok, n):
    if ok.ndim == 0:
        return (jnp.zeros((1, n), I32) + ok.astype(I32)) > 0
    return ok


def _group(mixed):
    return 4 if mixed else 2


def _softmax_init(n):
    return jnp.full((1, n), NEG, F32), jnp.zeros((1, n), F32)


def _tile_groups(n_tiles, step, carry, group):
    def body(u, carry):
        carry = step(group * u, 0, None, carry)
        for j in range(1, group):
            t = group * u + j
            carry = step(jnp.minimum(t, n_tiles - 1), j, t < n_tiles, carry)
        return carry
    return lax.fori_loop(0, (n_tiles + group - 1) // group, body, carry)


def _dense_attend(k_ref, n_k, vT_ref, n_v, qa, slope, qpos, blk, pos0=0, mask_fn=None, bias=None):
    n = qa.shape[1]
    tile = k_ref.shape[2]
    rows = n_k * tile
    s = jnp.dot(k_ref[0, 0:n_k].reshape(rows, KD), qa, preferred_element_type=F32)
    if bias is not None:
        s = s + bias
    nblk = rows // blk
    s = s.reshape(nblk, blk, n)
    b3 = lax.broadcasted_iota(I32, (nblk, 1, 1), 0)
    base = pos0 + (b3 * blk) // tile * tile
    s = s + slope * (base - qpos).astype(F32)
    if mask_fn is not None:
        kpos = pos0 + b3 * blk + lax.broadcasted_iota(I32, (nblk, blk, n), 1)
        s = jnp.where(mask_fn(b3, kpos), s, NEG)
    s = s.reshape(rows, n)
    m = jnp.max(s, axis=0, keepdims=True)
    p = jnp.where(s > 0.5 * NEG, jnp.exp2(s - m), 0.0)
    l = jnp.sum(p, axis=0, keepdims=True)
    vT = jnp.concatenate([vT_ref[0, t] for t in range(n_v)], axis=1)
    return jnp.dot(vT, p.astype(BF16), preferred_element_type=F32) / jnp.maximum(l, 1e-30)


def _step_queries(q_ref, meta_ref, qa_s, g):
    tq = q_ref.shape[1]
    n = qa_s.shape[1]
    qa_s[...] = jnp.zeros(qa_s.shape, BF16)
    pieces = jnp.concatenate([meta_ref[META_PIECES:META_ROWS, :], jnp.zeros((8, n), F32)], axis=0)
    qa_s[FEAT_POS:FEAT_POS + 16, :] = pieces.astype(BF16)
    w0 = (g == 0).astype(F32)
    qf = []
    for r in range(q_ref.shape[0] // HEAD_DIM):
        q = q_ref[r * HEAD_DIM:(r + 1) * HEAD_DIM, :]
        blk = jnp.concatenate([q * w0, q * (1.0 - w0)], axis=0)
        qa_s[0:N_KV * HEAD_DIM, r * tq:(r + 1) * tq] = (blk * (ATTN_SCALE * LOG2E)).astype(BF16)
        qf.append(blk)
    return qa_s[...], jnp.concatenate(qf, axis=1)


def _store_heads(o_ref, o, mixed):
    if mixed:
        o_ref[...] = o
        return
    tq = o_ref.shape[1]
    for r in range(o_ref.shape[0] // HEAD_DIM):
        o_ref[r * HEAD_DIM:(r + 1) * HEAD_DIM, :] = o[:, r * tq:(r + 1) * tq]


def _value_rows(vT_ref, n, g, mixed):
    if mixed:
        return vT_ref[0, n]
    return vT_ref[0, n, pl.ds(pl.multiple_of(g * HEAD_DIM, HEAD_DIM), HEAD_DIM), :]


def _top_rows(vals, rowi, k):
    sel = jnp.zeros(vals.shape, F32)
    for _ in range(k):
        m = jnp.max(vals, axis=0, keepdims=True)
        first = jnp.min(jnp.where(vals == m, rowi, 1 << 20), axis=0, keepdims=True)
        pick = (rowi == first) & (m > SENTINEL)
        sel = jnp.where(pick, 1.0, sel)
        vals = jnp.where(pick, SENTINEL, vals)
    return sel


def _moba_kernel(qa_ref, qf_ref, meta_ref, kmean_ref, k_ref, vT_ref, o_ref, sel_ref, acc_ref, qa_s,
                 *, span, q_off, mixed):
    i, g = pl.program_id(1), pl.program_id(2)
    n = meta_ref.shape[1]
    qa, qf = (qa_ref[...], qf_ref[...]) if mixed else _step_queries(qa_ref, meta_ref, qa_s, g)
    own = (q_off + i * span) // MOBA_BLOCK
    slope = meta_ref[0:1, :]
    qpos = meta_ref[1:2, :].astype(I32)
    nrow = kmean_ref.shape[1]
    gate = jnp.dot(kmean_ref[0], qf, precision=lax.Precision.HIGHEST, preferred_element_type=F32)
    rowi = lax.broadcasted_iota(I32, (nrow, n), 0)
    sel_ref[...] = _top_rows(jnp.where(rowi < own, gate, SENTINEL), rowi, MOBA_TOPK)
    if mixed:
        own_s = q_off // MOBA_BLOCK
        sel3 = sel_ref[0:own_s + 1, :][:, None, :] > 0.5
        o_ref[...] = _dense_attend(k_ref, own_s + 1, vT_ref, own_s + 1, qa, slope, qpos, KV_TILE,
                                   mask_fn=lambda b3, kpos: ((b3 < own_s) & sel3) | ((b3 >= own_s) & (kpos <= qpos)))
        return
    acc_ref[...] = jnp.zeros(acc_ref.shape, F32)

    def step(t, slot, ok, carry):
        s = jnp.dot(k_ref[0, t], qa, preferred_element_type=F32)
        c = slope * (t * KV_TILE - qpos).astype(F32)
        sel = sel_ref[pl.ds(t, 1), :] > 0.5
        ok = sel if ok is None else sel & _row_flag(ok, n)
        return _softmax_step(s, c, ok, _value_rows(vT_ref, t, g, mixed), *carry, acc_ref)

    m, l = _tile_groups(own, step, _softmax_init(n), _group(mixed))
    s = jnp.dot(k_ref[0, own], qa, preferred_element_type=F32)
    krow = lax.broadcasted_iota(I32, (KV_TILE, n), 0)
    s = jnp.where(krow <= qpos - own * KV_TILE, s, NEG)
    c = slope * (own * KV_TILE - qpos).astype(F32)
    _, l = _softmax_step(s, c, None, _value_rows(vT_ref, own, g, mixed), m, l, acc_ref)
    _store_heads(o_ref, acc_ref[...] / jnp.maximum(l, 1e-30), mixed)


def _query_specs(cols, heads, q_row, n_cols):
    nq, n_hc, n = cols.nq, cols.n_hc(heads), cols.step_cols(heads)
    if cols.sample:
        col = lambda b, i, g: (0, (b * nq + i) * n_hc + g)
        return (pl.BlockSpec((KD, n), col), pl.BlockSpec((N_KV * HEAD_DIM, n), col),
                jax.ShapeDtypeStruct((N_KV * HEAD_DIM, n_cols), F32), N_KV * HEAD_DIM)
    rows = heads // N_KV * HEAD_DIM
    blk0 = q_row // rows
    return (pl.BlockSpec((rows, cols.tq), lambda b, i, g: (blk0 + g, b * nq + i)),
            pl.BlockSpec((rows, cols.tq), lambda b, i, g: (g, b * nq + i)),
            jax.ShapeDtypeStruct((heads * HEAD_DIM, cols.nb * nq * cols.tq), F32), HEAD_DIM)


def _moba(cols, q, q_row, qf, meta, kmean, k, vT):
    nb, nq = cols.nb, cols.nq
    n_hc, n = cols.n_hc(N_A_HEADS), cols.step_cols(N_A_HEADS)
    nT, nblk = k.shape[1], kmean.shape[1]
    q_spec, o_spec, o_shape, vrows = _query_specs(cols, N_A_HEADS, q_row, meta.shape[1])
    col = lambda b, i, g: (0, (b * nq + i) * n_hc + g)
    return pl.pallas_call(
        functools.partial(_moba_kernel, span=cols.span, q_off=cols.q_off, mixed=cols.sample), grid=(nb, nq, n_hc),
        in_specs=[q_spec,
                  pl.BlockSpec((qf.shape[0], n), col),
                  pl.BlockSpec((META_ROWS, n), col),
                  pl.BlockSpec((1, nblk, N_KV * HEAD_DIM), lambda b, i, g: (b, 0, 0)),
                  pl.BlockSpec((1, nT, KV_TILE, KD), lambda b, i, g: (b, 0, 0, 0)),
                  pl.BlockSpec((1, nT, N_KV * HEAD_DIM, KV_TILE), lambda b, i, g: (b, 0, 0, 0))],
        out_specs=o_spec, out_shape=o_shape,
        scratch_shapes=[pltpu.VMEM((nblk, n), F32), pltpu.VMEM((vrows, n), F32), pltpu.VMEM((KD, n), BF16)],
        compiler_params=_cparams(("parallel", "parallel", "arbitrary")), name="moba")(q, qf, meta, kmean, k, vT)


def _cmp_kernel(r_ref, pe_ref, w1_ref, w2_ref, o_ref):
    n = o_ref.shape[1]
    hid = 2 * CMP_HIDDEN
    c = r_ref.shape[2]
    u = jnp.zeros((n, hid), F32)
    lo = jnp.zeros((n, hid), F32)
    for p in range(CMP_STRIDE):
        rp = r_ref[0, pl.ds(p, n, stride=CMP_STRIDE), :]
        cols = slice(p * c, (p + 1) * c)
        u = u + jnp.dot((rp + pe_ref[0:1, cols]).astype(BF16), w1_ref[cols, 0:hid], preferred_element_type=F32)
        lo = lo + jnp.dot((rp + pe_ref[1:2, cols]).astype(BF16), w1_ref[cols, hid:2 * hid],
                          preferred_element_type=F32)
    pre = u + pltpu.roll(lo, n - 1, 0)
    h = pre / (1.0 + jnp.exp(-pre))
    o_ref[0] = jnp.dot(h.astype(BF16), w2_ref[...], preferred_element_type=F32)


def _compress(raw, pe2, w1big, w2big, n_rows):
    nb, _, c = raw.shape
    return pl.pallas_call(
        _cmp_kernel, grid=(nb,),
        in_specs=[pl.BlockSpec((1, n_rows * CMP_STRIDE, c), lambda b: (b, 0, 0)),
                  pl.BlockSpec((2, 2048), lambda b: (0, 0)),
                  pl.BlockSpec((2048, 4 * CMP_HIDDEN), lambda b: (0, 0)),
                  pl.BlockSpec((2 * CMP_HIDDEN, 128), lambda b: (0, 0))],
        out_specs=pl.BlockSpec((1, n_rows, 128), lambda b: (b, 0, 0)),
        out_shape=jax.ShapeDtypeStruct((nb, n_rows, 128), F32),
        compiler_params=_cparams(("parallel",)), name="compress")(raw, pe2, w1big, w2big)


def _cmp_weights(pe, w1, w2):
    w1r = w1.reshape(2, CMP_STRIDE, HEAD_DIM, CMP_HIDDEN)
    eye = jnp.eye(N_KV, dtype=w1.dtype)
    big = jnp.einsum('hpdj,ge->hpdgej', w1r, eye)
    big = big.transpose(1, 3, 2, 0, 4, 5).reshape(CMP_STRIDE * N_KV * HEAD_DIM, 2 * N_KV * CMP_HIDDEN)
    w2big = jnp.einsum('jd,ge->gjed', w2, eye).reshape(N_KV * CMP_HIDDEN, N_KV * HEAD_DIM)
    pe2 = jnp.broadcast_to(pe.reshape(2, CMP_STRIDE, 1, HEAD_DIM), (2, CMP_STRIDE, N_KV, HEAD_DIM)).reshape(2, 2048)
    return pe2, big.astype(BF16), w2big.astype(BF16)


def _nsa_kernel(qa_ref, meta_ref, kc_ref, vcT_ref, taps_ref, grp_ref, ks_ref, vsT_ref, kw_ref, vwT_ref, o_ref,
                sel_ref, qs_ref, acc_ref, qa_s, *, span, q_off, kw_base, mixed):
    i, g = pl.program_id(1), pl.program_id(2)
    n = meta_ref.shape[1]
    nq = n if mixed else n // (N_B_HEADS // N_KV)
    rep = n // nq
    t0 = q_off + i * span
    slope = meta_ref[0:1, :]
    qpos = meta_ref[1:2, :].astype(I32)
    qa = qa_ref[...] if mixed else _step_queries(qa_ref, meta_ref, qa_s, g)[0]
    n_cmp = kc_ref.shape[1]
    n_tab = sel_ref.shape[0]

    def vrows(ref):
        if mixed:
            return ref[0]
        return ref[0, pl.ds(pl.multiple_of(g * HEAD_DIM, HEAD_DIM), HEAD_DIM), :]

    cend = CMP_STRIDE * lax.broadcasted_iota(I32, (n_cmp, n), 0) + (CMP_LEN - 1)
    cmask = cend <= qpos
    s = jnp.dot(kc_ref[0], qa[0:N_KV * HEAD_DIM, :], preferred_element_type=F32) + slope * (cend - qpos).astype(F32)
    s = jnp.where(cmask, s, NEG)
    p = jnp.where(cmask, jnp.exp2(s - jnp.max(s, axis=0, keepdims=True)), 0.0)
    p = p * (1.0 / jnp.maximum(jnp.sum(p, axis=0, keepdims=True), 1e-30))
    o_c = jnp.dot(vrows(vcT_ref), p.astype(BF16), preferred_element_type=F32)

    if mixed:
        imp = jnp.dot(p, grp_ref[...], precision=lax.Precision.HIGHEST, preferred_element_type=F32)
    else:
        imp = p[:, 0:nq]
        for r in range(1, rep):
            imp = imp + p[:, r * nq:(r + 1) * nq]
    p_slc = jnp.dot(taps_ref[...], imp, precision=lax.Precision.HIGHEST, preferred_element_type=F32)
    p_slc = jnp.concatenate([p_slc, jnp.full((n_tab - p_slc.shape[0], nq), SENTINEL, F32)], axis=0)
    rowi = lax.broadcasted_iota(I32, (n_tab, nq), 0)
    own = qpos[:, 0:nq] >> 6
    sel = _top_rows(jnp.where(rowi < own, p_slc, SENTINEL), rowi, SEL_TOPK)
    sel_ref[...] = jnp.where(rowi == own, 1.0, sel)

    gt = 1.0 / (1.0 + jnp.exp(-meta_ref[2:5, :]))
    if mixed:
        n_s = q_off // SEL_TILE + 1
        sel3 = sel_ref[0:n_s * (SEL_TILE // SEL_BLOCK), :][:, None, :] > 0.5
        o_s = _dense_attend(ks_ref, n_s, vsT_ref, n_s, qa, slope, qpos, SEL_BLOCK,
                            mask_fn=lambda b3, kpos: sel3 & (kpos <= qpos))
        n_w = min((q_off - kw_base) // KV_TILE + 1, kw_ref.shape[1])
        o_w = _dense_attend(kw_ref, n_w, vwT_ref, n_w, qa, slope, qpos, KV_TILE, pos0=kw_base,
                            mask_fn=lambda b3, kpos: (kpos <= qpos) & (kpos > qpos - WINDOW))
        o_ref[...] = gt[0:1, :] * o_c + gt[1:2, :] * o_s + gt[2:3, :] * o_w
        return

    per_tile = SEL_TILE // SEL_BLOCK
    last = (t0 + span - 1) // SEL_TILE
    for slot in range(qs_ref.shape[0]):
        qs_ref[slot] = qa
    acc_ref[...] = jnp.zeros(acc_ref.shape, F32)

    def sel_scores(t, slot):
        rows = sel_ref[pl.ds(pl.multiple_of(t * per_tile, per_tile), per_tile), :]
        bias = jnp.where(rows > 0.5, 0.0, NEG)
        if rep > 1:
            bias = jnp.concatenate([bias] * rep, axis=1)
        bias = jnp.concatenate([bias, jnp.zeros((per_tile, n), F32)], axis=0)
        qs_ref[slot, FEAT_SEL:FEAT_SEL + 2 * per_tile, :] = bias.astype(BF16)
        s = jnp.dot(ks_ref[0, t], qs_ref[slot], preferred_element_type=F32)
        return s, slope * (t * SEL_TILE - qpos).astype(F32)

    def sel_step(t, slot, ok, carry):
        s, c = sel_scores(t, slot)
        return _softmax_step(s, c, ok, _value_rows(vsT_ref, t, g, mixed), *carry, acc_ref)

    m, l = _tile_groups(last, sel_step, _softmax_init(n), qs_ref.shape[0])
    s, c = sel_scores(last, 0)
    srow = lax.broadcasted_iota(I32, (SEL_TILE, n), 0)
    s = jnp.where(srow <= qpos - last * SEL_TILE, s, NEG)
    _, l = _softmax_step(s, c, None, _value_rows(vsT_ref, last, g, mixed), m, l, acc_ref)
    o_s = acc_ref[...] / jnp.maximum(l, 1e-30)

    lo = jnp.maximum(t0 - (WINDOW - 1) - kw_base, 0) // KV_TILE
    hi = jnp.minimum((t0 + span - 1 - kw_base) // KV_TILE + 1, kw_ref.shape[1])
    wrow = lax.broadcasted_iota(I32, (KV_TILE, n), 0)
    acc_ref[...] = jnp.zeros(acc_ref.shape, F32)

    def win_body(t, carry):
        s = jnp.dot(kw_ref[0, t], qa, preferred_element_type=F32)
        base = kw_base + t * KV_TILE
        dist = qpos - base - wrow
        s = jnp.where((dist >= 0) & (dist < WINDOW), s, NEG)
        return _softmax_step(s, slope * (base - qpos).astype(F32), None, _value_rows(vwT_ref, t, g, mixed),
                             *carry, acc_ref)

    _, l = lax.fori_loop(lo, hi, win_body, _softmax_init(n))
    o_w = acc_ref[...] / jnp.maximum(l, 1e-30)

    _store_heads(o_ref, gt[0:1, :] * o_c + gt[1:2, :] * o_s + gt[2:3, :] * o_w, mixed)


def _taps_matrix(n_sel, n_cmp):
    m = np.zeros((n_sel, n_cmp), np.float32)
    for j in range(n_sel):
        for off, w in ((-1, 1.0), (0, 2.0), (1, 2.0), (2, 2.0), (3, 1.0)):
            c = 4 * j + off
            if 0 <= c < n_cmp:
                m[j, c] = w
    return jnp.asarray(m)


def _group_matrix(cols):
    grp = cols.head_group(N_B_HEADS)
    m = np.zeros((SAMPLE_COLS, SAMPLE_COLS), np.float32)
    m[:N_B_HEADS, :N_B_HEADS] = grp[:, None] == grp[None, :]
    return jnp.asarray(m)


def _nsa(cols, q, q_row, meta, kc, vcT, ks, vsT, kw, vwT, kw_base):
    nb, nq = cols.nb, cols.nq
    n_hc, n = cols.n_hc(N_B_HEADS), cols.step_cols(N_B_HEADS)
    nTs, nTw = ks.shape[1], kw.shape[1]
    n_cmp = kc.shape[1]
    n_sel = n_cmp // 4
    n_tab = n_sel + SEL_TILE // SEL_BLOCK
    nsel_cols = n if cols.sample else n // (N_B_HEADS // N_KV)
    q_spec, o_spec, o_shape, vrows = _query_specs(cols, N_B_HEADS, q_row, meta.shape[1])
    col = lambda b, i, g: (0, (b * nq + i) * n_hc + g)
    seq3 = lambda b, i, g: (b, 0, 0)
    seq4 = lambda b, i, g: (b, 0, 0, 0)
    return pl.pallas_call(
        functools.partial(_nsa_kernel, span=cols.span, q_off=cols.q_off, kw_base=kw_base, mixed=cols.sample),
        grid=(nb, nq, n_hc),
        in_specs=[q_spec,
                  pl.BlockSpec((META_ROWS, n), col),
                  pl.BlockSpec((1, n_cmp, N_KV * HEAD_DIM), seq3),
                  pl.BlockSpec((1, N_KV * HEAD_DIM, n_cmp), seq3),
                  pl.BlockSpec((n_sel, n_cmp), lambda b, i, g: (0, 0)),
                  pl.BlockSpec((SAMPLE_COLS, SAMPLE_COLS), lambda b, i, g: (0, 0)),
                  pl.BlockSpec((1, nTs, SEL_TILE, KD), seq4),
                  pl.BlockSpec((1, nTs, N_KV * HEAD_DIM, SEL_TILE), seq4),
                  pl.BlockSpec((1, nTw, KV_TILE, KD), seq4),
                  pl.BlockSpec((1, nTw, N_KV * HEAD_DIM, KV_TILE), seq4)],
        out_specs=o_spec, out_shape=o_shape,
        scratch_shapes=[pltpu.VMEM((n_tab, nsel_cols), F32), pltpu.VMEM((_group(cols.sample), KD, n), BF16),
                        pltpu.VMEM((vrows, n), F32), pltpu.VMEM((KD, n), BF16)],
        compiler_params=_cparams(("parallel", "parallel", "arbitrary")), name="nsa")(
            q, meta, kc, vcT, _taps_matrix(n_sel, n_cmp), _group_matrix(cols), ks, vsT, kw, vwT)


def _dsa_kernel(qi_ref, wi_ref, ki_ref, qa_ref, meta_ref, k_ref, vT_ref, o_ref, key_ref, row_ref, acc_ref, qa_s,
                *, span, q_off, topk, mixed, hc_per_group):
    i, hc = pl.program_id(1), pl.program_id(2)
    n = meta_ref.shape[1]
    nq = qi_ref.shape[1]
    rep = n // nq
    t0 = q_off + i * span
    n_kv = (t0 + span - 1) // KV_TILE + 1
    krow = lax.broadcasted_iota(I32, (KV_TILE, nq), 0)

    def search(count, shape):
        kf = float(topk)
        thr = jnp.where(count(lambda key, kpos: key >= 0) >= kf, 0, INT_MIN).astype(I32)

        def bit_body(j, thr):
            cand = thr | (1 << (30 - j))
            return jnp.where(count(lambda key, kpos: key >= cand) >= kf, cand, thr)
        thr = lax.fori_loop(0, 31, bit_body, thr)
        need = kf - count(lambda key, kpos: key > thr)

        def tie_search():
            def idx_body(j, jb):
                cand = jb | (1 << (14 - j))
                c = count(lambda key, kpos: (key == thr) & (kpos < cand))
                return jnp.where(c <= need, cand, jb)
            return lax.fori_loop(0, 15, idx_body, jnp.zeros(shape, I32))

        n_ge = count(lambda key, kpos: key >= thr)
        jb = lax.cond(jnp.max(n_ge) > kf, tie_search, lambda: jnp.full(shape, (1 << 15) - 1, I32))
        return thr, jb

    @pl.when(hc == 0)
    def _():
        qpos = t0 + (lax.broadcasted_iota(I32, (1, nq), 1) if span > 1 else jnp.zeros((1, nq), I32))
        w = wi_ref[...] * (IDX_SCALE * N_IDX_HEADS ** -0.5)
        if mixed:
            qi = qi_ref[...].astype(BF16)
        else:
            qi = [qi_ref[a * IDX_DIM:(a + 1) * IDX_DIM, :].astype(BF16) for a in range(N_IDX_HEADS)]
        if mixed:
            row_ref[...] = jnp.full(row_ref.shape, INT_MIN, I32)

        def score_body(t, _):
            ki = ki_ref[0, t]
            if mixed:
                rel = jnp.dot(ki, qi, preferred_element_type=F32)
                acc = jnp.sum(jnp.maximum(rel, 0.0) * w[0:1, :], axis=1, keepdims=True)
                acc = jnp.broadcast_to(acc, (KV_TILE, nq))
            else:
                acc = jnp.zeros((KV_TILE, nq), F32)
                for a in range(N_IDX_HEADS):
                    rel = jnp.dot(ki, qi[a], preferred_element_type=F32)
                    acc = acc + jnp.maximum(rel, 0.0) * w[a:a + 1, :]
            sc = jnp.where(t * KV_TILE + krow <= qpos, acc, NEG)
            bits = lax.bitcast_convert_type(sc, I32)
            key = bits ^ ((bits >> 31) & 0x7FFFFFFF)
            key_ref[t] = key
            if mixed:
                row_ref[pl.ds(t, 1), :] = jnp.transpose(key)[0:1, :]
            return 0
        lax.fori_loop(0, n_kv, score_body, 0)

        if mixed:
            keys = row_ref[...]
            kpos_r = (lax.broadcasted_iota(I32, keys.shape, 0) * KV_TILE + lax.broadcasted_iota(I32, keys.shape, 1))

            def count(pred):
                c = jnp.sum(jnp.where(pred(keys, kpos_r), 1.0, 0.0), axis=0, keepdims=True)
                return jnp.sum(c, axis=1, keepdims=True)
            thr, jb = search(count, (1, 1))
        else:
            n_grp = (n_kv + COUNT_GROUP - 1) // COUNT_GROUP

            def pad_body(t, _):
                key_ref[t] = jnp.full((KV_TILE, nq), INT_MIN, I32)
                return 0
            lax.fori_loop(n_kv, n_grp * COUNT_GROUP, pad_body, 0)

            def count(pred):
                def body(u, c):
                    for j in range(COUNT_GROUP):
                        t = u * COUNT_GROUP + j
                        hit = pred(key_ref[t], t * KV_TILE + krow)
                        c = c + jnp.sum(jnp.where(hit, 1.0, 0.0), axis=0, keepdims=True)
                    return c
                return lax.fori_loop(0, n_grp, body, jnp.zeros((1, nq), F32))
            thr, jb = search(count, (1, nq))

        def bias_body(t, _):
            key = key_ref[t]
            kpos = t * KV_TILE + krow
            keep = ((key > thr) | ((key == thr) & (kpos < jb))) & (kpos <= qpos)
            key_ref[t] = lax.bitcast_convert_type(jnp.where(keep, 0.0, NEG), I32)
            return 0
        lax.fori_loop(0, n_kv, bias_body, 0)

    g = hc // hc_per_group
    slope = meta_ref[0:1, :]
    qpos_c = meta_ref[1:2, :].astype(I32)
    qa = qa_ref[...] if mixed else _step_queries(qa_ref, meta_ref, qa_s, g)[0]
    if mixed:
        n_t = q_off // KV_TILE + 1
        bias = lax.bitcast_convert_type(key_ref[0:n_t], F32).reshape(n_t * KV_TILE, nq)
        o_ref[...] = _dense_attend(k_ref, n_t, vT_ref, n_t, qa, slope, qpos_c, KV_TILE, bias=bias)
        return
    acc_ref[...] = jnp.zeros(acc_ref.shape, F32)

    def step(t, slot, ok, carry):
        bias = lax.bitcast_convert_type(key_ref[t], F32)
        if rep > 1:
            bias = jnp.concatenate([bias] * rep, axis=1)
        s = jnp.dot(k_ref[0, t], qa, preferred_element_type=F32) + bias
        c = slope * (t * KV_TILE - qpos_c).astype(F32)
        return _softmax_step(s, c, ok, _value_rows(vT_ref, t, g, mixed), *carry, acc_ref)
    _, l = _tile_groups(n_kv, step, _softmax_init(n), _group(mixed))
    _store_heads(o_ref, acc_ref[...] / jnp.maximum(l, 1e-30), mixed)


def _dsa(cols, qi, qi_row, wi, wi_row, ki, q, q_row, meta, k, vT, topk):
    nb, nq = cols.nb, cols.nq
    n_hc, n = cols.n_hc(N_C_HEADS), cols.step_cols(N_C_HEADS)
    ncol = SAMPLE_COLS if cols.sample else cols.tq
    nT = k.shape[1]
    assert nT % COUNT_GROUP == 0
    q_spec, o_spec, o_shape, vrows = _query_specs(cols, N_C_HEADS, q_row, meta.shape[1])
    qi_rows = IDX_DIM if cols.sample else N_IDX_HEADS * IDX_DIM
    qi_b, wi_b = qi_row // qi_rows, wi_row // N_IDX_HEADS
    col = lambda b, i, h: (0, (b * nq + i) * n_hc + h)
    seq4 = lambda b, i, h: (b, 0, 0, 0)
    return pl.pallas_call(
        functools.partial(_dsa_kernel, span=cols.span, q_off=cols.q_off, topk=topk, mixed=cols.sample,
                          hc_per_group=max(n_hc // N_KV, 1)),
        grid=(nb, nq, n_hc),
        in_specs=[pl.BlockSpec((qi_rows, ncol), lambda b, i, h: (qi_b, b * nq + i)),
                  pl.BlockSpec((N_IDX_HEADS, ncol), lambda b, i, h: (wi_b, b * nq + i)),
                  pl.BlockSpec((1, nT, KV_TILE, IDX_DIM), seq4),
                  q_spec,
                  pl.BlockSpec((META_ROWS, n), col),
                  pl.BlockSpec((1, nT, KV_TILE, KD), seq4),
                  pl.BlockSpec((1, nT, N_KV * HEAD_DIM, KV_TILE), seq4)],
        out_specs=o_spec, out_shape=o_shape,
        scratch_shapes=[pltpu.VMEM((nT, KV_TILE, ncol), I32), pltpu.VMEM((-(-nT // 8) * 8, KV_TILE), I32),
                        pltpu.VMEM((vrows, n), F32), pltpu.VMEM((KD, n), BF16)],
        compiler_params=_cparams(("parallel", "parallel", "arbitrary")), name="dsa")(
            qi, wi, ki, q, meta, k, vT)


def _mem_kernel(q_ref, mk_ref, mvT_ref, o_ref):
    for h in range(N_MEM_HEADS):
        rows = slice(h * HEAD_DIM, (h + 1) * HEAD_DIM)
        q = (q_ref[rows, :] * ATTN_SCALE).astype(BF16)
        s = jnp.dot(mk_ref[0, h].astype(BF16), q, preferred_element_type=F32)
        p = jnp.exp(s - jnp.max(s, axis=0, keepdims=True))
        l = jnp.sum(p, axis=0, keepdims=True)
        o_ref[rows, :] = jnp.dot(mvT_ref[0, h].astype(BF16), p.astype(BF16), preferred_element_type=F32) / l


def _mem(hT, q_row, mk, mvT, nb, nq, tq):
    n_mem = mk.shape[2]
    n = N_MEM_HEADS * HEAD_DIM
    qb = q_row // n
    return pl.pallas_call(
        _mem_kernel, grid=(nb, nq),
        in_specs=[pl.BlockSpec((n, tq), lambda b, i: (qb, b * nq + i)),
                  pl.BlockSpec((1, N_MEM_HEADS, n_mem, HEAD_DIM), lambda b, i: (b, 0, 0, 0)),
                  pl.BlockSpec((1, N_MEM_HEADS, HEAD_DIM, n_mem), lambda b, i: (b, 0, 0, 0))],
        out_specs=pl.BlockSpec((n, tq), lambda b, i: (0, b * nq + i)),
        out_shape=jax.ShapeDtypeStruct((n, nb * nq * tq), F32),
        compiler_params=_cparams(("parallel", "parallel")), name="mem")(hT, mk, mvT)


def _mem_branch(cols, hT, q_row, mk, mvT):
    n = N_MEM_HEADS * HEAD_DIM
    if cols.sample:
        o = _mem(cols.tile_cols(hT[q_row:q_row + n]), 0, mk, mvT, cols.nb, 1, SAMPLE_COLS)
        return cols.from_tile_cols(o)
    return _mem(hT, q_row, mk, mvT, cols.nb, cols.nq, cols.tq)


def _even_wT(w_in):
    sizes = (384, 128, 128, 384, 384, 128, 128, 128, 128, 128, 128, 18, 384, 256, 256)
    offs = np.cumsum((0,) + sizes)
    (a_q, a_k, a_v, a_z, b_q, b_kc, b_vc, b_ks, b_vs, b_kw, b_vw, b_g, b_z, m_q, m_z) = [
        w_in[:, offs[n]:offs[n + 1]] for n in range(len(sizes))]
    d = w_in.shape[0]
    bg = jnp.pad(b_g, ((0, 0), (0, 32 - b_g.shape[1])))
    cols = [a_q, b_q, m_q, a_z, b_z, m_z, a_k, a_v, b_kc, b_vc, b_ks, b_vs, b_kw, b_vw, bg]
    w = jnp.concatenate(cols, axis=1)
    w = jnp.pad(w, ((0, 0), (0, EVEN_TOTAL - w.shape[1])))
    return w.T


def _odd_wT(w_in):
    sizes = (768, 128, 128, 512, 64, 8, 768, 256, 256)
    offs = np.cumsum((0,) + sizes)
    (c_q, c_k, c_v, c_qi, c_ki, c_wi, c_z, m_q, m_z) = [w_in[:, offs[n]:offs[n + 1]] for n in range(len(sizes))]
    w = jnp.concatenate([c_q, m_q, c_z, m_z, c_qi, c_k, c_v, c_ki, c_wi], axis=1)
    w = jnp.pad(w, ((0, 0), (0, ODD_TOTAL - w.shape[1])))
    return w.T


def _keys(cols, new_tok, pool, page_table, tile, mode):
    if cols.sample:
        return _gather(pool, page_table, new_tok, mode, tile)
    return _kprep(new_tok, tile, mode)


def _values(cols, new_tok, pool, page_table, tile):
    if cols.sample:
        return _gather(pool, page_table, new_tok, 'v', tile)
    return _vT_tiles(new_tok, tile)


def _raw_rows(cols, new_tok, pool, page_table):
    if cols.sample:
        return _gather(pool, page_table, None, 'raw', KV_TILE)
    return new_tok


def _queries(cols, hT, row, heads, slopes, extra=()):
    meta = cols.meta(heads, slopes, extra)
    if cols.sample:
        qa, qf = cols.columns(hT[row:row + heads * HEAD_DIM], heads, meta)
        return qa, 0, qf, meta
    return hT, row, meta, meta


def _heads_out(cols, o, heads):
    return cols.collapse(o, heads) if cols.sample else o


def _gate_cols(cols, gT):
    return [cols.per_token(gT, N_B_HEADS)]


def _even_layer(cols, xT, mk, mvT, past, page_table, wT, woT, g, b, cmpw, alpha):
    R = EVEN_ROWS
    nb = cols.nb
    hT = _mm(wT, xT, 640, min(1024, xT.shape[1]))
    names = ('a_k', 'a_v', 'b_kc', 'b_vc', 'b_ks', 'b_vs')
    new = [cols.tokens(hT[R[nm]:R[nm] + 128]) for nm in names]
    kw_new, vw_new = cols.tokens(hT[R['b_kw']:R['b_kw'] + 128]), cols.tokens(hT[R['b_vw']:R['b_vw'] + 128])
    pools = past[:6] if cols.sample else [None] * 6
    if cols.sample:
        kw = jnp.concatenate([past[6].reshape(nb, -1, 128), kw_new], axis=1)
        vw = jnp.concatenate([past[7].reshape(nb, -1, 128), vw_new], axis=1)
        n_win = past[6].shape[1]
        kw_base = cols.q_off - n_win
        win_out = (kw[:, -n_win:], vw[:, -n_win:])
        pad = -kw.shape[1] % 1024
        kw = jnp.pad(kw, ((0, 0), (0, pad), (0, 0)))
        vw = jnp.pad(vw, ((0, 0), (0, pad), (0, 0)))
    else:
        kw, vw, kw_base = kw_new, vw_new, 0
        n_win = min(WINDOW, cols.t)
        win_out = (kw[:, -n_win:], vw[:, -n_win:])

    sl = _alibi(N_A_HEADS + N_B_HEADS)
    ak, amean = _keys(cols, new[0], pools[0], page_table, KV_TILE, 'k')
    avT = _values(cols, new[1], pools[1], page_table, KV_TILE)
    q, q_row, qf, meta = _queries(cols, hT, R['a_q'], N_A_HEADS, sl[0::2])
    o_a = _moba(cols, q, q_row, qf, meta, amean, ak, avT)

    pe2k, w1k, w2k = cmpw[0]
    pe2v, w1v, w2v = cmpw[1]
    n_cmp = max(cols.q_off, cols.t) // CMP_STRIDE
    kc_raw = _raw_rows(cols, new[2], pools[2], page_table)
    vc_raw = _raw_rows(cols, new[3], pools[3], page_table)
    kc = _compress(kc_raw, pe2k, w1k, w2k, n_cmp).astype(BF16)
    vcT = _compress(vc_raw, pe2v, w1v, w2v, n_cmp).transpose(0, 2, 1).astype(BF16)
    ks = _keys(cols, new[4], pools[4], page_table, SEL_TILE, 'ks')
    vsT = _values(cols, new[5], pools[5], page_table, SEL_TILE)
    kwt = _kprep(kw, KV_TILE, 'ks')
    q, q_row, _, meta = _queries(cols, hT, R['b_q'], N_B_HEADS, sl[1::2],
                               extra=_gate_cols(cols, hT[R['b_g']:R['b_g'] + 18]))
    o_b = _nsa(cols, q, q_row, meta, kc, vcT, ks, vsT, kwt, _vT_tiles(vw, KV_TILE), kw_base)

    o_m = _mem_branch(cols, hT, R['m_q'], mk, mvT)
    parts = [_heads_out(cols, o_a, N_A_HEADS), _heads_out(cols, o_b, N_B_HEADS), o_m]
    yT = _outln(parts, hT, R['z'], xT, woT, g, b, alpha, min(512, xT.shape[1]))
    state = [n.reshape(nb, -1, N_KV, HEAD_DIM) for n in new]
    state += [w.reshape(nb, -1, N_KV, HEAD_DIM) for w in win_out]
    return yT, state


def _odd_layer(cols, xT, mk, mvT, past, page_table, wT, woT, g, b, alpha):
    R = ODD_ROWS
    nb = cols.nb
    hT = _mm(wT, xT, 768, min(1024, xT.shape[1]))
    new = [cols.tokens(hT[R['c_k']:R['c_k'] + 128]), cols.tokens(hT[R['c_v']:R['c_v'] + 128]),
           cols.tokens(hT[R['c_ki']:R['c_ki'] + IDX_DIM])]
    pools = past if cols.sample else [None] * 3
    topk = min(DSA_TOPK, (cols.q_off + cols.t) // 4)
    ck = _keys(cols, new[0], pools[0], page_table, KV_TILE, 'ks')
    cvT = _values(cols, new[1], pools[1], page_table, KV_TILE)
    if cols.sample:
        ki = _gather(pools[2], page_table, new[2], 'i', KV_TILE)
    else:
        ki = new[2].reshape(nb, -1, KV_TILE, IDX_DIM).astype(BF16)
    q, q_row, _, meta = _queries(cols, hT, R['c_q'], N_C_HEADS, _alibi(N_C_HEADS))
    if cols.sample:
        qi = cols.per_token(hT[R['c_qi']:R['c_qi'] + 512], N_IDX_HEADS)
        wi = jnp.pad(cols.per_token(hT[R['c_wi']:R['c_wi'] + 8], N_IDX_HEADS), ((0, 7), (0, 0)))
        o_c = _dsa(cols, qi, 0, wi, 0, ki, q, q_row, meta, ck, cvT, topk)
    else:
        o_c = _dsa(cols, hT, R['c_qi'], hT, R['c_wi'], ki, q, q_row, meta, ck, cvT, topk)
    o_m = _mem_branch(cols, hT, R['m_q'], mk, mvT)
    parts = [_heads_out(cols, o_c, N_C_HEADS), o_m]
    yT = _outln(parts, hT, R['z'], xT, woT, g, b, alpha, min(512, xT.shape[1]))
    state = [new[0].reshape(nb, -1, N_KV, HEAD_DIM), new[1].reshape(nb, -1, N_KV, HEAD_DIM), new[2]]
    return yT, state


def kernel(x_prompt, x_sample, cache_a_k, cache_a_v, cache_b_cmp_k, cache_b_cmp_v, cache_b_sel_k, cache_b_sel_v,
           state_b_win_k, state_b_win_v, cache_c_k, cache_c_v, cache_c_idx_k, cache_mem_k, cache_mem_v,
           page_table, mem_prompt, w_in_even, w_out_even, w_in_odd, w_out_odd, w_mem_kv, ln_g, ln_b,
           cmp_pe, cmp_w1, cmp_w2):
    bp, seq, d = x_prompt.shape
    bs = x_sample.shape[0]
    depth = w_mem_kv.shape[0]
    n_mem = mem_prompt.shape[1]
    past_len = page_table.shape[1] * cache_a_k.shape[2]
    alpha = (2.0 * depth) ** 0.25
    assert x_sample.shape[1] == 1 and bs <= 128 and seq % 1024 == 0 and past_len % 1024 == 0

    gp = _Cols(bp, seq, 0)
    gs = _Cols(bs, 1, past_len)
    xp = x_prompt.reshape(bp * seq, d).T
    xs = jnp.pad(x_sample.reshape(bs, d).T, ((0, 0), (0, 128 - bs)))
    n_pool = cache_a_k.shape[1]

    def pages(c):
        return jnp.moveaxis(c.reshape(c.shape[:3] + (-1,)), 2, 3).reshape(c.shape[0] * n_pool, -1, c.shape[2])

    even_paged = tuple(pages(c) for c in (cache_a_k, cache_a_v, cache_b_cmp_k, cache_b_cmp_v, cache_b_sel_k,
                                          cache_b_sel_v))
    odd_paged = tuple(pages(c) for c in (cache_c_k, cache_c_v, cache_c_idx_k))
    ev_p, ev_s, od_p, od_s, mk_list, mv_list = [], [], [], [], [], []
    for layer in range(depth):
        j = layer // 2
        mem_kv = _mm(mem_prompt.reshape(bp * n_mem, d), w_mem_kv[layer], bp * n_mem, 2 * N_MEM_HEADS * HEAD_DIM)
        mem_kv = mem_kv.reshape(bp, n_mem, 2, N_MEM_HEADS, HEAD_DIM)
        mk_p, mv_p = mem_kv[:, :, 0], mem_kv[:, :, 1]
        mk_list.append(mk_p)
        mv_list.append(mv_p)
        mem_p = (mk_p.transpose(0, 2, 1, 3), mv_p.transpose(0, 2, 3, 1))
        mem_s = (cache_mem_k[layer].transpose(0, 2, 1, 3), cache_mem_v[layer].transpose(0, 2, 3, 1))
        g = ln_g[layer].reshape(d, 1)
        b = ln_b[layer].reshape(d, 1)
        if layer % 2 == 0:
            wT = _even_wT(w_in_even[j])
            woT = w_out_even[j].T
            cmpw = [_cmp_weights(cmp_pe[j, c], cmp_w1[j, c], cmp_w2[j, c]) for c in range(2)]
            past = even_paged + (state_b_win_k[j], state_b_win_v[j])
            xp, st_p = _even_layer(gp, xp, *mem_p, None, None, wT.astype(BF16), woT.astype(BF16), g, b, cmpw, alpha)
            xs, st_s = _even_layer(gs, xs, *mem_s, past, page_table + j * n_pool, wT, woT, g, b, cmpw, alpha)
            ev_p.append(st_p)
            ev_s.append(st_s)
        else:
            wT = _odd_wT(w_in_odd[j])
            woT = w_out_odd[j].T
            past = odd_paged
            xp, st_p = _odd_layer(gp, xp, *mem_p, None, None, wT.astype(BF16), woT.astype(BF16), g, b, alpha)
            xs, st_s = _odd_layer(gs, xs, *mem_s, past, page_table + j * n_pool, wT, woT, g, b, alpha)
            od_p.append(st_p)
            od_s.append(st_s)
    y_p = xp.T.reshape(bp, seq, d)
    y_s = xs[:, :bs].T.reshape(bs, 1, d)
    ev_p = [jnp.stack(l) for l in zip(*ev_p)]
    ev_s = [jnp.stack(l) for l in zip(*ev_s)]
    od_p = [jnp.stack(l) for l in zip(*od_p)]
    od_s = [jnp.stack(l) for l in zip(*od_s)]
    out = [y_p, y_s]
    for p, s in zip(ev_p, ev_s):
        out += [p, s]
    for p, s in zip(od_p, od_s):
        out += [p, s]
    out += [jnp.stack(mk_list), jnp.stack(mv_list)]
    return tuple(out)
```

```python
import functools

import numpy as np
import jax
import jax.numpy as jnp
from jax import lax
from jax.experimental import pallas as pl
from jax.experimental.pallas import tpu as pltpu

F32 = jnp.float32
BF16 = jnp.bfloat16
I32 = jnp.int32

HEAD_DIM = 64
N_MEM_HEADS = 4
N_A_HEADS = 6
N_B_HEADS = 6
N_C_HEADS = 12
N_KV = 2
N_IDX_HEADS = 8
IDX_DIM = 64
MOBA_BLOCK = 256
MOBA_TOPK = 3
CMP_STRIDE = 16
CMP_LEN = 32
CMP_HIDDEN = 128
SEL_BLOCK = 64
SEL_TOPK = 15
WINDOW = 512
DSA_TOPK = 256
ATTN_SCALE = HEAD_DIM ** -0.5
IDX_SCALE = IDX_DIM ** -0.5
LN_EPS = 1e-5
LOG2E = 1.4426950408889634
NEG = -1e30
SENTINEL = -3e38
INT_MIN = -(2 ** 31)

KV_TILE = 256
SEL_TILE = 512
KD = 256
FEAT_POS = 128
FEAT_SEL = 144
COUNT_GROUP = 4
SAMPLE_COLS = 128
META_ROWS = 16
META_PIECES = 8
PAGES_PER_STEP = 8
GATHER_PAGES = 16
VMEM_LIMIT = 56 * 1024 * 1024

EVEN_ROWS = dict(a_q=0, b_q=384, m_q=768, z=1024, a_k=2048, a_v=2176, b_kc=2304, b_vc=2432,
                 b_ks=2560, b_vs=2688, b_kw=2816, b_vw=2944, b_g=3072)
EVEN_TOTAL = 3200
ODD_ROWS = dict(c_q=0, m_q=768, z=1024, c_qi=2048, c_k=2560, c_v=2688, c_ki=2816, c_wi=2880)
ODD_TOTAL = 3072


def _cparams(sem):
    return pltpu.CompilerParams(dimension_semantics=sem, vmem_limit_bytes=VMEM_LIMIT)


def _alibi(n):
    return 2.0 ** (-8.0 * np.arange(1, n + 1) / n)


def _mm_kernel(a_ref, b_ref, o_ref, *, precise):
    if precise:
        o_ref[...] = jnp.dot(a_ref[...], b_ref[...], precision=lax.Precision.HIGHEST, preferred_element_type=F32)
    else:
        o_ref[...] = jnp.dot(a_ref[...].astype(BF16), b_ref[...].astype(BF16), preferred_element_type=F32)


def _mm(a, b, bm, bn):
    m, k = a.shape
    n = b.shape[1]
    return pl.pallas_call(
        functools.partial(_mm_kernel, precise=a.dtype == F32 and b.dtype == F32 and n <= 128), grid=(n // bn, m // bm),
        in_specs=[pl.BlockSpec((bm, k), lambda j, i: (i, 0)), pl.BlockSpec((k, bn), lambda j, i: (0, j))],
        out_specs=pl.BlockSpec((bm, bn), lambda j, i: (i, j)),
        out_shape=jax.ShapeDtypeStruct((m, n), F32),
        compiler_params=_cparams(("parallel", "parallel")), name="mm")(a, b)


def _outln_kernel(*refs, n_parts, alpha):
    o_refs = refs[:n_parts]
    z_ref, x_ref, w_ref, g_ref, b_ref, y_ref = refs[n_parts:]
    o = jnp.concatenate([r[...] for r in o_refs], axis=0)
    z = z_ref[...]
    mixed = o * (z / (1.0 + jnp.exp(-z)))
    if w_ref.dtype == F32:
        y = jnp.dot(w_ref[...], mixed, precision=lax.Precision.HIGHEST, preferred_element_type=F32)
    else:
        y = jnp.dot(w_ref[...], mixed.astype(BF16), preferred_element_type=F32)
    y = alpha * x_ref[...] + y
    mu = jnp.mean(y, axis=0, keepdims=True)
    d = y - mu
    var = jnp.mean(d * d, axis=0, keepdims=True)
    y_ref[...] = d * lax.rsqrt(var + LN_EPS) * g_ref[...] + b_ref[...]


def _outln(o_parts, hT, z_row, xT, woT, g, b, alpha, bn):
    d, n = xT.shape
    zb = z_row // d
    in_specs = [pl.BlockSpec((p.shape[0], bn), lambda j: (0, j)) for p in o_parts]
    in_specs += [pl.BlockSpec((d, bn), lambda j: (zb, j)),
                 pl.BlockSpec((d, bn), lambda j: (0, j)),
                 pl.BlockSpec((d, d), lambda j: (0, 0)),
                 pl.BlockSpec((d, 1), lambda j: (0, 0)),
                 pl.BlockSpec((d, 1), lambda j: (0, 0))]
    return pl.pallas_call(
        functools.partial(_outln_kernel, n_parts=len(o_parts), alpha=alpha), grid=(n // bn,),
        in_specs=in_specs, out_specs=pl.BlockSpec((d, bn), lambda j: (0, j)),
        out_shape=jax.ShapeDtypeStruct((d, n), F32),
        compiler_params=_cparams(("parallel",)), name="outln")(*o_parts, hT, xT, woT, g, b)


def _key_features(tile):
    r = lax.broadcasted_iota(I32, (tile, 128), 0)
    c = lax.broadcasted_iota(I32, (tile, 128), 1)
    sel_c = c - (FEAT_SEL - FEAT_POS)
    feat = jnp.where(c < 3, r >> 4, jnp.where(c < 6, r & 15, jnp.where(sel_c == (r >> 6), 1, 0)))
    return feat.astype(F32).astype(BF16)


def _emit_tiles(x, outs, mode, tile, t0=0):
    nt = x.shape[0] // tile
    if mode == 'raw':
        outs[0][0, t0 * tile:(t0 + nt) * tile, :] = x
        return
    feat = _key_features(tile) if mode in ('k', 'ks') else None
    for t in range(nt):
        xt = x[t * tile:(t + 1) * tile]
        if mode in ('k', 'ks'):
            outs[0][0, t0 + t] = jnp.concatenate([xt.astype(BF16), feat], axis=1)
        elif mode == 'v':
            outs[0][0, t0 + t] = jnp.transpose(xt).astype(BF16)
        else:
            outs[0][0, t0 + t] = xt.astype(BF16)
    if mode == 'k':
        nblk = x.shape[0] // MOBA_BLOCK
        outs[1][0, 0, t0:t0 + nblk, :] = jnp.mean(x.reshape(nblk, MOBA_BLOCK, x.shape[1]), axis=1)


def _zero_tiles(outs, mode, tile, t0, t1):
    for t in range(t0, t1):
        outs[0][0, t] = jnp.zeros(outs[0].shape[2:], outs[0].dtype)
    if mode == 'k':
        outs[1][0, 0, t0:t1, :] = jnp.zeros((t1 - t0, outs[1].shape[3]), F32)


def _tile_outputs(mode, nb, n_steps, rows, tile, c):
    nt = rows // tile
    if mode == 'raw':
        return ([pl.BlockSpec((1, rows, c), lambda b, j, *_: (b, j, 0))],
                [jax.ShapeDtypeStruct((nb, n_steps * rows, c), F32)])
    if mode in ('k', 'ks'):
        shape, blk = (nb, n_steps * nt, tile, KD), (1, nt, tile, KD)
    elif mode == 'v':
        shape, blk = (nb, n_steps * nt, c, tile), (1, nt, c, tile)
    else:
        shape, blk = (nb, n_steps * nt, tile, c), (1, nt, tile, c)
    specs = [pl.BlockSpec(blk, lambda b, j, *_: (b, j, 0, 0))]
    shapes = [jax.ShapeDtypeStruct(shape, BF16)]
    if mode == 'k':
        nblk = rows // MOBA_BLOCK
        specs.append(pl.BlockSpec((1, 1, nblk, c), lambda b, j, *_: (b, j, 0, 0)))
        shapes.append(jax.ShapeDtypeStruct((nb, n_steps, nblk, c), F32))
    return specs, shapes


def _gather_kernel(pt_ref, *refs, npg, mode, tile, last, has_tail):
    pool_refs = refs[:npg]
    outs = refs[npg + (1 if has_tail else 0):]
    j = pl.program_id(1)

    def token_major(p):
        c = p.shape[0]
        if c == p.shape[1]:
            return jnp.transpose(p)
        return jnp.transpose(jnp.concatenate([p, jnp.zeros((p.shape[1] - c, p.shape[1]), p.dtype)], axis=0))[:, :c]

    def pages():
        if mode == 'v':
            xT = jnp.concatenate([r[0] for r in pool_refs], axis=1)
            for t in range(xT.shape[1] // tile):
                outs[0][0, t] = xT[:, t * tile:(t + 1) * tile].astype(BF16)
        else:
            _emit_tiles(jnp.concatenate([token_major(r[0]) for r in pool_refs], axis=0), outs, mode, tile)

    if not has_tail:
        pages()
        return
    pl.when(j < last)(pages)

    @pl.when(j == last)
    def _():
        _emit_tiles(refs[npg][0], outs, mode, tile)
        if mode != 'raw':
            _zero_tiles(outs, mode, tile, 1, outs[0].shape[1])


def _gather(pool, page_table, new_tok, mode, tile):
    nb, n_pages = page_table.shape
    n_pool, c, page_tokens = pool.shape
    npg = GATHER_PAGES if n_pages % GATHER_PAGES == 0 else PAGES_PER_STEP
    last = n_pages // npg
    has_tail = new_tok is not None
    rows = npg * page_tokens
    assert rows % tile == 0 and (has_tail or mode == 'raw')

    def pool_spec(k):
        return pl.BlockSpec((1, c, page_tokens),
                            lambda b, j, pt: (pt[b, jnp.minimum(j * npg + k, n_pages - 1)], 0, 0))

    in_specs = [pool_spec(k) for k in range(npg)]
    args = [pool] * npg
    if has_tail:
        tail = jnp.zeros((nb, tile, c), F32).at[:, 0:1, :].set(new_tok)
        in_specs.append(pl.BlockSpec((1, tile, c), lambda b, j, pt: (b, 0, 0)))
        args.append(tail)
    n_steps = last + (1 if has_tail else 0)
    out_specs, out_shape = _tile_outputs(mode, nb, n_steps, rows, tile, c)
    grid_spec = pltpu.PrefetchScalarGridSpec(num_scalar_prefetch=1, grid=(nb, n_steps), in_specs=in_specs,
                                             out_specs=out_specs)
    out = pl.pallas_call(
        functools.partial(_gather_kernel, npg=npg, mode=mode, tile=tile, last=last, has_tail=has_tail),
        grid_spec=grid_spec, out_shape=out_shape,
        compiler_params=_cparams(("parallel", "arbitrary")), name="gather_" + mode)(page_table, *args)
    if mode == 'k':
        return out[0], out[1].reshape(nb, -1, c)
    return out[0]


def _kprep_kernel(x_ref, *outs, mode, tile):
    _emit_tiles(x_ref[0], outs, mode, tile)


def _kprep(x, tile, mode='k'):
    nb, L, c = x.shape
    rows = 1024
    out_specs, out_shape = _tile_outputs(mode, nb, L // rows, rows, tile, c)
    out = pl.pallas_call(
        functools.partial(_kprep_kernel, mode=mode, tile=tile), grid=(nb, L // rows),
        in_specs=[pl.BlockSpec((1, rows, c), lambda b, j: (b, j, 0))],
        out_specs=out_specs, out_shape=out_shape,
        compiler_params=_cparams(("parallel", "parallel")), name="kprep")(x)
    if mode == 'k':
        return out[0], out[1].reshape(nb, -1, c)
    return out[0]


def _vT_tiles(v, tile):
    nb, L, c = v.shape
    return v.reshape(nb, L // tile, tile, c).transpose(0, 1, 3, 2).astype(BF16)


class _Cols:
    def __init__(self, nb, t, q_off):
        self.nb, self.t, self.q_off = nb, t, q_off
        self.sample = q_off > 0
        self.tq = 1 if self.sample else KV_TILE
        self.nq = 1 if self.sample else t // KV_TILE
        self.span = 1 if self.sample else KV_TILE

    def n_hc(self, heads):
        return 1 if self.sample else N_KV

    def step_cols(self, heads):
        return SAMPLE_COLS if self.sample else heads // N_KV * self.tq

    def head_group(self, heads):
        return np.arange(heads) // (heads // N_KV)

    def per_head(self, vals, heads):
        vals = jnp.asarray(vals, F32)
        k = vals.shape[0]
        if self.sample:
            return jnp.tile(jnp.pad(vals, ((0, 0), (0, SAMPLE_COLS - heads))), (1, self.nb))
        hps = heads // N_KV
        shape = (k, self.nb, self.nq, N_KV, hps, self.tq)
        return jnp.broadcast_to(vals.reshape(k, 1, 1, N_KV, hps, 1), shape).reshape(k, -1)

    def per_token(self, rows, heads):
        nb, nq, tq = self.nb, self.nq, self.tq
        k = rows.shape[0] // heads
        if self.sample:
            x = rows[:, :nb].reshape(heads, k, nb).transpose(1, 2, 0)
            return jnp.pad(x, ((0, 0), (0, 0), (0, SAMPLE_COLS - heads))).reshape(k, nb * SAMPLE_COLS)
        x = rows.reshape(N_KV, heads // N_KV, k, nb, nq, tq).transpose(2, 3, 4, 0, 1, 5)
        return x.reshape(k, -1)

    def qpos_row(self, heads):
        if self.sample:
            return jnp.full((1, self.nb * SAMPLE_COLS), self.q_off, F32)
        pos = (jnp.arange(self.nq, dtype=F32)[:, None] * self.tq + jnp.arange(self.tq, dtype=F32)[None, :])
        shape = (1, self.nb, self.nq, N_KV, heads // N_KV, self.tq)
        return jnp.broadcast_to(pos.reshape(1, 1, self.nq, 1, 1, self.tq), shape).reshape(1, -1)

    def meta(self, heads, slopes, extra=()):
        s2 = np.asarray(slopes, np.float64) * LOG2E
        hi = s2.astype(jnp.bfloat16).astype(np.float64)
        mid = (s2 - hi).astype(jnp.bfloat16).astype(np.float64)
        lo = (s2 - hi - mid).astype(jnp.bfloat16).astype(np.float64)
        rows = [self.per_head((hi + mid + lo)[None, :], heads), self.qpos_row(heads)] + list(extra)
        n = rows[0].shape[1]
        rows.append(jnp.zeros((META_PIECES - sum(m.shape[0] for m in rows), n), F32))
        rows.append(self.per_head(np.stack([16 * hi, 16 * mid, 16 * lo, hi, mid, lo]), heads))
        rows.append(jnp.zeros((2, n), F32))
        return jnp.concatenate(rows, axis=0)

    def columns(self, rows, heads, meta):
        x = self.per_token(rows, heads)
        grp = self.head_group(heads)
        gsel = self.per_head(np.stack([grp == 0, grp == 1]).astype(np.float32), heads)
        qf = jnp.concatenate([x * gsel[0:1], x * gsel[1:2]], axis=0)
        qa = jnp.concatenate([(qf * (ATTN_SCALE * LOG2E)).astype(BF16), meta[META_PIECES:].astype(BF16),
                              jnp.zeros((8, x.shape[1]), BF16),
                              jnp.zeros((KD - FEAT_POS - 16, x.shape[1]), BF16)], axis=0)
        return qa, qf

    def collapse(self, o, heads):
        nb, nq, tq = self.nb, self.nq, self.tq
        if self.sample:
            o = o.reshape(N_KV, HEAD_DIM, nb, SAMPLE_COLS)
            grp = self.head_group(heads)
            x = jnp.stack([o[grp[h], :, :, h] for h in range(heads)])
            return jnp.pad(x.reshape(heads * HEAD_DIM, nb), ((0, 0), (0, 128 - nb)))
        o = o.reshape(HEAD_DIM, nb, nq, N_KV, heads // N_KV, tq).transpose(3, 4, 0, 1, 2, 5)
        return o.reshape(heads * HEAD_DIM, nb * nq * tq)

    def tokens(self, rows):
        n = rows.shape[0]
        if self.sample:
            return rows[:, :self.nb].T.reshape(self.nb, 1, n)
        return rows.reshape(n, self.nb, self.t).transpose(1, 2, 0)

    def tile_cols(self, rows):
        if not self.sample:
            return rows
        n = rows.shape[0]
        return jnp.broadcast_to(rows[:, :self.nb, None], (n, self.nb, SAMPLE_COLS)).reshape(n, self.nb * SAMPLE_COLS)

    def from_tile_cols(self, o):
        if not self.sample:
            return o
        n = o.shape[0]
        return jnp.pad(o.reshape(n, self.nb, SAMPLE_COLS)[:, :, 0], ((0, 0), (0, 128 - self.nb)))


def _softmax_step(s, c, ok, vT, m, l, acc_ref):
    smax = jnp.max(s, axis=0, keepdims=True)
    live = smax > 0.5 * NEG
    if ok is not None:
        live = live & _row_flag(ok, s.shape[1])
    m_new = jnp.where(live, jnp.maximum(m, smax + c), m)
    shift = jnp.where(live, m_new - c, -NEG)
    p = jnp.exp2(s - shift)
    alpha = jnp.exp2(m - m_new)
    l = alpha * l + jnp.sum(p, axis=0, keepdims=True)
    acc_ref[...] = alpha * acc_ref[...] + _pv(vT, p, acc_ref.shape[0])
    return m_new, l


def _pv(vT, p, rows):
    pb = p.astype(BF16)
    if vT.shape[0] == rows:
        return jnp.dot(vT, pb, preferred_element_type=F32)
    h = pb.shape[1] // N_KV
    return jnp.concatenate([jnp.dot(vT[g * HEAD_DIM:(g + 1) * HEAD_DIM], pb[:, g * h:(g + 1) * h],
                                    preferred_element_type=F32) for g in range(N_KV)], axis=1)


def _row_flag(ok, n):
    if ok.ndim == 0:
        return (jnp.zeros((1, n), I32) + ok.astype(I32)) > 0
    return ok


def _group(mixed):
    return 4 if mixed else 2


def _softmax_init(n):
    return jnp.full((1, n), NEG, F32), jnp.zeros((1, n), F32)


def _tile_groups(n_tiles, step, carry, group):
    def body(u, carry):
        carry = step(group * u, 0, None, carry)
        for j in range(1, group):
            t = group * u + j
            carry = step(jnp.minimum(t, n_tiles - 1), j, t < n_tiles, carry)
        return carry
    return lax.fori_loop(0, (n_tiles + group - 1) // group, body, carry)


def _dense_attend(k_ref, n_k, vT_ref, n_v, qa, slope, qpos, blk, pos0=0, mask_fn=None, bias=None):
    n = qa.shape[1]
    tile = k_ref.shape[2]
    rows = n_k * tile
    s = jnp.dot(k_ref[0, 0:n_k].reshape(rows, KD), qa, preferred_element_type=F32)
    if bias is not None:
        s = s + bias
    nblk = rows // blk
    s = s.reshape(nblk, blk, n)
    b3 = lax.broadcasted_iota(I32, (nblk, 1, 1), 0)
    base = pos0 + (b3 * blk) // tile * tile
    s = s + slope * (base - qpos).astype(F32)
    if mask_fn is not None:
        kpos = pos0 + b3 * blk + lax.broadcasted_iota(I32, (nblk, blk, n), 1)
        s = jnp.where(mask_fn(b3, kpos), s, NEG)
    s = s.reshape(rows, n)
    m = jnp.max(s, axis=0, keepdims=True)
    p = jnp.where(s > 0.5 * NEG, jnp.exp2(s - m), 0.0)
    l = jnp.sum(p, axis=0, keepdims=True)
    vT = jnp.concatenate([vT_ref[0, t] for t in range(n_v)], axis=1)
    return jnp.dot(vT, p.astype(BF16), preferred_element_type=F32) / jnp.maximum(l, 1e-30)


def _step_queries(q_ref, meta_ref, qa_s, g):
    tq = q_ref.shape[1]
    n = qa_s.shape[1]
    qa_s[...] = jnp.zeros(qa_s.shape, BF16)
    pieces = jnp.concatenate([meta_ref[META_PIECES:META_ROWS, :], jnp.zeros((8, n), F32)], axis=0)
    qa_s[FEAT_POS:FEAT_POS + 16, :] = pieces.astype(BF16)
    heads = q_ref.shape[0] // HEAD_DIM
    qf = []
    for r in range(heads):
        q = q_ref[r * HEAD_DIM:(r + 1) * HEAD_DIM, :]
        if g is None:
            w0 = 1.0 if r < heads // N_KV else 0.0
        else:
            w0 = (g == 0).astype(F32)
        blk = jnp.concatenate([q * w0, q * (1.0 - w0)], axis=0)
        qa_s[0:N_KV * HEAD_DIM, r * tq:(r + 1) * tq] = (blk * (ATTN_SCALE * LOG2E)).astype(BF16)
        qf.append(blk)
    return qa_s[...], jnp.concatenate(qf, axis=1)


def _store_heads(o_ref, o, mixed):
    if mixed:
        o_ref[...] = o
        return
    tq = o_ref.shape[1]
    for r in range(o_ref.shape[0] // HEAD_DIM):
        o_ref[r * HEAD_DIM:(r + 1) * HEAD_DIM, :] = o[:, r * tq:(r + 1) * tq]


def _value_rows(vT_ref, n, g, mixed):
    if mixed or g is None:
        return vT_ref[0, n]
    return vT_ref[0, n, pl.ds(pl.multiple_of(g * HEAD_DIM, HEAD_DIM), HEAD_DIM), :]


def _top_rows(vals, rowi, k):
    sel = jnp.zeros(vals.shape, F32)
    for _ in range(k):
        m = jnp.max(vals, axis=0, keepdims=True)
        first = jnp.min(jnp.where(vals == m, rowi, 1 << 20), axis=0, keepdims=True)
        pick = (rowi == first) & (m > SENTINEL)
        sel = jnp.where(pick, 1.0, sel)
        vals = jnp.where(pick, SENTINEL, vals)
    return sel


def _moba_kernel(qa_ref, qf_ref, meta_ref, kmean_ref, k_ref, vT_ref, o_ref, sel_ref, acc_ref, qa_s,
                 *, span, q_off, mixed):
    i, g = pl.program_id(1), None
    n = meta_ref.shape[1]
    qa, qf = (qa_ref[...], qf_ref[...]) if mixed else _step_queries(qa_ref, meta_ref, qa_s, g)
    own = (q_off + i * span) // MOBA_BLOCK
    slope = meta_ref[0:1, :]
    qpos = meta_ref[1:2, :].astype(I32)
    nrow = kmean_ref.shape[1]
    gate = jnp.dot(kmean_ref[0], qf, precision=lax.Precision.HIGHEST, preferred_element_type=F32)
    rowi = lax.broadcasted_iota(I32, (nrow, n), 0)
    sel_ref[...] = _top_rows(jnp.where(rowi < own, gate, SENTINEL), rowi, MOBA_TOPK)
    if mixed:
        own_s = q_off // MOBA_BLOCK
        sel3 = sel_ref[0:own_s + 1, :][:, None, :] > 0.5
        o_ref[...] = _dense_attend(k_ref, own_s + 1, vT_ref, own_s + 1, qa, slope, qpos, KV_TILE,
                                   mask_fn=lambda b3, kpos: ((b3 < own_s) & sel3) | ((b3 >= own_s) & (kpos <= qpos)))
        return
    acc_ref[...] = jnp.zeros(acc_ref.shape, F32)

    def step(t, slot, ok, carry):
        s = jnp.dot(k_ref[0, t], qa, preferred_element_type=F32)
        c = slope * (t * KV_TILE - qpos).astype(F32)
        sel = sel_ref[pl.ds(t, 1), :] > 0.5
        ok = sel if ok is None else sel & _row_flag(ok, n)
        return _softmax_step(s, c, ok, _value_rows(vT_ref, t, g, mixed), *carry, acc_ref)

    m, l = _tile_groups(own, step, _softmax_init(n), _group(mixed))
    s = jnp.dot(k_ref[0, own], qa, preferred_element_type=F32)
    krow = lax.broadcasted_iota(I32, (KV_TILE, n), 0)
    s = jnp.where(krow <= qpos - own * KV_TILE, s, NEG)
    c = slope * (own * KV_TILE - qpos).astype(F32)
    _, l = _softmax_step(s, c, None, _value_rows(vT_ref, own, g, mixed), m, l, acc_ref)
    _store_heads(o_ref, acc_ref[...] / jnp.maximum(l, 1e-30), mixed)


def _query_specs(cols, heads, q_row, n_cols, whole=False):
    nq, n_hc, n = cols.nq, cols.n_hc(heads), cols.step_cols(heads)
    if cols.sample:
        col = lambda b, i, g: (0, (b * nq + i) * n_hc + g)
        return (pl.BlockSpec((KD, n), col), pl.BlockSpec((N_KV * HEAD_DIM, n), col),
                jax.ShapeDtypeStruct((N_KV * HEAD_DIM, n_cols), F32), N_KV * HEAD_DIM)
    rows = heads * HEAD_DIM if whole else heads // N_KV * HEAD_DIM
    blk0 = q_row // rows
    return (pl.BlockSpec((rows, cols.tq), lambda b, i, g: (blk0 + g, b * nq + i)),
            pl.BlockSpec((rows, cols.tq), lambda b, i, g: (g, b * nq + i)),
            jax.ShapeDtypeStruct((heads * HEAD_DIM, cols.nb * nq * cols.tq), F32), HEAD_DIM)


def _moba(cols, q, q_row, qf, meta, kmean, k, vT):
    nb, nq = cols.nb, cols.nq
    n_hc, n = 1, (SAMPLE_COLS if cols.sample else N_A_HEADS * cols.tq)
    nT, nblk = k.shape[1], kmean.shape[1]
    q_spec, o_spec, o_shape, vrows = _query_specs(cols, N_A_HEADS, q_row, meta.shape[1], whole=True)
    col = lambda b, i, g: (0, (b * nq + i) * n_hc + g)
    return pl.pallas_call(
        functools.partial(_moba_kernel, span=cols.span, q_off=cols.q_off, mixed=cols.sample), grid=(nb, nq, n_hc),
        in_specs=[q_spec,
                  pl.BlockSpec((qf.shape[0], n), col),
                  pl.BlockSpec((META_ROWS, n), col),
                  pl.BlockSpec((1, nblk, N_KV * HEAD_DIM), lambda b, i, g: (b, 0, 0)),
                  pl.BlockSpec((1, nT, KV_TILE, KD), lambda b, i, g: (b, 0, 0, 0)),
                  pl.BlockSpec((1, nT, N_KV * HEAD_DIM, KV_TILE), lambda b, i, g: (b, 0, 0, 0))],
        out_specs=o_spec, out_shape=o_shape,
        scratch_shapes=[pltpu.VMEM((nblk, n), F32), pltpu.VMEM((vrows, n), F32), pltpu.VMEM((KD, n), BF16)],
        compiler_params=_cparams(("parallel", "parallel", "arbitrary")), name="moba")(q, qf, meta, kmean, k, vT)


def _cmp_kernel(r_ref, pe_ref, w1_ref, w2_ref, o_ref):
    n = o_ref.shape[1]
    hid = 2 * CMP_HIDDEN
    c = r_ref.shape[2]
    u = jnp.zeros((n, hid), F32)
    lo = jnp.zeros((n, hid), F32)
    for p in range(CMP_STRIDE):
        rp = r_ref[0, pl.ds(p, n, stride=CMP_STRIDE), :]
        cols = slice(p * c, (p + 1) * c)
        u = u + jnp.dot((rp + pe_ref[0:1, cols]).astype(BF16), w1_ref[cols, 0:hid], preferred_element_type=F32)
        lo = lo + jnp.dot((rp + pe_ref[1:2, cols]).astype(BF16), w1_ref[cols, hid:2 * hid],
                          preferred_element_type=F32)
    pre = u + pltpu.roll(lo, n - 1, 0)
    h = pre / (1.0 + jnp.exp(-pre))
    o_ref[0] = jnp.dot(h.astype(BF16), w2_ref[...], preferred_element_type=F32)


def _compress(raw, pe2, w1big, w2big, n_rows):
    nb, _, c = raw.shape
    return pl.pallas_call(
        _cmp_kernel, grid=(nb,),
        in_specs=[pl.BlockSpec((1, n_rows * CMP_STRIDE, c), lambda b: (b, 0, 0)),
                  pl.BlockSpec((2, 2048), lambda b: (0, 0)),
                  pl.BlockSpec((2048, 4 * CMP_HIDDEN), lambda b: (0, 0)),
                  pl.BlockSpec((2 * CMP_HIDDEN, 128), lambda b: (0, 0))],
        out_specs=pl.BlockSpec((1, n_rows, 128), lambda b: (b, 0, 0)),
        out_shape=jax.ShapeDtypeStruct((nb, n_rows, 128), F32),
        compiler_params=_cparams(("parallel",)), name="compress")(raw, pe2, w1big, w2big)


def _cmp_weights(pe, w1, w2):
    w1r = w1.reshape(2, CMP_STRIDE, HEAD_DIM, CMP_HIDDEN)
    eye = jnp.eye(N_KV, dtype=w1.dtype)
    big = jnp.einsum('hpdj,ge->hpdgej', w1r, eye)
    big = big.transpose(1, 3, 2, 0, 4, 5).reshape(CMP_STRIDE * N_KV * HEAD_DIM, 2 * N_KV * CMP_HIDDEN)
    w2big = jnp.einsum('jd,ge->gjed', w2, eye).reshape(N_KV * CMP_HIDDEN, N_KV * HEAD_DIM)
    pe2 = jnp.broadcast_to(pe.reshape(2, CMP_STRIDE, 1, HEAD_DIM), (2, CMP_STRIDE, N_KV, HEAD_DIM)).reshape(2, 2048)
    return pe2, big.astype(BF16), w2big.astype(BF16)


def _nsa_kernel(qa_ref, meta_ref, kc_ref, vcT_ref, taps_ref, grp_ref, ks_ref, vsT_ref, kw_ref, vwT_ref, o_ref,
                sel_ref, qs_ref, acc_ref, qa_s, *, span, q_off, kw_base, mixed):
    i, g = pl.program_id(1), None
    n = meta_ref.shape[1]
    hps = N_B_HEADS // N_KV
    tq = n if mixed else n // N_B_HEADS
    nq = n if mixed else N_KV * tq
    t0 = q_off + i * span
    slope = meta_ref[0:1, :]
    qpos = meta_ref[1:2, :].astype(I32)
    qa = qa_ref[...] if mixed else _step_queries(qa_ref, meta_ref, qa_s, g)[0]
    n_cmp = kc_ref.shape[1]
    n_tab = sel_ref.shape[0]

    def per_head(x):
        if mixed:
            return x
        return jnp.concatenate([x[:, (r // hps) * tq:(r // hps + 1) * tq] for r in range(N_B_HEADS)], axis=1)

    cend = CMP_STRIDE * lax.broadcasted_iota(I32, (n_cmp, n), 0) + (CMP_LEN - 1)
    cmask = cend <= qpos
    s = jnp.dot(kc_ref[0], qa[0:N_KV * HEAD_DIM, :], preferred_element_type=F32) + slope * (cend - qpos).astype(F32)
    s = jnp.where(cmask, s, NEG)
    p = jnp.where(cmask, jnp.exp2(s - jnp.max(s, axis=0, keepdims=True)), 0.0)
    p = p * (1.0 / jnp.maximum(jnp.sum(p, axis=0, keepdims=True), 1e-30))
    o_c = _pv(vcT_ref[0], p, acc_ref.shape[0])

    if mixed:
        imp = jnp.dot(p, grp_ref[...], precision=lax.Precision.HIGHEST, preferred_element_type=F32)
    else:
        imp = jnp.concatenate([sum(p[:, (gg * hps + r) * tq:(gg * hps + r + 1) * tq] for r in range(hps))
                               for gg in range(N_KV)], axis=1)
    p_slc = jnp.dot(taps_ref[...], imp, precision=lax.Precision.HIGHEST, preferred_element_type=F32)
    p_slc = jnp.concatenate([p_slc, jnp.full((n_tab - p_slc.shape[0], nq), SENTINEL, F32)], axis=0)
    rowi = lax.broadcasted_iota(I32, (n_tab, nq), 0)
    own = (qpos if mixed else jnp.concatenate([qpos[:, 0:tq]] * N_KV, axis=1)) >> 6
    sel = _top_rows(jnp.where(rowi < own, p_slc, SENTINEL), rowi, SEL_TOPK)
    sel_ref[...] = jnp.where(rowi == own, 1.0, sel)

    gt = 1.0 / (1.0 + jnp.exp(-meta_ref[2:5, :]))
    if mixed:
        n_s = q_off // SEL_TILE + 1
        sel3 = sel_ref[0:n_s * (SEL_TILE // SEL_BLOCK), :][:, None, :] > 0.5
        o_s = _dense_attend(ks_ref, n_s, vsT_ref, n_s, qa, slope, qpos, SEL_BLOCK,
                            mask_fn=lambda b3, kpos: sel3 & (kpos <= qpos))
        n_w = min((q_off - kw_base) // KV_TILE + 1, kw_ref.shape[1])
        o_w = _dense_attend(kw_ref, n_w, vwT_ref, n_w, qa, slope, qpos, KV_TILE, pos0=kw_base,
                            mask_fn=lambda b3, kpos: (kpos <= qpos) & (kpos > qpos - WINDOW))
        o_ref[...] = gt[0:1, :] * o_c + gt[1:2, :] * o_s + gt[2:3, :] * o_w
        return

    per_tile = SEL_TILE // SEL_BLOCK
    last = (t0 + span - 1) // SEL_TILE
    for slot in range(qs_ref.shape[0]):
        qs_ref[slot] = qa
    acc_ref[...] = jnp.zeros(acc_ref.shape, F32)

    def sel_scores(t, slot):
        rows = sel_ref[pl.ds(pl.multiple_of(t * per_tile, per_tile), per_tile), :]
        bias = per_head(jnp.where(rows > 0.5, 0.0, NEG))
        bias = jnp.concatenate([bias, jnp.zeros((per_tile, n), F32)], axis=0)
        qs_ref[slot, FEAT_SEL:FEAT_SEL + 2 * per_tile, :] = bias.astype(BF16)
        s = jnp.dot(ks_ref[0, t], qs_ref[slot], preferred_element_type=F32)
        return s, slope * (t * SEL_TILE - qpos).astype(F32)

    def sel_step(t, slot, ok, carry):
        s, c = sel_scores(t, slot)
        return _softmax_step(s, c, ok, _value_rows(vsT_ref, t, g, mixed), *carry, acc_ref)

    m, l = _tile_groups(last, sel_step, _softmax_init(n), qs_ref.shape[0])
    s, c = sel_scores(last, 0)
    srow = lax.broadcasted_iota(I32, (SEL_TILE, n), 0)
    s = jnp.where(srow <= qpos - last * SEL_TILE, s, NEG)
    _, l = _softmax_step(s, c, None, _value_rows(vsT_ref, last, g, mixed), m, l, acc_ref)
    o_s = acc_ref[...] / jnp.maximum(l, 1e-30)

    lo = jnp.maximum(t0 - (WINDOW - 1) - kw_base, 0) // KV_TILE
    hi = jnp.minimum((t0 + span - 1 - kw_base) // KV_TILE + 1, kw_ref.shape[1])
    wrow = lax.broadcasted_iota(I32, (KV_TILE, n), 0)
    acc_ref[...] = jnp.zeros(acc_ref.shape, F32)

    def win_body(t, carry):
        s = jnp.dot(kw_ref[0, t], qa, preferred_element_type=F32)
        base = kw_base + t * KV_TILE
        dist = qpos - base - wrow
        s = jnp.where((dist >= 0) & (dist < WINDOW), s, NEG)
        return _softmax_step(s, slope * (base - qpos).astype(F32), None, _value_rows(vwT_ref, t, g, mixed),
                             *carry, acc_ref)

    _, l = lax.fori_loop(lo, hi, win_body, _softmax_init(n))
    o_w = acc_ref[...] / jnp.maximum(l, 1e-30)

    _store_heads(o_ref, gt[0:1, :] * o_c + gt[1:2, :] * o_s + gt[2:3, :] * o_w, mixed)


def _taps_matrix(n_sel, n_cmp):
    m = np.zeros((n_sel, n_cmp), np.float32)
    for j in range(n_sel):
        for off, w in ((-1, 1.0), (0, 2.0), (1, 2.0), (2, 2.0), (3, 1.0)):
            c = 4 * j + off
            if 0 <= c < n_cmp:
                m[j, c] = w
    return jnp.asarray(m)


def _group_matrix(cols):
    grp = cols.head_group(N_B_HEADS)
    m = np.zeros((SAMPLE_COLS, SAMPLE_COLS), np.float32)
    m[:N_B_HEADS, :N_B_HEADS] = grp[:, None] == grp[None, :]
    return jnp.asarray(m)


def _nsa(cols, q, q_row, meta, kc, vcT, ks, vsT, kw, vwT, kw_base):
    nb, nq = cols.nb, cols.nq
    n_hc, n = 1, (SAMPLE_COLS if cols.sample else N_B_HEADS * cols.tq)
    nTs, nTw = ks.shape[1], kw.shape[1]
    n_cmp = kc.shape[1]
    n_sel = n_cmp // 4
    n_tab = n_sel + SEL_TILE // SEL_BLOCK
    nsel_cols = n if cols.sample else N_KV * cols.tq
    q_spec, o_spec, o_shape, vrows = _query_specs(cols, N_B_HEADS, q_row, meta.shape[1], whole=True)
    col = lambda b, i, g: (0, (b * nq + i) * n_hc + g)
    seq3 = lambda b, i, g: (b, 0, 0)
    seq4 = lambda b, i, g: (b, 0, 0, 0)
    return pl.pallas_call(
        functools.partial(_nsa_kernel, span=cols.span, q_off=cols.q_off, kw_base=kw_base, mixed=cols.sample),
        grid=(nb, nq, n_hc),
        in_specs=[q_spec,
                  pl.BlockSpec((META_ROWS, n), col),
                  pl.BlockSpec((1, n_cmp, N_KV * HEAD_DIM), seq3),
                  pl.BlockSpec((1, N_KV * HEAD_DIM, n_cmp), seq3),
                  pl.BlockSpec((n_sel, n_cmp), lambda b, i, g: (0, 0)),
                  pl.BlockSpec((SAMPLE_COLS, SAMPLE_COLS), lambda b, i, g: (0, 0)),
                  pl.BlockSpec((1, nTs, SEL_TILE, KD), seq4),
                  pl.BlockSpec((1, nTs, N_KV * HEAD_DIM, SEL_TILE), seq4),
                  pl.BlockSpec((1, nTw, KV_TILE, KD), seq4),
                  pl.BlockSpec((1, nTw, N_KV * HEAD_DIM, KV_TILE), seq4)],
        out_specs=o_spec, out_shape=o_shape,
        scratch_shapes=[pltpu.VMEM((n_tab, nsel_cols), F32), pltpu.VMEM((_group(cols.sample), KD, n), BF16),
                        pltpu.VMEM((vrows, n), F32), pltpu.VMEM((KD, n), BF16)],
        compiler_params=_cparams(("parallel", "parallel", "arbitrary")), name="nsa")(
            q, meta, kc, vcT, _taps_matrix(n_sel, n_cmp), _group_matrix(cols), ks, vsT, kw, vwT)


def _dsa_kernel(qi_ref, wi_ref, ki_ref, qa_ref, meta_ref, k_ref, vT_ref, o_ref, key_ref, row_ref, acc_ref, qa_s,
                *, span, q_off, topk, mixed, hc_per_group):
    i, hc = pl.program_id(1), pl.program_id(2)
    n = meta_ref.shape[1]
    nq = qi_ref.shape[1]
    rep = n // nq
    t0 = q_off + i * span
    n_kv = (t0 + span - 1) // KV_TILE + 1
    krow = lax.broadcasted_iota(I32, (KV_TILE, nq), 0)

    def search(count, shape):
        kf = float(topk)
        thr = jnp.where(count(lambda key, kpos: key >= 0) >= kf, 0, INT_MIN).astype(I32)

        def bit_body(j, thr):
            cand = thr | (1 << (30 - j))
            return jnp.where(count(lambda key, kpos: key >= cand) >= kf, cand, thr)
        thr = lax.fori_loop(0, 31, bit_body, thr)
        need = kf - count(lambda key, kpos: key > thr)

        def tie_search():
            def idx_body(j, jb):
                cand = jb | (1 << (14 - j))
                c = count(lambda key, kpos: (key == thr) & (kpos < cand))
                return jnp.where(c <= need, cand, jb)
            return lax.fori_loop(0, 15, idx_body, jnp.zeros(shape, I32))

        n_ge = count(lambda key, kpos: key >= thr)
        jb = lax.cond(jnp.max(n_ge) > kf, tie_search, lambda: jnp.full(shape, (1 << 15) - 1, I32))
        return thr, jb

    @pl.when(hc == 0)
    def _():
        qpos = t0 + (lax.broadcasted_iota(I32, (1, nq), 1) if span > 1 else jnp.zeros((1, nq), I32))
        w = wi_ref[...] * (IDX_SCALE * N_IDX_HEADS ** -0.5)
        if mixed:
            qi = qi_ref[...].astype(BF16)
        else:
            qi = [qi_ref[a * IDX_DIM:(a + 1) * IDX_DIM, :].astype(BF16) for a in range(N_IDX_HEADS)]
        if mixed:
            row_ref[...] = jnp.full(row_ref.shape, INT_MIN, I32)

        def score_body(t, _):
            ki = ki_ref[0, t]
            if mixed:
                rel = jnp.dot(ki, qi, preferred_element_type=F32)
                acc = jnp.sum(jnp.maximum(rel, 0.0) * w[0:1, :], axis=1, keepdims=True)
                acc = jnp.broadcast_to(acc, (KV_TILE, nq))
            else:
                acc = jnp.zeros((KV_TILE, nq), F32)
                for a in range(N_IDX_HEADS):
                    rel = jnp.dot(ki, qi[a], preferred_element_type=F32)
                    acc = acc + jnp.maximum(rel, 0.0) * w[a:a + 1, :]
            sc = jnp.where(t * KV_TILE + krow <= qpos, acc, NEG)
            bits = lax.bitcast_convert_type(sc, I32)
            key = bits ^ ((bits >> 31) & 0x7FFFFFFF)
            key_ref[t] = key
            if mixed:
                row_ref[pl.ds(t, 1), :] = jnp.transpose(key)[0:1, :]
            return 0
        lax.fori_loop(0, n_kv, score_body, 0)

        if mixed:
            keys = row_ref[...]
            kpos_r = (lax.broadcasted_iota(I32, keys.shape, 0) * KV_TILE + lax.broadcasted_iota(I32, keys.shape, 1))

            def count(pred):
                c = jnp.sum(jnp.where(pred(keys, kpos_r), 1.0, 0.0), axis=0, keepdims=True)
                return jnp.sum(c, axis=1, keepdims=True)
            thr, jb = search(count, (1, 1))
        else:
            n_grp = (n_kv + COUNT_GROUP - 1) // COUNT_GROUP

            def pad_body(t, _):
                key_ref[t] = jnp.full((KV_TILE, nq), INT_MIN, I32)
                return 0
            lax.fori_loop(n_kv, n_grp * COUNT_GROUP, pad_body, 0)

            def count(pred):
                def body(u, c):
                    for j in range(COUNT_GROUP):
                        t = u * COUNT_GROUP + j
                        hit = pred(key_ref[t], t * KV_TILE + krow)
                        c = c + jnp.sum(jnp.where(hit, 1.0, 0.0), axis=0, keepdims=True)
                    return c
                return lax.fori_loop(0, n_grp, body, jnp.zeros((1, nq), F32))
            thr, jb = search(count, (1, nq))

        def bias_body(t, _):
            key = key_ref[t]
            kpos = t * KV_TILE + krow
            keep = ((key > thr) | ((key == thr) & (kpos < jb))) & (kpos <= qpos)
            key_ref[t] = lax.bitcast_convert_type(jnp.where(keep, 0.0, NEG), I32)
            return 0
        lax.fori_loop(0, n_kv, bias_body, 0)

    g = hc // hc_per_group
    slope = meta_ref[0:1, :]
    qpos_c = meta_ref[1:2, :].astype(I32)
    qa = qa_ref[...] if mixed else _step_queries(qa_ref, meta_ref, qa_s, g)[0]
    if mixed:
        n_t = q_off // KV_TILE + 1
        bias = lax.bitcast_convert_type(key_ref[0:n_t], F32).reshape(n_t * KV_TILE, nq)
        o_ref[...] = _dense_attend(k_ref, n_t, vT_ref, n_t, qa, slope, qpos_c, KV_TILE, bias=bias)
        return
    acc_ref[...] = jnp.zeros(acc_ref.shape, F32)

    def step(t, slot, ok, carry):
        bias = lax.bitcast_convert_type(key_ref[t], F32)
        if rep > 1:
            bias = jnp.concatenate([bias] * rep, axis=1)
        s = jnp.dot(k_ref[0, t], qa, preferred_element_type=F32) + bias
        c = slope * (t * KV_TILE - qpos_c).astype(F32)
        return _softmax_step(s, c, ok, _value_rows(vT_ref, t, g, mixed), *carry, acc_ref)
    _, l = _tile_groups(n_kv, step, _softmax_init(n), _group(mixed))
    _store_heads(o_ref, acc_ref[...] / jnp.maximum(l, 1e-30), mixed)


def _dsa(cols, qi, qi_row, wi, wi_row, ki, q, q_row, meta, k, vT, topk):
    nb, nq = cols.nb, cols.nq
    n_hc, n = cols.n_hc(N_C_HEADS), cols.step_cols(N_C_HEADS)
    ncol = SAMPLE_COLS if cols.sample else cols.tq
    nT = k.shape[1]
    assert nT % COUNT_GROUP == 0
    q_spec, o_spec, o_shape, vrows = _query_specs(cols, N_C_HEADS, q_row, meta.shape[1])
    qi_rows = IDX_DIM if cols.sample else N_IDX_HEADS * IDX_DIM
    qi_b, wi_b = qi_row // qi_rows, wi_row // N_IDX_HEADS
    col = lambda b, i, h: (0, (b * nq + i) * n_hc + h)
    seq4 = lambda b, i, h: (b, 0, 0, 0)
    return pl.pallas_call(
        functools.partial(_dsa_kernel, span=cols.span, q_off=cols.q_off, topk=topk, mixed=cols.sample,
                          hc_per_group=max(n_hc // N_KV, 1)),
        grid=(nb, nq, n_hc),
        in_specs=[pl.BlockSpec((qi_rows, ncol), lambda b, i, h: (qi_b, b * nq + i)),
                  pl.BlockSpec((N_IDX_HEADS, ncol), lambda b, i, h: (wi_b, b * nq + i)),
                  pl.BlockSpec((1, nT, KV_TILE, IDX_DIM), seq4),
                  q_spec,
                  pl.BlockSpec((META_ROWS, n), col),
                  pl.BlockSpec((1, nT, KV_TILE, KD), seq4),
                  pl.BlockSpec((1, nT, N_KV * HEAD_DIM, KV_TILE), seq4)],
        out_specs=o_spec, out_shape=o_shape,
        scratch_shapes=[pltpu.VMEM((nT, KV_TILE, ncol), I32), pltpu.VMEM((-(-nT // 8) * 8, KV_TILE), I32),
                        pltpu.VMEM((vrows, n), F32), pltpu.VMEM((KD, n), BF16)],
        compiler_params=_cparams(("parallel", "parallel", "arbitrary")), name="dsa")(
            qi, wi, ki, q, meta, k, vT)


def _mem_kernel(q_ref, mk_ref, mvT_ref, o_ref):
    for h in range(N_MEM_HEADS):
        rows = slice(h * HEAD_DIM, (h + 1) * HEAD_DIM)
        q = (q_ref[rows, :] * ATTN_SCALE).astype(BF16)
        s = jnp.dot(mk_ref[0, h].astype(BF16), q, preferred_element_type=F32)
        p = jnp.exp(s - jnp.max(s, axis=0, keepdims=True))
        l = jnp.sum(p, axis=0, keepdims=True)
        o_ref[rows, :] = jnp.dot(mvT_ref[0, h].astype(BF16), p.astype(BF16), preferred_element_type=F32) / l


def _mem(hT, q_row, mk, mvT, nb, nq, tq):
    n_mem = mk.shape[2]
    n = N_MEM_HEADS * HEAD_DIM
    qb = q_row // n
    return pl.pallas_call(
        _mem_kernel, grid=(nb, nq),
        in_specs=[pl.BlockSpec((n, tq), lambda b, i: (qb, b * nq + i)),
                  pl.BlockSpec((1, N_MEM_HEADS, n_mem, HEAD_DIM), lambda b, i: (b, 0, 0, 0)),
                  pl.BlockSpec((1, N_MEM_HEADS, HEAD_DIM, n_mem), lambda b, i: (b, 0, 0, 0))],
        out_specs=pl.BlockSpec((n, tq), lambda b, i: (0, b * nq + i)),
        out_shape=jax.ShapeDtypeStruct((n, nb * nq * tq), F32),
        compiler_params=_cparams(("parallel", "parallel")), name="mem")(hT, mk, mvT)


def _mem_branch(cols, hT, q_row, mk, mvT):
    n = N_MEM_HEADS * HEAD_DIM
    if cols.sample:
        o = _mem(cols.tile_cols(hT[q_row:q_row + n]), 0, mk, mvT, cols.nb, 1, SAMPLE_COLS)
        return cols.from_tile_cols(o)
    return _mem(hT, q_row, mk, mvT, cols.nb, cols.nq, cols.tq)


def _even_wT(w_in):
    sizes = (384, 128, 128, 384, 384, 128, 128, 128, 128, 128, 128, 18, 384, 256, 256)
    offs = np.cumsum((0,) + sizes)
    (a_q, a_k, a_v, a_z, b_q, b_kc, b_vc, b_ks, b_vs, b_kw, b_vw, b_g, b_z, m_q, m_z) = [
        w_in[:, offs[n]:offs[n + 1]] for n in range(len(sizes))]
    d = w_in.shape[0]
    bg = jnp.pad(b_g, ((0, 0), (0, 32 - b_g.shape[1])))
    cols = [a_q, b_q, m_q, a_z, b_z, m_z, a_k, a_v, b_kc, b_vc, b_ks, b_vs, b_kw, b_vw, bg]
    w = jnp.concatenate(cols, axis=1)
    w = jnp.pad(w, ((0, 0), (0, EVEN_TOTAL - w.shape[1])))
    return w.T


def _odd_wT(w_in):
    sizes = (768, 128, 128, 512, 64, 8, 768, 256, 256)
    offs = np.cumsum((0,) + sizes)
    (c_q, c_k, c_v, c_qi, c_ki, c_wi, c_z, m_q, m_z) = [w_in[:, offs[n]:offs[n + 1]] for n in range(len(sizes))]
    w = jnp.concatenate([c_q, m_q, c_z, m_z, c_qi, c_k, c_v, c_ki, c_wi], axis=1)
    w = jnp.pad(w, ((0, 0), (0, ODD_TOTAL - w.shape[1])))
    return w.T


def _keys(cols, new_tok, pool, page_table, tile, mode):
    if cols.sample:
        return _gather(pool, page_table, new_tok, mode, tile)
    return _kprep(new_tok, tile, mode)


def _values(cols, new_tok, pool, page_table, tile):
    if cols.sample:
        return _gather(pool, page_table, new_tok, 'v', tile)
    return _vT_tiles(new_tok, tile)


def _raw_rows(cols, new_tok, pool, page_table):
    if cols.sample:
        return _gather(pool, page_table, None, 'raw', KV_TILE)
    return new_tok


def _queries(cols, hT, row, heads, slopes, extra=()):
    meta = cols.meta(heads, slopes, extra)
    if cols.sample:
        qa, qf = cols.columns(hT[row:row + heads * HEAD_DIM], heads, meta)
        return qa, 0, qf, meta
    return hT, row, meta, meta


def _heads_out(cols, o, heads):
    return cols.collapse(o, heads) if cols.sample else o


def _gate_cols(cols, gT):
    return [cols.per_token(gT, N_B_HEADS)]


def _even_layer(cols, xT, mk, mvT, past, page_table, wT, woT, g, b, cmpw, alpha):
    R = EVEN_ROWS
    nb = cols.nb
    hT = _mm(wT, xT, 640, min(1024, xT.shape[1]))
    names = ('a_k', 'a_v', 'b_kc', 'b_vc', 'b_ks', 'b_vs')
    new = [cols.tokens(hT[R[nm]:R[nm] + 128]) for nm in names]
    kw_new, vw_new = cols.tokens(hT[R['b_kw']:R['b_kw'] + 128]), cols.tokens(hT[R['b_vw']:R['b_vw'] + 128])
    pools = past[:6] if cols.sample else [None] * 6
    if cols.sample:
        kw = jnp.concatenate([past[6].reshape(nb, -1, 128), kw_new], axis=1)
        vw = jnp.concatenate([past[7].reshape(nb, -1, 128), vw_new], axis=1)
        n_win = past[6].shape[1]
        kw_base = cols.q_off - n_win
        win_out = (kw[:, -n_win:], vw[:, -n_win:])
        pad = -kw.shape[1] % 1024
        kw = jnp.pad(kw, ((0, 0), (0, pad), (0, 0)))
        vw = jnp.pad(vw, ((0, 0), (0, pad), (0, 0)))
    else:
        kw, vw, kw_base = kw_new, vw_new, 0
        n_win = min(WINDOW, cols.t)
        win_out = (kw[:, -n_win:], vw[:, -n_win:])

    sl = _alibi(N_A_HEADS + N_B_HEADS)
    ak, amean = _keys(cols, new[0], pools[0], page_table, KV_TILE, 'k')
    avT = _values(cols, new[1], pools[1], page_table, KV_TILE)
    q, q_row, qf, meta = _queries(cols, hT, R['a_q'], N_A_HEADS, sl[0::2])
    o_a = _moba(cols, q, q_row, qf, meta, amean, ak, avT)

    pe2k, w1k, w2k = cmpw[0]
    pe2v, w1v, w2v = cmpw[1]
    n_cmp = max(cols.q_off, cols.t) // CMP_STRIDE
    kc_raw = _raw_rows(cols, new[2], pools[2], page_table)
    vc_raw = _raw_rows(cols, new[3], pools[3], page_table)
    kc = _compress(kc_raw, pe2k, w1k, w2k, n_cmp).astype(BF16)
    vcT = _compress(vc_raw, pe2v, w1v, w2v, n_cmp).transpose(0, 2, 1).astype(BF16)
    ks = _keys(cols, new[4], pools[4], page_table, SEL_TILE, 'ks')
    vsT = _values(cols, new[5], pools[5], page_table, SEL_TILE)
    kwt = _kprep(kw, KV_TILE, 'ks')
    q, q_row, _, meta = _queries(cols, hT, R['b_q'], N_B_HEADS, sl[1::2],
                               extra=_gate_cols(cols, hT[R['b_g']:R['b_g'] + 18]))
    o_b = _nsa(cols, q, q_row, meta, kc, vcT, ks, vsT, kwt, _vT_tiles(vw, KV_TILE), kw_base)

    o_m = _mem_branch(cols, hT, R['m_q'], mk, mvT)
    parts = [_heads_out(cols, o_a, N_A_HEADS), _heads_out(cols, o_b, N_B_HEADS), o_m]
    yT = _outln(parts, hT, R['z'], xT, woT, g, b, alpha, min(512, xT.shape[1]))
    state = [n.reshape(nb, -1, N_KV, HEAD_DIM) for n in new]
    state += [w.reshape(nb, -1, N_KV, HEAD_DIM) for w in win_out]
    return yT, state


def _odd_layer(cols, xT, mk, mvT, past, page_table, wT, woT, g, b, alpha):
    R = ODD_ROWS
    nb = cols.nb
    hT = _mm(wT, xT, 768, min(1024, xT.shape[1]))
    new = [cols.tokens(hT[R['c_k']:R['c_k'] + 128]), cols.tokens(hT[R['c_v']:R['c_v'] + 128]),
           cols.tokens(hT[R['c_ki']:R['c_ki'] + IDX_DIM])]
    pools = past if cols.sample else [None] * 3
    topk = min(DSA_TOPK, (cols.q_off + cols.t) // 4)
    ck = _keys(cols, new[0], pools[0], page_table, KV_TILE, 'ks')
    cvT = _values(cols, new[1], pools[1], page_table, KV_TILE)
    if cols.sample:
        ki = _gather(pools[2], page_table, new[2], 'i', KV_TILE)
    else:
        ki = new[2].reshape(nb, -1, KV_TILE, IDX_DIM).astype(BF16)
    q, q_row, _, meta = _queries(cols, hT, R['c_q'], N_C_HEADS, _alibi(N_C_HEADS))
    if cols.sample:
        qi = cols.per_token(hT[R['c_qi']:R['c_qi'] + 512], N_IDX_HEADS)
        wi = jnp.pad(cols.per_token(hT[R['c_wi']:R['c_wi'] + 8], N_IDX_HEADS), ((0, 7), (0, 0)))
        o_c = _dsa(cols, qi, 0, wi, 0, ki, q, q_row, meta, ck, cvT, topk)
    else:
        o_c = _dsa(cols, hT, R['c_qi'], hT, R['c_wi'], ki, q, q_row, meta, ck, cvT, topk)
    o_m = _mem_branch(cols, hT, R['m_q'], mk, mvT)
    parts = [_heads_out(cols, o_c, N_C_HEADS), o_m]
    yT = _outln(parts, hT, R['z'], xT, woT, g, b, alpha, min(512, xT.shape[1]))
    state = [new[0].reshape(nb, -1, N_KV, HEAD_DIM), new[1].reshape(nb, -1, N_KV, HEAD_DIM), new[2]]
    return yT, state


def kernel(x_prompt, x_sample, cache_a_k, cache_a_v, cache_b_cmp_k, cache_b_cmp_v, cache_b_sel_k, cache_b_sel_v,
           state_b_win_k, state_b_win_v, cache_c_k, cache_c_v, cache_c_idx_k, cache_mem_k, cache_mem_v,
           page_table, mem_prompt, w_in_even, w_out_even, w_in_odd, w_out_odd, w_mem_kv, ln_g, ln_b,
           cmp_pe, cmp_w1, cmp_w2):
    bp, seq, d = x_prompt.shape
    bs = x_sample.shape[0]
    depth = w_mem_kv.shape[0]
    n_mem = mem_prompt.shape[1]
    past_len = page_table.shape[1] * cache_a_k.shape[2]
    alpha = (2.0 * depth) ** 0.25
    assert x_sample.shape[1] == 1 and bs <= 128 and seq % 1024 == 0 and past_len % 1024 == 0

    gp = _Cols(bp, seq, 0)
    gs = _Cols(bs, 1, past_len)
    xp = x_prompt.reshape(bp * seq, d).T
    xs = jnp.pad(x_sample.reshape(bs, d).T, ((0, 0), (0, 128 - bs)))
    n_pool = cache_a_k.shape[1]

    def pages(c):
        return jnp.moveaxis(c.reshape(c.shape[:3] + (-1,)), 2, 3).reshape(c.shape[0] * n_pool, -1, c.shape[2])

    even_paged = tuple(pages(c) for c in (cache_a_k, cache_a_v, cache_b_cmp_k, cache_b_cmp_v, cache_b_sel_k,
                                          cache_b_sel_v))
    odd_paged = tuple(pages(c) for c in (cache_c_k, cache_c_v, cache_c_idx_k))
    ev_p, ev_s, od_p, od_s, mk_list, mv_list = [], [], [], [], [], []
    for layer in range(depth):
        j = layer // 2
        mem_kv = _mm(mem_prompt.reshape(bp * n_mem, d), w_mem_kv[layer], bp * n_mem, 2 * N_MEM_HEADS * HEAD_DIM)
        mem_kv = mem_kv.reshape(bp, n_mem, 2, N_MEM_HEADS, HEAD_DIM)
        mk_p, mv_p = mem_kv[:, :, 0], mem_kv[:, :, 1]
        mk_list.append(mk_p)
        mv_list.append(mv_p)
        mem_p = (mk_p.transpose(0, 2, 1, 3), mv_p.transpose(0, 2, 3, 1))
        mem_s = (cache_mem_k[layer].transpose(0, 2, 1, 3), cache_mem_v[layer].transpose(0, 2, 3, 1))
        g = ln_g[layer].reshape(d, 1)
        b = ln_b[layer].reshape(d, 1)
        if layer % 2 == 0:
            wT = _even_wT(w_in_even[j])
            woT = w_out_even[j].T
            cmpw = [_cmp_weights(cmp_pe[j, c], cmp_w1[j, c], cmp_w2[j, c]) for c in range(2)]
            past = even_paged + (state_b_win_k[j], state_b_win_v[j])
            xp, st_p = _even_layer(gp, xp, *mem_p, None, None, wT.astype(BF16), woT.astype(BF16), g, b, cmpw, alpha)
            xs, st_s = _even_layer(gs, xs, *mem_s, past, page_table + j * n_pool, wT, woT, g, b, cmpw, alpha)
            ev_p.append(st_p)
            ev_s.append(st_s)
        else:
            wT = _odd_wT(w_in_odd[j])
            woT = w_out_odd[j].T
            past = odd_paged
            xp, st_p = _odd_layer(gp, xp, *mem_p, None, None, wT.astype(BF16), woT.astype(BF16), g, b, alpha)
            xs, st_s = _odd_layer(gs, xs, *mem_s, past, page_table + j * n_pool, wT, woT, g, b, alpha)
            od_p.append(st_p)
            od_s.append(st_s)
    y_p = xp.T.reshape(bp, seq, d)
    y_s = xs[:, :bs].T.reshape(bs, 1, d)
    ev_p = [jnp.stack(l) for l in zip(*ev_p)]
    ev_s = [jnp.stack(l) for l in zip(*ev_s)]
    od_p = [jnp.stack(l) for l in zip(*od_p)]
    od_s = [jnp.stack(l) for l in zip(*od_s)]
    out = [y_p, y_s]
    for p, s in zip(ev_p, ev_s):
        out += [p, s]
    for p, s in zip(od_p, od_s):
        out += [p, s]
    out += [jnp.stack(mk_list), jnp.stack(mv_list)]
    return tuple(out)
```

```python
import functools

import numpy as np
import jax
import jax.numpy as jnp
from jax import lax
from jax.experimental import pallas as pl
from jax.experimental.pallas import tpu as pltpu

F32 = jnp.float32
BF16 = jnp.bfloat16
I32 = jnp.int32

HEAD_DIM = 64
N_MEM_HEADS = 4
N_A_HEADS = 6
N_B_HEADS = 6
N_C_HEADS = 12
N_KV = 2
N_IDX_HEADS = 8
IDX_DIM = 64
MOBA_BLOCK = 256
MOBA_TOPK = 3
CMP_STRIDE = 16
CMP_LEN = 32
CMP_HIDDEN = 128
SEL_BLOCK = 64
SEL_TOPK = 15
WINDOW = 512
DSA_TOPK = 256
ATTN_SCALE = HEAD_DIM ** -0.5
IDX_SCALE = IDX_DIM ** -0.5
LN_EPS = 1e-5
LOG2E = 1.4426950408889634
NEG = -1e30
SENTINEL = -3e38
INT_MIN = -(2 ** 31)

KV_TILE = 256
SEL_TILE = 512
KD = 256
FEAT_POS = 128
FEAT_SEL = 144
COUNT_GROUP = 4
SAMPLE_COLS = 128
META_ROWS = 16
META_PIECES = 8
PAGES_PER_STEP = 8
GATHER_PAGES = 16
VMEM_LIMIT = 56 * 1024 * 1024

EVEN_ROWS = dict(a_q=0, b_q=384, m_q=768, z=1024, a_k=2048, a_v=2176, b_kc=2304, b_vc=2432,
                 b_ks=2560, b_vs=2688, b_kw=2816, b_vw=2944, b_g=3072)
EVEN_TOTAL = 3200
ODD_ROWS = dict(c_q=0, m_q=768, z=1024, c_qi=2048, c_k=2560, c_v=2688, c_ki=2816, c_wi=2880)
ODD_TOTAL = 3072


def _cparams(sem):
    return pltpu.CompilerParams(dimension_semantics=sem, vmem_limit_bytes=VMEM_LIMIT)


def _alibi(n):
    return 2.0 ** (-8.0 * np.arange(1, n + 1) / n)


def _mm_kernel(a_ref, b_ref, o_ref, *, precise):
    if precise:
        o_ref[...] = jnp.dot(a_ref[...], b_ref[...], precision=lax.Precision.HIGHEST, preferred_element_type=F32)
    else:
        o_ref[...] = jnp.dot(a_ref[...].astype(BF16), b_ref[...].astype(BF16), preferred_element_type=F32)


def _mm(a, b, bm, bn):
    m, k = a.shape
    n = b.shape[1]
    return pl.pallas_call(
        functools.partial(_mm_kernel, precise=a.dtype == F32 and b.dtype == F32 and n <= 128), grid=(n // bn, m // bm),
        in_specs=[pl.BlockSpec((bm, k), lambda j, i: (i, 0)), pl.BlockSpec((k, bn), lambda j, i: (0, j))],
        out_specs=pl.BlockSpec((bm, bn), lambda j, i: (i, j)),
        out_shape=jax.ShapeDtypeStruct((m, n), F32),
        compiler_params=_cparams(("parallel", "parallel")), name="mm")(a, b)


def _outln_kernel(*refs, n_parts, alpha):
    o_refs = refs[:n_parts]
    z_ref, x_ref, w_ref, g_ref, b_ref, y_ref = refs[n_parts:]
    o = jnp.concatenate([r[...] for r in o_refs], axis=0)
    z = z_ref[...]
    mixed = o * (z / (1.0 + jnp.exp(-z)))
    if w_ref.dtype == F32:
        y = jnp.dot(w_ref[...], mixed, precision=lax.Precision.HIGHEST, preferred_element_type=F32)
    else:
        y = jnp.dot(w_ref[...], mixed.astype(BF16), preferred_element_type=F32)
    y = alpha * x_ref[...] + y
    mu = jnp.mean(y, axis=0, keepdims=True)
    d = y - mu
    var = jnp.mean(d * d, axis=0, keepdims=True)
    y_ref[...] = d * lax.rsqrt(var + LN_EPS) * g_ref[...] + b_ref[...]


def _outln(o_parts, hT, z_row, xT, woT, g, b, alpha, bn):
    d, n = xT.shape
    zb = z_row // d
    in_specs = [pl.BlockSpec((p.shape[0], bn), lambda j: (0, j)) for p in o_parts]
    in_specs += [pl.BlockSpec((d, bn), lambda j: (zb, j)),
                 pl.BlockSpec((d, bn), lambda j: (0, j)),
                 pl.BlockSpec((d, d), lambda j: (0, 0)),
                 pl.BlockSpec((d, 1), lambda j: (0, 0)),
                 pl.BlockSpec((d, 1), lambda j: (0, 0))]
    return pl.pallas_call(
        functools.partial(_outln_kernel, n_parts=len(o_parts), alpha=alpha), grid=(n // bn,),
        in_specs=in_specs, out_specs=pl.BlockSpec((d, bn), lambda j: (0, j)),
        out_shape=jax.ShapeDtypeStruct((d, n), F32),
        compiler_params=_cparams(("parallel",)), name="outln")(*o_parts, hT, xT, woT, g, b)


def _key_features(tile):
    r = lax.broadcasted_iota(I32, (tile, 128), 0)
    c = lax.broadcasted_iota(I32, (tile, 128), 1)
    sel_c = c - (FEAT_SEL - FEAT_POS)
    feat = jnp.where(c < 3, r >> 4, jnp.where(c < 6, r & 15, jnp.where(sel_c == (r >> 6), 1, 0)))
    return feat.astype(F32).astype(BF16)


def _emit_tiles(x, outs, mode, tile, t0=0):
    nt = x.shape[0] // tile
    if mode == 'raw':
        outs[0][0, t0 * tile:(t0 + nt) * tile, :] = x
        return
    feat = _key_features(tile) if mode in ('k', 'ks') else None
    for t in range(nt):
        xt = x[t * tile:(t + 1) * tile]
        if mode in ('k', 'ks'):
            outs[0][0, t0 + t] = jnp.concatenate([xt.astype(BF16), feat], axis=1)
        elif mode == 'v':
            outs[0][0, t0 + t] = jnp.transpose(xt).astype(BF16)
        else:
            outs[0][0, t0 + t] = xt.astype(BF16)
    if mode == 'k':
        nblk = x.shape[0] // MOBA_BLOCK
        outs[1][0, 0, t0:t0 + nblk, :] = jnp.mean(x.reshape(nblk, MOBA_BLOCK, x.shape[1]), axis=1)


def _zero_tiles(outs, mode, tile, t0, t1):
    for t in range(t0, t1):
        outs[0][0, t] = jnp.zeros(outs[0].shape[2:], outs[0].dtype)
    if mode == 'k':
        outs[1][0, 0, t0:t1, :] = jnp.zeros((t1 - t0, outs[1].shape[3]), F32)


def _tile_outputs(mode, nb, n_steps, rows, tile, c):
    nt = rows // tile
    if mode == 'raw':
        return ([pl.BlockSpec((1, rows, c), lambda b, j, *_: (b, j, 0))],
                [jax.ShapeDtypeStruct((nb, n_steps * rows, c), F32)])
    if mode in ('k', 'ks'):
        shape, blk = (nb, n_steps * nt, tile, KD), (1, nt, tile, KD)
    elif mode == 'v':
        shape, blk = (nb, n_steps * nt, c, tile), (1, nt, c, tile)
    else:
        shape, blk = (nb, n_steps * nt, tile, c), (1, nt, tile, c)
    specs = [pl.BlockSpec(blk, lambda b, j, *_: (b, j, 0, 0))]
    shapes = [jax.ShapeDtypeStruct(shape, BF16)]
    if mode == 'k':
        nblk = rows // MOBA_BLOCK
        specs.append(pl.BlockSpec((1, 1, nblk, c), lambda b, j, *_: (b, j, 0, 0)))
        shapes.append(jax.ShapeDtypeStruct((nb, n_steps, nblk, c), F32))
    return specs, shapes


def _gather_kernel(pt_ref, *refs, npg, mode, tile, last, has_tail):
    pool_refs = refs[:npg]
    outs = refs[npg + (1 if has_tail else 0):]
    j = pl.program_id(1)

    def token_major(p):
        c = p.shape[0]
        if c == p.shape[1]:
            return jnp.transpose(p)
        return jnp.transpose(jnp.concatenate([p, jnp.zeros((p.shape[1] - c, p.shape[1]), p.dtype)], axis=0))[:, :c]

    def pages():
        if mode == 'v':
            xT = jnp.concatenate([r[0] for r in pool_refs], axis=1)
            for t in range(xT.shape[1] // tile):
                outs[0][0, t] = xT[:, t * tile:(t + 1) * tile].astype(BF16)
        else:
            _emit_tiles(jnp.concatenate([token_major(r[0]) for r in pool_refs], axis=0), outs, mode, tile)

    if not has_tail:
        pages()
        return
    pl.when(j < last)(pages)

    @pl.when(j == last)
    def _():
        _emit_tiles(refs[npg][0], outs, mode, tile)
        if mode != 'raw':
            _zero_tiles(outs, mode, tile, 1, outs[0].shape[1])


def _gather(pool, page_table, new_tok, mode, tile):
    nb, n_pages = page_table.shape
    n_pool, c, page_tokens = pool.shape
    npg = GATHER_PAGES if n_pages % GATHER_PAGES == 0 else PAGES_PER_STEP
    last = n_pages // npg
    has_tail = new_tok is not None
    rows = npg * page_tokens
    assert rows % tile == 0 and (has_tail or mode == 'raw')

    def pool_spec(k):
        return pl.BlockSpec((1, c, page_tokens),
                            lambda b, j, pt: (pt[b, jnp.minimum(j * npg + k, n_pages - 1)], 0, 0))

    in_specs = [pool_spec(k) for k in range(npg)]
    args = [pool] * npg
    if has_tail:
        tail = jnp.zeros((nb, tile, c), F32).at[:, 0:1, :].set(new_tok)
        in_specs.append(pl.BlockSpec((1, tile, c), lambda b, j, pt: (b, 0, 0)))
        args.append(tail)
    n_steps = last + (1 if has_tail else 0)
    out_specs, out_shape = _tile_outputs(mode, nb, n_steps, rows, tile, c)
    grid_spec = pltpu.PrefetchScalarGridSpec(num_scalar_prefetch=1, grid=(nb, n_steps), in_specs=in_specs,
                                             out_specs=out_specs)
    out = pl.pallas_call(
        functools.partial(_gather_kernel, npg=npg, mode=mode, tile=tile, last=last, has_tail=has_tail),
        grid_spec=grid_spec, out_shape=out_shape,
        compiler_params=_cparams(("parallel", "arbitrary")), name="gather_" + mode)(page_table, *args)
    if mode == 'k':
        return out[0], out[1].reshape(nb, -1, c)
    return out[0]


def _kprep_kernel(x_ref, *outs, mode, tile):
    _emit_tiles(x_ref[0], outs, mode, tile)


def _kprep(x, tile, mode='k'):
    nb, L, c = x.shape
    rows = 1024
    out_specs, out_shape = _tile_outputs(mode, nb, L // rows, rows, tile, c)
    out = pl.pallas_call(
        functools.partial(_kprep_kernel, mode=mode, tile=tile), grid=(nb, L // rows),
        in_specs=[pl.BlockSpec((1, rows, c), lambda b, j: (b, j, 0))],
        out_specs=out_specs, out_shape=out_shape,
        compiler_params=_cparams(("parallel", "parallel")), name="kprep")(x)
    if mode == 'k':
        return out[0], out[1].reshape(nb, -1, c)
    return out[0]


def _vT_tiles(v, tile):
    nb, L, c = v.shape
    return v.reshape(nb, L // tile, tile, c).transpose(0, 1, 3, 2).astype(BF16)


class _Cols:
    def __init__(self, nb, t, q_off):
        self.nb, self.t, self.q_off = nb, t, q_off
        self.sample = q_off > 0
        self.tq = 1 if self.sample else KV_TILE
        self.nq = 1 if self.sample else t // KV_TILE
        self.span = 1 if self.sample else KV_TILE

    def head_group(self, heads):
        return np.arange(heads) // (heads // N_KV)

    def per_head(self, vals, heads):
        vals = jnp.asarray(vals, F32)
        k = vals.shape[0]
        if self.sample:
            return jnp.tile(jnp.pad(vals, ((0, 0), (0, SAMPLE_COLS - heads))), (1, self.nb))
        hps = heads // N_KV
        shape = (k, self.nb, self.nq, N_KV, hps, self.tq)
        return jnp.broadcast_to(vals.reshape(k, 1, 1, N_KV, hps, 1), shape).reshape(k, -1)

    def per_token(self, rows, heads):
        nb, nq, tq = self.nb, self.nq, self.tq
        k = rows.shape[0] // heads
        if self.sample:
            x = rows[:, :nb].reshape(heads, k, nb).transpose(1, 2, 0)
            return jnp.pad(x, ((0, 0), (0, 0), (0, SAMPLE_COLS - heads))).reshape(k, nb * SAMPLE_COLS)
        x = rows.reshape(N_KV, heads // N_KV, k, nb, nq, tq).transpose(2, 3, 4, 0, 1, 5)
        return x.reshape(k, -1)

    def qpos_row(self, heads):
        if self.sample:
            return jnp.full((1, self.nb * SAMPLE_COLS), self.q_off, F32)
        pos = (jnp.arange(self.nq, dtype=F32)[:, None] * self.tq + jnp.arange(self.tq, dtype=F32)[None, :])
        shape = (1, self.nb, self.nq, N_KV, heads // N_KV, self.tq)
        return jnp.broadcast_to(pos.reshape(1, 1, self.nq, 1, 1, self.tq), shape).reshape(1, -1)

    def meta(self, heads, slopes, extra=()):
        s2 = np.asarray(slopes, np.float64) * LOG2E
        hi = s2.astype(jnp.bfloat16).astype(np.float64)
        mid = (s2 - hi).astype(jnp.bfloat16).astype(np.float64)
        lo = (s2 - hi - mid).astype(jnp.bfloat16).astype(np.float64)
        rows = [self.per_head((hi + mid + lo)[None, :], heads), self.qpos_row(heads)] + list(extra)
        n = rows[0].shape[1]
        rows.append(jnp.zeros((META_PIECES - sum(m.shape[0] for m in rows), n), F32))
        rows.append(self.per_head(np.stack([16 * hi, 16 * mid, 16 * lo, hi, mid, lo]), heads))
        rows.append(jnp.zeros((2, n), F32))
        return jnp.concatenate(rows, axis=0)

    def columns(self, rows, heads, meta):
        x = self.per_token(rows, heads)
        grp = self.head_group(heads)
        gsel = self.per_head(np.stack([grp == 0, grp == 1]).astype(np.float32), heads)
        qf = jnp.concatenate([x * gsel[0:1], x * gsel[1:2]], axis=0)
        qa = jnp.concatenate([(qf * (ATTN_SCALE * LOG2E)).astype(BF16), meta[META_PIECES:].astype(BF16),
                              jnp.zeros((8, x.shape[1]), BF16),
                              jnp.zeros((KD - FEAT_POS - 16, x.shape[1]), BF16)], axis=0)
        return qa, qf

    def collapse(self, o, heads):
        nb, nq, tq = self.nb, self.nq, self.tq
        if self.sample:
            o = o.reshape(N_KV, HEAD_DIM, nb, SAMPLE_COLS)
            grp = self.head_group(heads)
            x = jnp.stack([o[grp[h], :, :, h] for h in range(heads)])
            return jnp.pad(x.reshape(heads * HEAD_DIM, nb), ((0, 0), (0, 128 - nb)))
        o = o.reshape(HEAD_DIM, nb, nq, N_KV, heads // N_KV, tq).transpose(3, 4, 0, 1, 2, 5)
        return o.reshape(heads * HEAD_DIM, nb * nq * tq)

    def tokens(self, rows):
        n = rows.shape[0]
        if self.sample:
            return rows[:, :self.nb].T.reshape(self.nb, 1, n)
        return rows.reshape(n, self.nb, self.t).transpose(1, 2, 0)

    def tile_cols(self, rows):
        if not self.sample:
            return rows
        n = rows.shape[0]
        return jnp.broadcast_to(rows[:, :self.nb, None], (n, self.nb, SAMPLE_COLS)).reshape(n, self.nb * SAMPLE_COLS)

    def from_tile_cols(self, o):
        if not self.sample:
            return o
        n = o.shape[0]
        return jnp.pad(o.reshape(n, self.nb, SAMPLE_COLS)[:, :, 0], ((0, 0), (0, 128 - self.nb)))


def _softmax_step(s, c, ok, vT, m, l, acc_ref):
    smax = jnp.max(s, axis=0, keepdims=True)
    live = smax > 0.5 * NEG
    if ok is not None:
        live = live & _row_flag(ok, s.shape[1])
    m_new = jnp.where(live, jnp.maximum(m, smax + c), m)
    shift = jnp.where(live, m_new - c, -NEG)
    p = jnp.exp2(s - shift)
    alpha = jnp.exp2(m - m_new)
    l = alpha * l + jnp.sum(p, axis=0, keepdims=True)
    acc_ref[...] = alpha * acc_ref[...] + _pv(vT, p, acc_ref.shape[0])
    return m_new, l


def _pv(vT, p, rows):
    pb = p.astype(BF16)
    if vT.shape[0] == rows:
        return jnp.dot(vT, pb, preferred_element_type=F32)
    h = pb.shape[1] // N_KV
    return jnp.concatenate([jnp.dot(vT[g * HEAD_DIM:(g + 1) * HEAD_DIM], pb[:, g * h:(g + 1) * h],
                                    preferred_element_type=F32) for g in range(N_KV)], axis=1)


def _row_flag(ok, n):
    if ok.ndim == 0:
        return (jnp.zeros((1, n), I32) + ok.astype(I32)) > 0
    return ok


def _group(mixed):
    return 4 if mixed else 2


def _softmax_init(n):
    return jnp.full((1, n), NEG, F32), jnp.zeros((1, n), F32)


def _tile_groups(n_tiles, step, carry, group):
    def body(u, carry):
        carry = step(group * u, 0, None, carry)
        for j in range(1, group):
            t = group * u + j
            carry = step(jnp.minimum(t, n_tiles - 1), j, t < n_tiles, carry)
        return carry
    return lax.fori_loop(0, (n_tiles + group - 1) // group, body, carry)


def _dense_attend(k_ref, n_k, vT_ref, n_v, qa, slope, qpos, blk, pos0=0, mask_fn=None, bias=None):
    n = qa.shape[1]
    tile = k_ref.shape[2]
    rows = n_k * tile
    s = jnp.dot(k_ref[0, 0:n_k].reshape(rows, KD), qa, preferred_element_type=F32)
    if bias is not None:
        s = s + bias
    nblk = rows // blk
    s = s.reshape(nblk, blk, n)
    b3 = lax.broadcasted_iota(I32, (nblk, 1, 1), 0)
    base = pos0 + (b3 * blk) // tile * tile
    s = s + slope * (base - qpos).astype(F32)
    if mask_fn is not None:
        kpos = pos0 + b3 * blk + lax.broadcasted_iota(I32, (nblk, blk, n), 1)
        s = jnp.where(mask_fn(b3, kpos), s, NEG)
    s = s.reshape(rows, n)
    m = jnp.max(s, axis=0, keepdims=True)
    p = jnp.where(s > 0.5 * NEG, jnp.exp2(s - m), 0.0)
    l = jnp.sum(p, axis=0, keepdims=True)
    vT = jnp.concatenate([vT_ref[0, t] for t in range(n_v)], axis=1)
    return jnp.dot(vT, p.astype(BF16), preferred_element_type=F32) / jnp.maximum(l, 1e-30)


def _step_queries(q_ref, meta_ref, qa_s):
    tq = q_ref.shape[1]
    n = qa_s.shape[1]
    qa_s[...] = jnp.zeros(qa_s.shape, BF16)
    pieces = jnp.concatenate([meta_ref[META_PIECES:META_ROWS, :], jnp.zeros((8, n), F32)], axis=0)
    qa_s[FEAT_POS:FEAT_POS + 16, :] = pieces.astype(BF16)
    heads = q_ref.shape[0] // HEAD_DIM
    qf = []
    for r in range(heads):
        q = q_ref[r * HEAD_DIM:(r + 1) * HEAD_DIM, :]
        z = jnp.zeros_like(q)
        blk = jnp.concatenate([q, z] if r < heads // N_KV else [z, q], axis=0)
        qa_s[0:N_KV * HEAD_DIM, r * tq:(r + 1) * tq] = (blk * (ATTN_SCALE * LOG2E)).astype(BF16)
        qf.append(blk)
    return qa_s[...], jnp.concatenate(qf, axis=1)


def _store_heads(o_ref, o, mixed):
    if mixed:
        o_ref[...] = o
        return
    tq = o_ref.shape[1]
    for r in range(o_ref.shape[0] // HEAD_DIM):
        o_ref[r * HEAD_DIM:(r + 1) * HEAD_DIM, :] = o[:, r * tq:(r + 1) * tq]


def _top_rows(vals, rowi, k):
    sel = jnp.zeros(vals.shape, F32)
    for _ in range(k):
        m = jnp.max(vals, axis=0, keepdims=True)
        first = jnp.min(jnp.where(vals == m, rowi, 1 << 20), axis=0, keepdims=True)
        pick = (rowi == first) & (m > SENTINEL)
        sel = jnp.where(pick, 1.0, sel)
        vals = jnp.where(pick, SENTINEL, vals)
    return sel


def _moba_kernel(qa_ref, qf_ref, meta_ref, kmean_ref, k_ref, vT_ref, o_ref, sel_ref, acc_ref, qa_s,
                 *, span, q_off, mixed):
    i = pl.program_id(1)
    n = meta_ref.shape[1]
    qa, qf = (qa_ref[...], qf_ref[...]) if mixed else _step_queries(qa_ref, meta_ref, qa_s)
    own = (q_off + i * span) // MOBA_BLOCK
    slope = meta_ref[0:1, :]
    qpos = meta_ref[1:2, :].astype(I32)
    nrow = kmean_ref.shape[1]
    gate = jnp.dot(kmean_ref[0], qf, precision=lax.Precision.HIGHEST, preferred_element_type=F32)
    rowi = lax.broadcasted_iota(I32, (nrow, n), 0)
    sel_ref[...] = _top_rows(jnp.where(rowi < own, gate, SENTINEL), rowi, MOBA_TOPK)
    if mixed:
        own_s = q_off // MOBA_BLOCK
        sel3 = sel_ref[0:own_s + 1, :][:, None, :] > 0.5
        o_ref[...] = _dense_attend(k_ref, own_s + 1, vT_ref, own_s + 1, qa, slope, qpos, KV_TILE,
                                   mask_fn=lambda b3, kpos: ((b3 < own_s) & sel3) | ((b3 >= own_s) & (kpos <= qpos)))
        return
    acc_ref[...] = jnp.zeros(acc_ref.shape, F32)

    def step(t, slot, ok, carry):
        s = jnp.dot(k_ref[0, t], qa, preferred_element_type=F32)
        c = slope * (t * KV_TILE - qpos).astype(F32)
        sel = sel_ref[pl.ds(t, 1), :] > 0.5
        ok = sel if ok is None else sel & _row_flag(ok, n)
        return _softmax_step(s, c, ok, vT_ref[0, t], *carry, acc_ref)

    m, l = _tile_groups(own, step, _softmax_init(n), _group(mixed))
    s = jnp.dot(k_ref[0, own], qa, preferred_element_type=F32)
    krow = lax.broadcasted_iota(I32, (KV_TILE, n), 0)
    s = jnp.where(krow <= qpos - own * KV_TILE, s, NEG)
    c = slope * (own * KV_TILE - qpos).astype(F32)
    _, l = _softmax_step(s, c, None, vT_ref[0, own], m, l, acc_ref)
    _store_heads(o_ref, acc_ref[...] / jnp.maximum(l, 1e-30), mixed)


def _query_specs(cols, heads, q_row, n_cols):
    nq = cols.nq
    if cols.sample:
        col = lambda b, i: (0, b * nq + i)
        return (pl.BlockSpec((KD, SAMPLE_COLS), col), pl.BlockSpec((N_KV * HEAD_DIM, SAMPLE_COLS), col),
                jax.ShapeDtypeStruct((N_KV * HEAD_DIM, n_cols), F32), N_KV * HEAD_DIM)
    rows = heads * HEAD_DIM
    blk0 = q_row // rows
    return (pl.BlockSpec((rows, cols.tq), lambda b, i: (blk0, b * nq + i)),
            pl.BlockSpec((rows, cols.tq), lambda b, i: (0, b * nq + i)),
            jax.ShapeDtypeStruct((heads * HEAD_DIM, cols.nb * nq * cols.tq), F32), HEAD_DIM)


def _moba(cols, q, q_row, qf, meta, kmean, k, vT):
    nb, nq = cols.nb, cols.nq
    n = SAMPLE_COLS if cols.sample else N_A_HEADS * cols.tq
    nT, nblk = k.shape[1], kmean.shape[1]
    q_spec, o_spec, o_shape, vrows = _query_specs(cols, N_A_HEADS, q_row, meta.shape[1])
    col = lambda b, i: (0, b * nq + i)
    return pl.pallas_call(
        functools.partial(_moba_kernel, span=cols.span, q_off=cols.q_off, mixed=cols.sample), grid=(nb, nq),
        in_specs=[q_spec,
                  pl.BlockSpec((qf.shape[0], n), col),
                  pl.BlockSpec((META_ROWS, n), col),
                  pl.BlockSpec((1, nblk, N_KV * HEAD_DIM), lambda b, i: (b, 0, 0)),
                  pl.BlockSpec((1, nT, KV_TILE, KD), lambda b, i: (b, 0, 0, 0)),
                  pl.BlockSpec((1, nT, N_KV * HEAD_DIM, KV_TILE), lambda b, i: (b, 0, 0, 0))],
        out_specs=o_spec, out_shape=o_shape,
        scratch_shapes=[pltpu.VMEM((nblk, n), F32), pltpu.VMEM((vrows, n), F32), pltpu.VMEM((KD, n), BF16)],
        compiler_params=_cparams(("parallel", "parallel")), name="moba")(q, qf, meta, kmean, k, vT)


def _cmp_kernel(r_ref, pe_ref, w1_ref, w2_ref, o_ref):
    n = o_ref.shape[1]
    hid = 2 * CMP_HIDDEN
    c = r_ref.shape[2]
    u = jnp.zeros((n, hid), F32)
    lo = jnp.zeros((n, hid), F32)
    for p in range(CMP_STRIDE):
        rp = r_ref[0, pl.ds(p, n, stride=CMP_STRIDE), :]
        cols = slice(p * c, (p + 1) * c)
        u = u + jnp.dot((rp + pe_ref[0:1, cols]).astype(BF16), w1_ref[cols, 0:hid], preferred_element_type=F32)
        lo = lo + jnp.dot((rp + pe_ref[1:2, cols]).astype(BF16), w1_ref[cols, hid:2 * hid],
                          preferred_element_type=F32)
    pre = u + pltpu.roll(lo, n - 1, 0)
    h = pre / (1.0 + jnp.exp(-pre))
    o_ref[0] = jnp.dot(h.astype(BF16), w2_ref[...], preferred_element_type=F32)


def _compress(raw, pe2, w1big, w2big, n_rows):
    nb, _, c = raw.shape
    return pl.pallas_call(
        _cmp_kernel, grid=(nb,),
        in_specs=[pl.BlockSpec((1, n_rows * CMP_STRIDE, c), lambda b: (b, 0, 0)),
                  pl.BlockSpec((2, 2048), lambda b: (0, 0)),
                  pl.BlockSpec((2048, 4 * CMP_HIDDEN), lambda b: (0, 0)),
                  pl.BlockSpec((2 * CMP_HIDDEN, 128), lambda b: (0, 0))],
        out_specs=pl.BlockSpec((1, n_rows, 128), lambda b: (b, 0, 0)),
        out_shape=jax.ShapeDtypeStruct((nb, n_rows, 128), F32),
        compiler_params=_cparams(("parallel",)), name="compress")(raw, pe2, w1big, w2big)


def _cmp_weights(pe, w1, w2):
    w1r = w1.reshape(2, CMP_STRIDE, HEAD_DIM, CMP_HIDDEN)
    eye = jnp.eye(N_KV, dtype=w1.dtype)
    big = jnp.einsum('hpdj,ge->hpdgej', w1r, eye)
    big = big.transpose(1, 3, 2, 0, 4, 5).reshape(CMP_STRIDE * N_KV * HEAD_DIM, 2 * N_KV * CMP_HIDDEN)
    w2big = jnp.einsum('jd,ge->gjed', w2, eye).reshape(N_KV * CMP_HIDDEN, N_KV * HEAD_DIM)
    pe2 = jnp.broadcast_to(pe.reshape(2, CMP_STRIDE, 1, HEAD_DIM), (2, CMP_STRIDE, N_KV, HEAD_DIM)).reshape(2, 2048)
    return pe2, big.astype(BF16), w2big.astype(BF16)


def _nsa_kernel(qa_ref, meta_ref, kc_ref, vcT_ref, taps_ref, grp_ref, ks_ref, vsT_ref, kw_ref, vwT_ref, o_ref,
                sel_ref, qs_ref, acc_ref, qa_s, *, span, q_off, kw_base, mixed):
    i = pl.program_id(1)
    n = meta_ref.shape[1]
    hps = N_B_HEADS // N_KV
    tq = n if mixed else n // N_B_HEADS
    nq = n if mixed else N_KV * tq
    t0 = q_off + i * span
    slope = meta_ref[0:1, :]
    qpos = meta_ref[1:2, :].astype(I32)
    qa = qa_ref[...] if mixed else _step_queries(qa_ref, meta_ref, qa_s)[0]
    n_cmp = kc_ref.shape[1]
    n_tab = sel_ref.shape[0]

    def per_head(x):
        if mixed:
            return x
        return jnp.concatenate([x[:, (r // hps) * tq:(r // hps + 1) * tq] for r in range(N_B_HEADS)], axis=1)

    cend = CMP_STRIDE * lax.broadcasted_iota(I32, (n_cmp, n), 0) + (CMP_LEN - 1)
    cmask = cend <= qpos
    s = jnp.dot(kc_ref[0], qa[0:N_KV * HEAD_DIM, :], preferred_element_type=F32) + slope * (cend - qpos).astype(F32)
    s = jnp.where(cmask, s, NEG)
    p = jnp.where(cmask, jnp.exp2(s - jnp.max(s, axis=0, keepdims=True)), 0.0)
    p = p * (1.0 / jnp.maximum(jnp.sum(p, axis=0, keepdims=True), 1e-30))
    o_c = _pv(vcT_ref[0], p, acc_ref.shape[0])

    if mixed:
        imp = jnp.dot(p, grp_ref[...], precision=lax.Precision.HIGHEST, preferred_element_type=F32)
    else:
        imp = jnp.concatenate([sum(p[:, (gg * hps + r) * tq:(gg * hps + r + 1) * tq] for r in range(hps))
                               for gg in range(N_KV)], axis=1)
    p_slc = jnp.dot(taps_ref[...], imp, precision=lax.Precision.HIGHEST, preferred_element_type=F32)
    p_slc = jnp.concatenate([p_slc, jnp.full((n_tab - p_slc.shape[0], nq), SENTINEL, F32)], axis=0)
    rowi = lax.broadcasted_iota(I32, (n_tab, nq), 0)
    own = (qpos if mixed else jnp.concatenate([qpos[:, 0:tq]] * N_KV, axis=1)) >> 6
    sel = _top_rows(jnp.where(rowi < own, p_slc, SENTINEL), rowi, SEL_TOPK)
    sel_ref[...] = jnp.where(rowi == own, 1.0, sel)

    gt = 1.0 / (1.0 + jnp.exp(-meta_ref[2:5, :]))
    if mixed:
        n_s = q_off // SEL_TILE + 1
        sel3 = sel_ref[0:n_s * (SEL_TILE // SEL_BLOCK), :][:, None, :] > 0.5
        o_s = _dense_attend(ks_ref, n_s, vsT_ref, n_s, qa, slope, qpos, SEL_BLOCK,
                            mask_fn=lambda b3, kpos: sel3 & (kpos <= qpos))
        n_w = min((q_off - kw_base) // KV_TILE + 1, kw_ref.shape[1])
        o_w = _dense_attend(kw_ref, n_w, vwT_ref, n_w, qa, slope, qpos, KV_TILE, pos0=kw_base,
                            mask_fn=lambda b3, kpos: (kpos <= qpos) & (kpos > qpos - WINDOW))
        o_ref[...] = gt[0:1, :] * o_c + gt[1:2, :] * o_s + gt[2:3, :] * o_w
        return

    per_tile = SEL_TILE // SEL_BLOCK
    last = (t0 + span - 1) // SEL_TILE
    for slot in range(qs_ref.shape[0]):
        qs_ref[slot] = qa
    acc_ref[...] = jnp.zeros(acc_ref.shape, F32)

    def sel_scores(t, slot):
        rows = sel_ref[pl.ds(pl.multiple_of(t * per_tile, per_tile), per_tile), :]
        bias = per_head(jnp.where(rows > 0.5, 0.0, NEG))
        bias = jnp.concatenate([bias, jnp.zeros((per_tile, n), F32)], axis=0)
        qs_ref[slot, FEAT_SEL:FEAT_SEL + 2 * per_tile, :] = bias.astype(BF16)
        s = jnp.dot(ks_ref[0, t], qs_ref[slot], preferred_element_type=F32)
        return s, slope * (t * SEL_TILE - qpos).astype(F32)

    def sel_step(t, slot, ok, carry):
        s, c = sel_scores(t, slot)
        return _softmax_step(s, c, ok, vsT_ref[0, t], *carry, acc_ref)

    m, l = _tile_groups(last, sel_step, _softmax_init(n), qs_ref.shape[0])
    s, c = sel_scores(last, 0)
    srow = lax.broadcasted_iota(I32, (SEL_TILE, n), 0)
    s = jnp.where(srow <= qpos - last * SEL_TILE, s, NEG)
    _, l = _softmax_step(s, c, None, vsT_ref[0, last], m, l, acc_ref)
    o_s = acc_ref[...] / jnp.maximum(l, 1e-30)

    lo = jnp.maximum(t0 - (WINDOW - 1) - kw_base, 0) // KV_TILE
    hi = jnp.minimum((t0 + span - 1 - kw_base) // KV_TILE + 1, kw_ref.shape[1])
    wrow = lax.broadcasted_iota(I32, (KV_TILE, n), 0)
    acc_ref[...] = jnp.zeros(acc_ref.shape, F32)

    def win_body(t, carry):
        s = jnp.dot(kw_ref[0, t], qa, preferred_element_type=F32)
        base = kw_base + t * KV_TILE
        dist = qpos - base - wrow
        s = jnp.where((dist >= 0) & (dist < WINDOW), s, NEG)
        return _softmax_step(s, slope * (base - qpos).astype(F32), None, vwT_ref[0, t],
                             *carry, acc_ref)

    _, l = lax.fori_loop(lo, hi, win_body, _softmax_init(n))
    o_w = acc_ref[...] / jnp.maximum(l, 1e-30)

    _store_heads(o_ref, gt[0:1, :] * o_c + gt[1:2, :] * o_s + gt[2:3, :] * o_w, mixed)


def _taps_matrix(n_sel, n_cmp):
    m = np.zeros((n_sel, n_cmp), np.float32)
    for j in range(n_sel):
        for off, w in ((-1, 1.0), (0, 2.0), (1, 2.0), (2, 2.0), (3, 1.0)):
            c = 4 * j + off
            if 0 <= c < n_cmp:
                m[j, c] = w
    return jnp.asarray(m)


def _group_matrix(cols):
    grp = cols.head_group(N_B_HEADS)
    m = np.zeros((SAMPLE_COLS, SAMPLE_COLS), np.float32)
    m[:N_B_HEADS, :N_B_HEADS] = grp[:, None] == grp[None, :]
    return jnp.asarray(m)


def _nsa(cols, q, q_row, meta, kc, vcT, ks, vsT, kw, vwT, kw_base):
    nb, nq = cols.nb, cols.nq
    n = SAMPLE_COLS if cols.sample else N_B_HEADS * cols.tq
    nTs, nTw = ks.shape[1], kw.shape[1]
    n_cmp = kc.shape[1]
    n_sel = n_cmp // 4
    n_tab = n_sel + SEL_TILE // SEL_BLOCK
    nsel_cols = n if cols.sample else N_KV * cols.tq
    q_spec, o_spec, o_shape, vrows = _query_specs(cols, N_B_HEADS, q_row, meta.shape[1])
    col = lambda b, i: (0, b * nq + i)
    seq3 = lambda b, i: (b, 0, 0)
    seq4 = lambda b, i: (b, 0, 0, 0)
    return pl.pallas_call(
        functools.partial(_nsa_kernel, span=cols.span, q_off=cols.q_off, kw_base=kw_base, mixed=cols.sample),
        grid=(nb, nq),
        in_specs=[q_spec,
                  pl.BlockSpec((META_ROWS, n), col),
                  pl.BlockSpec((1, n_cmp, N_KV * HEAD_DIM), seq3),
                  pl.BlockSpec((1, N_KV * HEAD_DIM, n_cmp), seq3),
                  pl.BlockSpec((n_sel, n_cmp), lambda b, i: (0, 0)),
                  pl.BlockSpec((SAMPLE_COLS, SAMPLE_COLS), lambda b, i: (0, 0)),
                  pl.BlockSpec((1, nTs, SEL_TILE, KD), seq4),
                  pl.BlockSpec((1, nTs, N_KV * HEAD_DIM, SEL_TILE), seq4),
                  pl.BlockSpec((1, nTw, KV_TILE, KD), seq4),
                  pl.BlockSpec((1, nTw, N_KV * HEAD_DIM, KV_TILE), seq4)],
        out_specs=o_spec, out_shape=o_shape,
        scratch_shapes=[pltpu.VMEM((n_tab, nsel_cols), F32), pltpu.VMEM((_group(cols.sample), KD, n), BF16),
                        pltpu.VMEM((vrows, n), F32), pltpu.VMEM((KD, n), BF16)],
        compiler_params=_cparams(("parallel", "parallel")), name="nsa")(
            q, meta, kc, vcT, _taps_matrix(n_sel, n_cmp), _group_matrix(cols), ks, vsT, kw, vwT)


def _dsa_kernel(qi_ref, wi_ref, ki_ref, qa_ref, meta_ref, k_ref, vT_ref, o_ref, key_ref, row_ref, acc_ref, qa_s,
                *, span, q_off, topk, mixed):
    i = pl.program_id(1)
    n = meta_ref.shape[1]
    nq = qi_ref.shape[1]
    rep = n // nq
    t0 = q_off + i * span
    n_kv = (t0 + span - 1) // KV_TILE + 1
    krow = lax.broadcasted_iota(I32, (KV_TILE, nq), 0)

    def search(count, shape):
        kf = float(topk)
        thr = jnp.where(count(lambda key, kpos: key >= 0) >= kf, 0, INT_MIN).astype(I32)

        def bit_body(j, thr):
            cand = thr | (1 << (30 - j))
            return jnp.where(count(lambda key, kpos: key >= cand) >= kf, cand, thr)
        thr = lax.fori_loop(0, 31, bit_body, thr)
        need = kf - count(lambda key, kpos: key > thr)

        def tie_search():
            def idx_body(j, jb):
                cand = jb | (1 << (14 - j))
                c = count(lambda key, kpos: (key == thr) & (kpos < cand))
                return jnp.where(c <= need, cand, jb)
            return lax.fori_loop(0, 15, idx_body, jnp.zeros(shape, I32))

        n_ge = count(lambda key, kpos: key >= thr)
        jb = lax.cond(jnp.max(n_ge) > kf, tie_search, lambda: jnp.full(shape, (1 << 15) - 1, I32))
        return thr, jb

    def select():
        qpos = t0 + (lax.broadcasted_iota(I32, (1, nq), 1) if span > 1 else jnp.zeros((1, nq), I32))
        w = wi_ref[...] * (IDX_SCALE * N_IDX_HEADS ** -0.5)
        if mixed:
            qi = qi_ref[...].astype(BF16)
        else:
            qi = [qi_ref[a * IDX_DIM:(a + 1) * IDX_DIM, :].astype(BF16) for a in range(N_IDX_HEADS)]
        if mixed:
            row_ref[...] = jnp.full(row_ref.shape, INT_MIN, I32)

        def score_body(t, _):
            ki = ki_ref[0, t]
            if mixed:
                rel = jnp.dot(ki, qi, preferred_element_type=F32)
                acc = jnp.sum(jnp.maximum(rel, 0.0) * w[0:1, :], axis=1, keepdims=True)
                acc = jnp.broadcast_to(acc, (KV_TILE, nq))
            else:
                acc = jnp.zeros((KV_TILE, nq), F32)
                for a in range(N_IDX_HEADS):
                    rel = jnp.dot(ki, qi[a], preferred_element_type=F32)
                    acc = acc + jnp.maximum(rel, 0.0) * w[a:a + 1, :]
            sc = jnp.where(t * KV_TILE + krow <= qpos, acc, NEG)
            bits = lax.bitcast_convert_type(sc, I32)
            key = bits ^ ((bits >> 31) & 0x7FFFFFFF)
            key_ref[t] = key
            if mixed:
                row_ref[pl.ds(t, 1), :] = jnp.transpose(key)[0:1, :]
            return 0
        lax.fori_loop(0, n_kv, score_body, 0)

        if mixed:
            keys = row_ref[...]
            kpos_r = (lax.broadcasted_iota(I32, keys.shape, 0) * KV_TILE + lax.broadcasted_iota(I32, keys.shape, 1))

            def count(pred):
                c = jnp.sum(jnp.where(pred(keys, kpos_r), 1.0, 0.0), axis=0, keepdims=True)
                return jnp.sum(c, axis=1, keepdims=True)
            thr, jb = search(count, (1, 1))
        else:
            n_grp = (n_kv + COUNT_GROUP - 1) // COUNT_GROUP

            def pad_body(t, _):
                key_ref[t] = jnp.full((KV_TILE, nq), INT_MIN, I32)
                return 0
            lax.fori_loop(n_kv, n_grp * COUNT_GROUP, pad_body, 0)

            def count(pred):
                def body(u, c):
                    for j in range(COUNT_GROUP):
                        t = u * COUNT_GROUP + j
                        hit = pred(key_ref[t], t * KV_TILE + krow)
                        c = c + jnp.sum(jnp.where(hit, 1.0, 0.0), axis=0, keepdims=True)
                    return c
                return lax.fori_loop(0, n_grp, body, jnp.zeros((1, nq), F32))
            thr, jb = search(count, (1, nq))

        def bias_body(t, _):
            key = key_ref[t]
            kpos = t * KV_TILE + krow
            keep = ((key > thr) | ((key == thr) & (kpos < jb))) & (kpos <= qpos)
            key_ref[t] = lax.bitcast_convert_type(jnp.where(keep, 0.0, NEG), I32)
            return 0
        lax.fori_loop(0, n_kv, bias_body, 0)

    select()
    slope = meta_ref[0:1, :]
    qpos_c = meta_ref[1:2, :].astype(I32)
    qa = qa_ref[...] if mixed else _step_queries(qa_ref, meta_ref, qa_s)[0]
    if mixed:
        n_t = q_off // KV_TILE + 1
        bias = lax.bitcast_convert_type(key_ref[0:n_t], F32).reshape(n_t * KV_TILE, nq)
        o_ref[...] = _dense_attend(k_ref, n_t, vT_ref, n_t, qa, slope, qpos_c, KV_TILE, bias=bias)
        return
    acc_ref[...] = jnp.zeros(acc_ref.shape, F32)

    def step(t, slot, ok, carry):
        bias = lax.bitcast_convert_type(key_ref[t], F32)
        if rep > 1:
            bias = jnp.concatenate([bias] * rep, axis=1)
        s = jnp.dot(k_ref[0, t], qa, preferred_element_type=F32) + bias
        c = slope * (t * KV_TILE - qpos_c).astype(F32)
        return _softmax_step(s, c, ok, vT_ref[0, t], *carry, acc_ref)
    _, l = _tile_groups(n_kv, step, _softmax_init(n), _group(mixed))
    _store_heads(o_ref, acc_ref[...] / jnp.maximum(l, 1e-30), mixed)


def _dsa(cols, qi, qi_row, wi, wi_row, ki, q, q_row, meta, k, vT, topk):
    nb, nq = cols.nb, cols.nq
    n = SAMPLE_COLS if cols.sample else N_C_HEADS * cols.tq
    ncol = SAMPLE_COLS if cols.sample else cols.tq
    nT = k.shape[1]
    assert nT % COUNT_GROUP == 0
    q_spec, o_spec, o_shape, vrows = _query_specs(cols, N_C_HEADS, q_row, meta.shape[1])
    qi_rows = IDX_DIM if cols.sample else N_IDX_HEADS * IDX_DIM
    qi_b, wi_b = qi_row // qi_rows, wi_row // N_IDX_HEADS
    col = lambda b, i: (0, b * nq + i)
    seq4 = lambda b, i: (b, 0, 0, 0)
    return pl.pallas_call(
        functools.partial(_dsa_kernel, span=cols.span, q_off=cols.q_off, topk=topk, mixed=cols.sample),
        grid=(nb, nq),
        in_specs=[pl.BlockSpec((qi_rows, ncol), lambda b, i: (qi_b, b * nq + i)),
                  pl.BlockSpec((N_IDX_HEADS, ncol), lambda b, i: (wi_b, b * nq + i)),
                  pl.BlockSpec((1, nT, KV_TILE, IDX_DIM), seq4),
                  q_spec,
                  pl.BlockSpec((META_ROWS, n), col),
                  pl.BlockSpec((1, nT, KV_TILE, KD), seq4),
                  pl.BlockSpec((1, nT, N_KV * HEAD_DIM, KV_TILE), seq4)],
        out_specs=o_spec, out_shape=o_shape,
        scratch_shapes=[pltpu.VMEM((nT, KV_TILE, ncol), I32), pltpu.VMEM((-(-nT // 8) * 8, KV_TILE), I32),
                        pltpu.VMEM((vrows, n), F32), pltpu.VMEM((KD, n), BF16)],
        compiler_params=_cparams(("parallel", "parallel")), name="dsa")(
            qi, wi, ki, q, meta, k, vT)


def _mem_kernel(q_ref, mk_ref, mvT_ref, o_ref):
    for h in range(N_MEM_HEADS):
        rows = slice(h * HEAD_DIM, (h + 1) * HEAD_DIM)
        q = (q_ref[rows, :] * ATTN_SCALE).astype(BF16)
        s = jnp.dot(mk_ref[0, h].astype(BF16), q, preferred_element_type=F32)
        p = jnp.exp(s - jnp.max(s, axis=0, keepdims=True))
        l = jnp.sum(p, axis=0, keepdims=True)
        o_ref[rows, :] = jnp.dot(mvT_ref[0, h].astype(BF16), p.astype(BF16), preferred_element_type=F32) / l


def _mem(hT, q_row, mk, mvT, nb, nq, tq):
    n_mem = mk.shape[2]
    n = N_MEM_HEADS * HEAD_DIM
    qb = q_row // n
    return pl.pallas_call(
        _mem_kernel, grid=(nb, nq),
        in_specs=[pl.BlockSpec((n, tq), lambda b, i: (qb, b * nq + i)),
                  pl.BlockSpec((1, N_MEM_HEADS, n_mem, HEAD_DIM), lambda b, i: (b, 0, 0, 0)),
                  pl.BlockSpec((1, N_MEM_HEADS, HEAD_DIM, n_mem), lambda b, i: (b, 0, 0, 0))],
        out_specs=pl.BlockSpec((n, tq), lambda b, i: (0, b * nq + i)),
        out_shape=jax.ShapeDtypeStruct((n, nb * nq * tq), F32),
        compiler_params=_cparams(("parallel", "parallel")), name="mem")(hT, mk, mvT)


def _mem_branch(cols, hT, q_row, mk, mvT):
    n = N_MEM_HEADS * HEAD_DIM
    if cols.sample:
        o = _mem(cols.tile_cols(hT[q_row:q_row + n]), 0, mk, mvT, cols.nb, 1, SAMPLE_COLS)
        return cols.from_tile_cols(o)
    return _mem(hT, q_row, mk, mvT, cols.nb, cols.nq, cols.tq)


def _even_wT(w_in):
    sizes = (384, 128, 128, 384, 384, 128, 128, 128, 128, 128, 128, 18, 384, 256, 256)
    offs = np.cumsum((0,) + sizes)
    (a_q, a_k, a_v, a_z, b_q, b_kc, b_vc, b_ks, b_vs, b_kw, b_vw, b_g, b_z, m_q, m_z) = [
        w_in[:, offs[n]:offs[n + 1]] for n in range(len(sizes))]
    d = w_in.shape[0]
    bg = jnp.pad(b_g, ((0, 0), (0, 32 - b_g.shape[1])))
    cols = [a_q, b_q, m_q, a_z, b_z, m_z, a_k, a_v, b_kc, b_vc, b_ks, b_vs, b_kw, b_vw, bg]
    w = jnp.concatenate(cols, axis=1)
    w = jnp.pad(w, ((0, 0), (0, EVEN_TOTAL - w.shape[1])))
    return w.T


def _odd_wT(w_in):
    sizes = (768, 128, 128, 512, 64, 8, 768, 256, 256)
    offs = np.cumsum((0,) + sizes)
    (c_q, c_k, c_v, c_qi, c_ki, c_wi, c_z, m_q, m_z) = [w_in[:, offs[n]:offs[n + 1]] for n in range(len(sizes))]
    w = jnp.concatenate([c_q, m_q, c_z, m_z, c_qi, c_k, c_v, c_ki, c_wi], axis=1)
    w = jnp.pad(w, ((0, 0), (0, ODD_TOTAL - w.shape[1])))
    return w.T


def _keys(cols, new_tok, pool, page_table, tile, mode):
    if cols.sample:
        return _gather(pool, page_table, new_tok, mode, tile)
    return _kprep(new_tok, tile, mode)


def _values(cols, new_tok, pool, page_table, tile):
    if cols.sample:
        return _gather(pool, page_table, new_tok, 'v', tile)
    return _vT_tiles(new_tok, tile)


def _raw_rows(cols, new_tok, pool, page_table):
    if cols.sample:
        return _gather(pool, page_table, None, 'raw', KV_TILE)
    return new_tok


def _queries(cols, hT, row, heads, slopes, extra=()):
    meta = cols.meta(heads, slopes, extra)
    if cols.sample:
        qa, qf = cols.columns(hT[row:row + heads * HEAD_DIM], heads, meta)
        return qa, 0, qf, meta
    return hT, row, meta, meta


def _heads_out(cols, o, heads):
    return cols.collapse(o, heads) if cols.sample else o


def _gate_cols(cols, gT):
    return [cols.per_token(gT, N_B_HEADS)]


def _even_layer(cols, xT, mk, mvT, past, page_table, wT, woT, g, b, cmpw, alpha):
    R = EVEN_ROWS
    nb = cols.nb
    hT = _mm(wT, xT, 640, min(1024, xT.shape[1]))
    names = ('a_k', 'a_v', 'b_kc', 'b_vc', 'b_ks', 'b_vs')
    new = [cols.tokens(hT[R[nm]:R[nm] + 128]) for nm in names]
    kw_new, vw_new = cols.tokens(hT[R['b_kw']:R['b_kw'] + 128]), cols.tokens(hT[R['b_vw']:R['b_vw'] + 128])
    pools = past[:6] if cols.sample else [None] * 6
    if cols.sample:
        kw = jnp.concatenate([past[6].reshape(nb, -1, 128), kw_new], axis=1)
        vw = jnp.concatenate([past[7].reshape(nb, -1, 128), vw_new], axis=1)
        n_win = past[6].shape[1]
        kw_base = cols.q_off - n_win
        win_out = (kw[:, -n_win:], vw[:, -n_win:])
        pad = -kw.shape[1] % 1024
        kw = jnp.pad(kw, ((0, 0), (0, pad), (0, 0)))
        vw = jnp.pad(vw, ((0, 0), (0, pad), (0, 0)))
    else:
        kw, vw, kw_base = kw_new, vw_new, 0
        n_win = min(WINDOW, cols.t)
        win_out = (kw[:, -n_win:], vw[:, -n_win:])

    sl = _alibi(N_A_HEADS + N_B_HEADS)
    ak, amean = _keys(cols, new[0], pools[0], page_table, KV_TILE, 'k')
    avT = _values(cols, new[1], pools[1], page_table, KV_TILE)
    q, q_row, qf, meta = _queries(cols, hT, R['a_q'], N_A_HEADS, sl[0::2])
    o_a = _moba(cols, q, q_row, qf, meta, amean, ak, avT)

    pe2k, w1k, w2k = cmpw[0]
    pe2v, w1v, w2v = cmpw[1]
    n_cmp = max(cols.q_off, cols.t) // CMP_STRIDE
    kc_raw = _raw_rows(cols, new[2], pools[2], page_table)
    vc_raw = _raw_rows(cols, new[3], pools[3], page_table)
    kc = _compress(kc_raw, pe2k, w1k, w2k, n_cmp).astype(BF16)
    vcT = _compress(vc_raw, pe2v, w1v, w2v, n_cmp).transpose(0, 2, 1).astype(BF16)
    ks = _keys(cols, new[4], pools[4], page_table, SEL_TILE, 'ks')
    vsT = _values(cols, new[5], pools[5], page_table, SEL_TILE)
    kwt = _kprep(kw, KV_TILE, 'ks')
    q, q_row, _, meta = _queries(cols, hT, R['b_q'], N_B_HEADS, sl[1::2],
                               extra=_gate_cols(cols, hT[R['b_g']:R['b_g'] + 18]))
    o_b = _nsa(cols, q, q_row, meta, kc, vcT, ks, vsT, kwt, _vT_tiles(vw, KV_TILE), kw_base)

    o_m = _mem_branch(cols, hT, R['m_q'], mk, mvT)
    parts = [_heads_out(cols, o_a, N_A_HEADS), _heads_out(cols, o_b, N_B_HEADS), o_m]
    yT = _outln(parts, hT, R['z'], xT, woT, g, b, alpha, min(512, xT.shape[1]))
    state = [n.reshape(nb, -1, N_KV, HEAD_DIM) for n in new]
    state += [w.reshape(nb, -1, N_KV, HEAD_DIM) for w in win_out]
    return yT, state


def _odd_layer(cols, xT, mk, mvT, past, page_table, wT, woT, g, b, alpha):
    R = ODD_ROWS
    nb = cols.nb
    hT = _mm(wT, xT, 768, min(1024, xT.shape[1]))
    new = [cols.tokens(hT[R['c_k']:R['c_k'] + 128]), cols.tokens(hT[R['c_v']:R['c_v'] + 128]),
           cols.tokens(hT[R['c_ki']:R['c_ki'] + IDX_DIM])]
    pools = past if cols.sample else [None] * 3
    topk = min(DSA_TOPK, (cols.q_off + cols.t) // 4)
    ck = _keys(cols, new[0], pools[0], page_table, KV_TILE, 'ks')
    cvT = _values(cols, new[1], pools[1], page_table, KV_TILE)
    if cols.sample:
        ki = _gather(pools[2], page_table, new[2], 'i', KV_TILE)
    else:
        ki = new[2].reshape(nb, -1, KV_TILE, IDX_DIM).astype(BF16)
    q, q_row, _, meta = _queries(cols, hT, R['c_q'], N_C_HEADS, _alibi(N_C_HEADS))
    if cols.sample:
        qi = cols.per_token(hT[R['c_qi']:R['c_qi'] + 512], N_IDX_HEADS)
        wi = jnp.pad(cols.per_token(hT[R['c_wi']:R['c_wi'] + 8], N_IDX_HEADS), ((0, 7), (0, 0)))
        o_c = _dsa(cols, qi, 0, wi, 0, ki, q, q_row, meta, ck, cvT, topk)
    else:
        o_c = _dsa(cols, hT, R['c_qi'], hT, R['c_wi'], ki, q, q_row, meta, ck, cvT, topk)
    o_m = _mem_branch(cols, hT, R['m_q'], mk, mvT)
    parts = [_heads_out(cols, o_c, N_C_HEADS), o_m]
    yT = _outln(parts, hT, R['z'], xT, woT, g, b, alpha, min(512, xT.shape[1]))
    state = [new[0].reshape(nb, -1, N_KV, HEAD_DIM), new[1].reshape(nb, -1, N_KV, HEAD_DIM), new[2]]
    return yT, state


def kernel(x_prompt, x_sample, cache_a_k, cache_a_v, cache_b_cmp_k, cache_b_cmp_v, cache_b_sel_k, cache_b_sel_v,
           state_b_win_k, state_b_win_v, cache_c_k, cache_c_v, cache_c_idx_k, cache_mem_k, cache_mem_v,
           page_table, mem_prompt, w_in_even, w_out_even, w_in_odd, w_out_odd, w_mem_kv, ln_g, ln_b,
           cmp_pe, cmp_w1, cmp_w2):
    bp, seq, d = x_prompt.shape
    bs = x_sample.shape[0]
    depth = w_mem_kv.shape[0]
    n_mem = mem_prompt.shape[1]
    past_len = page_table.shape[1] * cache_a_k.shape[2]
    alpha = (2.0 * depth) ** 0.25
    assert x_sample.shape[1] == 1 and bs <= 128 and seq % 1024 == 0 and past_len % 1024 == 0

    gp = _Cols(bp, seq, 0)
    gs = _Cols(bs, 1, past_len)
    xp = x_prompt.reshape(bp * seq, d).T
    xs = jnp.pad(x_sample.reshape(bs, d).T, ((0, 0), (0, 128 - bs)))
    n_pool = cache_a_k.shape[1]

    def pages(c):
        return jnp.moveaxis(c.reshape(c.shape[:3] + (-1,)), 2, 3).reshape(c.shape[0] * n_pool, -1, c.shape[2])

    even_paged = tuple(pages(c) for c in (cache_a_k, cache_a_v, cache_b_cmp_k, cache_b_cmp_v, cache_b_sel_k,
                                          cache_b_sel_v))
    odd_paged = tuple(pages(c) for c in (cache_c_k, cache_c_v, cache_c_idx_k))
    ev_p, ev_s, od_p, od_s, mk_list, mv_list = [], [], [], [], [], []
    for layer in range(depth):
        j = layer // 2
        mem_kv = _mm(mem_prompt.reshape(bp * n_mem, d), w_mem_kv[layer], bp * n_mem, 2 * N_MEM_HEADS * HEAD_DIM)
        mem_kv = mem_kv.reshape(bp, n_mem, 2, N_MEM_HEADS, HEAD_DIM)
        mk_p, mv_p = mem_kv[:, :, 0], mem_kv[:, :, 1]
        mk_list.append(mk_p)
        mv_list.append(mv_p)
        mem_p = (mk_p.transpose(0, 2, 1, 3), mv_p.transpose(0, 2, 3, 1))
        mem_s = (cache_mem_k[layer].transpose(0, 2, 1, 3), cache_mem_v[layer].transpose(0, 2, 3, 1))
        g = ln_g[layer].reshape(d, 1)
        b = ln_b[layer].reshape(d, 1)
        if layer % 2 == 0:
            wT = _even_wT(w_in_even[j])
            woT = w_out_even[j].T
            cmpw = [_cmp_weights(cmp_pe[j, c], cmp_w1[j, c], cmp_w2[j, c]) for c in range(2)]
            past = even_paged + (state_b_win_k[j], state_b_win_v[j])
            xp, st_p = _even_layer(gp, xp, *mem_p, None, None, wT.astype(BF16), woT.astype(BF16), g, b, cmpw, alpha)
            xs, st_s = _even_layer(gs, xs, *mem_s, past, page_table + j * n_pool, wT, woT, g, b, cmpw, alpha)
            ev_p.append(st_p)
            ev_s.append(st_s)
        else:
            wT = _odd_wT(w_in_odd[j])
            woT = w_out_odd[j].T
            past = odd_paged
            xp, st_p = _odd_layer(gp, xp, *mem_p, None, None, wT.astype(BF16), woT.astype(BF16), g, b, alpha)
            xs, st_s = _odd_layer(gs, xs, *mem_s, past, page_table + j * n_pool, wT, woT, g, b, alpha)
            od_p.append(st_p)
            od_s.append(st_s)
    y_p = xp.T.reshape(bp, seq, d)
    y_s = xs[:, :bs].T.reshape(bs, 1, d)
    ev_p = [jnp.stack(l) for l in zip(*ev_p)]
    ev_s = [jnp.stack(l) for l in zip(*ev_s)]
    od_p = [jnp.stack(l) for l in zip(*od_p)]
    od_s = [jnp.stack(l) for l in zip(*od_s)]
    out = [y_p, y_s]
    for p, s in zip(ev_p, ev_s):
        out += [p, s]
    for p, s in zip(od_p, od_s):
        out += [p, s]
    out += [jnp.stack(mk_list), jnp.stack(mv_list)]
    return tuple(out)
```

```python
import functools

import numpy as np
import jax
import jax.numpy as jnp
from jax import lax
from jax.experimental import pallas as pl
from jax.experimental.pallas import tpu as pltpu

F32 = jnp.float32
BF16 = jnp.bfloat16
I32 = jnp.int32

HEAD_DIM = 64
N_MEM_HEADS = 4
N_A_HEADS = 6
N_B_HEADS = 6
N_C_HEADS = 12
N_KV = 2
N_IDX_HEADS = 8
IDX_DIM = 64
MOBA_BLOCK = 256
MOBA_TOPK = 3
CMP_STRIDE = 16
CMP_LEN = 32
CMP_HIDDEN = 128
SEL_BLOCK = 64
SEL_TOPK = 15
WINDOW = 512
DSA_TOPK = 256
ATTN_SCALE = HEAD_DIM ** -0.5
IDX_SCALE = IDX_DIM ** -0.5
LN_EPS = 1e-5
LOG2E = 1.4426950408889634
NEG = -1e30
SENTINEL = -3e38
INT_MIN = -(2 ** 31)

KV_TILE = 256
SEL_TILE = 512
KD = 256
FEAT_POS = 128
FEAT_SEL = 144
COUNT_GROUP = 4
SAMPLE_COLS = 128
META_ROWS = 16
META_PIECES = 8
PAGES_PER_STEP = 8
GATHER_PAGES = 32
VMEM_LIMIT = 56 * 1024 * 1024

EVEN_ROWS = dict(a_q=0, b_q=384, m_q=768, z=1024, a_k=2048, a_v=2176, b_kc=2304, b_vc=2432,
                 b_ks=2560, b_vs=2688, b_kw=2816, b_vw=2944, b_g=3072)
EVEN_TOTAL = 3200
ODD_ROWS = dict(c_q=0, m_q=768, z=1024, c_qi=2048, c_k=2560, c_v=2688, c_ki=2816, c_wi=2880)
ODD_TOTAL = 3072


def _cparams(sem):
    return pltpu.CompilerParams(dimension_semantics=sem, vmem_limit_bytes=VMEM_LIMIT)


def _alibi(n):
    return 2.0 ** (-8.0 * np.arange(1, n + 1) / n)


def _mm_kernel(a_ref, b_ref, o_ref, *, precise):
    if precise:
        o_ref[...] = jnp.dot(a_ref[...], b_ref[...], precision=lax.Precision.HIGHEST, preferred_element_type=F32)
    else:
        o_ref[...] = jnp.dot(a_ref[...].astype(BF16), b_ref[...].astype(BF16), preferred_element_type=F32)


def _mm(a, b, bm, bn):
    m, k = a.shape
    n = b.shape[1]
    return pl.pallas_call(
        functools.partial(_mm_kernel, precise=a.dtype == F32 and b.dtype == F32 and n <= 128), grid=(n // bn, m // bm),
        in_specs=[pl.BlockSpec((bm, k), lambda j, i: (i, 0)), pl.BlockSpec((k, bn), lambda j, i: (0, j))],
        out_specs=pl.BlockSpec((bm, bn), lambda j, i: (i, j)),
        out_shape=jax.ShapeDtypeStruct((m, n), F32),
        compiler_params=_cparams(("parallel", "parallel")), name="mm")(a, b)


def _outln_kernel(*refs, n_parts, alpha):
    o_refs = refs[:n_parts]
    z_ref, x_ref, w_ref, g_ref, b_ref, y_ref = refs[n_parts:]
    o = jnp.concatenate([r[...] for r in o_refs], axis=0)
    z = z_ref[...]
    mixed = o * (z / (1.0 + jnp.exp(-z)))
    if w_ref.dtype == F32:
        y = jnp.dot(w_ref[...], mixed, precision=lax.Precision.HIGHEST, preferred_element_type=F32)
    else:
        y = jnp.dot(w_ref[...], mixed.astype(BF16), preferred_element_type=F32)
    y = alpha * x_ref[...] + y
    mu = jnp.mean(y, axis=0, keepdims=True)
    d = y - mu
    var = jnp.mean(d * d, axis=0, keepdims=True)
    y_ref[...] = d * lax.rsqrt(var + LN_EPS) * g_ref[...] + b_ref[...]


def _outln(o_parts, hT, z_row, xT, woT, g, b, alpha, bn):
    d, n = xT.shape
    zb = z_row // d
    in_specs = [pl.BlockSpec((p.shape[0], bn), lambda j: (0, j)) for p in o_parts]
    in_specs += [pl.BlockSpec((d, bn), lambda j: (zb, j)),
                 pl.BlockSpec((d, bn), lambda j: (0, j)),
                 pl.BlockSpec((d, d), lambda j: (0, 0)),
                 pl.BlockSpec((d, 1), lambda j: (0, 0)),
                 pl.BlockSpec((d, 1), lambda j: (0, 0))]
    return pl.pallas_call(
        functools.partial(_outln_kernel, n_parts=len(o_parts), alpha=alpha), grid=(n // bn,),
        in_specs=in_specs, out_specs=pl.BlockSpec((d, bn), lambda j: (0, j)),
        out_shape=jax.ShapeDtypeStruct((d, n), F32),
        compiler_params=_cparams(("parallel",)), name="outln")(*o_parts, hT, xT, woT, g, b)


def _key_features(tile):
    r = lax.broadcasted_iota(I32, (tile, 128), 0)
    c = lax.broadcasted_iota(I32, (tile, 128), 1)
    sel_c = c - (FEAT_SEL - FEAT_POS)
    feat = jnp.where(c < 3, r >> 4, jnp.where(c < 6, r & 15, jnp.where(sel_c == (r >> 6), 1, 0)))
    return feat.astype(F32).astype(BF16)


def _emit_tiles(x, outs, mode, tile, t0=0):
    nt = x.shape[0] // tile
    if mode == 'raw':
        outs[0][0, t0 * tile:(t0 + nt) * tile, :] = x
        return
    feat = _key_features(tile) if mode in ('k', 'ks') else None
    for t in range(nt):
        xt = x[t * tile:(t + 1) * tile]
        if mode in ('k', 'ks'):
            outs[0][0, t0 + t] = jnp.concatenate([xt.astype(BF16), feat], axis=1)
        elif mode == 'v':
            outs[0][0, t0 + t] = jnp.transpose(xt).astype(BF16)
        else:
            outs[0][0, t0 + t] = xt.astype(BF16)
    if mode == 'k':
        nblk = x.shape[0] // MOBA_BLOCK
        outs[1][0, 0, t0:t0 + nblk, :] = jnp.mean(x.reshape(nblk, MOBA_BLOCK, x.shape[1]), axis=1)


def _zero_tiles(outs, mode, tile, t0, t1):
    for t in range(t0, t1):
        outs[0][0, t] = jnp.zeros(outs[0].shape[2:], outs[0].dtype)
    if mode == 'k':
        outs[1][0, 0, t0:t1, :] = jnp.zeros((t1 - t0, outs[1].shape[3]), F32)


def _tile_outputs(mode, nb, n_steps, rows, tile, c):
    nt = rows // tile
    if mode == 'raw':
        return ([pl.BlockSpec((1, rows, c), lambda b, j, *_: (b, j, 0))],
                [jax.ShapeDtypeStruct((nb, n_steps * rows, c), F32)])
    if mode in ('k', 'ks'):
        shape, blk = (nb, n_steps * nt, tile, KD), (1, nt, tile, KD)
    elif mode == 'v':
        shape, blk = (nb, n_steps * nt, c, tile), (1, nt, c, tile)
    else:
        shape, blk = (nb, n_steps * nt, tile, c), (1, nt, tile, c)
    specs = [pl.BlockSpec(blk, lambda b, j, *_: (b, j, 0, 0))]
    shapes = [jax.ShapeDtypeStruct(shape, BF16)]
    if mode == 'k':
        nblk = rows // MOBA_BLOCK
        specs.append(pl.BlockSpec((1, 1, nblk, c), lambda b, j, *_: (b, j, 0, 0)))
        shapes.append(jax.ShapeDtypeStruct((nb, n_steps, nblk, c), F32))
    return specs, shapes


def _gather_kernel(pt_ref, *refs, npg, mode, tile, last, has_tail):
    pool_refs = refs[:npg]
    outs = refs[npg + (1 if has_tail else 0):]
    j = pl.program_id(1)

    def token_major(p):
        c = p.shape[0]
        if c == p.shape[1]:
            return jnp.transpose(p)
        return jnp.transpose(jnp.concatenate([p, jnp.zeros((p.shape[1] - c, p.shape[1]), p.dtype)], axis=0))[:, :c]

    def pages():
        if mode == 'v':
            xT = jnp.concatenate([r[0] for r in pool_refs], axis=1)
            for t in range(xT.shape[1] // tile):
                outs[0][0, t] = xT[:, t * tile:(t + 1) * tile].astype(BF16)
        else:
            _emit_tiles(jnp.concatenate([token_major(r[0]) for r in pool_refs], axis=0), outs, mode, tile)

    if not has_tail:
        pages()
        return
    pl.when(j < last)(pages)

    @pl.when(j == last)
    def _():
        _emit_tiles(refs[npg][0], outs, mode, tile)
        if mode != 'raw':
            _zero_tiles(outs, mode, tile, 1, outs[0].shape[1])


def _gather(pool, page_table, new_tok, mode, tile):
    nb, n_pages = page_table.shape
    n_pool, c, page_tokens = pool.shape
    npg = GATHER_PAGES if n_pages % GATHER_PAGES == 0 else PAGES_PER_STEP
    last = n_pages // npg
    has_tail = new_tok is not None
    rows = npg * page_tokens
    assert rows % tile == 0 and (has_tail or mode == 'raw')

    def pool_spec(k):
        return pl.BlockSpec((1, c, page_tokens),
                            lambda b, j, pt: (pt[b, jnp.minimum(j * npg + k, n_pages - 1)], 0, 0))

    in_specs = [pool_spec(k) for k in range(npg)]
    args = [pool] * npg
    if has_tail:
        tail = jnp.zeros((nb, tile, c), F32).at[:, 0:1, :].set(new_tok)
        in_specs.append(pl.BlockSpec((1, tile, c), lambda b, j, pt: (b, 0, 0)))
        args.append(tail)
    n_steps = last + (1 if has_tail else 0)
    out_specs, out_shape = _tile_outputs(mode, nb, n_steps, rows, tile, c)
    grid_spec = pltpu.PrefetchScalarGridSpec(num_scalar_prefetch=1, grid=(nb, n_steps), in_specs=in_specs,
                                             out_specs=out_specs)
    out = pl.pallas_call(
        functools.partial(_gather_kernel, npg=npg, mode=mode, tile=tile, last=last, has_tail=has_tail),
        grid_spec=grid_spec, out_shape=out_shape,
        compiler_params=_cparams(("parallel", "arbitrary")), name="gather_" + mode)(page_table, *args)
    if mode == 'k':
        return out[0], out[1].reshape(nb, -1, c)
    return out[0]


def _kprep_kernel(x_ref, *outs, mode, tile):
    _emit_tiles(x_ref[0], outs, mode, tile)


def _kprep(x, tile, mode='k'):
    nb, L, c = x.shape
    rows = 1024
    out_specs, out_shape = _tile_outputs(mode, nb, L // rows, rows, tile, c)
    out = pl.pallas_call(
        functools.partial(_kprep_kernel, mode=mode, tile=tile), grid=(nb, L // rows),
        in_specs=[pl.BlockSpec((1, rows, c), lambda b, j: (b, j, 0))],
        out_specs=out_specs, out_shape=out_shape,
        compiler_params=_cparams(("parallel", "parallel")), name="kprep")(x)
    if mode == 'k':
        return out[0], out[1].reshape(nb, -1, c)
    return out[0]


def _vT_tiles(v, tile):
    nb, L, c = v.shape
    return v.reshape(nb, L // tile, tile, c).transpose(0, 1, 3, 2).astype(BF16)


class _Cols:
    def __init__(self, nb, t, q_off):
        self.nb, self.t, self.q_off = nb, t, q_off
        self.sample = q_off > 0
        self.tq = 1 if self.sample else KV_TILE
        self.nq = 1 if self.sample else t // KV_TILE
        self.span = 1 if self.sample else KV_TILE

    def head_group(self, heads):
        return np.arange(heads) // (heads // N_KV)

    def per_head(self, vals, heads):
        vals = jnp.asarray(vals, F32)
        k = vals.shape[0]
        if self.sample:
            return jnp.tile(jnp.pad(vals, ((0, 0), (0, SAMPLE_COLS - heads))), (1, self.nb))
        hps = heads // N_KV
        shape = (k, self.nb, self.nq, N_KV, hps, self.tq)
        return jnp.broadcast_to(vals.reshape(k, 1, 1, N_KV, hps, 1), shape).reshape(k, -1)

    def per_token(self, rows, heads):
        nb, nq, tq = self.nb, self.nq, self.tq
        k = rows.shape[0] // heads
        if self.sample:
            x = rows[:, :nb].reshape(heads, k, nb).transpose(1, 2, 0)
            return jnp.pad(x, ((0, 0), (0, 0), (0, SAMPLE_COLS - heads))).reshape(k, nb * SAMPLE_COLS)
        x = rows.reshape(N_KV, heads // N_KV, k, nb, nq, tq).transpose(2, 3, 4, 0, 1, 5)
        return x.reshape(k, -1)

    def qpos_row(self, heads):
        if self.sample:
            return jnp.full((1, self.nb * SAMPLE_COLS), self.q_off, F32)
        pos = (jnp.arange(self.nq, dtype=F32)[:, None] * self.tq + jnp.arange(self.tq, dtype=F32)[None, :])
        shape = (1, self.nb, self.nq, N_KV, heads // N_KV, self.tq)
        return jnp.broadcast_to(pos.reshape(1, 1, self.nq, 1, 1, self.tq), shape).reshape(1, -1)

    def meta(self, heads, slopes, extra=()):
        s2 = np.asarray(slopes, np.float64) * LOG2E
        hi = s2.astype(jnp.bfloat16).astype(np.float64)
        mid = (s2 - hi).astype(jnp.bfloat16).astype(np.float64)
        lo = (s2 - hi - mid).astype(jnp.bfloat16).astype(np.float64)
        rows = [self.per_head((hi + mid + lo)[None, :], heads), self.qpos_row(heads)] + list(extra)
        n = rows[0].shape[1]
        rows.append(jnp.zeros((META_PIECES - sum(m.shape[0] for m in rows), n), F32))
        rows.append(self.per_head(np.stack([16 * hi, 16 * mid, 16 * lo, hi, mid, lo]), heads))
        rows.append(jnp.zeros((2, n), F32))
        return jnp.concatenate(rows, axis=0)

    def columns(self, rows, heads, meta):
        x = self.per_token(rows, heads)
        grp = self.head_group(heads)
        gsel = self.per_head(np.stack([grp == 0, grp == 1]).astype(np.float32), heads)
        qf = jnp.concatenate([x * gsel[0:1], x * gsel[1:2]], axis=0)
        qa = jnp.concatenate([(qf * (ATTN_SCALE * LOG2E)).astype(BF16), meta[META_PIECES:].astype(BF16),
                              jnp.zeros((8, x.shape[1]), BF16),
                              jnp.zeros((KD - FEAT_POS - 16, x.shape[1]), BF16)], axis=0)
        return qa, qf

    def collapse(self, o, heads):
        nb, nq, tq = self.nb, self.nq, self.tq
        if self.sample:
            o = o.reshape(N_KV, HEAD_DIM, nb, SAMPLE_COLS)
            grp = self.head_group(heads)
            x = jnp.stack([o[grp[h], :, :, h] for h in range(heads)])
            return jnp.pad(x.reshape(heads * HEAD_DIM, nb), ((0, 0), (0, 128 - nb)))
        o = o.reshape(HEAD_DIM, nb, nq, N_KV, heads // N_KV, tq).transpose(3, 4, 0, 1, 2, 5)
        return o.reshape(heads * HEAD_DIM, nb * nq * tq)

    def tokens(self, rows):
        n = rows.shape[0]
        if self.sample:
            return rows[:, :self.nb].T.reshape(self.nb, 1, n)
        return rows.reshape(n, self.nb, self.t).transpose(1, 2, 0)

    def tile_cols(self, rows):
        if not self.sample:
            return rows
        n = rows.shape[0]
        return jnp.broadcast_to(rows[:, :self.nb, None], (n, self.nb, SAMPLE_COLS)).reshape(n, self.nb * SAMPLE_COLS)

    def from_tile_cols(self, o):
        if not self.sample:
            return o
        n = o.shape[0]
        return jnp.pad(o.reshape(n, self.nb, SAMPLE_COLS)[:, :, 0], ((0, 0), (0, 128 - self.nb)))


def _softmax_step(s, c, ok, vT, m, l, acc_ref):
    smax = jnp.max(s, axis=0, keepdims=True)
    live = smax > 0.5 * NEG
    if ok is not None:
        live = live & _row_flag(ok, s.shape[1])
    m_new = jnp.where(live, jnp.maximum(m, smax + c), m)
    shift = jnp.where(live, m_new - c, -NEG)
    p = jnp.exp2(s - shift)
    alpha = jnp.exp2(m - m_new)
    l = alpha * l + jnp.sum(p, axis=0, keepdims=True)
    acc_ref[...] = alpha * acc_ref[...] + _pv(vT, p, acc_ref.shape[0])
    return m_new, l


def _pv(vT, p, rows):
    pb = p.astype(BF16)
    if vT.shape[0] == rows:
        return jnp.dot(vT, pb, preferred_element_type=F32)
    h = pb.shape[1] // N_KV
    return jnp.concatenate([jnp.dot(vT[g * HEAD_DIM:(g + 1) * HEAD_DIM], pb[:, g * h:(g + 1) * h],
                                    preferred_element_type=F32) for g in range(N_KV)], axis=1)


def _row_flag(ok, n):
    if ok.ndim == 0:
        return (jnp.zeros((1, n), I32) + ok.astype(I32)) > 0
    return ok


def _group(mixed):
    return 4 if mixed else 2


def _softmax_init(n):
    return jnp.full((1, n), NEG, F32), jnp.zeros((1, n), F32)


def _tile_groups(n_tiles, step, carry, group):
    def body(u, carry):
        carry = step(group * u, 0, None, carry)
        for j in range(1, group):
            t = group * u + j
            carry = step(jnp.minimum(t, n_tiles - 1), j, t < n_tiles, carry)
        return carry
    return lax.fori_loop(0, (n_tiles + group - 1) // group, body, carry)


def _dense_attend(k_ref, n_k, vT_ref, n_v, qa, slope, qpos, blk, pos0=0, mask_fn=None, bias=None):
    n = qa.shape[1]
    tile = k_ref.shape[2]
    rows = n_k * tile
    s = jnp.dot(k_ref[0, 0:n_k].reshape(rows, KD), qa, preferred_element_type=F32)
    if bias is not None:
        s = s + bias
    nblk = rows // blk
    s = s.reshape(nblk, blk, n)
    b3 = lax.broadcasted_iota(I32, (nblk, 1, 1), 0)
    base = pos0 + (b3 * blk) // tile * tile
    s = s + slope * (base - qpos).astype(F32)
    if mask_fn is not None:
        kpos = pos0 + b3 * blk + lax.broadcasted_iota(I32, (nblk, blk, n), 1)
        s = jnp.where(mask_fn(b3, kpos), s, NEG)
    s = s.reshape(rows, n)
    m = jnp.max(s, axis=0, keepdims=True)
    p = jnp.where(s > 0.5 * NEG, jnp.exp2(s - m), 0.0)
    l = jnp.sum(p, axis=0, keepdims=True)
    vT = jnp.concatenate([vT_ref[0, t] for t in range(n_v)], axis=1)
    return jnp.dot(vT, p.astype(BF16), preferred_element_type=F32) / jnp.maximum(l, 1e-30)


def _step_queries(q_ref, meta_ref, qa_s):
    tq = q_ref.shape[1]
    n = qa_s.shape[1]
    qa_s[...] = jnp.zeros(qa_s.shape, BF16)
    pieces = jnp.concatenate([meta_ref[META_PIECES:META_ROWS, :], jnp.zeros((8, n), F32)], axis=0)
    qa_s[FEAT_POS:FEAT_POS + 16, :] = pieces.astype(BF16)
    heads = q_ref.shape[0] // HEAD_DIM
    qf = []
    for r in range(heads):
        q = q_ref[r * HEAD_DIM:(r + 1) * HEAD_DIM, :]
        z = jnp.zeros_like(q)
        blk = jnp.concatenate([q, z] if r < heads // N_KV else [z, q], axis=0)
        qa_s[0:N_KV * HEAD_DIM, r * tq:(r + 1) * tq] = (blk * (ATTN_SCALE * LOG2E)).astype(BF16)
        qf.append(blk)
    return qa_s[...], jnp.concatenate(qf, axis=1)


def _store_heads(o_ref, o, mixed):
    if mixed:
        o_ref[...] = o
        return
    tq = o_ref.shape[1]
    for r in range(o_ref.shape[0] // HEAD_DIM):
        o_ref[r * HEAD_DIM:(r + 1) * HEAD_DIM, :] = o[:, r * tq:(r + 1) * tq]


def _top_rows(vals, rowi, k):
    sel = jnp.zeros(vals.shape, F32)
    for _ in range(k):
        m = jnp.max(vals, axis=0, keepdims=True)
        first = jnp.min(jnp.where(vals == m, rowi, 1 << 20), axis=0, keepdims=True)
        pick = (rowi == first) & (m > SENTINEL)
        sel = jnp.where(pick, 1.0, sel)
        vals = jnp.where(pick, SENTINEL, vals)
    return sel


def _moba_kernel(qa_ref, qf_ref, meta_ref, kmean_ref, k_ref, vT_ref, o_ref, sel_ref, acc_ref, qa_s,
                 *, span, q_off, mixed):
    i = pl.program_id(1)
    n = meta_ref.shape[1]
    qa, qf = (qa_ref[...], qf_ref[...]) if mixed else _step_queries(qa_ref, meta_ref, qa_s)
    own = (q_off + i * span) // MOBA_BLOCK
    slope = meta_ref[0:1, :]
    qpos = meta_ref[1:2, :].astype(I32)
    nrow = kmean_ref.shape[1]
    gate = jnp.dot(kmean_ref[0], qf, precision=lax.Precision.HIGHEST, preferred_element_type=F32)
    rowi = lax.broadcasted_iota(I32, (nrow, n), 0)
    sel_ref[...] = _top_rows(jnp.where(rowi < own, gate, SENTINEL), rowi, MOBA_TOPK)
    if mixed:
        own_s = q_off // MOBA_BLOCK
        sel3 = sel_ref[0:own_s + 1, :][:, None, :] > 0.5
        o_ref[...] = _dense_attend(k_ref, own_s + 1, vT_ref, own_s + 1, qa, slope, qpos, KV_TILE,
                                   mask_fn=lambda b3, kpos: ((b3 < own_s) & sel3) | ((b3 >= own_s) & (kpos <= qpos)))
        return
    acc_ref[...] = jnp.zeros(acc_ref.shape, F32)

    def step(t, slot, ok, carry):
        s = jnp.dot(k_ref[0, t], qa, preferred_element_type=F32)
        c = slope * (t * KV_TILE - qpos).astype(F32)
        sel = sel_ref[pl.ds(t, 1), :] > 0.5
        ok = sel if ok is None else sel & _row_flag(ok, n)
        return _softmax_step(s, c, ok, vT_ref[0, t], *carry, acc_ref)

    m, l = _tile_groups(own, step, _softmax_init(n), _group(mixed))
    s = jnp.dot(k_ref[0, own], qa, preferred_element_type=F32)
    krow = lax.broadcasted_iota(I32, (KV_TILE, n), 0)
    s = jnp.where(krow <= qpos - own * KV_TILE, s, NEG)
    c = slope * (own * KV_TILE - qpos).astype(F32)
    _, l = _softmax_step(s, c, None, vT_ref[0, own], m, l, acc_ref)
    _store_heads(o_ref, acc_ref[...] / jnp.maximum(l, 1e-30), mixed)


def _query_specs(cols, heads, q_row, n_cols):
    nq = cols.nq
    if cols.sample:
        col = lambda b, i: (0, b * nq + i)
        return (pl.BlockSpec((KD, SAMPLE_COLS), col), pl.BlockSpec((N_KV * HEAD_DIM, SAMPLE_COLS), col),
                jax.ShapeDtypeStruct((N_KV * HEAD_DIM, n_cols), F32), N_KV * HEAD_DIM)
    rows = heads * HEAD_DIM
    blk0 = q_row // rows
    return (pl.BlockSpec((rows, cols.tq), lambda b, i: (blk0, b * nq + i)),
            pl.BlockSpec((rows, cols.tq), lambda b, i: (0, b * nq + i)),
            jax.ShapeDtypeStruct((heads * HEAD_DIM, cols.nb * nq * cols.tq), F32), HEAD_DIM)


def _moba(cols, q, q_row, qf, meta, kmean, k, vT):
    nb, nq = cols.nb, cols.nq
    n = SAMPLE_COLS if cols.sample else N_A_HEADS * cols.tq
    nT, nblk = k.shape[1], kmean.shape[1]
    q_spec, o_spec, o_shape, vrows = _query_specs(cols, N_A_HEADS, q_row, meta.shape[1])
    col = lambda b, i: (0, b * nq + i)
    return pl.pallas_call(
        functools.partial(_moba_kernel, span=cols.span, q_off=cols.q_off, mixed=cols.sample), grid=(nb, nq),
        in_specs=[q_spec,
                  pl.BlockSpec((qf.shape[0], n), col),
                  pl.BlockSpec((META_ROWS, n), col),
                  pl.BlockSpec((1, nblk, N_KV * HEAD_DIM), lambda b, i: (b, 0, 0)),
                  pl.BlockSpec((1, nT, KV_TILE, KD), lambda b, i: (b, 0, 0, 0)),
                  pl.BlockSpec((1, nT, N_KV * HEAD_DIM, KV_TILE), lambda b, i: (b, 0, 0, 0))],
        out_specs=o_spec, out_shape=o_shape,
        scratch_shapes=[pltpu.VMEM((nblk, n), F32), pltpu.VMEM((vrows, n), F32), pltpu.VMEM((KD, n), BF16)],
        compiler_params=_cparams(("parallel", "parallel")), name="moba")(q, qf, meta, kmean, k, vT)


def _cmp_kernel(r_ref, pe_ref, w1_ref, w2_ref, o_ref):
    n = o_ref.shape[1]
    hid = 2 * CMP_HIDDEN
    c = r_ref.shape[2]
    u = jnp.zeros((n, hid), F32)
    lo = jnp.zeros((n, hid), F32)
    for p in range(CMP_STRIDE):
        rp = r_ref[0, pl.ds(p, n, stride=CMP_STRIDE), :]
        cols = slice(p * c, (p + 1) * c)
        u = u + jnp.dot((rp + pe_ref[0:1, cols]).astype(BF16), w1_ref[cols, 0:hid], preferred_element_type=F32)
        lo = lo + jnp.dot((rp + pe_ref[1:2, cols]).astype(BF16), w1_ref[cols, hid:2 * hid],
                          preferred_element_type=F32)
    pre = u + pltpu.roll(lo, n - 1, 0)
    h = pre / (1.0 + jnp.exp(-pre))
    o_ref[0] = jnp.dot(h.astype(BF16), w2_ref[...], preferred_element_type=F32)


def _compress(raw, pe2, w1big, w2big, n_rows):
    nb, _, c = raw.shape
    return pl.pallas_call(
        _cmp_kernel, grid=(nb,),
        in_specs=[pl.BlockSpec((1, n_rows * CMP_STRIDE, c), lambda b: (b, 0, 0)),
                  pl.BlockSpec((2, 2048), lambda b: (0, 0)),
                  pl.BlockSpec((2048, 4 * CMP_HIDDEN), lambda b: (0, 0)),
                  pl.BlockSpec((2 * CMP_HIDDEN, 128), lambda b: (0, 0))],
        out_specs=pl.BlockSpec((1, n_rows, 128), lambda b: (b, 0, 0)),
        out_shape=jax.ShapeDtypeStruct((nb, n_rows, 128), F32),
        compiler_params=_cparams(("parallel",)), name="compress")(raw, pe2, w1big, w2big)


def _cmp_weights(pe, w1, w2):
    w1r = w1.reshape(2, CMP_STRIDE, HEAD_DIM, CMP_HIDDEN)
    eye = jnp.eye(N_KV, dtype=w1.dtype)
    big = jnp.einsum('hpdj,ge->hpdgej', w1r, eye)
    big = big.transpose(1, 3, 2, 0, 4, 5).reshape(CMP_STRIDE * N_KV * HEAD_DIM, 2 * N_KV * CMP_HIDDEN)
    w2big = jnp.einsum('jd,ge->gjed', w2, eye).reshape(N_KV * CMP_HIDDEN, N_KV * HEAD_DIM)
    pe2 = jnp.broadcast_to(pe.reshape(2, CMP_STRIDE, 1, HEAD_DIM), (2, CMP_STRIDE, N_KV, HEAD_DIM)).reshape(2, 2048)
    return pe2, big.astype(BF16), w2big.astype(BF16)


def _nsa_kernel(qa_ref, meta_ref, kc_ref, vcT_ref, taps_ref, grp_ref, ks_ref, vsT_ref, kw_ref, vwT_ref, o_ref,
                sel_ref, qs_ref, acc_ref, qa_s, *, span, q_off, kw_base, mixed):
    i = pl.program_id(1)
    n = meta_ref.shape[1]
    hps = N_B_HEADS // N_KV
    tq = n if mixed else n // N_B_HEADS
    nq = n if mixed else N_KV * tq
    t0 = q_off + i * span
    slope = meta_ref[0:1, :]
    qpos = meta_ref[1:2, :].astype(I32)
    qa = qa_ref[...] if mixed else _step_queries(qa_ref, meta_ref, qa_s)[0]
    n_cmp = kc_ref.shape[1]
    n_tab = sel_ref.shape[0]

    def per_head(x):
        if mixed:
            return x
        return jnp.concatenate([x[:, (r // hps) * tq:(r // hps + 1) * tq] for r in range(N_B_HEADS)], axis=1)

    cend = CMP_STRIDE * lax.broadcasted_iota(I32, (n_cmp, n), 0) + (CMP_LEN - 1)
    cmask = cend <= qpos
    s = jnp.dot(kc_ref[0], qa[0:N_KV * HEAD_DIM, :], preferred_element_type=F32) + slope * (cend - qpos).astype(F32)
    s = jnp.where(cmask, s, NEG)
    p = jnp.where(cmask, jnp.exp2(s - jnp.max(s, axis=0, keepdims=True)), 0.0)
    p = p * (1.0 / jnp.maximum(jnp.sum(p, axis=0, keepdims=True), 1e-30))
    o_c = _pv(vcT_ref[0], p, acc_ref.shape[0])

    if mixed:
        imp = jnp.dot(p, grp_ref[...], precision=lax.Precision.HIGHEST, preferred_element_type=F32)
    else:
        imp = jnp.concatenate([sum(p[:, (gg * hps + r) * tq:(gg * hps + r + 1) * tq] for r in range(hps))
                               for gg in range(N_KV)], axis=1)
    p_slc = jnp.dot(taps_ref[...], imp, precision=lax.Precision.HIGHEST, preferred_element_type=F32)
    p_slc = jnp.concatenate([p_slc, jnp.full((n_tab - p_slc.shape[0], nq), SENTINEL, F32)], axis=0)
    rowi = lax.broadcasted_iota(I32, (n_tab, nq), 0)
    own = (qpos if mixed else jnp.concatenate([qpos[:, 0:tq]] * N_KV, axis=1)) >> 6
    sel = _top_rows(jnp.where(rowi < own, p_slc, SENTINEL), rowi, SEL_TOPK)
    sel_ref[...] = jnp.where(rowi == own, 1.0, sel)

    gt = 1.0 / (1.0 + jnp.exp(-meta_ref[2:5, :]))
    if mixed:
        n_s = q_off // SEL_TILE + 1
        sel3 = sel_ref[0:n_s * (SEL_TILE // SEL_BLOCK), :][:, None, :] > 0.5
        o_s = _dense_attend(ks_ref, n_s, vsT_ref, n_s, qa, slope, qpos, SEL_BLOCK,
                            mask_fn=lambda b3, kpos: sel3 & (kpos <= qpos))
        n_w = min((q_off - kw_base) // KV_TILE + 1, kw_ref.shape[1])
        o_w = _dense_attend(kw_ref, n_w, vwT_ref, n_w, qa, slope, qpos, KV_TILE, pos0=kw_base,
                            mask_fn=lambda b3, kpos: (kpos <= qpos) & (kpos > qpos - WINDOW))
        o_ref[...] = gt[0:1, :] * o_c + gt[1:2, :] * o_s + gt[2:3, :] * o_w
        return

    per_tile = SEL_TILE // SEL_BLOCK
    last = (t0 + span - 1) // SEL_TILE
    for slot in range(qs_ref.shape[0]):
        qs_ref[slot] = qa
    acc_ref[...] = jnp.zeros(acc_ref.shape, F32)

    def sel_scores(t, slot):
        rows = sel_ref[pl.ds(pl.multiple_of(t * per_tile, per_tile), per_tile), :]
        bias = per_head(jnp.where(rows > 0.5, 0.0, NEG))
        bias = jnp.concatenate([bias, jnp.zeros((per_tile, n), F32)], axis=0)
        qs_ref[slot, FEAT_SEL:FEAT_SEL + 2 * per_tile, :] = bias.astype(BF16)
        s = jnp.dot(ks_ref[0, t], qs_ref[slot], preferred_element_type=F32)
        return s, slope * (t * SEL_TILE - qpos).astype(F32)

    def sel_step(t, slot, ok, carry):
        s, c = sel_scores(t, slot)
        return _softmax_step(s, c, ok, vsT_ref[0, t], *carry, acc_ref)

    m, l = _tile_groups(last, sel_step, _softmax_init(n), qs_ref.shape[0])
    s, c = sel_scores(last, 0)
    srow = lax.broadcasted_iota(I32, (SEL_TILE, n), 0)
    s = jnp.where(srow <= qpos - last * SEL_TILE, s, NEG)
    _, l = _softmax_step(s, c, None, vsT_ref[0, last], m, l, acc_ref)
    o_s = acc_ref[...] / jnp.maximum(l, 1e-30)

    lo = jnp.maximum(t0 - (WINDOW - 1) - kw_base, 0) // KV_TILE
    hi = jnp.minimum((t0 + span - 1 - kw_base) // KV_TILE + 1, kw_ref.shape[1])
    wrow = lax.broadcasted_iota(I32, (KV_TILE, n), 0)
    acc_ref[...] = jnp.zeros(acc_ref.shape, F32)

    def win_body(t, carry):
        s = jnp.dot(kw_ref[0, t], qa, preferred_element_type=F32)
        base = kw_base + t * KV_TILE
        dist = qpos - base - wrow
        s = jnp.where((dist >= 0) & (dist < WINDOW), s, NEG)
        return _softmax_step(s, slope * (base - qpos).astype(F32), None, vwT_ref[0, t],
                             *carry, acc_ref)

    _, l = lax.fori_loop(lo, hi, win_body, _softmax_init(n))
    o_w = acc_ref[...] / jnp.maximum(l, 1e-30)

    _store_heads(o_ref, gt[0:1, :] * o_c + gt[1:2, :] * o_s + gt[2:3, :] * o_w, mixed)


def _taps_matrix(n_sel, n_cmp):
    m = np.zeros((n_sel, n_cmp), np.float32)
    for j in range(n_sel):
        for off, w in ((-1, 1.0), (0, 2.0), (1, 2.0), (2, 2.0), (3, 1.0)):
            c = 4 * j + off
            if 0 <= c < n_cmp:
                m[j, c] = w
    return jnp.asarray(m)


def _group_matrix(cols):
    grp = cols.head_group(N_B_HEADS)
    m = np.zeros((SAMPLE_COLS, SAMPLE_COLS), np.float32)
    m[:N_B_HEADS, :N_B_HEADS] = grp[:, None] == grp[None, :]
    return jnp.asarray(m)


def _nsa(cols, q, q_row, meta, kc, vcT, ks, vsT, kw, vwT, kw_base):
    nb, nq = cols.nb, cols.nq
    n = SAMPLE_COLS if cols.sample else N_B_HEADS * cols.tq
    nTs, nTw = ks.shape[1], kw.shape[1]
    n_cmp = kc.shape[1]
    n_sel = n_cmp // 4
    n_tab = n_sel + SEL_TILE // SEL_BLOCK
    nsel_cols = n if cols.sample else N_KV * cols.tq
    q_spec, o_spec, o_shape, vrows = _query_specs(cols, N_B_HEADS, q_row, meta.shape[1])
    col = lambda b, i: (0, b * nq + i)
    seq3 = lambda b, i: (b, 0, 0)
    seq4 = lambda b, i: (b, 0, 0, 0)
    return pl.pallas_call(
        functools.partial(_nsa_kernel, span=cols.span, q_off=cols.q_off, kw_base=kw_base, mixed=cols.sample),
        grid=(nb, nq),
        in_specs=[q_spec,
                  pl.BlockSpec((META_ROWS, n), col),
                  pl.BlockSpec((1, n_cmp, N_KV * HEAD_DIM), seq3),
                  pl.BlockSpec((1, N_KV * HEAD_DIM, n_cmp), seq3),
                  pl.BlockSpec((n_sel, n_cmp), lambda b, i: (0, 0)),
                  pl.BlockSpec((SAMPLE_COLS, SAMPLE_COLS), lambda b, i: (0, 0)),
                  pl.BlockSpec((1, nTs, SEL_TILE, KD), seq4),
                  pl.BlockSpec((1, nTs, N_KV * HEAD_DIM, SEL_TILE), seq4),
                  pl.BlockSpec((1, nTw, KV_TILE, KD), seq4),
                  pl.BlockSpec((1, nTw, N_KV * HEAD_DIM, KV_TILE), seq4)],
        out_specs=o_spec, out_shape=o_shape,
        scratch_shapes=[pltpu.VMEM((n_tab, nsel_cols), F32), pltpu.VMEM((_group(cols.sample), KD, n), BF16),
                        pltpu.VMEM((vrows, n), F32), pltpu.VMEM((KD, n), BF16)],
        compiler_params=_cparams(("parallel", "parallel")), name="nsa")(
            q, meta, kc, vcT, _taps_matrix(n_sel, n_cmp), _group_matrix(cols), ks, vsT, kw, vwT)


def _dsa_kernel(qi_ref, wi_ref, ki_ref, qa_ref, meta_ref, k_ref, vT_ref, o_ref, key_ref, row_ref, acc_ref, qa_s,
                *, span, q_off, topk, mixed):
    i = pl.program_id(1)
    n = meta_ref.shape[1]
    nq = qi_ref.shape[1]
    rep = n // nq
    t0 = q_off + i * span
    n_kv = (t0 + span - 1) // KV_TILE + 1
    krow = lax.broadcasted_iota(I32, (KV_TILE, nq), 0)

    def search(count, shape):
        kf = float(topk)
        thr = jnp.where(count(lambda key, kpos: key >= 0) >= kf, 0, INT_MIN).astype(I32)

        def bit_body(j, thr):
            cand = thr | (1 << (30 - j))
            return jnp.where(count(lambda key, kpos: key >= cand) >= kf, cand, thr)
        thr = lax.fori_loop(0, 31, bit_body, thr)
        need = kf - count(lambda key, kpos: key > thr)

        def tie_search():
            def idx_body(j, jb):
                cand = jb | (1 << (14 - j))
                c = count(lambda key, kpos: (key == thr) & (kpos < cand))
                return jnp.where(c <= need, cand, jb)
            return lax.fori_loop(0, 15, idx_body, jnp.zeros(shape, I32))

        n_ge = count(lambda key, kpos: key >= thr)
        jb = lax.cond(jnp.max(n_ge) > kf, tie_search, lambda: jnp.full(shape, (1 << 15) - 1, I32))
        return thr, jb

    def select():
        qpos = t0 + (lax.broadcasted_iota(I32, (1, nq), 1) if span > 1 else jnp.zeros((1, nq), I32))
        w = wi_ref[...] * (IDX_SCALE * N_IDX_HEADS ** -0.5)
        if mixed:
            qi = qi_ref[...].astype(BF16)
        else:
            qi = [qi_ref[a * IDX_DIM:(a + 1) * IDX_DIM, :].astype(BF16) for a in range(N_IDX_HEADS)]
        if mixed:
            row_ref[...] = jnp.full(row_ref.shape, INT_MIN, I32)

        def score_body(t, _):
            ki = ki_ref[0, t]
            if mixed:
                rel = jnp.dot(ki, qi, preferred_element_type=F32)
                acc = jnp.sum(jnp.maximum(rel, 0.0) * w[0:1, :], axis=1, keepdims=True)
                acc = jnp.broadcast_to(acc, (KV_TILE, nq))
            else:
                acc = jnp.zeros((KV_TILE, nq), F32)
                for a in range(N_IDX_HEADS):
                    rel = jnp.dot(ki, qi[a], preferred_element_type=F32)
                    acc = acc + jnp.maximum(rel, 0.0) * w[a:a + 1, :]
            sc = jnp.where(t * KV_TILE + krow <= qpos, acc, NEG)
            bits = lax.bitcast_convert_type(sc, I32)
            key = bits ^ ((bits >> 31) & 0x7FFFFFFF)
            key_ref[t] = key
            if mixed:
                row_ref[pl.ds(t, 1), :] = jnp.transpose(key)[0:1, :]
            return 0
        lax.fori_loop(0, n_kv, score_body, 0)

        if mixed:
            keys = row_ref[...]
            kpos_r = (lax.broadcasted_iota(I32, keys.shape, 0) * KV_TILE + lax.broadcasted_iota(I32, keys.shape, 1))

            def count(pred):
                c = jnp.sum(jnp.where(pred(keys, kpos_r), 1.0, 0.0), axis=0, keepdims=True)
                return jnp.sum(c, axis=1, keepdims=True)
            thr, jb = search(count, (1, 1))
        else:
            n_grp = (n_kv + COUNT_GROUP - 1) // COUNT_GROUP

            def pad_body(t, _):
                key_ref[t] = jnp.full((KV_TILE, nq), INT_MIN, I32)
                return 0
            lax.fori_loop(n_kv, n_grp * COUNT_GROUP, pad_body, 0)

            def count(pred):
                def body(u, c):
                    for j in range(COUNT_GROUP):
                        t = u * COUNT_GROUP + j
                        hit = pred(key_ref[t], t * KV_TILE + krow)
                        c = c + jnp.sum(jnp.where(hit, 1.0, 0.0), axis=0, keepdims=True)
                    return c
                return lax.fori_loop(0, n_grp, body, jnp.zeros((1, nq), F32))
            thr, jb = search(count, (1, nq))

        def bias_body(t, _):
            key = key_ref[t]
            kpos = t * KV_TILE + krow
            keep = ((key > thr) | ((key == thr) & (kpos < jb))) & (kpos <= qpos)
            key_ref[t] = lax.bitcast_convert_type(jnp.where(keep, 0.0, NEG), I32)
            return 0
        lax.fori_loop(0, n_kv, bias_body, 0)

    select()
    slope = meta_ref[0:1, :]
    qpos_c = meta_ref[1:2, :].astype(I32)
    qa = qa_ref[...] if mixed else _step_queries(qa_ref, meta_ref, qa_s)[0]
    if mixed:
        n_t = q_off // KV_TILE + 1
        bias = lax.bitcast_convert_type(key_ref[0:n_t], F32).reshape(n_t * KV_TILE, nq)
        o_ref[...] = _dense_attend(k_ref, n_t, vT_ref, n_t, qa, slope, qpos_c, KV_TILE, bias=bias)
        return
    acc_ref[...] = jnp.zeros(acc_ref.shape, F32)

    def step(t, slot, ok, carry):
        bias = lax.bitcast_convert_type(key_ref[t], F32)
        if rep > 1:
            bias = jnp.concatenate([bias] * rep, axis=1)
        s = jnp.dot(k_ref[0, t], qa, preferred_element_type=F32) + bias
        c = slope * (t * KV_TILE - qpos_c).astype(F32)
        return _softmax_step(s, c, ok, vT_ref[0, t], *carry, acc_ref)
    _, l = _tile_groups(n_kv, step, _softmax_init(n), _group(mixed))
    _store_heads(o_ref, acc_ref[...] / jnp.maximum(l, 1e-30), mixed)


def _dsa(cols, qi, qi_row, wi, wi_row, ki, q, q_row, meta, k, vT, topk):
    nb, nq = cols.nb, cols.nq
    n = SAMPLE_COLS if cols.sample else N_C_HEADS * cols.tq
    ncol = SAMPLE_COLS if cols.sample else cols.tq
    nT = k.shape[1]
    assert nT % COUNT_GROUP == 0
    q_spec, o_spec, o_shape, vrows = _query_specs(cols, N_C_HEADS, q_row, meta.shape[1])
    qi_rows = IDX_DIM if cols.sample else N_IDX_HEADS * IDX_DIM
    qi_b, wi_b = qi_row // qi_rows, wi_row // N_IDX_HEADS
    col = lambda b, i: (0, b * nq + i)
    seq4 = lambda b, i: (b, 0, 0, 0)
    return pl.pallas_call(
        functools.partial(_dsa_kernel, span=cols.span, q_off=cols.q_off, topk=topk, mixed=cols.sample),
        grid=(nb, nq),
        in_specs=[pl.BlockSpec((qi_rows, ncol), lambda b, i: (qi_b, b * nq + i)),
                  pl.BlockSpec((N_IDX_HEADS, ncol), lambda b, i: (wi_b, b * nq + i)),
                  pl.BlockSpec((1, nT, KV_TILE, IDX_DIM), seq4),
                  q_spec,
                  pl.BlockSpec((META_ROWS, n), col),
                  pl.BlockSpec((1, nT, KV_TILE, KD), seq4),
                  pl.BlockSpec((1, nT, N_KV * HEAD_DIM, KV_TILE), seq4)],
        out_specs=o_spec, out_shape=o_shape,
        scratch_shapes=[pltpu.VMEM((nT, KV_TILE, ncol), I32), pltpu.VMEM((-(-nT // 8) * 8, KV_TILE), I32),
                        pltpu.VMEM((vrows, n), F32), pltpu.VMEM((KD, n), BF16)],
        compiler_params=_cparams(("parallel", "parallel")), name="dsa")(
            qi, wi, ki, q, meta, k, vT)


def _mem_kernel(q_ref, mk_ref, mvT_ref, o_ref):
    for h in range(N_MEM_HEADS):
        rows = slice(h * HEAD_DIM, (h + 1) * HEAD_DIM)
        q = (q_ref[rows, :] * ATTN_SCALE).astype(BF16)
        s = jnp.dot(mk_ref[0, h].astype(BF16), q, preferred_element_type=F32)
        p = jnp.exp(s - jnp.max(s, axis=0, keepdims=True))
        l = jnp.sum(p, axis=0, keepdims=True)
        o_ref[rows, :] = jnp.dot(mvT_ref[0, h].astype(BF16), p.astype(BF16), preferred_element_type=F32) / l


def _mem(hT, q_row, mk, mvT, nb, nq, tq):
    n_mem = mk.shape[2]
    n = N_MEM_HEADS * HEAD_DIM
    qb = q_row // n
    return pl.pallas_call(
        _mem_kernel, grid=(nb, nq),
        in_specs=[pl.BlockSpec((n, tq), lambda b, i: (qb, b * nq + i)),
                  pl.BlockSpec((1, N_MEM_HEADS, n_mem, HEAD_DIM), lambda b, i: (b, 0, 0, 0)),
                  pl.BlockSpec((1, N_MEM_HEADS, HEAD_DIM, n_mem), lambda b, i: (b, 0, 0, 0))],
        out_specs=pl.BlockSpec((n, tq), lambda b, i: (0, b * nq + i)),
        out_shape=jax.ShapeDtypeStruct((n, nb * nq * tq), F32),
        compiler_params=_cparams(("parallel", "parallel")), name="mem")(hT, mk, mvT)


def _mem_branch(cols, hT, q_row, mk, mvT):
    n = N_MEM_HEADS * HEAD_DIM
    if cols.sample:
        o = _mem(cols.tile_cols(hT[q_row:q_row + n]), 0, mk, mvT, cols.nb, 1, SAMPLE_COLS)
        return cols.from_tile_cols(o)
    return _mem(hT, q_row, mk, mvT, cols.nb, cols.nq, cols.tq)


def _even_wT(w_in):
    sizes = (384, 128, 128, 384, 384, 128, 128, 128, 128, 128, 128, 18, 384, 256, 256)
    offs = np.cumsum((0,) + sizes)
    (a_q, a_k, a_v, a_z, b_q, b_kc, b_vc, b_ks, b_vs, b_kw, b_vw, b_g, b_z, m_q, m_z) = [
        w_in[:, offs[n]:offs[n + 1]] for n in range(len(sizes))]
    d = w_in.shape[0]
    bg = jnp.pad(b_g, ((0, 0), (0, 32 - b_g.shape[1])))
    cols = [a_q, b_q, m_q, a_z, b_z, m_z, a_k, a_v, b_kc, b_vc, b_ks, b_vs, b_kw, b_vw, bg]
    w = jnp.concatenate(cols, axis=1)
    w = jnp.pad(w, ((0, 0), (0, EVEN_TOTAL - w.shape[1])))
    return w.T


def _odd_wT(w_in):
    sizes = (768, 128, 128, 512, 64, 8, 768, 256, 256)
    offs = np.cumsum((0,) + sizes)
    (c_q, c_k, c_v, c_qi, c_ki, c_wi, c_z, m_q, m_z) = [w_in[:, offs[n]:offs[n + 1]] for n in range(len(sizes))]
    w = jnp.concatenate([c_q, m_q, c_z, m_z, c_qi, c_k, c_v, c_ki, c_wi], axis=1)
    w = jnp.pad(w, ((0, 0), (0, ODD_TOTAL - w.shape[1])))
    return w.T


def _keys(cols, new_tok, pool, page_table, tile, mode):
    if cols.sample:
        return _gather(pool, page_table, new_tok, mode, tile)
    return _kprep(new_tok, tile, mode)


def _values(cols, new_tok, pool, page_table, tile):
    if cols.sample:
        return _gather(pool, page_table, new_tok, 'v', tile)
    return _vT_tiles(new_tok, tile)


def _raw_rows(cols, new_tok, pool, page_table):
    if cols.sample:
        return _gather(pool, page_table, None, 'raw', KV_TILE)
    return new_tok


def _queries(cols, hT, row, heads, slopes, extra=()):
    meta = cols.meta(heads, slopes, extra)
    if cols.sample:
        qa, qf = cols.columns(hT[row:row + heads * HEAD_DIM], heads, meta)
        return qa, 0, qf, meta
    return hT, row, meta, meta


def _heads_out(cols, o, heads):
    return cols.collapse(o, heads) if cols.sample else o


def _gate_cols(cols, gT):
    return [cols.per_token(gT, N_B_HEADS)]


def _even_layer(cols, xT, mk, mvT, past, page_table, wT, woT, g, b, cmpw, alpha):
    R = EVEN_ROWS
    nb = cols.nb
    hT = _mm(wT, xT, 640, min(1024, xT.shape[1]))
    names = ('a_k', 'a_v', 'b_kc', 'b_vc', 'b_ks', 'b_vs')
    new = [cols.tokens(hT[R[nm]:R[nm] + 128]) for nm in names]
    kw_new, vw_new = cols.tokens(hT[R['b_kw']:R['b_kw'] + 128]), cols.tokens(hT[R['b_vw']:R['b_vw'] + 128])
    pools = past[:6] if cols.sample else [None] * 6
    if cols.sample:
        kw = jnp.concatenate([past[6].reshape(nb, -1, 128), kw_new], axis=1)
        vw = jnp.concatenate([past[7].reshape(nb, -1, 128), vw_new], axis=1)
        n_win = past[6].shape[1]
        kw_base = cols.q_off - n_win
        win_out = (kw[:, -n_win:], vw[:, -n_win:])
        pad = -kw.shape[1] % 1024
        kw = jnp.pad(kw, ((0, 0), (0, pad), (0, 0)))
        vw = jnp.pad(vw, ((0, 0), (0, pad), (0, 0)))
    else:
        kw, vw, kw_base = kw_new, vw_new, 0
        n_win = min(WINDOW, cols.t)
        win_out = (kw[:, -n_win:], vw[:, -n_win:])

    sl = _alibi(N_A_HEADS + N_B_HEADS)
    ak, amean = _keys(cols, new[0], pools[0], page_table, KV_TILE, 'k')
    avT = _values(cols, new[1], pools[1], page_table, KV_TILE)
    q, q_row, qf, meta = _queries(cols, hT, R['a_q'], N_A_HEADS, sl[0::2])
    o_a = _moba(cols, q, q_row, qf, meta, amean, ak, avT)

    pe2k, w1k, w2k = cmpw[0]
    pe2v, w1v, w2v = cmpw[1]
    n_cmp = max(cols.q_off, cols.t) // CMP_STRIDE
    kc_raw = _raw_rows(cols, new[2], pools[2], page_table)
    vc_raw = _raw_rows(cols, new[3], pools[3], page_table)
    kc = _compress(kc_raw, pe2k, w1k, w2k, n_cmp).astype(BF16)
    vcT = _compress(vc_raw, pe2v, w1v, w2v, n_cmp).transpose(0, 2, 1).astype(BF16)
    ks = _keys(cols, new[4], pools[4], page_table, SEL_TILE, 'ks')
    vsT = _values(cols, new[5], pools[5], page_table, SEL_TILE)
    kwt = _kprep(kw, KV_TILE, 'ks')
    q, q_row, _, meta = _queries(cols, hT, R['b_q'], N_B_HEADS, sl[1::2],
                               extra=_gate_cols(cols, hT[R['b_g']:R['b_g'] + 18]))
    o_b = _nsa(cols, q, q_row, meta, kc, vcT, ks, vsT, kwt, _vT_tiles(vw, KV_TILE), kw_base)

    o_m = _mem_branch(cols, hT, R['m_q'], mk, mvT)
    parts = [_heads_out(cols, o_a, N_A_HEADS), _heads_out(cols, o_b, N_B_HEADS), o_m]
    yT = _outln(parts, hT, R['z'], xT, woT, g, b, alpha, min(512, xT.shape[1]))
    state = [n.reshape(nb, -1, N_KV, HEAD_DIM) for n in new]
    state += [w.reshape(nb, -1, N_KV, HEAD_DIM) for w in win_out]
    return yT, state


def _odd_layer(cols, xT, mk, mvT, past, page_table, wT, woT, g, b, alpha):
    R = ODD_ROWS
    nb = cols.nb
    hT = _mm(wT, xT, 768, min(1024, xT.shape[1]))
    new = [cols.tokens(hT[R['c_k']:R['c_k'] + 128]), cols.tokens(hT[R['c_v']:R['c_v'] + 128]),
           cols.tokens(hT[R['c_ki']:R['c_ki'] + IDX_DIM])]
    pools = past if cols.sample else [None] * 3
    topk = min(DSA_TOPK, (cols.q_off + cols.t) // 4)
    ck = _keys(cols, new[0], pools[0], page_table, KV_TILE, 'ks')
    cvT = _values(cols, new[1], pools[1], page_table, KV_TILE)
    if cols.sample:
        ki = _gather(pools[2], page_table, new[2], 'i', KV_TILE)
    else:
        ki = new[2].reshape(nb, -1, KV_TILE, IDX_DIM).astype(BF16)
    q, q_row, _, meta = _queries(cols, hT, R['c_q'], N_C_HEADS, _alibi(N_C_HEADS))
    if cols.sample:
        qi = cols.per_token(hT[R['c_qi']:R['c_qi'] + 512], N_IDX_HEADS)
        wi = jnp.pad(cols.per_token(hT[R['c_wi']:R['c_wi'] + 8], N_IDX_HEADS), ((0, 7), (0, 0)))
        o_c = _dsa(cols, qi, 0, wi, 0, ki, q, q_row, meta, ck, cvT, topk)
    else:
        o_c = _dsa(cols, hT, R['c_qi'], hT, R['c_wi'], ki, q, q_row, meta, ck, cvT, topk)
    o_m = _mem_branch(cols, hT, R['m_q'], mk, mvT)
    parts = [_heads_out(cols, o_c, N_C_HEADS), o_m]
    yT = _outln(parts, hT, R['z'], xT, woT, g, b, alpha, min(512, xT.shape[1]))
    state = [new[0].reshape(nb, -1, N_KV, HEAD_DIM), new[1].reshape(nb, -1, N_KV, HEAD_DIM), new[2]]
    return yT, state


def kernel(x_prompt, x_sample, cache_a_k, cache_a_v, cache_b_cmp_k, cache_b_cmp_v, cache_b_sel_k, cache_b_sel_v,
           state_b_win_k, state_b_win_v, cache_c_k, cache_c_v, cache_c_idx_k, cache_mem_k, cache_mem_v,
           page_table, mem_prompt, w_in_even, w_out_even, w_in_odd, w_out_odd, w_mem_kv, ln_g, ln_b,
           cmp_pe, cmp_w1, cmp_w2):
    bp, seq, d = x_prompt.shape
    bs = x_sample.shape[0]
    depth = w_mem_kv.shape[0]
    n_mem = mem_prompt.shape[1]
    past_len = page_table.shape[1] * cache_a_k.shape[2]
    alpha = (2.0 * depth) ** 0.25
    assert x_sample.shape[1] == 1 and bs <= 128 and seq % 1024 == 0 and past_len % 1024 == 0

    gp = _Cols(bp, seq, 0)
    gs = _Cols(bs, 1, past_len)
    xp = x_prompt.reshape(bp * seq, d).T
    xs = jnp.pad(x_sample.reshape(bs, d).T, ((0, 0), (0, 128 - bs)))
    n_pool = cache_a_k.shape[1]

    def pages(c):
        return jnp.moveaxis(c.reshape(c.shape[:3] + (-1,)), 2, 3).reshape(c.shape[0] * n_pool, -1, c.shape[2])

    even_paged = tuple(pages(c) for c in (cache_a_k, cache_a_v, cache_b_cmp_k, cache_b_cmp_v, cache_b_sel_k,
                                          cache_b_sel_v))
    odd_paged = tuple(pages(c) for c in (cache_c_k, cache_c_v, cache_c_idx_k))
    ev_p, ev_s, od_p, od_s, mk_list, mv_list = [], [], [], [], [], []
    for layer in range(depth):
        j = layer // 2
        mem_kv = _mm(mem_prompt.reshape(bp * n_mem, d), w_mem_kv[layer], bp * n_mem, 2 * N_MEM_HEADS * HEAD_DIM)
        mem_kv = mem_kv.reshape(bp, n_mem, 2, N_MEM_HEADS, HEAD_DIM)
        mk_p, mv_p = mem_kv[:, :, 0], mem_kv[:, :, 1]
        mk_list.append(mk_p)
        mv_list.append(mv_p)
        mem_p = (mk_p.transpose(0, 2, 1, 3), mv_p.transpose(0, 2, 3, 1))
        mem_s = (cache_mem_k[layer].transpose(0, 2, 1, 3), cache_mem_v[layer].transpose(0, 2, 3, 1))
        g = ln_g[layer].reshape(d, 1)
        b = ln_b[layer].reshape(d, 1)
        if layer % 2 == 0:
            wT = _even_wT(w_in_even[j])
            woT = w_out_even[j].T
            cmpw = [_cmp_weights(cmp_pe[j, c], cmp_w1[j, c], cmp_w2[j, c]) for c in range(2)]
            past = even_paged + (state_b_win_k[j], state_b_win_v[j])
            xp, st_p = _even_layer(gp, xp, *mem_p, None, None, wT.astype(BF16), woT.astype(BF16), g, b, cmpw, alpha)
            xs, st_s = _even_layer(gs, xs, *mem_s, past, page_table + j * n_pool, wT, woT, g, b, cmpw, alpha)
            ev_p.append(st_p)
            ev_s.append(st_s)
        else:
            wT = _odd_wT(w_in_odd[j])
            woT = w_out_odd[j].T
            past = odd_paged
            xp, st_p = _odd_layer(gp, xp, *mem_p, None, None, wT.astype(BF16), woT.astype(BF16), g, b, alpha)
            xs, st_s = _odd_layer(gs, xs, *mem_s, past, page_table + j * n_pool, wT, woT, g, b, alpha)
            od_p.append(st_p)
            od_s.append(st_s)
    y_p = xp.T.reshape(bp, seq, d)
    y_s = xs[:, :bs].T.reshape(bs, 1, d)
    ev_p = [jnp.stack(l) for l in zip(*ev_p)]
    ev_s = [jnp.stack(l) for l in zip(*ev_s)]
    od_p = [jnp.stack(l) for l in zip(*od_p)]
    od_s = [jnp.stack(l) for l in zip(*od_s)]
    out = [y_p, y_s]
    for p, s in zip(ev_p, ev_s):
        out += [p, s]
    for p, s in zip(od_p, od_s):
        out += [p, s]
    out += [jnp.stack(mk_list), jnp.stack(mv_list)]
    return tuple(out)
```

```python
import functools

import numpy as np
import jax
import jax.numpy as jnp
from jax import lax
from jax.experimental import pallas as pl
from jax.experimental.pallas import tpu as pltpu

F32 = jnp.float32
BF16 = jnp.bfloat16
I32 = jnp.int32

HEAD_DIM = 64
N_MEM_HEADS = 4
N_A_HEADS = 6
N_B_HEADS = 6
N_C_HEADS = 12
N_KV = 2
N_IDX_HEADS = 8
IDX_DIM = 64
MOBA_BLOCK = 256
MOBA_TOPK = 3
CMP_STRIDE = 16
CMP_LEN = 32
CMP_HIDDEN = 128
SEL_BLOCK = 64
SEL_TOPK = 15
WINDOW = 512
DSA_TOPK = 256
ATTN_SCALE = HEAD_DIM ** -0.5
IDX_SCALE = IDX_DIM ** -0.5
LN_EPS = 1e-5
LOG2E = 1.4426950408889634
NEG = -1e30
SENTINEL = -3e38
INT_MIN = -(2 ** 31)

KV_TILE = 256
SEL_TILE = 512
KD = 256
FEAT_POS = 128
FEAT_SEL = 144
COUNT_GROUP = 4
SAMPLE_COLS = 128
META_ROWS = 16
META_PIECES = 8
PAGES_PER_STEP = 8
GATHER_PAGES = 16
GATHER_PAGES_RAW = 32
VMEM_LIMIT = 56 * 1024 * 1024

EVEN_ROWS = dict(a_q=0, b_q=384, m_q=768, z=1024, a_k=2048, a_v=2176, b_kc=2304, b_vc=2432,
                 b_ks=2560, b_vs=2688, b_kw=2816, b_vw=2944, b_g=3072)
EVEN_TOTAL = 3200
ODD_ROWS = dict(c_q=0, m_q=768, z=1024, c_qi=2048, c_k=2560, c_v=2688, c_ki=2816, c_wi=2880)
ODD_TOTAL = 3072


def _cparams(sem):
    return pltpu.CompilerParams(dimension_semantics=sem, vmem_limit_bytes=VMEM_LIMIT)


def _alibi(n):
    return 2.0 ** (-8.0 * np.arange(1, n + 1) / n)


def _mm_kernel(a_ref, b_ref, o_ref, *, precise):
    if precise:
        o_ref[...] = jnp.dot(a_ref[...], b_ref[...], precision=lax.Precision.HIGHEST, preferred_element_type=F32)
    else:
        o_ref[...] = jnp.dot(a_ref[...].astype(BF16), b_ref[...].astype(BF16), preferred_element_type=F32)


def _mm(a, b, bm, bn):
    m, k = a.shape
    n = b.shape[1]
    return pl.pallas_call(
        functools.partial(_mm_kernel, precise=a.dtype == F32 and b.dtype == F32 and n <= 128), grid=(n // bn, m // bm),
        in_specs=[pl.BlockSpec((bm, k), lambda j, i: (i, 0)), pl.BlockSpec((k, bn), lambda j, i: (0, j))],
        out_specs=pl.BlockSpec((bm, bn), lambda j, i: (i, j)),
        out_shape=jax.ShapeDtypeStruct((m, n), F32),
        compiler_params=_cparams(("parallel", "parallel")), name="mm")(a, b)


def _outln_kernel(*refs, n_parts, alpha):
    o_refs = refs[:n_parts]
    z_ref, x_ref, w_ref, g_ref, b_ref, y_ref = refs[n_parts:]
    o = jnp.concatenate([r[...] for r in o_refs], axis=0)
    z = z_ref[...]
    mixed = o * (z / (1.0 + jnp.exp(-z)))
    if w_ref.dtype == F32:
        y = jnp.dot(w_ref[...], mixed, precision=lax.Precision.HIGHEST, preferred_element_type=F32)
    else:
        y = jnp.dot(w_ref[...], mixed.astype(BF16), preferred_element_type=F32)
    y = alpha * x_ref[...] + y
    mu = jnp.mean(y, axis=0, keepdims=True)
    d = y - mu
    var = jnp.mean(d * d, axis=0, keepdims=True)
    y_ref[...] = d * lax.rsqrt(var + LN_EPS) * g_ref[...] + b_ref[...]


def _outln(o_parts, hT, z_row, xT, woT, g, b, alpha, bn):
    d, n = xT.shape
    zb = z_row // d
    in_specs = [pl.BlockSpec((p.shape[0], bn), lambda j: (0, j)) for p in o_parts]
    in_specs += [pl.BlockSpec((d, bn), lambda j: (zb, j)),
                 pl.BlockSpec((d, bn), lambda j: (0, j)),
                 pl.BlockSpec((d, d), lambda j: (0, 0)),
                 pl.BlockSpec((d, 1), lambda j: (0, 0)),
                 pl.BlockSpec((d, 1), lambda j: (0, 0))]
    return pl.pallas_call(
        functools.partial(_outln_kernel, n_parts=len(o_parts), alpha=alpha), grid=(n // bn,),
        in_specs=in_specs, out_specs=pl.BlockSpec((d, bn), lambda j: (0, j)),
        out_shape=jax.ShapeDtypeStruct((d, n), F32),
        compiler_params=_cparams(("parallel",)), name="outln")(*o_parts, hT, xT, woT, g, b)


def _key_features(tile):
    r = lax.broadcasted_iota(I32, (tile, 128), 0)
    c = lax.broadcasted_iota(I32, (tile, 128), 1)
    sel_c = c - (FEAT_SEL - FEAT_POS)
    feat = jnp.where(c < 3, r >> 4, jnp.where(c < 6, r & 15, jnp.where(sel_c == (r >> 6), 1, 0)))
    return feat.astype(F32).astype(BF16)


def _emit_tiles(x, outs, mode, tile, t0=0):
    nt = x.shape[0] // tile
    if mode == 'raw':
        outs[0][0, t0 * tile:(t0 + nt) * tile, :] = x
        return
    feat = _key_features(tile) if mode in ('k', 'ks') else None
    for t in range(nt):
        xt = x[t * tile:(t + 1) * tile]
        if mode in ('k', 'ks'):
            outs[0][0, t0 + t] = jnp.concatenate([xt.astype(BF16), feat], axis=1)
        elif mode == 'v':
            outs[0][0, t0 + t] = jnp.transpose(xt).astype(BF16)
        else:
            outs[0][0, t0 + t] = xt.astype(BF16)
    if mode == 'k':
        nblk = x.shape[0] // MOBA_BLOCK
        outs[1][0, 0, t0:t0 + nblk, :] = jnp.mean(x.reshape(nblk, MOBA_BLOCK, x.shape[1]), axis=1)


def _zero_tiles(outs, mode, tile, t0, t1):
    for t in range(t0, t1):
        outs[0][0, t] = jnp.zeros(outs[0].shape[2:], outs[0].dtype)
    if mode == 'k':
        outs[1][0, 0, t0:t1, :] = jnp.zeros((t1 - t0, outs[1].shape[3]), F32)


def _tile_outputs(mode, nb, n_steps, rows, tile, c):
    nt = rows // tile
    if mode == 'raw':
        return ([pl.BlockSpec((1, rows, c), lambda b, j, *_: (b, j, 0))],
                [jax.ShapeDtypeStruct((nb, n_steps * rows, c), F32)])
    if mode in ('k', 'ks'):
        shape, blk = (nb, n_steps * nt, tile, KD), (1, nt, tile, KD)
    elif mode == 'v':
        shape, blk = (nb, n_steps * nt, c, tile), (1, nt, c, tile)
    else:
        shape, blk = (nb, n_steps * nt, tile, c), (1, nt, tile, c)
    specs = [pl.BlockSpec(blk, lambda b, j, *_: (b, j, 0, 0))]
    shapes = [jax.ShapeDtypeStruct(shape, BF16)]
    if mode == 'k':
        nblk = rows // MOBA_BLOCK
        specs.append(pl.BlockSpec((1, 1, nblk, c), lambda b, j, *_: (b, j, 0, 0)))
        shapes.append(jax.ShapeDtypeStruct((nb, n_steps, nblk, c), F32))
    return specs, shapes


def _gather_kernel(pt_ref, *refs, npg, mode, tile, last, has_tail):
    pool_refs = refs[:npg]
    outs = refs[npg + (1 if has_tail else 0):]
    j = pl.program_id(1)

    def token_major(p):
        c = p.shape[0]
        if c == p.shape[1]:
            return jnp.transpose(p)
        return jnp.transpose(jnp.concatenate([p, jnp.zeros((p.shape[1] - c, p.shape[1]), p.dtype)], axis=0))[:, :c]

    def pages():
        if mode == 'v':
            xT = jnp.concatenate([r[0] for r in pool_refs], axis=1)
            for t in range(xT.shape[1] // tile):
                outs[0][0, t] = xT[:, t * tile:(t + 1) * tile].astype(BF16)
        else:
            _emit_tiles(jnp.concatenate([token_major(r[0]) for r in pool_refs], axis=0), outs, mode, tile)

    if not has_tail:
        pages()
        return
    pl.when(j < last)(pages)

    @pl.when(j == last)
    def _():
        _emit_tiles(refs[npg][0], outs, mode, tile)
        if mode != 'raw':
            _zero_tiles(outs, mode, tile, 1, outs[0].shape[1])


def _gather(pool, page_table, new_tok, mode, tile):
    nb, n_pages = page_table.shape
    n_pool, c, page_tokens = pool.shape
    npg = GATHER_PAGES_RAW if mode == 'raw' else GATHER_PAGES
    npg = npg if n_pages % npg == 0 else PAGES_PER_STEP
    last = n_pages // npg
    has_tail = new_tok is not None
    rows = npg * page_tokens
    assert rows % tile == 0 and (has_tail or mode == 'raw')

    def pool_spec(k):
        return pl.BlockSpec((1, c, page_tokens),
                            lambda b, j, pt: (pt[b, jnp.minimum(j * npg + k, n_pages - 1)], 0, 0))

    in_specs = [pool_spec(k) for k in range(npg)]
    args = [pool] * npg
    if has_tail:
        tail = jnp.zeros((nb, tile, c), F32).at[:, 0:1, :].set(new_tok)
        in_specs.append(pl.BlockSpec((1, tile, c), lambda b, j, pt: (b, 0, 0)))
        args.append(tail)
    n_steps = last + (1 if has_tail else 0)
    out_specs, out_shape = _tile_outputs(mode, nb, n_steps, rows, tile, c)
    grid_spec = pltpu.PrefetchScalarGridSpec(num_scalar_prefetch=1, grid=(nb, n_steps), in_specs=in_specs,
                                             out_specs=out_specs)
    out = pl.pallas_call(
        functools.partial(_gather_kernel, npg=npg, mode=mode, tile=tile, last=last, has_tail=has_tail),
        grid_spec=grid_spec, out_shape=out_shape,
        compiler_params=_cparams(("parallel", "arbitrary")), name="gather_" + mode)(page_table, *args)
    if mode == 'k':
        return out[0], out[1].reshape(nb, -1, c)
    return out[0]


def _kprep_kernel(x_ref, *outs, mode, tile):
    _emit_tiles(x_ref[0], outs, mode, tile)


def _kprep(x, tile, mode='k'):
    nb, L, c = x.shape
    rows = 1024
    out_specs, out_shape = _tile_outputs(mode, nb, L // rows, rows, tile, c)
    out = pl.pallas_call(
        functools.partial(_kprep_kernel, mode=mode, tile=tile), grid=(nb, L // rows),
        in_specs=[pl.BlockSpec((1, rows, c), lambda b, j: (b, j, 0))],
        out_specs=out_specs, out_shape=out_shape,
        compiler_params=_cparams(("parallel", "parallel")), name="kprep")(x)
    if mode == 'k':
        return out[0], out[1].reshape(nb, -1, c)
    return out[0]


def _vT_tiles(v, tile):
    nb, L, c = v.shape
    return v.reshape(nb, L // tile, tile, c).transpose(0, 1, 3, 2).astype(BF16)


class _Cols:
    def __init__(self, nb, t, q_off):
        self.nb, self.t, self.q_off = nb, t, q_off
        self.sample = q_off > 0
        self.tq = 1 if self.sample else KV_TILE
        self.nq = 1 if self.sample else t // KV_TILE
        self.span = 1 if self.sample else KV_TILE

    def head_group(self, heads):
        return np.arange(heads) // (heads // N_KV)

    def per_head(self, vals, heads):
        vals = jnp.asarray(vals, F32)
        k = vals.shape[0]
        if self.sample:
            return jnp.tile(jnp.pad(vals, ((0, 0), (0, SAMPLE_COLS - heads))), (1, self.nb))
        hps = heads // N_KV
        shape = (k, self.nb, self.nq, N_KV, hps, self.tq)
        return jnp.broadcast_to(vals.reshape(k, 1, 1, N_KV, hps, 1), shape).reshape(k, -1)

    def per_token(self, rows, heads):
        nb, nq, tq = self.nb, self.nq, self.tq
        k = rows.shape[0] // heads
        if self.sample:
            x = rows[:, :nb].reshape(heads, k, nb).transpose(1, 2, 0)
            return jnp.pad(x, ((0, 0), (0, 0), (0, SAMPLE_COLS - heads))).reshape(k, nb * SAMPLE_COLS)
        x = rows.reshape(N_KV, heads // N_KV, k, nb, nq, tq).transpose(2, 3, 4, 0, 1, 5)
        return x.reshape(k, -1)

    def qpos_row(self, heads):
        if self.sample:
            return jnp.full((1, self.nb * SAMPLE_COLS), self.q_off, F32)
        pos = (jnp.arange(self.nq, dtype=F32)[:, None] * self.tq + jnp.arange(self.tq, dtype=F32)[None, :])
        shape = (1, self.nb, self.nq, N_KV, heads // N_KV, self.tq)
        return jnp.broadcast_to(pos.reshape(1, 1, self.nq, 1, 1, self.tq), shape).reshape(1, -1)

    def meta(self, heads, slopes, extra=()):
        s2 = np.asarray(slopes, np.float64) * LOG2E
        hi = s2.astype(jnp.bfloat16).astype(np.float64)
        mid = (s2 - hi).astype(jnp.bfloat16).astype(np.float64)
        lo = (s2 - hi - mid).astype(jnp.bfloat16).astype(np.float64)
        rows = [self.per_head((hi + mid + lo)[None, :], heads), self.qpos_row(heads)] + list(extra)
        n = rows[0].shape[1]
        rows.append(jnp.zeros((META_PIECES - sum(m.shape[0] for m in rows), n), F32))
        rows.append(self.per_head(np.stack([16 * hi, 16 * mid, 16 * lo, hi, mid, lo]), heads))
        rows.append(jnp.zeros((2, n), F32))
        return jnp.concatenate(rows, axis=0)

    def columns(self, rows, heads, meta):
        x = self.per_token(rows, heads)
        grp = self.head_group(heads)
        gsel = self.per_head(np.stack([grp == 0, grp == 1]).astype(np.float32), heads)
        qf = jnp.concatenate([x * gsel[0:1], x * gsel[1:2]], axis=0)
        qa = jnp.concatenate([(qf * (ATTN_SCALE * LOG2E)).astype(BF16), meta[META_PIECES:].astype(BF16),
                              jnp.zeros((8, x.shape[1]), BF16),
                              jnp.zeros((KD - FEAT_POS - 16, x.shape[1]), BF16)], axis=0)
        return qa, qf

    def collapse(self, o, heads):
        nb, nq, tq = self.nb, self.nq, self.tq
        if self.sample:
            o = o.reshape(N_KV, HEAD_DIM, nb, SAMPLE_COLS)
            grp = self.head_group(heads)
            x = jnp.stack([o[grp[h], :, :, h] for h in range(heads)])
            return jnp.pad(x.reshape(heads * HEAD_DIM, nb), ((0, 0), (0, 128 - nb)))
        o = o.reshape(HEAD_DIM, nb, nq, N_KV, heads // N_KV, tq).transpose(3, 4, 0, 1, 2, 5)
        return o.reshape(heads * HEAD_DIM, nb * nq * tq)

    def tokens(self, rows):
        n = rows.shape[0]
        if self.sample:
            return rows[:, :self.nb].T.reshape(self.nb, 1, n)
        return rows.reshape(n, self.nb, self.t).transpose(1, 2, 0)

    def tile_cols(self, rows):
        if not self.sample:
            return rows
        n = rows.shape[0]
        return jnp.broadcast_to(rows[:, :self.nb, None], (n, self.nb, SAMPLE_COLS)).reshape(n, self.nb * SAMPLE_COLS)

    def from_tile_cols(self, o):
        if not self.sample:
            return o
        n = o.shape[0]
        return jnp.pad(o.reshape(n, self.nb, SAMPLE_COLS)[:, :, 0], ((0, 0), (0, 128 - self.nb)))


def _softmax_step(s, c, ok, vT, m, l, acc_ref):
    smax = jnp.max(s, axis=0, keepdims=True)
    live = smax > 0.5 * NEG
    if ok is not None:
        live = live & _row_flag(ok, s.shape[1])
    m_new = jnp.where(live, jnp.maximum(m, smax + c), m)
    shift = jnp.where(live, m_new - c, -NEG)
    p = jnp.exp2(s - shift)
    alpha = jnp.exp2(m - m_new)
    l = alpha * l + jnp.sum(p, axis=0, keepdims=True)
    acc_ref[...] = alpha * acc_ref[...] + _pv(vT, p, acc_ref.shape[0])
    return m_new, l


def _pv(vT, p, rows):
    pb = p.astype(BF16)
    if vT.shape[0] == rows:
        return jnp.dot(vT, pb, preferred_element_type=F32)
    h = pb.shape[1] // N_KV
    return jnp.concatenate([jnp.dot(vT[g * HEAD_DIM:(g + 1) * HEAD_DIM], pb[:, g * h:(g + 1) * h],
                                    preferred_element_type=F32) for g in range(N_KV)], axis=1)


def _row_flag(ok, n):
    if ok.ndim == 0:
        return (jnp.zeros((1, n), I32) + ok.astype(I32)) > 0
    return ok


def _group(mixed):
    return 4 if mixed else 2


def _softmax_init(n):
    return jnp.full((1, n), NEG, F32), jnp.zeros((1, n), F32)


def _tile_groups(n_tiles, step, carry, group):
    def body(u, carry):
        carry = step(group * u, 0, None, carry)
        for j in range(1, group):
            t = group * u + j
            carry = step(jnp.minimum(t, n_tiles - 1), j, t < n_tiles, carry)
        return carry
    return lax.fori_loop(0, (n_tiles + group - 1) // group, body, carry)


def _dense_attend(k_ref, n_k, vT_ref, n_v, qa, slope, qpos, blk, pos0=0, mask_fn=None, bias=None):
    n = qa.shape[1]
    tile = k_ref.shape[2]
    rows = n_k * tile
    s = jnp.dot(k_ref[0, 0:n_k].reshape(rows, KD), qa, preferred_element_type=F32)
    if bias is not None:
        s = s + bias
    nblk = rows // blk
    s = s.reshape(nblk, blk, n)
    b3 = lax.broadcasted_iota(I32, (nblk, 1, 1), 0)
    base = pos0 + (b3 * blk) // tile * tile
    s = s + slope * (base - qpos).astype(F32)
    if mask_fn is not None:
        kpos = pos0 + b3 * blk + lax.broadcasted_iota(I32, (nblk, blk, n), 1)
        s = jnp.where(mask_fn(b3, kpos), s, NEG)
    s = s.reshape(rows, n)
    m = jnp.max(s, axis=0, keepdims=True)
    p = jnp.where(s > 0.5 * NEG, jnp.exp2(s - m), 0.0)
    l = jnp.sum(p, axis=0, keepdims=True)
    vT = jnp.concatenate([vT_ref[0, t] for t in range(n_v)], axis=1)
    return jnp.dot(vT, p.astype(BF16), preferred_element_type=F32) / jnp.maximum(l, 1e-30)


def _step_queries(q_ref, meta_ref, qa_s):
    tq = q_ref.shape[1]
    n = qa_s.shape[1]
    qa_s[...] = jnp.zeros(qa_s.shape, BF16)
    pieces = jnp.concatenate([meta_ref[META_PIECES:META_ROWS, :], jnp.zeros((8, n), F32)], axis=0)
    qa_s[FEAT_POS:FEAT_POS + 16, :] = pieces.astype(BF16)
    heads = q_ref.shape[0] // HEAD_DIM
    qf = []
    for r in range(heads):
        q = q_ref[r * HEAD_DIM:(r + 1) * HEAD_DIM, :]
        z = jnp.zeros_like(q)
        blk = jnp.concatenate([q, z] if r < heads // N_KV else [z, q], axis=0)
        qa_s[0:N_KV * HEAD_DIM, r * tq:(r + 1) * tq] = (blk * (ATTN_SCALE * LOG2E)).astype(BF16)
        qf.append(blk)
    return qa_s[...], jnp.concatenate(qf, axis=1)


def _store_heads(o_ref, o, mixed):
    if mixed:
        o_ref[...] = o
        return
    tq = o_ref.shape[1]
    for r in range(o_ref.shape[0] // HEAD_DIM):
        o_ref[r * HEAD_DIM:(r + 1) * HEAD_DIM, :] = o[:, r * tq:(r + 1) * tq]


def _top_rows(vals, rowi, k):
    sel = jnp.zeros(vals.shape, F32)
    for _ in range(k):
        m = jnp.max(vals, axis=0, keepdims=True)
        first = jnp.min(jnp.where(vals == m, rowi, 1 << 20), axis=0, keepdims=True)
        pick = (rowi == first) & (m > SENTINEL)
        sel = jnp.where(pick, 1.0, sel)
        vals = jnp.where(pick, SENTINEL, vals)
    return sel


def _moba_kernel(qa_ref, qf_ref, meta_ref, kmean_ref, k_ref, vT_ref, o_ref, sel_ref, acc_ref, qa_s,
                 *, span, q_off, mixed):
    i = pl.program_id(1)
    n = meta_ref.shape[1]
    qa, qf = (qa_ref[...], qf_ref[...]) if mixed else _step_queries(qa_ref, meta_ref, qa_s)
    own = (q_off + i * span) // MOBA_BLOCK
    slope = meta_ref[0:1, :]
    qpos = meta_ref[1:2, :].astype(I32)
    nrow = kmean_ref.shape[1]
    gate = jnp.dot(kmean_ref[0], qf, precision=lax.Precision.HIGHEST, preferred_element_type=F32)
    rowi = lax.broadcasted_iota(I32, (nrow, n), 0)
    sel_ref[...] = _top_rows(jnp.where(rowi < own, gate, SENTINEL), rowi, MOBA_TOPK)
    if mixed:
        own_s = q_off // MOBA_BLOCK
        sel3 = sel_ref[0:own_s + 1, :][:, None, :] > 0.5
        o_ref[...] = _dense_attend(k_ref, own_s + 1, vT_ref, own_s + 1, qa, slope, qpos, KV_TILE,
                                   mask_fn=lambda b3, kpos: ((b3 < own_s) & sel3) | ((b3 >= own_s) & (kpos <= qpos)))
        return
    acc_ref[...] = jnp.zeros(acc_ref.shape, F32)

    def step(t, slot, ok, carry):
        s = jnp.dot(k_ref[0, t], qa, preferred_element_type=F32)
        c = slope * (t * KV_TILE - qpos).astype(F32)
        sel = sel_ref[pl.ds(t, 1), :] > 0.5
        ok = sel if ok is None else sel & _row_flag(ok, n)
        return _softmax_step(s, c, ok, vT_ref[0, t], *carry, acc_ref)

    m, l = _tile_groups(own, step, _softmax_init(n), _group(mixed))
    s = jnp.dot(k_ref[0, own], qa, preferred_element_type=F32)
    krow = lax.broadcasted_iota(I32, (KV_TILE, n), 0)
    s = jnp.where(krow <= qpos - own * KV_TILE, s, NEG)
    c = slope * (own * KV_TILE - qpos).astype(F32)
    _, l = _softmax_step(s, c, None, vT_ref[0, own], m, l, acc_ref)
    _store_heads(o_ref, acc_ref[...] / jnp.maximum(l, 1e-30), mixed)


def _query_specs(cols, heads, q_row, n_cols):
    nq = cols.nq
    if cols.sample:
        col = lambda b, i: (0, b * nq + i)
        return (pl.BlockSpec((KD, SAMPLE_COLS), col), pl.BlockSpec((N_KV * HEAD_DIM, SAMPLE_COLS), col),
                jax.ShapeDtypeStruct((N_KV * HEAD_DIM, n_cols), F32), N_KV * HEAD_DIM)
    rows = heads * HEAD_DIM
    blk0 = q_row // rows
    return (pl.BlockSpec((rows, cols.tq), lambda b, i: (blk0, b * nq + i)),
            pl.BlockSpec((rows, cols.tq), lambda b, i: (0, b * nq + i)),
            jax.ShapeDtypeStruct((heads * HEAD_DIM, cols.nb * nq * cols.tq), F32), HEAD_DIM)


def _moba(cols, q, q_row, qf, meta, kmean, k, vT):
    nb, nq = cols.nb, cols.nq
    n = SAMPLE_COLS if cols.sample else N_A_HEADS * cols.tq
    nT, nblk = k.shape[1], kmean.shape[1]
    q_spec, o_spec, o_shape, vrows = _query_specs(cols, N_A_HEADS, q_row, meta.shape[1])
    col = lambda b, i: (0, b * nq + i)
    return pl.pallas_call(
        functools.partial(_moba_kernel, span=cols.span, q_off=cols.q_off, mixed=cols.sample), grid=(nb, nq),
        in_specs=[q_spec,
                  pl.BlockSpec((qf.shape[0], n), col),
                  pl.BlockSpec((META_ROWS, n), col),
                  pl.BlockSpec((1, nblk, N_KV * HEAD_DIM), lambda b, i: (b, 0, 0)),
                  pl.BlockSpec((1, nT, KV_TILE, KD), lambda b, i: (b, 0, 0, 0)),
                  pl.BlockSpec((1, nT, N_KV * HEAD_DIM, KV_TILE), lambda b, i: (b, 0, 0, 0))],
        out_specs=o_spec, out_shape=o_shape,
        scratch_shapes=[pltpu.VMEM((nblk, n), F32), pltpu.VMEM((vrows, n), F32), pltpu.VMEM((KD, n), BF16)],
        compiler_params=_cparams(("parallel", "parallel")), name="moba")(q, qf, meta, kmean, k, vT)


def _cmp_kernel(r_ref, pe_ref, w1_ref, w2_ref, o_ref):
    n = o_ref.shape[1]
    hid = 2 * CMP_HIDDEN
    c = r_ref.shape[2]
    u = jnp.zeros((n, hid), F32)
    lo = jnp.zeros((n, hid), F32)
    for p in range(CMP_STRIDE):
        rp = r_ref[0, pl.ds(p, n, stride=CMP_STRIDE), :]
        cols = slice(p * c, (p + 1) * c)
        u = u + jnp.dot((rp + pe_ref[0:1, cols]).astype(BF16), w1_ref[cols, 0:hid], preferred_element_type=F32)
        lo = lo + jnp.dot((rp + pe_ref[1:2, cols]).astype(BF16), w1_ref[cols, hid:2 * hid],
                          preferred_element_type=F32)
    pre = u + pltpu.roll(lo, n - 1, 0)
    h = pre / (1.0 + jnp.exp(-pre))
    o_ref[0] = jnp.dot(h.astype(BF16), w2_ref[...], preferred_element_type=F32)


def _compress(raw, pe2, w1big, w2big, n_rows):
    nb, _, c = raw.shape
    return pl.pallas_call(
        _cmp_kernel, grid=(nb,),
        in_specs=[pl.BlockSpec((1, n_rows * CMP_STRIDE, c), lambda b: (b, 0, 0)),
                  pl.BlockSpec((2, 2048), lambda b: (0, 0)),
                  pl.BlockSpec((2048, 4 * CMP_HIDDEN), lambda b: (0, 0)),
                  pl.BlockSpec((2 * CMP_HIDDEN, 128), lambda b: (0, 0))],
        out_specs=pl.BlockSpec((1, n_rows, 128), lambda b: (b, 0, 0)),
        out_shape=jax.ShapeDtypeStruct((nb, n_rows, 128), F32),
        compiler_params=_cparams(("parallel",)), name="compress")(raw, pe2, w1big, w2big)


def _cmp_weights(pe, w1, w2):
    w1r = w1.reshape(2, CMP_STRIDE, HEAD_DIM, CMP_HIDDEN)
    eye = jnp.eye(N_KV, dtype=w1.dtype)
    big = jnp.einsum('hpdj,ge->hpdgej', w1r, eye)
    big = big.transpose(1, 3, 2, 0, 4, 5).reshape(CMP_STRIDE * N_KV * HEAD_DIM, 2 * N_KV * CMP_HIDDEN)
    w2big = jnp.einsum('jd,ge->gjed', w2, eye).reshape(N_KV * CMP_HIDDEN, N_KV * HEAD_DIM)
    pe2 = jnp.broadcast_to(pe.reshape(2, CMP_STRIDE, 1, HEAD_DIM), (2, CMP_STRIDE, N_KV, HEAD_DIM)).reshape(2, 2048)
    return pe2, big.astype(BF16), w2big.astype(BF16)


def _nsa_kernel(qa_ref, meta_ref, kc_ref, vcT_ref, taps_ref, grp_ref, ks_ref, vsT_ref, kw_ref, vwT_ref, o_ref,
                sel_ref, qs_ref, acc_ref, qa_s, *, span, q_off, kw_base, mixed):
    i = pl.program_id(1)
    n = meta_ref.shape[1]
    hps = N_B_HEADS // N_KV
    tq = n if mixed else n // N_B_HEADS
    nq = n if mixed else N_KV * tq
    t0 = q_off + i * span
    slope = meta_ref[0:1, :]
    qpos = meta_ref[1:2, :].astype(I32)
    qa = qa_ref[...] if mixed else _step_queries(qa_ref, meta_ref, qa_s)[0]
    n_cmp = kc_ref.shape[1]
    n_tab = sel_ref.shape[0]

    def per_head(x):
        if mixed:
            return x
        return jnp.concatenate([x[:, (r // hps) * tq:(r // hps + 1) * tq] for r in range(N_B_HEADS)], axis=1)

    cend = CMP_STRIDE * lax.broadcasted_iota(I32, (n_cmp, n), 0) + (CMP_LEN - 1)
    cmask = cend <= qpos
    s = jnp.dot(kc_ref[0], qa[0:N_KV * HEAD_DIM, :], preferred_element_type=F32) + slope * (cend - qpos).astype(F32)
    s = jnp.where(cmask, s, NEG)
    p = jnp.where(cmask, jnp.exp2(s - jnp.max(s, axis=0, keepdims=True)), 0.0)
    p = p * (1.0 / jnp.maximum(jnp.sum(p, axis=0, keepdims=True), 1e-30))
    o_c = _pv(vcT_ref[0], p, acc_ref.shape[0])

    if mixed:
        imp = jnp.dot(p, grp_ref[...], precision=lax.Precision.HIGHEST, preferred_element_type=F32)
    else:
        imp = jnp.concatenate([sum(p[:, (gg * hps + r) * tq:(gg * hps + r + 1) * tq] for r in range(hps))
                               for gg in range(N_KV)], axis=1)
    p_slc = jnp.dot(taps_ref[...], imp, precision=lax.Precision.HIGHEST, preferred_element_type=F32)
    p_slc = jnp.concatenate([p_slc, jnp.full((n_tab - p_slc.shape[0], nq), SENTINEL, F32)], axis=0)
    rowi = lax.broadcasted_iota(I32, (n_tab, nq), 0)
    own = (qpos if mixed else jnp.concatenate([qpos[:, 0:tq]] * N_KV, axis=1)) >> 6
    sel = _top_rows(jnp.where(rowi < own, p_slc, SENTINEL), rowi, SEL_TOPK)
    sel_ref[...] = jnp.where(rowi == own, 1.0, sel)

    gt = 1.0 / (1.0 + jnp.exp(-meta_ref[2:5, :]))
    if mixed:
        n_s = q_off // SEL_TILE + 1
        sel3 = sel_ref[0:n_s * (SEL_TILE // SEL_BLOCK), :][:, None, :] > 0.5
        o_s = _dense_attend(ks_ref, n_s, vsT_ref, n_s, qa, slope, qpos, SEL_BLOCK,
                            mask_fn=lambda b3, kpos: sel3 & (kpos <= qpos))
        n_w = min((q_off - kw_base) // KV_TILE + 1, kw_ref.shape[1])
        o_w = _dense_attend(kw_ref, n_w, vwT_ref, n_w, qa, slope, qpos, KV_TILE, pos0=kw_base,
                            mask_fn=lambda b3, kpos: (kpos <= qpos) & (kpos > qpos - WINDOW))
        o_ref[...] = gt[0:1, :] * o_c + gt[1:2, :] * o_s + gt[2:3, :] * o_w
        return

    per_tile = SEL_TILE // SEL_BLOCK
    last = (t0 + span - 1) // SEL_TILE
    for slot in range(qs_ref.shape[0]):
        qs_ref[slot] = qa
    acc_ref[...] = jnp.zeros(acc_ref.shape, F32)

    def sel_scores(t, slot):
        rows = sel_ref[pl.ds(pl.multiple_of(t * per_tile, per_tile), per_tile), :]
        bias = per_head(jnp.where(rows > 0.5, 0.0, NEG))
        bias = jnp.concatenate([bias, jnp.zeros((per_tile, n), F32)], axis=0)
        qs_ref[slot, FEAT_SEL:FEAT_SEL + 2 * per_tile, :] = bias.astype(BF16)
        s = jnp.dot(ks_ref[0, t], qs_ref[slot], preferred_element_type=F32)
        return s, slope * (t * SEL_TILE - qpos).astype(F32)

    def sel_step(t, slot, ok, carry):
        s, c = sel_scores(t, slot)
        return _softmax_step(s, c, ok, vsT_ref[0, t], *carry, acc_ref)

    m, l = _tile_groups(last, sel_step, _softmax_init(n), qs_ref.shape[0])
    s, c = sel_scores(last, 0)
    srow = lax.broadcasted_iota(I32, (SEL_TILE, n), 0)
    s = jnp.where(srow <= qpos - last * SEL_TILE, s, NEG)
    _, l = _softmax_step(s, c, None, vsT_ref[0, last], m, l, acc_ref)
    o_s = acc_ref[...] / jnp.maximum(l, 1e-30)

    lo = jnp.maximum(t0 - (WINDOW - 1) - kw_base, 0) // KV_TILE
    hi = jnp.minimum((t0 + span - 1 - kw_base) // KV_TILE + 1, kw_ref.shape[1])
    wrow = lax.broadcasted_iota(I32, (KV_TILE, n), 0)
    acc_ref[...] = jnp.zeros(acc_ref.shape, F32)

    def win_body(t, carry):
        s = jnp.dot(kw_ref[0, t], qa, preferred_element_type=F32)
        base = kw_base + t * KV_TILE
        dist = qpos - base - wrow
        s = jnp.where((dist >= 0) & (dist < WINDOW), s, NEG)
        return _softmax_step(s, slope * (base - qpos).astype(F32), None, vwT_ref[0, t],
                             *carry, acc_ref)

    _, l = lax.fori_loop(lo, hi, win_body, _softmax_init(n))
    o_w = acc_ref[...] / jnp.maximum(l, 1e-30)

    _store_heads(o_ref, gt[0:1, :] * o_c + gt[1:2, :] * o_s + gt[2:3, :] * o_w, mixed)


def _taps_matrix(n_sel, n_cmp):
    m = np.zeros((n_sel, n_cmp), np.float32)
    for j in range(n_sel):
        for off, w in ((-1, 1.0), (0, 2.0), (1, 2.0), (2, 2.0), (3, 1.0)):
            c = 4 * j + off
            if 0 <= c < n_cmp:
                m[j, c] = w
    return jnp.asarray(m)


def _group_matrix(cols):
    grp = cols.head_group(N_B_HEADS)
    m = np.zeros((SAMPLE_COLS, SAMPLE_COLS), np.float32)
    m[:N_B_HEADS, :N_B_HEADS] = grp[:, None] == grp[None, :]
    return jnp.asarray(m)


def _nsa(cols, q, q_row, meta, kc, vcT, ks, vsT, kw, vwT, kw_base):
    nb, nq = cols.nb, cols.nq
    n = SAMPLE_COLS if cols.sample else N_B_HEADS * cols.tq
    nTs, nTw = ks.shape[1], kw.shape[1]
    n_cmp = kc.shape[1]
    n_sel = n_cmp // 4
    n_tab = n_sel + SEL_TILE // SEL_BLOCK
    nsel_cols = n if cols.sample else N_KV * cols.tq
    q_spec, o_spec, o_shape, vrows = _query_specs(cols, N_B_HEADS, q_row, meta.shape[1])
    col = lambda b, i: (0, b * nq + i)
    seq3 = lambda b, i: (b, 0, 0)
    seq4 = lambda b, i: (b, 0, 0, 0)
    return pl.pallas_call(
        functools.partial(_nsa_kernel, span=cols.span, q_off=cols.q_off, kw_base=kw_base, mixed=cols.sample),
        grid=(nb, nq),
        in_specs=[q_spec,
                  pl.BlockSpec((META_ROWS, n), col),
                  pl.BlockSpec((1, n_cmp, N_KV * HEAD_DIM), seq3),
                  pl.BlockSpec((1, N_KV * HEAD_DIM, n_cmp), seq3),
                  pl.BlockSpec((n_sel, n_cmp), lambda b, i: (0, 0)),
                  pl.BlockSpec((SAMPLE_COLS, SAMPLE_COLS), lambda b, i: (0, 0)),
                  pl.BlockSpec((1, nTs, SEL_TILE, KD), seq4),
                  pl.BlockSpec((1, nTs, N_KV * HEAD_DIM, SEL_TILE), seq4),
                  pl.BlockSpec((1, nTw, KV_TILE, KD), seq4),
                  pl.BlockSpec((1, nTw, N_KV * HEAD_DIM, KV_TILE), seq4)],
        out_specs=o_spec, out_shape=o_shape,
        scratch_shapes=[pltpu.VMEM((n_tab, nsel_cols), F32), pltpu.VMEM((_group(cols.sample), KD, n), BF16),
                        pltpu.VMEM((vrows, n), F32), pltpu.VMEM((KD, n), BF16)],
        compiler_params=_cparams(("parallel", "parallel")), name="nsa")(
            q, meta, kc, vcT, _taps_matrix(n_sel, n_cmp), _group_matrix(cols), ks, vsT, kw, vwT)


def _dsa_kernel(qi_ref, wi_ref, ki_ref, qa_ref, meta_ref, k_ref, vT_ref, o_ref, key_ref, row_ref, acc_ref, qa_s,
                *, span, q_off, topk, mixed):
    i = pl.program_id(1)
    n = meta_ref.shape[1]
    nq = qi_ref.shape[1]
    rep = n // nq
    t0 = q_off + i * span
    n_kv = (t0 + span - 1) // KV_TILE + 1
    krow = lax.broadcasted_iota(I32, (KV_TILE, nq), 0)

    def search(count, shape):
        kf = float(topk)
        thr = jnp.where(count(lambda key, kpos: key >= 0) >= kf, 0, INT_MIN).astype(I32)

        def bit_body(j, thr):
            cand = thr | (1 << (30 - j))
            return jnp.where(count(lambda key, kpos: key >= cand) >= kf, cand, thr)
        thr = lax.fori_loop(0, 31, bit_body, thr)
        need = kf - count(lambda key, kpos: key > thr)

        def tie_search():
            def idx_body(j, jb):
                cand = jb | (1 << (14 - j))
                c = count(lambda key, kpos: (key == thr) & (kpos < cand))
                return jnp.where(c <= need, cand, jb)
            return lax.fori_loop(0, 15, idx_body, jnp.zeros(shape, I32))

        n_ge = count(lambda key, kpos: key >= thr)
        jb = lax.cond(jnp.max(n_ge) > kf, tie_search, lambda: jnp.full(shape, (1 << 15) - 1, I32))
        return thr, jb

    def select():
        qpos = t0 + (lax.broadcasted_iota(I32, (1, nq), 1) if span > 1 else jnp.zeros((1, nq), I32))
        w = wi_ref[...] * (IDX_SCALE * N_IDX_HEADS ** -0.5)
        if mixed:
            qi = qi_ref[...].astype(BF16)
        else:
            qi = [qi_ref[a * IDX_DIM:(a + 1) * IDX_DIM, :].astype(BF16) for a in range(N_IDX_HEADS)]
        if mixed:
            row_ref[...] = jnp.full(row_ref.shape, INT_MIN, I32)

        def score_body(t, _):
            ki = ki_ref[0, t]
            if mixed:
                rel = jnp.dot(ki, qi, preferred_element_type=F32)
                acc = jnp.sum(jnp.maximum(rel, 0.0) * w[0:1, :], axis=1, keepdims=True)
                acc = jnp.broadcast_to(acc, (KV_TILE, nq))
            else:
                acc = jnp.zeros((KV_TILE, nq), F32)
                for a in range(N_IDX_HEADS):
                    rel = jnp.dot(ki, qi[a], preferred_element_type=F32)
                    acc = acc + jnp.maximum(rel, 0.0) * w[a:a + 1, :]
            sc = jnp.where(t * KV_TILE + krow <= qpos, acc, NEG)
            bits = lax.bitcast_convert_type(sc, I32)
            key = bits ^ ((bits >> 31) & 0x7FFFFFFF)
            key_ref[t] = key
            if mixed:
                row_ref[pl.ds(t, 1), :] = jnp.transpose(key)[0:1, :]
            return 0
        lax.fori_loop(0, n_kv, score_body, 0)

        if mixed:
            keys = row_ref[...]
            kpos_r = (lax.broadcasted_iota(I32, keys.shape, 0) * KV_TILE + lax.broadcasted_iota(I32, keys.shape, 1))

            def count(pred):
                c = jnp.sum(jnp.where(pred(keys, kpos_r), 1.0, 0.0), axis=0, keepdims=True)
                return jnp.sum(c, axis=1, keepdims=True)
            thr, jb = search(count, (1, 1))
        else:
            n_grp = (n_kv + COUNT_GROUP - 1) // COUNT_GROUP

            def pad_body(t, _):
                key_ref[t] = jnp.full((KV_TILE, nq), INT_MIN, I32)
                return 0
            lax.fori_loop(n_kv, n_grp * COUNT_GROUP, pad_body, 0)

            def count(pred):
                def body(u, c):
                    for j in range(COUNT_GROUP):
                        t = u * COUNT_GROUP + j
                        hit = pred(key_ref[t], t * KV_TILE + krow)
                        c = c + jnp.sum(jnp.where(hit, 1.0, 0.0), axis=0, keepdims=True)
                    return c
                return lax.fori_loop(0, n_grp, body, jnp.zeros((1, nq), F32))
            thr, jb = search(count, (1, nq))

        def bias_body(t, _):
            key = key_ref[t]
            kpos = t * KV_TILE + krow
            keep = ((key > thr) | ((key == thr) & (kpos < jb))) & (kpos <= qpos)
            key_ref[t] = lax.bitcast_convert_type(jnp.where(keep, 0.0, NEG), I32)
            return 0
        lax.fori_loop(0, n_kv, bias_body, 0)

    select()
    slope = meta_ref[0:1, :]
    qpos_c = meta_ref[1:2, :].astype(I32)
    qa = qa_ref[...] if mixed else _step_queries(qa_ref, meta_ref, qa_s)[0]
    if mixed:
        n_t = q_off // KV_TILE + 1
        bias = lax.bitcast_convert_type(key_ref[0:n_t], F32).reshape(n_t * KV_TILE, nq)
        o_ref[...] = _dense_attend(k_ref, n_t, vT_ref, n_t, qa, slope, qpos_c, KV_TILE, bias=bias)
        return
    acc_ref[...] = jnp.zeros(acc_ref.shape, F32)

    def step(t, slot, ok, carry):
        bias = lax.bitcast_convert_type(key_ref[t], F32)
        if rep > 1:
            bias = jnp.concatenate([bias] * rep, axis=1)
        s = jnp.dot(k_ref[0, t], qa, preferred_element_type=F32) + bias
        c = slope * (t * KV_TILE - qpos_c).astype(F32)
        return _softmax_step(s, c, ok, vT_ref[0, t], *carry, acc_ref)
    _, l = _tile_groups(n_kv, step, _softmax_init(n), _group(mixed))
    _store_heads(o_ref, acc_ref[...] / jnp.maximum(l, 1e-30), mixed)


def _dsa(cols, qi, qi_row, wi, wi_row, ki, q, q_row, meta, k, vT, topk):
    nb, nq = cols.nb, cols.nq
    n = SAMPLE_COLS if cols.sample else N_C_HEADS * cols.tq
    ncol = SAMPLE_COLS if cols.sample else cols.tq
    nT = k.shape[1]
    assert nT % COUNT_GROUP == 0
    q_spec, o_spec, o_shape, vrows = _query_specs(cols, N_C_HEADS, q_row, meta.shape[1])
    qi_rows = IDX_DIM if cols.sample else N_IDX_HEADS * IDX_DIM
    qi_b, wi_b = qi_row // qi_rows, wi_row // N_IDX_HEADS
    col = lambda b, i: (0, b * nq + i)
    seq4 = lambda b, i: (b, 0, 0, 0)
    return pl.pallas_call(
        functools.partial(_dsa_kernel, span=cols.span, q_off=cols.q_off, topk=topk, mixed=cols.sample),
        grid=(nb, nq),
        in_specs=[pl.BlockSpec((qi_rows, ncol), lambda b, i: (qi_b, b * nq + i)),
                  pl.BlockSpec((N_IDX_HEADS, ncol), lambda b, i: (wi_b, b * nq + i)),
                  pl.BlockSpec((1, nT, KV_TILE, IDX_DIM), seq4),
                  q_spec,
                  pl.BlockSpec((META_ROWS, n), col),
                  pl.BlockSpec((1, nT, KV_TILE, KD), seq4),
                  pl.BlockSpec((1, nT, N_KV * HEAD_DIM, KV_TILE), seq4)],
        out_specs=o_spec, out_shape=o_shape,
        scratch_shapes=[pltpu.VMEM((nT, KV_TILE, ncol), I32), pltpu.VMEM((-(-nT // 8) * 8, KV_TILE), I32),
                        pltpu.VMEM((vrows, n), F32), pltpu.VMEM((KD, n), BF16)],
        compiler_params=_cparams(("parallel", "parallel")), name="dsa")(
            qi, wi, ki, q, meta, k, vT)


def _mem_kernel(q_ref, mk_ref, mvT_ref, o_ref):
    for h in range(N_MEM_HEADS):
        rows = slice(h * HEAD_DIM, (h + 1) * HEAD_DIM)
        q = (q_ref[rows, :] * ATTN_SCALE).astype(BF16)
        s = jnp.dot(mk_ref[0, h].astype(BF16), q, preferred_element_type=F32)
        p = jnp.exp(s - jnp.max(s, axis=0, keepdims=True))
        l = jnp.sum(p, axis=0, keepdims=True)
        o_ref[rows, :] = jnp.dot(mvT_ref[0, h].astype(BF16), p.astype(BF16), preferred_element_type=F32) / l


def _mem(hT, q_row, mk, mvT, nb, nq, tq):
    n_mem = mk.shape[2]
    n = N_MEM_HEADS * HEAD_DIM
    qb = q_row // n
    return pl.pallas_call(
        _mem_kernel, grid=(nb, nq),
        in_specs=[pl.BlockSpec((n, tq), lambda b, i: (qb, b * nq + i)),
                  pl.BlockSpec((1, N_MEM_HEADS, n_mem, HEAD_DIM), lambda b, i: (b, 0, 0, 0)),
                  pl.BlockSpec((1, N_MEM_HEADS, HEAD_DIM, n_mem), lambda b, i: (b, 0, 0, 0))],
        out_specs=pl.BlockSpec((n, tq), lambda b, i: (0, b * nq + i)),
        out_shape=jax.ShapeDtypeStruct((n, nb * nq * tq), F32),
        compiler_params=_cparams(("parallel", "parallel")), name="mem")(hT, mk, mvT)


def _mem_branch(cols, hT, q_row, mk, mvT):
    n = N_MEM_HEADS * HEAD_DIM
    if cols.sample:
        o = _mem(cols.tile_cols(hT[q_row:q_row + n]), 0, mk, mvT, cols.nb, 1, SAMPLE_COLS)
        return cols.from_tile_cols(o)
    return _mem(hT, q_row, mk, mvT, cols.nb, cols.nq, cols.tq)


def _even_wT(w_in):
    sizes = (384, 128, 128, 384, 384, 128, 128, 128, 128, 128, 128, 18, 384, 256, 256)
    offs = np.cumsum((0,) + sizes)
    (a_q, a_k, a_v, a_z, b_q, b_kc, b_vc, b_ks, b_vs, b_kw, b_vw, b_g, b_z, m_q, m_z) = [
        w_in[:, offs[n]:offs[n + 1]] for n in range(len(sizes))]
    d = w_in.shape[0]
    bg = jnp.pad(b_g, ((0, 0), (0, 32 - b_g.shape[1])))
    cols = [a_q, b_q, m_q, a_z, b_z, m_z, a_k, a_v, b_kc, b_vc, b_ks, b_vs, b_kw, b_vw, bg]
    w = jnp.concatenate(cols, axis=1)
    w = jnp.pad(w, ((0, 0), (0, EVEN_TOTAL - w.shape[1])))
    return w.T


def _odd_wT(w_in):
    sizes = (768, 128, 128, 512, 64, 8, 768, 256, 256)
    offs = np.cumsum((0,) + sizes)
    (c_q, c_k, c_v, c_qi, c_ki, c_wi, c_z, m_q, m_z) = [w_in[:, offs[n]:offs[n + 1]] for n in range(len(sizes))]
    w = jnp.concatenate([c_q, m_q, c_z, m_z, c_qi, c_k, c_v, c_ki, c_wi], axis=1)
    w = jnp.pad(w, ((0, 0), (0, ODD_TOTAL - w.shape[1])))
    return w.T


def _keys(cols, new_tok, pool, page_table, tile, mode):
    if cols.sample:
        return _gather(pool, page_table, new_tok, mode, tile)
    return _kprep(new_tok, tile, mode)


def _values(cols, new_tok, pool, page_table, tile):
    if cols.sample:
        return _gather(pool, page_table, new_tok, 'v', tile)
    return _vT_tiles(new_tok, tile)


def _raw_rows(cols, new_tok, pool, page_table):
    if cols.sample:
        return _gather(pool, page_table, None, 'raw', KV_TILE)
    return new_tok


def _queries(cols, hT, row, heads, slopes, extra=()):
    meta = cols.meta(heads, slopes, extra)
    if cols.sample:
        qa, qf = cols.columns(hT[row:row + heads * HEAD_DIM], heads, meta)
        return qa, 0, qf, meta
    return hT, row, meta, meta


def _heads_out(cols, o, heads):
    return cols.collapse(o, heads) if cols.sample else o


def _gate_cols(cols, gT):
    return [cols.per_token(gT, N_B_HEADS)]


def _even_layer(cols, xT, mk, mvT, past, page_table, wT, woT, g, b, cmpw, alpha):
    R = EVEN_ROWS
    nb = cols.nb
    hT = _mm(wT, xT, 640, min(1024, xT.shape[1]))
    names = ('a_k', 'a_v', 'b_kc', 'b_vc', 'b_ks', 'b_vs')
    new = [cols.tokens(hT[R[nm]:R[nm] + 128]) for nm in names]
    kw_new, vw_new = cols.tokens(hT[R['b_kw']:R['b_kw'] + 128]), cols.tokens(hT[R['b_vw']:R['b_vw'] + 128])
    pools = past[:6] if cols.sample else [None] * 6
    if cols.sample:
        kw = jnp.concatenate([past[6].reshape(nb, -1, 128), kw_new], axis=1)
        vw = jnp.concatenate([past[7].reshape(nb, -1, 128), vw_new], axis=1)
        n_win = past[6].shape[1]
        kw_base = cols.q_off - n_win
        win_out = (kw[:, -n_win:], vw[:, -n_win:])
        pad = -kw.shape[1] % 1024
        kw = jnp.pad(kw, ((0, 0), (0, pad), (0, 0)))
        vw = jnp.pad(vw, ((0, 0), (0, pad), (0, 0)))
    else:
        kw, vw, kw_base = kw_new, vw_new, 0
        n_win = min(WINDOW, cols.t)
        win_out = (kw[:, -n_win:], vw[:, -n_win:])

    sl = _alibi(N_A_HEADS + N_B_HEADS)
    ak, amean = _keys(cols, new[0], pools[0], page_table, KV_TILE, 'k')
    avT = _values(cols, new[1], pools[1], page_table, KV_TILE)
    q, q_row, qf, meta = _queries(cols, hT, R['a_q'], N_A_HEADS, sl[0::2])
    o_a = _moba(cols, q, q_row, qf, meta, amean, ak, avT)

    pe2k, w1k, w2k = cmpw[0]
    pe2v, w1v, w2v = cmpw[1]
    n_cmp = max(cols.q_off, cols.t) // CMP_STRIDE
    kc_raw = _raw_rows(cols, new[2], pools[2], page_table)
    vc_raw = _raw_rows(cols, new[3], pools[3], page_table)
    kc = _compress(kc_raw, pe2k, w1k, w2k, n_cmp).astype(BF16)
    vcT = _compress(vc_raw, pe2v, w1v, w2v, n_cmp).transpose(0, 2, 1).astype(BF16)
    ks = _keys(cols, new[4], pools[4], page_table, SEL_TILE, 'ks')
    vsT = _values(cols, new[5], pools[5], page_table, SEL_TILE)
    kwt = _kprep(kw, KV_TILE, 'ks')
    q, q_row, _, meta = _queries(cols, hT, R['b_q'], N_B_HEADS, sl[1::2],
                               extra=_gate_cols(cols, hT[R['b_g']:R['b_g'] + 18]))
    o_b = _nsa(cols, q, q_row, meta, kc, vcT, ks, vsT, kwt, _vT_tiles(vw, KV_TILE), kw_base)

    o_m = _mem_branch(cols, hT, R['m_q'], mk, mvT)
    parts = [_heads_out(cols, o_a, N_A_HEADS), _heads_out(cols, o_b, N_B_HEADS), o_m]
    yT = _outln(parts, hT, R['z'], xT, woT, g, b, alpha, min(512, xT.shape[1]))
    state = [n.reshape(nb, -1, N_KV, HEAD_DIM) for n in new]
    state += [w.reshape(nb, -1, N_KV, HEAD_DIM) for w in win_out]
    return yT, state


def _odd_layer(cols, xT, mk, mvT, past, page_table, wT, woT, g, b, alpha):
    R = ODD_ROWS
    nb = cols.nb
    hT = _mm(wT, xT, 768, min(1024, xT.shape[1]))
    new = [cols.tokens(hT[R['c_k']:R['c_k'] + 128]), cols.tokens(hT[R['c_v']:R['c_v'] + 128]),
           cols.tokens(hT[R['c_ki']:R['c_ki'] + IDX_DIM])]
    pools = past if cols.sample else [None] * 3
    topk = min(DSA_TOPK, (cols.q_off + cols.t) // 4)
    ck = _keys(cols, new[0], pools[0], page_table, KV_TILE, 'ks')
    cvT = _values(cols, new[1], pools[1], page_table, KV_TILE)
    if cols.sample:
        ki = _gather(pools[2], page_table, new[2], 'i', KV_TILE)
    else:
        ki = new[2].reshape(nb, -1, KV_TILE, IDX_DIM).astype(BF16)
    q, q_row, _, meta = _queries(cols, hT, R['c_q'], N_C_HEADS, _alibi(N_C_HEADS))
    if cols.sample:
        qi = cols.per_token(hT[R['c_qi']:R['c_qi'] + 512], N_IDX_HEADS)
        wi = jnp.pad(cols.per_token(hT[R['c_wi']:R['c_wi'] + 8], N_IDX_HEADS), ((0, 7), (0, 0)))
        o_c = _dsa(cols, qi, 0, wi, 0, ki, q, q_row, meta, ck, cvT, topk)
    else:
        o_c = _dsa(cols, hT, R['c_qi'], hT, R['c_wi'], ki, q, q_row, meta, ck, cvT, topk)
    o_m = _mem_branch(cols, hT, R['m_q'], mk, mvT)
    parts = [_heads_out(cols, o_c, N_C_HEADS), o_m]
    yT = _outln(parts, hT, R['z'], xT, woT, g, b, alpha, min(512, xT.shape[1]))
    state = [new[0].reshape(nb, -1, N_KV, HEAD_DIM), new[1].reshape(nb, -1, N_KV, HEAD_DIM), new[2]]
    return yT, state


def kernel(x_prompt, x_sample, cache_a_k, cache_a_v, cache_b_cmp_k, cache_b_cmp_v, cache_b_sel_k, cache_b_sel_v,
           state_b_win_k, state_b_win_v, cache_c_k, cache_c_v, cache_c_idx_k, cache_mem_k, cache_mem_v,
           page_table, mem_prompt, w_in_even, w_out_even, w_in_odd, w_out_odd, w_mem_kv, ln_g, ln_b,
           cmp_pe, cmp_w1, cmp_w2):
    bp, seq, d = x_prompt.shape
    bs = x_sample.shape[0]
    depth = w_mem_kv.shape[0]
    n_mem = mem_prompt.shape[1]
    past_len = page_table.shape[1] * cache_a_k.shape[2]
    alpha = (2.0 * depth) ** 0.25
    assert x_sample.shape[1] == 1 and bs <= 128 and seq % 1024 == 0 and past_len % 1024 == 0

    gp = _Cols(bp, seq, 0)
    gs = _Cols(bs, 1, past_len)
    xp = x_prompt.reshape(bp * seq, d).T
    xs = jnp.pad(x_sample.reshape(bs, d).T, ((0, 0), (0, 128 - bs)))
    n_pool = cache_a_k.shape[1]

    def pages(c):
        return jnp.moveaxis(c.reshape(c.shape[:3] + (-1,)), 2, 3).reshape(c.shape[0] * n_pool, -1, c.shape[2])

    even_paged = tuple(pages(c) for c in (cache_a_k, cache_a_v, cache_b_cmp_k, cache_b_cmp_v, cache_b_sel_k,
                                          cache_b_sel_v))
    odd_paged = tuple(pages(c) for c in (cache_c_k, cache_c_v, cache_c_idx_k))
    ev_p, ev_s, od_p, od_s, mk_list, mv_list = [], [], [], [], [], []
    for layer in range(depth):
        j = layer // 2
        mem_kv = _mm(mem_prompt.reshape(bp * n_mem, d), w_mem_kv[layer], bp * n_mem, 2 * N_MEM_HEADS * HEAD_DIM)
        mem_kv = mem_kv.reshape(bp, n_mem, 2, N_MEM_HEADS, HEAD_DIM)
        mk_p, mv_p = mem_kv[:, :, 0], mem_kv[:, :, 1]
        mk_list.append(mk_p)
        mv_list.append(mv_p)
        mem_p = (mk_p.transpose(0, 2, 1, 3), mv_p.transpose(0, 2, 3, 1))
        mem_s = (cache_mem_k[layer].transpose(0, 2, 1, 3), cache_mem_v[layer].transpose(0, 2, 3, 1))
        g = ln_g[layer].reshape(d, 1)
        b = ln_b[layer].reshape(d, 1)
        if layer % 2 == 0:
            wT = _even_wT(w_in_even[j])
            woT = w_out_even[j].T
            cmpw = [_cmp_weights(cmp_pe[j, c], cmp_w1[j, c], cmp_w2[j, c]) for c in range(2)]
            past = even_paged + (state_b_win_k[j], state_b_win_v[j])
            xp, st_p = _even_layer(gp, xp, *mem_p, None, None, wT.astype(BF16), woT.astype(BF16), g, b, cmpw, alpha)
            xs, st_s = _even_layer(gs, xs, *mem_s, past, page_table + j * n_pool, wT, woT, g, b, cmpw, alpha)
            ev_p.append(st_p)
            ev_s.append(st_s)
        else:
            wT = _odd_wT(w_in_odd[j])
            woT = w_out_odd[j].T
            past = odd_paged
            xp, st_p = _odd_layer(gp, xp, *mem_p, None, None, wT.astype(BF16), woT.astype(BF16), g, b, alpha)
            xs, st_s = _odd_layer(gs, xs, *mem_s, past, page_table + j * n_pool, wT, woT, g, b, alpha)
            od_p.append(st_p)
            od_s.append(st_s)
    y_p = xp.T.reshape(bp, seq, d)
    y_s = xs[:, :bs].T.reshape(bs, 1, d)
    ev_p = [jnp.stack(l) for l in zip(*ev_p)]
    ev_s = [jnp.stack(l) for l in zip(*ev_s)]
    od_p = [jnp.stack(l) for l in zip(*od_p)]
    od_s = [jnp.stack(l) for l in zip(*od_s)]
    out = [y_p, y_s]
    for p, s in zip(ev_p, ev_s):
        out += [p, s]
    for p, s in zip(od_p, od_s):
        out += [p, s]
    out += [jnp.stack(mk_list), jnp.stack(mv_list)]
    return tuple(out)
```
